```python
import math
import jax, jax.numpy as jnp
from jax import lax
import numpy as np


D_MODEL = 1024
BATCH = 8
SEQ = 2048
DEPTH = 2
DEC_BATCH = 128
DEC_SEQ = 4
PAST_LEN = 16384
PAGE_SIZE = 128

N_META = 16
D_POOL = D_MODEL // 2
POOL_WINDOWS = (2, 4, 8, 16)
N_POOL_GROUPS = len(POOL_WINDOWS)
POOL_GROUP = D_POOL // N_POOL_GROUPS
POOL_BUF = max(POOL_WINDOWS) - 1
D_GLA = D_MODEL - D_POOL
GLA_HEADS = 4
GLA_DV = D_GLA // GLA_HEADS
GLA_DK = GLA_DV // 2
D_GLA_K = GLA_HEADS * GLA_DK
GATE_RANK = 16
GATE_TAU = 16.0
GLA_CHUNK = 64
D_FF = 2816
CONV_W = 3
CONV_BUF = CONV_W - 1
SPLIT_SIZES = (D_POOL, D_GLA_K, D_GLA_K, D_GLA, D_GLA, GATE_RANK)
D_IN = sum(SPLIT_SIZES)
SPLIT_IDX = tuple(int(i) for i in np.cumsum(SPLIT_SIZES)[:-1])
ALPHA = (2 * DEPTH) ** 0.25
BETA = (8 * DEPTH) ** -0.25
LN_EPS = 1e-5
RMS_EPS = 1e-6

kernel_name = "hybrid_pool_gla_convffn_step"


def layer_norm(x, g, b):
    xf = x.astype(jnp.float32)
    mu = jnp.mean(xf, -1, keepdims=True)
    var = jnp.mean(jnp.square(xf - mu), -1, keepdims=True)
    return ((xf - mu) * lax.rsqrt(var + LN_EPS)).astype(x.dtype) * g + b


def multiscale_pool(u, prev, pos0, w_pool, pool_scale):
    B, L, _ = u.shape
    P = prev.shape[1]
    full = jnp.concatenate([prev, u], 1).astype(jnp.float32)
    cs = jnp.cumsum(full, axis=1)
    cs = jnp.concatenate([jnp.zeros_like(cs[:, :1]), cs], 1)
    idx = P + jnp.arange(L)
    pos = pos0 + jnp.arange(L)
    means = []
    for g, w in enumerate(POOL_WINDOWS):
        c = cs[..., g * POOL_GROUP:(g + 1) * POOL_GROUP]
        lo = jnp.maximum(idx + 1 - w, 0)
        cnt = jnp.minimum(w, pos + 1).astype(jnp.float32)
        means.append((c[:, idx + 1] - c[:, lo]) / cnt[None, :, None])
    d = jnp.concatenate(means, -1) - u.astype(jnp.float32)
    d = d.reshape(B, L, N_POOL_GROUPS, POOL_GROUP)
    y = jnp.einsum('blgc,gcd->blgd', d, w_pool.astype(jnp.float32)).reshape(B, L, D_POOL)
    return (y * pool_scale.astype(jnp.float32)).astype(u.dtype), full[:, -POOL_BUF:].astype(u.dtype)


def gla_chunk(S, q, k, v, loga):
    C = q.shape[1]
    b = jnp.cumsum(loga, axis=1)
    causal = jnp.tril(jnp.ones((C, C), bool))
    rel = b[:, :, None] - b[:, None, :]
    decay = jnp.exp(jnp.where(causal[None, :, :, None, None], rel, -jnp.inf))
    attn = jnp.einsum('bthd,bshd,btshd->bhts', q, k, decay)
    o = jnp.einsum('bhts,bshv->bthv', attn, v) + jnp.einsum('bthd,bhdv->bthv', q * jnp.exp(b), S)
    b_end = b[:, -1]
    k_dec = k * jnp.exp(b_end[:, None] - b)
    S_new = jnp.exp(b_end)[..., None] * S + jnp.einsum('bshd,bshv->bhdv', k_dec, v)
    return S_new, o


def gla_prompt(S0, q, k, v, loga):
    B = q.shape[0]
    S1, o_meta = gla_chunk(S0, q[:, :N_META], k[:, :N_META], v[:, :N_META], loga[:, :N_META])

    def to_chunks(t):
        t = t[:, N_META:]
        return t.reshape(B, -1, GLA_CHUNK, *t.shape[2:]).swapaxes(0, 1)

    S_fin, o_rest = lax.scan(lambda S, xs: gla_chunk(S, *xs), S1,
                             (to_chunks(q), to_chunks(k), to_chunks(v), to_chunks(loga)))
    o_rest = o_rest.swapaxes(0, 1).reshape(B, -1, GLA_HEADS, GLA_DV)
    return S_fin, jnp.concatenate([o_meta, o_rest], 1)


def causal_dwconv(gt, prev, conv_w, conv_b):
    L = gt.shape[1]
    full = jnp.concatenate([prev, gt], 1)
    y = conv_b + conv_w[0] * full[:, 0:L] + conv_w[1] * full[:, 1:L + 1] + conv_w[2] * full[:, 2:L + 2]
    return y, full[:, -CONV_BUF:]


def trunk_layer(x, pool_prev, gla_S0, conv_prev, pos0, is_prompt,
                w_in, w_a2, b_a, w_pool, pool_scale, gla_norm, w_out, ln1_g, ln1_b,
                w_up, w_gate, conv_w, conv_b, w_down, ln2_g, ln2_b):
    B, L, _ = x.shape
    h = x @ w_in
    u_pool, q, k, v, r, zr = jnp.split(h, SPLIT_IDX, axis=-1)
    y_pool, pool_buf = multiscale_pool(u_pool, pool_prev, pos0, w_pool, pool_scale)
    f32 = jnp.float32
    loga = jax.nn.log_sigmoid((zr @ w_a2 + b_a).astype(f32)) / GATE_TAU
    qh = (q.astype(f32) * GLA_DK ** -0.5).reshape(B, L, GLA_HEADS, GLA_DK)
    kh = k.astype(f32).reshape(B, L, GLA_HEADS, GLA_DK)
    vh = v.astype(f32).reshape(B, L, GLA_HEADS, GLA_DV)
    lh = loga.reshape(B, L, GLA_HEADS, GLA_DK)
    S0 = gla_S0.astype(f32)
    if is_prompt:
        S_new, o = gla_prompt(S0, qh, kh, vh, lh)
    else:
        S_new, o = gla_chunk(S0, qh, kh, vh, lh)
    o = o * lax.rsqrt(jnp.mean(jnp.square(o), -1, keepdims=True) + RMS_EPS) * gla_norm.astype(f32)
    y_gla = (o.reshape(B, L, D_GLA) * jax.nn.silu(r.astype(f32))).astype(x.dtype)
    mix = jnp.concatenate([y_pool, y_gla], -1) @ w_out
    x = layer_norm(ALPHA * x + mix, ln1_g, ln1_b)
    a = x @ w_up
    gt = x @ w_gate
    gc, conv_buf = causal_dwconv(gt, conv_prev, conv_w, conv_b)
    f = (a * jax.nn.silu(gc)) @ w_down
    x = layer_norm(ALPHA * x + f, ln2_g, ln2_b)
    return x, pool_buf, S_new.astype(gla_S0.dtype), conv_buf


def setup_inputs(seed: int = 0) -> dict:
    key = jax.random.key(seed)
    ks = jax.random.split(key, 24)
    nrm = lambda k, s, sc: jax.random.normal(k, s, jnp.float32) * sc
    return {
        "x_prompt": nrm(ks[0], (BATCH, SEQ, D_MODEL), 1.0),
        "x_sample": nrm(ks[1], (DEC_BATCH, DEC_SEQ, D_MODEL), 1.0),
        "state_pool": nrm(ks[2], (DEPTH, DEC_BATCH, POOL_BUF, D_POOL), 1.0),
        "state_gla": nrm(ks[3], (DEPTH, DEC_BATCH, GLA_HEADS, GLA_DK, GLA_DV), 1.0),
        "state_conv": nrm(ks[4], (DEPTH, DEC_BATCH, CONV_BUF, D_FF), 1.0),
        "meta_tokens": nrm(ks[5], (N_META, D_MODEL), 1.0),
        "w_in": nrm(ks[6], (DEPTH, D_MODEL, D_IN), D_MODEL ** -0.5),
        "w_a2": nrm(ks[7], (DEPTH, GATE_RANK, D_GLA_K), GATE_RANK ** -0.5),
        "b_a": nrm(ks[8], (DEPTH, D_GLA_K), 0.1) + 1.0,
        "w_pool": nrm(ks[9], (DEPTH, N_POOL_GROUPS, POOL_GROUP, POOL_GROUP), POOL_GROUP ** -0.5),
        "pool_scale": 1.0 + nrm(ks[10], (DEPTH, D_POOL), 0.02),
        "gla_norm": 1.0 + nrm(ks[11], (DEPTH, GLA_DV), 0.02),
        "w_out": nrm(ks[12], (DEPTH, D_MODEL, D_MODEL), D_MODEL ** -0.5 * BETA),
        "ln1_g": 1.0 + nrm(ks[13], (DEPTH, D_MODEL), 0.02),
        "ln1_b": nrm(ks[14], (DEPTH, D_MODEL), 0.02),
        "w_up": nrm(ks[15], (DEPTH, D_MODEL, D_FF), D_MODEL ** -0.5),
        "w_gate": nrm(ks[16], (DEPTH, D_MODEL, D_FF), D_MODEL ** -0.5),
        "conv_w": nrm(ks[17], (DEPTH, CONV_W, D_FF), CONV_W ** -0.5),
        "conv_b": nrm(ks[18], (DEPTH, D_FF), 0.02),
        "w_down": nrm(ks[19], (DEPTH, D_FF, D_MODEL), D_FF ** -0.5 * BETA),
        "ln2_g": 1.0 + nrm(ks[20], (DEPTH, D_MODEL), 0.02),
        "ln2_b": nrm(ks[21], (DEPTH, D_MODEL), 0.02),
    }


def reference(x_prompt, x_sample, state_pool, state_gla, state_conv, meta_tokens,
              w_in, w_a2, b_a, w_pool, pool_scale, gla_norm, w_out, ln1_g, ln1_b,
              w_up, w_gate, conv_w, conv_b, w_down, ln2_g, ln2_b):
    B = x_prompt.shape[0]
    meta = jnp.broadcast_to(meta_tokens[None].astype(x_prompt.dtype), (B, N_META, D_MODEL))
    hp = jnp.concatenate([meta, x_prompt], 1)
    hs = x_sample
    pool_empty = jnp.zeros((B, 0, D_POOL), x_prompt.dtype)
    gla_zero = jnp.zeros((B, GLA_HEADS, GLA_DK, GLA_DV), state_gla.dtype)
    conv_zero = jnp.zeros((B, CONV_BUF, D_FF), x_prompt.dtype)
    pp, gp, cp, ps, gs, cs = [], [], [], [], [], []
    for l in range(DEPTH):
        prm = (w_in[l], w_a2[l], b_a[l], w_pool[l], pool_scale[l], gla_norm[l], w_out[l],
               ln1_g[l], ln1_b[l], w_up[l], w_gate[l], conv_w[l], conv_b[l], w_down[l],
               ln2_g[l], ln2_b[l])
        hp, a1, a2, a3 = trunk_layer(hp, pool_empty, gla_zero, conv_zero, 0, True, *prm)
        hs, b1, b2, b3 = trunk_layer(hs, state_pool[l], state_gla[l], state_conv[l], PAST_LEN, False, *prm)
        pp.append(a1); gp.append(a2); cp.append(a3)
        ps.append(b1); gs.append(b2); cs.append(b3)
    y_prompt = hp[:, N_META:]
    return (y_prompt, hs, jnp.stack(pp), jnp.stack(gp), jnp.stack(cp),
            jnp.stack(ps), jnp.stack(gs), jnp.stack(cs))
```

```python
import functools

import jax
import jax.numpy as jnp
from jax import lax
from jax.experimental import pallas as pl
from jax.experimental.pallas import tpu as pltpu

F32 = jnp.float32
BF16 = jnp.bfloat16

D_MODEL = 1024
N_META = 16
D_POOL = 512
POOL_WINDOWS = (2, 4, 8, 16)
POOL_GROUP = 128
POOL_BUF = 15
POOL_PAD = 16
D_GLA = 512
GLA_HEADS = 4
GLA_DV = 128
GLA_DK = 64
D_GLA_K = 256
GATE_RANK = 16
GATE_TAU = 16.0
D_FF = 2816
CONV_BUF = 2
CONV_PAD = 8
DEPTH = 2
ALPHA = (2 * DEPTH) ** 0.25
LN_EPS = 1e-5
RMS_EPS = 1e-6
PAST_LEN = 16384

C_POOL, C_Q, C_K, C_V, C_R, C_Z, C_END = 0, 512, 768, 1024, 1536, 2048, 2064

LONG_TILE = 512
LONG_CHUNK = 64
SHORT_ROWS = 64
FF_CHUNK = 256
VMEM_LIMIT = 56 * 1024 * 1024


def _dot(a, b):
    return jnp.dot(a, b, preferred_element_type=F32)


def _dot_nt(a, b):
    return lax.dot_general(a, b, (((1,), (1,)), ((), ())), preferred_element_type=F32)


def _dot_tn(a, b):
    return lax.dot_general(a, b, (((0,), (0,)), ((), ())), preferred_element_type=F32)


def _layer_norm(y, g, b):
    mu = jnp.mean(y, axis=-1, keepdims=True)
    yc = y - mu
    var = jnp.mean(yc * yc, axis=-1, keepdims=True)
    return yc * lax.rsqrt(var + LN_EPS) * g + b


def _silu(x):
    return x * (1.0 / (1.0 + jnp.exp(-x)))


def _log_sigmoid(z):
    return jnp.minimum(z, 0.0) - jnp.log(1.0 + jnp.exp(-jnp.abs(z)))


def _roll_rows(x, shift):
    n = x.shape[0]
    return pltpu.roll(x, shift % n, 0)


def _split_bf16(x):
    hi = x.astype(BF16)
    lo = (x - hi.astype(F32)).astype(BF16)
    return hi, lo


def _gate_log_decay(zr, w_a2_ref, b_a_ref):
    z = _dot(zr.astype(BF16), w_a2_ref[...]) + b_a_ref[...]
    return _log_sigmoid(z) * (1.0 / GATE_TAU)


def _gla_output_gate(o, r, gnorm):
    parts = []
    for h in range(GLA_HEADS):
        oh = o[:, h * GLA_DV:(h + 1) * GLA_DV]
        ms = jnp.mean(oh * oh, axis=-1, keepdims=True)
        parts.append(oh * lax.rsqrt(ms + RMS_EPS) * gnorm)
    return jnp.concatenate(parts, axis=1) * _silu(r)


def _pool_project(d_groups, w_pool_ref, pscale_ref):
    ys = [_dot(d.astype(BF16), w_pool_ref[g]) for g, d in enumerate(d_groups)]
    return jnp.concatenate(ys, axis=1) * pscale_ref[...]


def _mix_out(x, y_pool, y_gla, w_out_ref, g_ref, b_ref):
    mix = jnp.concatenate([y_pool, y_gla], axis=1).astype(BF16)
    return _layer_norm(ALPHA * x + _dot(mix, w_out_ref[...]), g_ref[...], b_ref[...])


def _mixer_long_kernel(x_ref, pprev_ref, s0_ref, w_in_ref, w_a2_ref, b_a_ref, w_pool_ref,
                       pscale_ref, gnorm_ref, w_out_ref, g_ref, b_ref,
                       x1_ref, pbuf_ref, snew_ref,
                       ubuf, sbd, o_scr, *, tile, chunk):
    T, C = tile, chunk
    t = pl.program_id(1)
    zero_blk = jnp.zeros((GLA_DK, GLA_DV), F32)

    @pl.when(t == 0)
    def _init():
        ubuf[0:POOL_PAD, :] = pprev_ref[0]
        for p in range(2):
            top = jnp.concatenate([s0_ref[0, 2 * p], zero_blk], axis=1)
            bot = jnp.concatenate([zero_blk, s0_ref[0, 2 * p + 1]], axis=1)
            sbd[p] = jnp.concatenate([top, bot], axis=0)

    x = x_ref[0]
    xb = x.astype(BF16)

    u = _dot(xb, w_in_ref[:, C_POOL:C_Q])
    ubuf[POOL_PAD:POOL_PAD + T, :] = u
    d_groups = []
    for g, w in enumerate(POOL_WINDOWS):
        s = ubuf[:, g * POOL_GROUP:(g + 1) * POOL_GROUP]
        sh = 1
        while sh < w:
            s = s + _roll_rows(s, sh)
            sh *= 2
        d_groups.append(s[POOL_PAD:, :] * (1.0 / w) - u[:, g * POOL_GROUP:(g + 1) * POOL_GROUP])
    y_pool = _pool_project(d_groups, w_pool_ref, pscale_ref)
    ubuf[0:POOL_PAD, :] = ubuf[T:T + POOL_PAD, :]

    q = _dot(xb, w_in_ref[:, C_Q:C_K]) * (GLA_DK ** -0.5)
    k = _dot(xb, w_in_ref[:, C_K:C_V])
    v = _dot(xb, w_in_ref[:, C_V:C_R])
    zr = _dot(xb, w_in_ref[:, C_Z:C_END])
    loga = _gate_log_decay(zr, w_a2_ref, b_a_ref)

    rowc = lax.broadcasted_iota(jnp.int32, (T, D_GLA_K), 0) & (C - 1)
    b = loga
    sh = 1
    while sh < C:
        b = b + jnp.where(rowc >= sh, _roll_rows(b, sh), 0.0)
        sh *= 2

    ar = lax.broadcasted_iota(jnp.int32, (C, 2 * C), 0)
    ac = lax.broadcasted_iota(jnp.int32, (C, 2 * C), 1) & (C - 1)
    causal = ac <= ar
    lane128 = lax.broadcasted_iota(jnp.int32, (C, 128), 1)
    lane256 = lax.broadcasted_iota(jnp.int32, (C, 256), 1)
    sr = lax.broadcasted_iota(jnp.int32, (128, 256), 0)
    sc = lax.broadcasted_iota(jnp.int32, (128, 256), 1)
    blockdiag = (sr >= GLA_DK) == (sc >= GLA_DV)
    mid = C // 2 - 1

    for c in range(T // C):
        lo = c * C
        bc = b[lo:lo + C]
        bmid = bc[mid:mid + 1]
        bend = bc[C - 1:C]
        qc = q[lo:lo + C]
        kc = k[lo:lo + C]
        q_in = (qc * jnp.exp(bc - bmid)).astype(BF16)
        k_in = (kc * jnp.exp(bmid - bc)).astype(BF16)
        q_st = (qc * jnp.exp(bc)).astype(BF16)
        k_st = (kc * jnp.exp(bend - bc)).astype(BF16)
        dec_t = jnp.transpose(jnp.broadcast_to(jnp.exp(bend), (128, D_GLA_K)))
        vc = v[lo:lo + C].astype(BF16)
        for p in range(2):
            ks = slice(128 * p, 128 * (p + 1))
            vs = slice(256 * p, 256 * (p + 1))
            k_in_p = k_in[:, ks]
            zk = jnp.zeros_like(k_in_p)
            kexp = jnp.concatenate([jnp.where(lane128 < GLA_DK, k_in_p, zk),
                                    jnp.where(lane128 >= GLA_DK, k_in_p, zk)], axis=0)
            a = _dot_nt(q_in[:, ks], kexp)
            a = jnp.where(causal, a, 0.0).astype(BF16)
            v_p = vc[:, vs]
            zv = jnp.zeros_like(v_p)
            vblk = jnp.concatenate([jnp.where(lane256 < GLA_DV, v_p, zv),
                                    jnp.where(lane256 >= GLA_DV, v_p, zv)], axis=0)
            s_p = sbd[p]
            lhs = jnp.concatenate([a, q_st[:, ks]], axis=1)
            rhs = jnp.concatenate([vblk, s_p.astype(BF16)], axis=0)
            o_scr[lo:lo + C, vs] = _dot(lhs, rhs)
            upd = _dot_tn(k_st[:, ks], v_p)
            dec_p = dec_t[ks, :]
            sbd[p] = jnp.concatenate([dec_p, dec_p], axis=1) * s_p + jnp.where(blockdiag, upd, 0.0)

    r = _dot(xb, w_in_ref[:, C_R:C_Z])
    y_gla = _gla_output_gate(o_scr[...], r, gnorm_ref[...])
    x1_ref[0] = _mix_out(x, y_pool, y_gla, w_out_ref, g_ref, b_ref)

    @pl.when(t == pl.num_programs(1) - 1)
    def _final():
        pbuf_ref[0] = ubuf[0:POOL_PAD, :]
        for p in range(2):
            s_p = sbd[p]
            snew_ref[0, 2 * p] = s_p[0:GLA_DK, 0:GLA_DV]
            snew_ref[0, 2 * p + 1] = s_p[GLA_DK:2 * GLA_DK, GLA_DV:2 * GLA_DV]


def _weight_spec(shape):
    nd = len(shape)
    return pl.BlockSpec(shape, lambda *_: (0,) * nd, pipeline_mode=pl.Buffered(1))


def _mixer_weight_specs():
    return [_weight_spec((D_MODEL, C_END)), _weight_spec((GATE_RANK, D_GLA_K)), _weight_spec((1, D_GLA_K)),
            _weight_spec((4, POOL_GROUP, POOL_GROUP)), _weight_spec((1, D_POOL)), _weight_spec((1, GLA_DV)),
            _weight_spec((D_MODEL, D_MODEL)), _weight_spec((1, D_MODEL)), _weight_spec((1, D_MODEL))]


def _mixer_long(x, pprev, s0, wts):
    B, L, _ = x.shape
    T = min(LONG_TILE, L)
    assert L % T == 0 and T % LONG_CHUNK == 0
    kern = functools.partial(_mixer_long_kernel, tile=T, chunk=LONG_CHUNK)
    return pl.pallas_call(
        kern,
        grid=(B, L // T),
        in_specs=[pl.BlockSpec((1, T, D_MODEL), lambda b, t: (b, t, 0)),
                  pl.BlockSpec((1, POOL_PAD, D_POOL), lambda b, t: (0, 0, 0)),
                  pl.BlockSpec((1, GLA_HEADS, GLA_DK, GLA_DV), lambda b, t: (0, 0, 0, 0))] + _mixer_weight_specs(),
        out_specs=[pl.BlockSpec((1, T, D_MODEL), lambda b, t: (b, t, 0)),
                   pl.BlockSpec((1, POOL_PAD, D_POOL), lambda b, t: (b, 0, 0)),
                   pl.BlockSpec((1, GLA_HEADS, GLA_DK, GLA_DV), lambda b, t: (b, 0, 0, 0))],
        out_shape=[jax.ShapeDtypeStruct((B, L, D_MODEL), F32),
                   jax.ShapeDtypeStruct((B, POOL_PAD, D_POOL), F32),
                   jax.ShapeDtypeStruct((B, GLA_HEADS, GLA_DK, GLA_DV), F32)],
        scratch_shapes=[pltpu.VMEM((T + POOL_PAD, D_POOL), F32),
                        pltpu.VMEM((2, 2 * GLA_DK, 2 * GLA_DV), F32),
                        pltpu.VMEM((T, D_GLA), F32)],
        compiler_params=pltpu.CompilerParams(dimension_semantics=("arbitrary", "arbitrary"),
                                             vmem_limit_bytes=VMEM_LIMIT),
        name="mixer_long",
    )(x, pprev, s0, *wts)


def _mixer_short_kernel(x_ref, pprev_ref, s0_ref, w_in_ref, w_a2_ref, b_a_ref, w_pool_ref,
                        pscale_ref, gnorm_ref, w_out_ref, g_ref, b_ref,
                        x1_ref, u_ref, snew_ref, *, n_seq, seq_len, pos0):
    G, Ls = n_seq, seq_len
    R = G * Ls
    ls_shift = Ls.bit_length() - 1
    NP = G * POOL_PAD
    NS = G * GLA_DK
    x = x_ref[...]
    xb = x.astype(BF16)

    u = _dot(xb, w_in_ref[:, C_POOL:C_Q])
    u_ref[...] = u
    pr = lax.broadcasted_iota(jnp.int32, (R, NP + R), 0)
    pc = lax.broadcasted_iota(jnp.int32, (R, NP + R), 1)
    p_seq, p_tok = pr >> ls_shift, pr & (Ls - 1)
    hist_seq, hist_row = pc >> 4, pc & (POOL_PAD - 1)
    cc = pc - NP
    new_seq, new_tok = cc >> ls_shift, cc & (Ls - 1)
    tok1 = lax.broadcasted_iota(jnp.int32, (R, 1), 0) & (Ls - 1)
    d_groups = []
    for g, w in enumerate(POOL_WINDOWS):
        in_hist = (pc < NP) & (hist_seq == p_seq) & (hist_row >= POOL_PAD - (w - 1 - p_tok))
        in_new = (pc >= NP) & (new_seq == p_seq) & (new_tok <= p_tok) & (new_tok > p_tok - w)
        sel = jnp.where(in_hist | in_new, 1.0, 0.0).astype(BF16)
        ug = u[:, g * POOL_GROUP:(g + 1) * POOL_GROUP]
        z = jnp.concatenate([pprev_ref[:, g * POOL_GROUP:(g + 1) * POOL_GROUP], ug], axis=0)
        z_hi, z_lo = _split_bf16(z)
        wsum = _dot(sel, z_hi) + _dot(sel, z_lo)
        if pos0 >= POOL_BUF:
            mean = wsum * (1.0 / w)
        else:
            mean = wsum / jnp.minimum(w, pos0 + tok1 + 1).astype(F32)
        d_groups.append(mean - ug)
    y_pool = _pool_project(d_groups, w_pool_ref, pscale_ref)

    q = _dot(xb, w_in_ref[:, C_Q:C_K]) * (GLA_DK ** -0.5)
    k = _dot(xb, w_in_ref[:, C_K:C_V])
    v = _dot(xb, w_in_ref[:, C_V:C_R]).astype(BF16)
    zr = _dot(xb, w_in_ref[:, C_Z:C_END])
    loga = _gate_log_decay(zr, w_a2_ref, b_a_ref)

    tok = lax.broadcasted_iota(jnp.int32, (R, D_GLA_K), 0) & (Ls - 1)
    b = loga
    sh = 1
    while sh < Ls:
        b = b + jnp.where(tok >= sh, _roll_rows(b, sh), 0.0)
        sh *= 2
    bend = jnp.where(tok == Ls - 1, b, 0.0)
    sh = 1
    while sh < Ls:
        bend = bend + jnp.where(tok + sh <= Ls - 1, _roll_rows(bend, -sh), 0.0)
        sh *= 2

    q_in = (q * jnp.exp(b)).astype(BF16)
    k_in = (k * jnp.exp(-b)).astype(BF16)
    k_st = k * jnp.exp(bend - b)
    dec_hi, dec_lo = _split_bf16(jnp.exp(bend))
    dec_rows = jnp.where(tok == Ls - 1, dec_hi, jnp.where(tok == Ls - 2, dec_lo, jnp.zeros_like(dec_lo)))

    ar = lax.broadcasted_iota(jnp.int32, (R, 2 * R), 0)
    ac = lax.broadcasted_iota(jnp.int32, (R, 2 * R), 1) & (R - 1)
    same_seq_causal = ((ac >> ls_shift) == (ar >> ls_shift)) & ((ac & (Ls - 1)) <= (ar & (Ls - 1)))
    lane128 = lax.broadcasted_iota(jnp.int32, (R, 128), 1)
    lane256 = lax.broadcasted_iota(jnp.int32, (R, 256), 1)
    br = lax.broadcasted_iota(jnp.int32, (R, NS), 0)
    bcol = lax.broadcasted_iota(jnp.int32, (R, NS), 1)
    own_state = (bcol >> 6) == (br >> ls_shift)
    ones_blk = jnp.ones((R, GLA_DV), BF16)
    zeros_blk = jnp.zeros((R, GLA_DV), BF16)

    def expand(xp, first):
        sw = pltpu.roll(xp, GLA_DK, 1)
        two = jnp.where(lane128 < GLA_DK, xp, sw) if first else jnp.where(lane128 < GLA_DK, sw, xp)
        rep = jnp.concatenate([two] * (NS // 128), axis=1)
        return jnp.where(own_state, rep, 0.0).astype(BF16)

    o_parts = []
    for p in range(2):
        ks = slice(128 * p, 128 * (p + 1))
        vs = slice(256 * p, 256 * (p + 1))
        k_in_p = k_in[:, ks]
        zk = jnp.zeros_like(k_in_p)
        kexp = jnp.concatenate([jnp.where(lane128 < GLA_DK, k_in_p, zk),
                                jnp.where(lane128 >= GLA_DK, k_in_p, zk)], axis=0)
        a = _dot_nt(q_in[:, ks], kexp)
        a = jnp.where(same_seq_causal, a, 0.0).astype(BF16)
        v_p = v[:, vs]
        zv = jnp.zeros_like(v_p)
        vblk = jnp.concatenate([jnp.where(lane256 < GLA_DV, v_p, zv),
                                jnp.where(lane256 >= GLA_DV, v_p, zv)], axis=0)
        o_intra = _dot(a, vblk)
        q_f = q_in[:, ks].astype(F32)
        inter = []
        for j in range(2):
            h = 2 * p + j
            s_flat = s0_ref[:, h].reshape(NS, GLA_DV)
            inter.append(_dot(expand(q_f, j == 0), s_flat.astype(BF16)))
            lhs = jnp.concatenate([expand(k_st[:, ks], j == 0),
                                   expand(dec_rows[:, ks].astype(F32), j == 0)], axis=0)
            v_h = v[:, h * GLA_DV:(h + 1) * GLA_DV]
            rhs = jnp.concatenate([jnp.concatenate([v_h, zeros_blk], axis=1),
                                   jnp.concatenate([zeros_blk, ones_blk], axis=1)], axis=0)
            ud = _dot_tn(lhs, rhs)
            s_new = ud[:, GLA_DV:] * s_flat + ud[:, :GLA_DV]
            snew_ref[:, h] = s_new.reshape(G, GLA_DK, GLA_DV)
        o_parts.append(o_intra + jnp.concatenate(inter, axis=1))
    o = jnp.concatenate(o_parts, axis=1)

    r = _dot(xb, w_in_ref[:, C_R:C_Z])
    y_gla = _gla_output_gate(o, r, gnorm_ref[...])
    x1_ref[...] = _mix_out(x, y_pool, y_gla, w_out_ref, g_ref, b_ref)


def _mixer_short(x, pprev, s0, wts, *, seq_len, pos0):
    rows = x.shape[0]
    n_total = rows // seq_len
    G = SHORT_ROWS // seq_len
    assert n_total % G == 0
    kern = functools.partial(_mixer_short_kernel, n_seq=G, seq_len=seq_len, pos0=pos0)
    return pl.pallas_call(
        kern,
        grid=(n_total // G,),
        in_specs=[pl.BlockSpec((SHORT_ROWS, D_MODEL), lambda i: (i, 0)),
                  pl.BlockSpec((G * POOL_PAD, D_POOL), lambda i: (i, 0)),
                  pl.BlockSpec((G, GLA_HEADS, GLA_DK, GLA_DV), lambda i: (i, 0, 0, 0))] + _mixer_weight_specs(),
        out_specs=[pl.BlockSpec((SHORT_ROWS, D_MODEL), lambda i: (i, 0)),
                   pl.BlockSpec((SHORT_ROWS, D_POOL), lambda i: (i, 0)),
                   pl.BlockSpec((G, GLA_HEADS, GLA_DK, GLA_DV), lambda i: (i, 0, 0, 0))],
        out_shape=[jax.ShapeDtypeStruct((rows, D_MODEL), F32),
                   jax.ShapeDtypeStruct((rows, D_POOL), F32),
                   jax.ShapeDtypeStruct((n_total, GLA_HEADS, GLA_DK, GLA_DV), F32)],
        compiler_params=pltpu.CompilerParams(dimension_semantics=("arbitrary",),
                                             vmem_limit_bytes=VMEM_LIMIT),
        name=f"mixer_short_len{seq_len}",
    )(x, pprev, s0, *wts)


def _ffn_kernel(x_ref, cprev_ref, w_up_ref, w_gate_ref, cw_ref, cb_ref, w_down_ref, g_ref, b_ref,
                y_ref, gt_ref, gbuf, act, *, rows, seq_len):
    T = rows
    x = x_ref[...].reshape(T, D_MODEL)
    xb = x.astype(BF16)
    if seq_len is None:
        t = pl.program_id(1)

        @pl.when(t == 0)
        def _init():
            gbuf[0:CONV_PAD, :] = cprev_ref[0]
    else:
        tok = lax.broadcasted_iota(jnp.int32, (T, FF_CHUNK), 0) & (seq_len - 1)

    for j in range(D_FF // FF_CHUNK):
        cs = slice(j * FF_CHUNK, (j + 1) * FF_CHUNK)
        a = _dot(xb, w_up_ref[:, cs])
        gt = _dot(xb, w_gate_ref[:, cs])
        if seq_len is None:
            gbuf[CONV_PAD:CONV_PAD + T, cs] = gt
            g1 = gbuf[CONV_PAD - 1:CONV_PAD - 1 + T, cs]
            g2 = gbuf[CONV_PAD - 2:CONV_PAD - 2 + T, cs]
        else:
            gt_ref[:, cs] = gt
            hist = cprev_ref[:, cs]
            g1 = jnp.where(tok >= 1, _roll_rows(gt, 1), _roll_rows(hist, -1))
            g2 = jnp.where(tok >= 2, _roll_rows(gt, 2), hist)
        gc = cb_ref[:, cs] + cw_ref[0:1, cs] * g2 + cw_ref[1:2, cs] * g1 + cw_ref[2:3, cs] * gt
        act[:, cs] = (a * _silu(gc)).astype(BF16)

    f = _dot(act[...], w_down_ref[...])
    y = _layer_norm(ALPHA * x + f, g_ref[...], b_ref[...])
    y_ref[...] = y.reshape(y_ref.shape)

    if seq_len is None:
        tail = gbuf[T:T + CONV_PAD, :]
        gbuf[0:CONV_PAD, :] = tail

        @pl.when(t == pl.num_programs(1) - 1)
        def _final():
            gt_ref[0] = tail


def _ffn_weight_specs():
    return [_weight_spec((D_MODEL, D_FF)), _weight_spec((D_MODEL, D_FF)), _weight_spec((3, D_FF)),
            _weight_spec((1, D_FF)), _weight_spec((D_FF, D_MODEL)), _weight_spec((1, D_MODEL)),
            _weight_spec((1, D_MODEL))]


def _ffn_long(x, cprev, wts):
    B, L, _ = x.shape
    T = min(LONG_TILE, L)
    kern = functools.partial(_ffn_kernel, rows=T, seq_len=None)
    return pl.pallas_call(
        kern,
        grid=(B, L // T),
        in_specs=[pl.BlockSpec((1, T, D_MODEL), lambda b, t: (b, t, 0)),
                  pl.BlockSpec((1, CONV_PAD, D_FF), lambda b, t: (0, 0, 0))] + _ffn_weight_specs(),
        out_specs=[pl.BlockSpec((1, T, D_MODEL), lambda b, t: (b, t, 0)),
                   pl.BlockSpec((1, CONV_PAD, D_FF), lambda b, t: (b, 0, 0))],
        out_shape=[jax.ShapeDtypeStruct((B, L, D_MODEL), F32),
                   jax.ShapeDtypeStruct((B, CONV_PAD, D_FF), F32)],
        scratch_shapes=[pltpu.VMEM((T + CONV_PAD, D_FF), F32),
                        pltpu.VMEM((T, D_FF), BF16)],
        compiler_params=pltpu.CompilerParams(dimension_semantics=("arbitrary", "arbitrary"),
                                             vmem_limit_bytes=VMEM_LIMIT),
        name="ffn_long",
    )(x, cprev, *wts)


def _ffn_short(x, cprev, wts, *, seq_len):
    rows = x.shape[0]
    kern = functools.partial(_ffn_kernel, rows=rows, seq_len=seq_len)
    return pl.pallas_call(
        kern,
        grid=(1,),
        in_specs=[pl.BlockSpec((rows, D_MODEL), lambda i: (0, 0)),
                  pl.BlockSpec((rows, D_FF), lambda i: (0, 0))] + _ffn_weight_specs(),
        out_specs=[pl.BlockSpec((rows, D_MODEL), lambda i: (0, 0)),
                   pl.BlockSpec((rows, D_FF), lambda i: (0, 0))],
        out_shape=[jax.ShapeDtypeStruct((rows, D_MODEL), F32),
                   jax.ShapeDtypeStruct((rows, D_FF), F32)],
        scratch_shapes=[pltpu.VMEM((8, 128), F32),
                        pltpu.VMEM((rows, D_FF), BF16)],
        compiler_params=pltpu.CompilerParams(dimension_semantics=("arbitrary",),
                                             vmem_limit_bytes=VMEM_LIMIT),
        name=f"ffn_short_len{seq_len}",
    )(x, cprev, *wts)


def _conv_history_rows(state, seq_len):
    n = state.shape[0]
    return jnp.pad(state, ((0, 0), (0, seq_len - CONV_BUF), (0, 0))).reshape(n * seq_len, D_FF)


def kernel(x_prompt, x_sample, state_pool, state_gla, state_conv, meta_tokens,
           w_in, w_a2, b_a, w_pool, pool_scale, gla_norm, w_out, ln1_g, ln1_b,
           w_up, w_gate, conv_w, conv_b, w_down, ln2_g, ln2_b):
    B = x_prompt.shape[0]
    NB, LS = x_sample.shape[0], x_sample.shape[1]
    n_meta_seq = SHORT_ROWS // N_META

    hm = jnp.pad(meta_tokens.astype(F32), ((0, SHORT_ROWS - N_META), (0, 0)))
    hp = x_prompt
    hs = x_sample.reshape(NB * LS, D_MODEL)
    meta_pool = jnp.zeros((n_meta_seq * POOL_PAD, D_POOL), F32)
    meta_gla = jnp.zeros((n_meta_seq, GLA_HEADS, GLA_DK, GLA_DV), F32)
    meta_conv = jnp.zeros((SHORT_ROWS, D_FF), F32)

    pp, gp, cp, ps, gs, cs = [], [], [], [], [], []
    for l in range(DEPTH):
        mix_w = (w_in[l].astype(BF16), w_a2[l].astype(BF16), b_a[l][None], w_pool[l].astype(BF16),
                 pool_scale[l][None], gla_norm[l][None], w_out[l].astype(BF16), ln1_g[l][None], ln1_b[l][None])
        ffn_w = (w_up[l].astype(BF16), w_gate[l].astype(BF16), conv_w[l], conv_b[l][None],
                 w_down[l].astype(BF16), ln2_g[l][None], ln2_b[l][None])

        hm1, um, sm = _mixer_short(hm, meta_pool, meta_gla, mix_w, seq_len=N_META, pos0=0)
        hm, gtm = _ffn_short(hm1, meta_conv, ffn_w, seq_len=N_META)

        cprev = jnp.pad(gtm[N_META - CONV_BUF:N_META], ((CONV_PAD - CONV_BUF, 0), (0, 0)))[None]
        hp1, pbuf, snew = _mixer_long(hp, um[None, 0:N_META], sm[0:1], mix_w)
        hp, cbuf = _ffn_long(hp1, cprev, ffn_w)
        pp.append(pbuf[:, 1:])
        gp.append(snew)
        cp.append(cbuf[:, CONV_PAD - CONV_BUF:])

        pool_hist = jnp.pad(state_pool[l], ((0, 0), (1, 0), (0, 0))).reshape(NB * POOL_PAD, D_POOL)
        hs1, us, ss = _mixer_short(hs, pool_hist, state_gla[l], mix_w, seq_len=LS, pos0=PAST_LEN)
        hs, gts = _ffn_short(hs1, _conv_history_rows(state_conv[l], LS), ffn_w, seq_len=LS)
        ps.append(jnp.concatenate([state_pool[l][:, LS:], us.reshape(NB, LS, D_POOL)], axis=1))
        gs.append(ss)
        cs.append(gts.reshape(NB, LS, D_FF)[:, LS - CONV_BUF:])

    return (hp, hs.reshape(NB, LS, D_MODEL), jnp.stack(pp), jnp.stack(gp), jnp.stack(cp),
            jnp.stack(ps), jnp.stack(gs), jnp.stack(cs))
```

```python
import functools
import itertools

import jax
import jax.numpy as jnp
from jax import lax
from jax.experimental import pallas as pl
from jax.experimental.pallas import tpu as pltpu

F32 = jnp.float32
BF16 = jnp.bfloat16

D_MODEL = 1024
N_META = 16
D_POOL = 512
POOL_WINDOWS = (2, 4, 8, 16)
POOL_GROUP = 128
POOL_BUF = 15
POOL_PAD = 16
D_GLA = 512
GLA_HEADS = 4
GLA_DV = 128
GLA_DK = 64
D_GLA_K = 256
GATE_RANK = 16
GATE_TAU = 16.0
D_FF = 2816
CONV_BUF = 2
CONV_PAD = 8
DEPTH = 2
ALPHA = (2 * DEPTH) ** 0.25
LN_EPS = 1e-5
RMS_EPS = 1e-6
PAST_LEN = 16384

C_POOL, C_Q, C_K, C_V, C_R, C_Z, C_END = 0, 512, 768, 1024, 1536, 2048, 2064

LONG_TILE = 512
LONG_CHUNK = 64
LONG_PAR = 2
FFN_TILE = 256
SHORT_ROWS = 64
FF_CHUNK = 256
VMEM_LIMIT = 56 * 1024 * 1024


def _dot(a, b):
    return jnp.dot(a, b, preferred_element_type=F32)


def _dot_nt(a, b):
    return lax.dot_general(a, b, (((1,), (1,)), ((), ())), preferred_element_type=F32)


def _dot_tn(a, b):
    return lax.dot_general(a, b, (((0,), (0,)), ((), ())), preferred_element_type=F32)


def _layer_norm(y, g, b):
    mu = jnp.mean(y, axis=-1, keepdims=True)
    yc = y - mu
    var = jnp.mean(yc * yc, axis=-1, keepdims=True)
    return yc * lax.rsqrt(var + LN_EPS) * g + b


def _silu(x):
    return x * (1.0 / (1.0 + jnp.exp(-x)))


def _log_sigmoid(z):
    return jnp.minimum(z, 0.0) - jnp.log(1.0 + jnp.exp(-jnp.abs(z)))


def _roll_rows(x, shift):
    n = x.shape[0]
    return pltpu.roll(x, shift % n, 0)


def _split_bf16(x):
    hi = x.astype(BF16)
    lo = (x - hi.astype(F32)).astype(BF16)
    return hi, lo


def _gate_log_decay(zr, w_a2_ref, b_a_ref):
    z = _dot(zr.astype(BF16), w_a2_ref[...]) + b_a_ref[...]
    return _log_sigmoid(z) * (1.0 / GATE_TAU)


def _gla_output_gate(o, r, gnorm):
    parts = []
    for h in range(GLA_HEADS):
        oh = o[:, h * GLA_DV:(h + 1) * GLA_DV]
        ms = jnp.mean(oh * oh, axis=-1, keepdims=True)
        parts.append(oh * lax.rsqrt(ms + RMS_EPS) * gnorm)
    return jnp.concatenate(parts, axis=1) * _silu(r)


def _pool_project(d_groups, w_pool_ref, pscale_ref):
    ys = [_dot(d.astype(BF16), w_pool_ref[g]) for g, d in enumerate(d_groups)]
    return jnp.concatenate(ys, axis=1) * pscale_ref[...]


def _mix_out(x, y_pool, y_gla, w_out_ref, g_ref, b_ref):
    mix = jnp.concatenate([y_pool, y_gla], axis=1).astype(BF16)
    return _layer_norm(ALPHA * x + _dot(mix, w_out_ref[...]), g_ref[...], b_ref[...])


def _mixer_long_kernel(x_ref, pprev_ref, s0_ref, w_in_ref, w_a2_ref, b_a_ref, w_pool_ref,
                       pscale_ref, gnorm_ref, w_out_ref, g_ref, b_ref,
                       x1_ref, pbuf_ref, snew_ref,
                       ubuf, sbd, *, n_par, tile, chunk):
    t = pl.program_id(1)
    zero_blk = jnp.zeros((GLA_DK, GLA_DV), F32)

    @pl.when(t == 0)
    def _init():
        for j in range(n_par):
            ubuf[j, 0:POOL_PAD, :] = pprev_ref[0]
            for p in range(2):
                top = jnp.concatenate([s0_ref[0, 2 * p], zero_blk], axis=1)
                bot = jnp.concatenate([zero_blk, s0_ref[0, 2 * p + 1]], axis=1)
                sbd[j, p] = jnp.concatenate([top, bot], axis=0)

    tiles = [_mixer_long_tile(x_ref.at[j], w_in_ref, w_a2_ref, b_a_ref, w_pool_ref,
                              pscale_ref, gnorm_ref, w_out_ref, g_ref, b_ref,
                              x1_ref.at[j], ubuf.at[j], sbd.at[j], tile=tile, chunk=chunk)
             for j in range(n_par)]
    for _ in itertools.zip_longest(*tiles):
        pass

    @pl.when(t == pl.num_programs(1) - 1)
    def _final():
        for j in range(n_par):
            pbuf_ref[j] = ubuf[j, 0:POOL_PAD, :]
            for p in range(2):
                s_p = sbd[j, p]
                snew_ref[j, 2 * p] = s_p[0:GLA_DK, 0:GLA_DV]
                snew_ref[j, 2 * p + 1] = s_p[GLA_DK:2 * GLA_DK, GLA_DV:2 * GLA_DV]


def _mixer_long_tile(x_ref, w_in_ref, w_a2_ref, b_a_ref, w_pool_ref,
                     pscale_ref, gnorm_ref, w_out_ref, g_ref, b_ref,
                     x1_ref, ubuf, sbd, *, tile, chunk):
    T, C = tile, chunk
    x = x_ref[...]
    xb = x.astype(BF16)

    u = _dot(xb, w_in_ref[:, C_POOL:C_Q])
    zr = _dot(xb, w_in_ref[:, C_Z:C_END])
    q = _dot(xb, w_in_ref[:, C_Q:C_K]) * (GLA_DK ** -0.5)
    k = _dot(xb, w_in_ref[:, C_K:C_V])
    v = _dot(xb, w_in_ref[:, C_V:C_R])
    r = _dot(xb, w_in_ref[:, C_R:C_Z])
    yield

    ubuf[POOL_PAD:POOL_PAD + T, :] = u
    d_groups = []
    for g, w in enumerate(POOL_WINDOWS):
        s = ubuf[:, g * POOL_GROUP:(g + 1) * POOL_GROUP]
        sh = 1
        while sh < w:
            s = s + _roll_rows(s, sh)
            sh *= 2
        d_groups.append(s[POOL_PAD:, :] * (1.0 / w) - u[:, g * POOL_GROUP:(g + 1) * POOL_GROUP])
    y_pool = _pool_project(d_groups, w_pool_ref, pscale_ref)
    ubuf[0:POOL_PAD, :] = ubuf[T:T + POOL_PAD, :]
    loga = _gate_log_decay(zr, w_a2_ref, b_a_ref)

    tr = lax.broadcasted_iota(jnp.int32, (C, C), 0)
    tc = lax.broadcasted_iota(jnp.int32, (C, C), 1)
    tri = jnp.where(tc <= tr, 1.0, 0.0).astype(BF16)
    ar = lax.broadcasted_iota(jnp.int32, (C, 2 * C), 0)
    ac = lax.broadcasted_iota(jnp.int32, (C, 2 * C), 1) & (C - 1)
    causal = ac <= ar
    lane128 = lax.broadcasted_iota(jnp.int32, (C, 128), 1)
    lane256 = lax.broadcasted_iota(jnp.int32, (C, 256), 1)
    sr = lax.broadcasted_iota(jnp.int32, (128, 256), 0)
    sc = lax.broadcasted_iota(jnp.int32, (128, 256), 1)
    blockdiag = (sr >= GLA_DK) == (sc >= GLA_DV)
    mid = C // 2 - 1

    n_chunks = T // C
    pairs = [(c, p) for c in range(n_chunks) for p in range(2)]
    ks = [slice(128 * p, 128 * (p + 1)) for p in range(2)]
    vs = [slice(256 * p, 256 * (p + 1)) for p in range(2)]

    bcs = []
    for c in range(n_chunks):
        la_hi, la_lo = _split_bf16(loga[c * C:(c + 1) * C])
        bb = _dot(tri, jnp.concatenate([la_hi, la_lo], axis=1))
        bcs.append(bb[:, :D_GLA_K] + bb[:, D_GLA_K:])
    yield

    q_in, k_in, q_st, k_st, dec_t, vc = [], [], [], [], [], []
    for c in range(n_chunks):
        bc = bcs[c]
        bmid = bc[mid:mid + 1]
        bend = bc[C - 1:C]
        qc = q[c * C:(c + 1) * C]
        kc = k[c * C:(c + 1) * C]
        q_in.append((qc * jnp.exp(bc - bmid)).astype(BF16))
        k_in.append((kc * jnp.exp(bmid - bc)).astype(BF16))
        q_st.append((qc * jnp.exp(bc)).astype(BF16))
        k_st.append((kc * jnp.exp(bend - bc)).astype(BF16))
        dec_t.append(jnp.transpose(jnp.broadcast_to(jnp.exp(bend), (128, D_GLA_K))))
        vc.append(v[c * C:(c + 1) * C].astype(BF16))
    yield

    attn, upd = {}, {}
    for c, p in pairs:
        k_in_p = k_in[c][:, ks[p]]
        zk = jnp.zeros_like(k_in_p)
        kexp = jnp.concatenate([jnp.where(lane128 < GLA_DK, k_in_p, zk),
                                jnp.where(lane128 >= GLA_DK, k_in_p, zk)], axis=0)
        a = _dot_nt(q_in[c][:, ks[p]], kexp)
        attn[c, p] = jnp.where(causal, a, 0.0).astype(BF16)
    for c, p in pairs:
        u_cp = _dot_tn(k_st[c][:, ks[p]], vc[c][:, vs[p]])
        upd[c, p] = jnp.where(blockdiag, u_cp, 0.0)
    yield

    s_vals = [sbd[p] for p in range(2)]
    s_start = {}
    for c, p in pairs:
        s_start[c, p] = s_vals[p].astype(BF16)
        dec_p = dec_t[c][ks[p], :]
        s_vals[p] = jnp.concatenate([dec_p, dec_p], axis=1) * s_vals[p] + upd[c, p]
    for p in range(2):
        sbd[p] = s_vals[p]

    o_rows = [[], []]
    for c, p in pairs:
        v_p = vc[c][:, vs[p]]
        zv = jnp.zeros_like(v_p)
        vblk = jnp.concatenate([jnp.where(lane256 < GLA_DV, v_p, zv),
                                jnp.where(lane256 >= GLA_DV, v_p, zv)], axis=0)
        lhs = jnp.concatenate([attn[c, p], q_st[c][:, ks[p]]], axis=1)
        rhs = jnp.concatenate([vblk, s_start[c, p]], axis=0)
        o_rows[p].append(_dot(lhs, rhs))
    o = jnp.concatenate([jnp.concatenate(o_rows[p], axis=0) for p in range(2)], axis=1)
    yield

    y_gla = _gla_output_gate(o, r, gnorm_ref[...])
    x1_ref[...] = _mix_out(x, y_pool, y_gla, w_out_ref, g_ref, b_ref)


def _weight_spec(shape):
    nd = len(shape)
    return pl.BlockSpec(shape, lambda *_: (0,) * nd, pipeline_mode=pl.Buffered(1))


def _mixer_weight_specs():
    return [_weight_spec((D_MODEL, C_END)), _weight_spec((GATE_RANK, D_GLA_K)), _weight_spec((1, D_GLA_K)),
            _weight_spec((4, POOL_GROUP, POOL_GROUP)), _weight_spec((1, D_POOL)), _weight_spec((1, GLA_DV)),
            _weight_spec((D_MODEL, D_MODEL)), _weight_spec((1, D_MODEL)), _weight_spec((1, D_MODEL))]


def _mixer_long(x, pprev, s0, wts):
    B, L, _ = x.shape
    T = min(LONG_TILE, L)
    P = LONG_PAR if B % LONG_PAR == 0 else 1
    assert L % T == 0 and T % LONG_CHUNK == 0
    kern = functools.partial(_mixer_long_kernel, n_par=P, tile=T, chunk=LONG_CHUNK)
    return pl.pallas_call(
        kern,
        grid=(B // P, L // T),
        in_specs=[pl.BlockSpec((P, T, D_MODEL), lambda b, t: (b, t, 0)),
                  pl.BlockSpec((1, POOL_PAD, D_POOL), lambda b, t: (0, 0, 0)),
                  pl.BlockSpec((1, GLA_HEADS, GLA_DK, GLA_DV), lambda b, t: (0, 0, 0, 0))] + _mixer_weight_specs(),
        out_specs=[pl.BlockSpec((P, T, D_MODEL), lambda b, t: (b, t, 0)),
                   pl.BlockSpec((P, POOL_PAD, D_POOL), lambda b, t: (b, 0, 0)),
                   pl.BlockSpec((P, GLA_HEADS, GLA_DK, GLA_DV), lambda b, t: (b, 0, 0, 0))],
        out_shape=[jax.ShapeDtypeStruct((B, L, D_MODEL), F32),
                   jax.ShapeDtypeStruct((B, POOL_PAD, D_POOL), F32),
                   jax.ShapeDtypeStruct((B, GLA_HEADS, GLA_DK, GLA_DV), F32)],
        scratch_shapes=[pltpu.VMEM((P, T + POOL_PAD, D_POOL), F32),
                        pltpu.VMEM((P, 2, 2 * GLA_DK, 2 * GLA_DV), F32)],
        compiler_params=pltpu.CompilerParams(dimension_semantics=("arbitrary", "arbitrary"),
                                             vmem_limit_bytes=VMEM_LIMIT),
        name="mixer_long",
    )(x, pprev, s0, *wts)


def _mixer_short_kernel(x_ref, pprev_ref, s0_ref, w_in_ref, w_a2_ref, b_a_ref, w_pool_ref,
                        pscale_ref, gnorm_ref, w_out_ref, g_ref, b_ref,
                        x1_ref, u_ref, snew_ref, *, n_seq, seq_len, pos0):
    G, Ls = n_seq, seq_len
    R = G * Ls
    ls_shift = Ls.bit_length() - 1
    NP = G * POOL_PAD
    NS = G * GLA_DK
    x = x_ref[...]
    xb = x.astype(BF16)

    u = _dot(xb, w_in_ref[:, C_POOL:C_Q])
    u_ref[...] = u
    pr = lax.broadcasted_iota(jnp.int32, (R, NP + R), 0)
    pc = lax.broadcasted_iota(jnp.int32, (R, NP + R), 1)
    p_seq, p_tok = pr >> ls_shift, pr & (Ls - 1)
    hist_seq, hist_row = pc >> 4, pc & (POOL_PAD - 1)
    cc = pc - NP
    new_seq, new_tok = cc >> ls_shift, cc & (Ls - 1)
    tok1 = lax.broadcasted_iota(jnp.int32, (R, 1), 0) & (Ls - 1)
    d_groups = []
    for g, w in enumerate(POOL_WINDOWS):
        in_hist = (pc < NP) & (hist_seq == p_seq) & (hist_row >= POOL_PAD - (w - 1 - p_tok))
        in_new = (pc >= NP) & (new_seq == p_seq) & (new_tok <= p_tok) & (new_tok > p_tok - w)
        sel = jnp.where(in_hist | in_new, 1.0, 0.0).astype(BF16)
        ug = u[:, g * POOL_GROUP:(g + 1) * POOL_GROUP]
        z = jnp.concatenate([pprev_ref[:, g * POOL_GROUP:(g + 1) * POOL_GROUP], ug], axis=0)
        z_hi, z_lo = _split_bf16(z)
        wsum = _dot(sel, z_hi) + _dot(sel, z_lo)
        if pos0 >= POOL_BUF:
            mean = wsum * (1.0 / w)
        else:
            mean = wsum / jnp.minimum(w, pos0 + tok1 + 1).astype(F32)
        d_groups.append(mean - ug)
    y_pool = _pool_project(d_groups, w_pool_ref, pscale_ref)

    q = _dot(xb, w_in_ref[:, C_Q:C_K]) * (GLA_DK ** -0.5)
    k = _dot(xb, w_in_ref[:, C_K:C_V])
    v = _dot(xb, w_in_ref[:, C_V:C_R]).astype(BF16)
    zr = _dot(xb, w_in_ref[:, C_Z:C_END])
    loga = _gate_log_decay(zr, w_a2_ref, b_a_ref)

    tok = lax.broadcasted_iota(jnp.int32, (R, D_GLA_K), 0) & (Ls - 1)
    b = loga
    sh = 1
    while sh < Ls:
        b = b + jnp.where(tok >= sh, _roll_rows(b, sh), 0.0)
        sh *= 2
    bend = jnp.where(tok == Ls - 1, b, 0.0)
    sh = 1
    while sh < Ls:
        bend = bend + jnp.where(tok + sh <= Ls - 1, _roll_rows(bend, -sh), 0.0)
        sh *= 2

    q_in = (q * jnp.exp(b)).astype(BF16)
    k_in = (k * jnp.exp(-b)).astype(BF16)
    k_st = k * jnp.exp(bend - b)
    dec_hi, dec_lo = _split_bf16(jnp.exp(bend))
    dec_rows = jnp.where(tok == Ls - 1, dec_hi, jnp.where(tok == Ls - 2, dec_lo, jnp.zeros_like(dec_lo)))

    ar = lax.broadcasted_iota(jnp.int32, (R, 2 * R), 0)
    ac = lax.broadcasted_iota(jnp.int32, (R, 2 * R), 1) & (R - 1)
    same_seq_causal = ((ac >> ls_shift) == (ar >> ls_shift)) & ((ac & (Ls - 1)) <= (ar & (Ls - 1)))
    lane128 = lax.broadcasted_iota(jnp.int32, (R, 128), 1)
    lane256 = lax.broadcasted_iota(jnp.int32, (R, 256), 1)
    br = lax.broadcasted_iota(jnp.int32, (R, NS), 0)
    bcol = lax.broadcasted_iota(jnp.int32, (R, NS), 1)
    own_state = (bcol >> 6) == (br >> ls_shift)
    ones_blk = jnp.ones((R, GLA_DV), BF16)
    zeros_blk = jnp.zeros((R, GLA_DV), BF16)

    def expand(xp, first):
        sw = pltpu.roll(xp, GLA_DK, 1)
        two = jnp.where(lane128 < GLA_DK, xp, sw) if first else jnp.where(lane128 < GLA_DK, sw, xp)
        rep = jnp.concatenate([two] * (NS // 128), axis=1)
        return jnp.where(own_state, rep, 0.0).astype(BF16)

    o_parts = []
    for p in range(2):
        ks = slice(128 * p, 128 * (p + 1))
        vs = slice(256 * p, 256 * (p + 1))
        k_in_p = k_in[:, ks]
        zk = jnp.zeros_like(k_in_p)
        kexp = jnp.concatenate([jnp.where(lane128 < GLA_DK, k_in_p, zk),
                                jnp.where(lane128 >= GLA_DK, k_in_p, zk)], axis=0)
        a = _dot_nt(q_in[:, ks], kexp)
        a = jnp.where(same_seq_causal, a, 0.0).astype(BF16)
        v_p = v[:, vs]
        zv = jnp.zeros_like(v_p)
        vblk = jnp.concatenate([jnp.where(lane256 < GLA_DV, v_p, zv),
                                jnp.where(lane256 >= GLA_DV, v_p, zv)], axis=0)
        o_intra = _dot(a, vblk)
        q_f = q_in[:, ks].astype(F32)
        inter = []
        for j in range(2):
            h = 2 * p + j
            s_flat = s0_ref[:, h].reshape(NS, GLA_DV)
            inter.append(_dot(expand(q_f, j == 0), s_flat.astype(BF16)))
            lhs = jnp.concatenate([expand(k_st[:, ks], j == 0),
                                   expand(dec_rows[:, ks].astype(F32), j == 0)], axis=0)
            v_h = v[:, h * GLA_DV:(h + 1) * GLA_DV]
            rhs = jnp.concatenate([jnp.concatenate([v_h, zeros_blk], axis=1),
                                   jnp.concatenate([zeros_blk, ones_blk], axis=1)], axis=0)
            ud = _dot_tn(lhs, rhs)
            s_new = ud[:, GLA_DV:] * s_flat + ud[:, :GLA_DV]
            snew_ref[:, h] = s_new.reshape(G, GLA_DK, GLA_DV)
        o_parts.append(o_intra + jnp.concatenate(inter, axis=1))
    o = jnp.concatenate(o_parts, axis=1)

    r = _dot(xb, w_in_ref[:, C_R:C_Z])
    y_gla = _gla_output_gate(o, r, gnorm_ref[...])
    x1_ref[...] = _mix_out(x, y_pool, y_gla, w_out_ref, g_ref, b_ref)


def _mixer_short(x, pprev, s0, wts, *, seq_len, pos0):
    rows = x.shape[0]
    n_total = rows // seq_len
    G = SHORT_ROWS // seq_len
    assert n_total % G == 0
    kern = functools.partial(_mixer_short_kernel, n_seq=G, seq_len=seq_len, pos0=pos0)
    return pl.pallas_call(
        kern,
        grid=(n_total // G,),
        in_specs=[pl.BlockSpec((SHORT_ROWS, D_MODEL), lambda i: (i, 0)),
                  pl.BlockSpec((G * POOL_PAD, D_POOL), lambda i: (i, 0)),
                  pl.BlockSpec((G, GLA_HEADS, GLA_DK, GLA_DV), lambda i: (i, 0, 0, 0))] + _mixer_weight_specs(),
        out_specs=[pl.BlockSpec((SHORT_ROWS, D_MODEL), lambda i: (i, 0)),
                   pl.BlockSpec((SHORT_ROWS, D_POOL), lambda i: (i, 0)),
                   pl.BlockSpec((G, GLA_HEADS, GLA_DK, GLA_DV), lambda i: (i, 0, 0, 0))],
        out_shape=[jax.ShapeDtypeStruct((rows, D_MODEL), F32),
                   jax.ShapeDtypeStruct((rows, D_POOL), F32),
                   jax.ShapeDtypeStruct((n_total, GLA_HEADS, GLA_DK, GLA_DV), F32)],
        compiler_params=pltpu.CompilerParams(dimension_semantics=("arbitrary",),
                                             vmem_limit_bytes=VMEM_LIMIT),
        name=f"mixer_short_len{seq_len}",
    )(x, pprev, s0, *wts)


def _ffn_tile(x_ref, hist_ref, w_up_ref, w_gate_ref, cw_ref, cb_ref, w_down_ref, g_ref, b_ref,
              y_ref, gt_ref, gbuf, act, *, seq_len):
    T = x_ref.shape[0]
    x = x_ref[...]
    xb = x.astype(BF16)
    if seq_len is not None:
        tok = lax.broadcasted_iota(jnp.int32, (T, FF_CHUNK), 0) & (seq_len - 1)

    for j in range(D_FF // FF_CHUNK):
        cs = slice(j * FF_CHUNK, (j + 1) * FF_CHUNK)
        a = _dot(xb, w_up_ref[:, cs])
        gt = _dot(xb, w_gate_ref[:, cs])
        if seq_len is None:
            gbuf[CONV_PAD:CONV_PAD + T, cs] = gt
            g1 = gbuf[CONV_PAD - 1:CONV_PAD - 1 + T, cs]
            g2 = gbuf[CONV_PAD - 2:CONV_PAD - 2 + T, cs]
        else:
            gt_ref[:, cs] = gt
            hist = hist_ref[:, cs]
            g1 = jnp.where(tok >= 1, _roll_rows(gt, 1), _roll_rows(hist, -1))
            g2 = jnp.where(tok >= 2, _roll_rows(gt, 2), hist)
        gc = cb_ref[:, cs] + cw_ref[0:1, cs] * g2 + cw_ref[1:2, cs] * g1 + cw_ref[2:3, cs] * gt
        act[:, cs] = (a * _silu(gc)).astype(BF16)
    if seq_len is None:
        gbuf[0:CONV_PAD, :] = gbuf[T:T + CONV_PAD, :]
    yield

    f = _dot(act[...], w_down_ref[...])
    yield

    y_ref[...] = _layer_norm(ALPHA * x + f, g_ref[...], b_ref[...])


def _ffn_long_kernel(x_ref, cprev_ref, w_up_ref, w_gate_ref, cw_ref, cb_ref, w_down_ref, g_ref, b_ref,
                     y_ref, hist_out_ref, gbuf, act, *, n_par):
    t = pl.program_id(1)

    @pl.when(t == 0)
    def _init():
        for j in range(n_par):
            gbuf[j, 0:CONV_PAD, :] = cprev_ref[0]

    tiles = [_ffn_tile(x_ref.at[j], None, w_up_ref, w_gate_ref, cw_ref, cb_ref, w_down_ref, g_ref, b_ref,
                       y_ref.at[j], None, gbuf.at[j], act.at[j], seq_len=None) for j in range(n_par)]
    for _ in itertools.zip_longest(*tiles):
        pass

    @pl.when(t == pl.num_programs(1) - 1)
    def _final():
        for j in range(n_par):
            hist_out_ref[j] = gbuf[j, 0:CONV_PAD, :]


def _ffn_short_kernel(x_ref, cprev_ref, w_up_ref, w_gate_ref, cw_ref, cb_ref, w_down_ref, g_ref, b_ref,
                      y_ref, gt_ref, act, *, seq_len):
    for _ in _ffn_tile(x_ref, cprev_ref, w_up_ref, w_gate_ref, cw_ref, cb_ref, w_down_ref, g_ref, b_ref,
                       y_ref, gt_ref, None, act, seq_len=seq_len):
        pass


def _ffn_weight_specs():
    return [_weight_spec((D_MODEL, D_FF)), _weight_spec((D_MODEL, D_FF)), _weight_spec((3, D_FF)),
            _weight_spec((1, D_FF)), _weight_spec((D_FF, D_MODEL)), _weight_spec((1, D_MODEL)),
            _weight_spec((1, D_MODEL))]


def _ffn_long(x, cprev, wts):
    B, L, _ = x.shape
    T = min(FFN_TILE, L)
    P = LONG_PAR if B % LONG_PAR == 0 else 1
    assert L % T == 0
    kern = functools.partial(_ffn_long_kernel, n_par=P)
    return pl.pallas_call(
        kern,
        grid=(B // P, L // T),
        in_specs=[pl.BlockSpec((P, T, D_MODEL), lambda b, t: (b, t, 0)),
                  pl.BlockSpec((1, CONV_PAD, D_FF), lambda b, t: (0, 0, 0))] + _ffn_weight_specs(),
        out_specs=[pl.BlockSpec((P, T, D_MODEL), lambda b, t: (b, t, 0)),
                   pl.BlockSpec((P, CONV_PAD, D_FF), lambda b, t: (b, 0, 0))],
        out_shape=[jax.ShapeDtypeStruct((B, L, D_MODEL), F32),
                   jax.ShapeDtypeStruct((B, CONV_PAD, D_FF), F32)],
        scratch_shapes=[pltpu.VMEM((P, T + CONV_PAD, D_FF), F32),
                        pltpu.VMEM((P, T, D_FF), BF16)],
        compiler_params=pltpu.CompilerParams(dimension_semantics=("arbitrary", "arbitrary"),
                                             vmem_limit_bytes=VMEM_LIMIT),
        name="ffn_long",
    )(x, cprev, *wts)


def _ffn_short(x, cprev, wts, *, seq_len):
    rows = x.shape[0]
    kern = functools.partial(_ffn_short_kernel, seq_len=seq_len)
    return pl.pallas_call(
        kern,
        grid=(1,),
        in_specs=[pl.BlockSpec((rows, D_MODEL), lambda i: (0, 0)),
                  pl.BlockSpec((rows, D_FF), lambda i: (0, 0))] + _ffn_weight_specs(),
        out_specs=[pl.BlockSpec((rows, D_MODEL), lambda i: (0, 0)),
                   pl.BlockSpec((rows, D_FF), lambda i: (0, 0))],
        out_shape=[jax.ShapeDtypeStruct((rows, D_MODEL), F32),
                   jax.ShapeDtypeStruct((rows, D_FF), F32)],
        scratch_shapes=[pltpu.VMEM((rows, D_FF), BF16)],
        compiler_params=pltpu.CompilerParams(dimension_semantics=("arbitrary",),
                                             vmem_limit_bytes=VMEM_LIMIT),
        name=f"ffn_short_len{seq_len}",
    )(x, cprev, *wts)


def _conv_history_rows(state, seq_len):
    n = state.shape[0]
    return jnp.pad(state, ((0, 0), (0, seq_len - CONV_BUF), (0, 0))).reshape(n * seq_len, D_FF)


def kernel(x_prompt, x_sample, state_pool, state_gla, state_conv, meta_tokens,
           w_in, w_a2, b_a, w_pool, pool_scale, gla_norm, w_out, ln1_g, ln1_b,
           w_up, w_gate, conv_w, conv_b, w_down, ln2_g, ln2_b):
    B = x_prompt.shape[0]
    NB, LS = x_sample.shape[0], x_sample.shape[1]
    n_meta_seq = SHORT_ROWS // N_META

    hm = jnp.pad(meta_tokens.astype(F32), ((0, SHORT_ROWS - N_META), (0, 0)))
    hp = x_prompt
    hs = x_sample.reshape(NB * LS, D_MODEL)
    meta_pool = jnp.zeros((n_meta_seq * POOL_PAD, D_POOL), F32)
    meta_gla = jnp.zeros((n_meta_seq, GLA_HEADS, GLA_DK, GLA_DV), F32)
    meta_conv = jnp.zeros((SHORT_ROWS, D_FF), F32)

    pp, gp, cp, ps, gs, cs = [], [], [], [], [], []
    for l in range(DEPTH):
        mix_w = (w_in[l].astype(BF16), w_a2[l].astype(BF16), b_a[l][None], w_pool[l].astype(BF16),
                 pool_scale[l][None], gla_norm[l][None], w_out[l].astype(BF16), ln1_g[l][None], ln1_b[l][None])
        ffn_w = (w_up[l].astype(BF16), w_gate[l].astype(BF16), conv_w[l], conv_b[l][None],
                 w_down[l].astype(BF16), ln2_g[l][None], ln2_b[l][None])

        hm1, um, sm = _mixer_short(hm, meta_pool, meta_gla, mix_w, seq_len=N_META, pos0=0)
        hm, gtm = _ffn_short(hm1, meta_conv, ffn_w, seq_len=N_META)

        cprev = jnp.pad(gtm[N_META - CONV_BUF:N_META], ((CONV_PAD - CONV_BUF, 0), (0, 0)))[None]
        hp1, pbuf, snew = _mixer_long(hp, um[None, 0:N_META], sm[0:1], mix_w)
        hp, cbuf = _ffn_long(hp1, cprev, ffn_w)
        pp.append(pbuf[:, 1:])
        gp.append(snew)
        cp.append(cbuf[:, CONV_PAD - CONV_BUF:])

        pool_hist = jnp.pad(state_pool[l], ((0, 0), (1, 0), (0, 0))).reshape(NB * POOL_PAD, D_POOL)
        hs1, us, ss = _mixer_short(hs, pool_hist, state_gla[l], mix_w, seq_len=LS, pos0=PAST_LEN)
        hs, gts = _ffn_short(hs1, _conv_history_rows(state_conv[l], LS), ffn_w, seq_len=LS)
        ps.append(jnp.concatenate([state_pool[l][:, LS:], us.reshape(NB, LS, D_POOL)], axis=1))
        gs.append(ss)
        cs.append(gts.reshape(NB, LS, D_FF)[:, LS - CONV_BUF:])

    return (hp, hs.reshape(NB, LS, D_MODEL), jnp.stack(pp), jnp.stack(gp), jnp.stack(cp),
            jnp.stack(ps), jnp.stack(gs), jnp.stack(cs))
```

```python
import functools
import itertools

import jax
import jax.numpy as jnp
from jax import lax
from jax.experimental import pallas as pl
from jax.experimental.pallas import tpu as pltpu

F32 = jnp.float32
BF16 = jnp.bfloat16

D_MODEL = 1024
N_META = 16
D_POOL = 512
POOL_WINDOWS = (2, 4, 8, 16)
POOL_GROUP = 128
POOL_BUF = 15
POOL_PAD = 16
D_GLA = 512
GLA_HEADS = 4
GLA_DV = 128
GLA_DK = 64
D_GLA_K = 256
GATE_RANK = 16
GATE_TAU = 16.0
D_FF = 2816
CONV_BUF = 2
CONV_PAD = 8
DEPTH = 2
ALPHA = (2 * DEPTH) ** 0.25
LN_EPS = 1e-5
RMS_EPS = 1e-6
PAST_LEN = 16384

C_POOL, C_Q, C_K, C_V, C_R, C_Z, C_END = 0, 512, 768, 1024, 1536, 2048, 2064

LONG_TILE = 512
LONG_CHUNK = 64
LONG_PAR = 2
FFN_TILE = 256
SHORT_ROWS = 64
FF_CHUNK = 256
VMEM_LIMIT = 56 * 1024 * 1024


def _dot(a, b):
    return jnp.dot(a, b, preferred_element_type=F32)


def _dot_nt(a, b):
    return lax.dot_general(a, b, (((1,), (1,)), ((), ())), preferred_element_type=F32)


def _dot_tn(a, b):
    return lax.dot_general(a, b, (((0,), (0,)), ((), ())), preferred_element_type=F32)


def _layer_norm(y, g, b):
    mu = jnp.mean(y, axis=-1, keepdims=True)
    yc = y - mu
    var = jnp.mean(yc * yc, axis=-1, keepdims=True)
    return yc * lax.rsqrt(var + LN_EPS) * g + b


def _silu(x):
    return x * (1.0 / (1.0 + jnp.exp(-x)))


def _log_sigmoid(z):
    return jnp.minimum(z, 0.0) - jnp.log(1.0 + jnp.exp(-jnp.abs(z)))


def _roll_rows(x, shift):
    n = x.shape[0]
    return pltpu.roll(x, shift % n, 0)


def _split_bf16(x):
    hi = x.astype(BF16)
    lo = (x - hi.astype(F32)).astype(BF16)
    return hi, lo


def _gate_log_decay(zr, w_a2_ref, b_a_ref):
    z = _dot(zr.astype(BF16), w_a2_ref[...]) + b_a_ref[...]
    return _log_sigmoid(z) * (1.0 / GATE_TAU)


def _gla_output_gate(o, r, gnorm):
    parts = []
    for h in range(GLA_HEADS):
        oh = o[:, h * GLA_DV:(h + 1) * GLA_DV]
        ms = jnp.mean(oh * oh, axis=-1, keepdims=True)
        parts.append(oh * lax.rsqrt(ms + RMS_EPS) * gnorm)
    return jnp.concatenate(parts, axis=1) * _silu(r)


def _pool_project(d_groups, w_pool_ref, pscale_ref):
    ys = [_dot(d.astype(BF16), w_pool_ref[g]) for g, d in enumerate(d_groups)]
    return jnp.concatenate(ys, axis=1) * pscale_ref[...]


def _mix_out(x, y_pool, y_gla, w_out_ref, g_ref, b_ref):
    mix = jnp.concatenate([y_pool, y_gla], axis=1).astype(BF16)
    return _layer_norm(ALPHA * x + _dot(mix, w_out_ref[...]), g_ref[...], b_ref[...])


def _mixer_long_kernel(x_ref, pprev_ref, s0_ref, w_in_ref, w_a2_ref, b_a_ref, w_pool_ref,
                       pscale_ref, gnorm_ref, w_out_ref, g_ref, b_ref,
                       x1_ref, pbuf_ref, snew_ref,
                       ubuf, sbd, *, n_par, tile, chunk):
    t = pl.program_id(1)
    zero_blk = jnp.zeros((GLA_DK, GLA_DV), F32)

    @pl.when(t == 0)
    def _init():
        for j in range(n_par):
            ubuf[j, 0:POOL_PAD, :] = pprev_ref[0]
            for p in range(2):
                top = jnp.concatenate([s0_ref[0, 2 * p], zero_blk], axis=1)
                bot = jnp.concatenate([zero_blk, s0_ref[0, 2 * p + 1]], axis=1)
                sbd[j, p] = jnp.concatenate([top, bot], axis=0)

    tiles = [_mixer_long_tile(x_ref.at[j], w_in_ref, w_a2_ref, b_a_ref, w_pool_ref,
                              pscale_ref, gnorm_ref, w_out_ref, g_ref, b_ref,
                              x1_ref.at[j], ubuf.at[j], sbd.at[j], tile=tile, chunk=chunk)
             for j in range(n_par)]
    for _ in itertools.zip_longest(*tiles):
        pass

    @pl.when(t == pl.num_programs(1) - 1)
    def _final():
        for j in range(n_par):
            pbuf_ref[j] = ubuf[j, 0:POOL_PAD, :]
            for p in range(2):
                s_p = sbd[j, p]
                snew_ref[j, 2 * p] = s_p[0:GLA_DK, 0:GLA_DV]
                snew_ref[j, 2 * p + 1] = s_p[GLA_DK:2 * GLA_DK, GLA_DV:2 * GLA_DV]


def _mixer_long_tile(x_ref, w_in_ref, w_a2_ref, b_a_ref, w_pool_ref,
                     pscale_ref, gnorm_ref, w_out_ref, g_ref, b_ref,
                     x1_ref, ubuf, sbd, *, tile, chunk):
    T, C = tile, chunk
    x = x_ref[...]
    xb = x.astype(BF16)

    u = _dot(xb, w_in_ref[:, C_POOL:C_Q])
    zr = _dot(xb, w_in_ref[:, C_Z:C_END])
    q = _dot(xb, w_in_ref[:, C_Q:C_K]) * (GLA_DK ** -0.5)
    k = _dot(xb, w_in_ref[:, C_K:C_V])
    v = _dot(xb, w_in_ref[:, C_V:C_R])
    r = _dot(xb, w_in_ref[:, C_R:C_Z])
    yield

    ubuf[POOL_PAD:POOL_PAD + T, :] = u
    d_groups = []
    for g, w in enumerate(POOL_WINDOWS):
        s = ubuf[:, g * POOL_GROUP:(g + 1) * POOL_GROUP]
        sh = 1
        while sh < w:
            s = s + _roll_rows(s, sh)
            sh *= 2
        d_groups.append(s[POOL_PAD:, :] * (1.0 / w) - u[:, g * POOL_GROUP:(g + 1) * POOL_GROUP])
    y_pool = _pool_project(d_groups, w_pool_ref, pscale_ref)
    ubuf[0:POOL_PAD, :] = ubuf[T:T + POOL_PAD, :]
    loga = _gate_log_decay(zr, w_a2_ref, b_a_ref)

    tr = lax.broadcasted_iota(jnp.int32, (C, C), 0)
    tc = lax.broadcasted_iota(jnp.int32, (C, C), 1)
    tri = jnp.where(tc <= tr, 1.0, 0.0).astype(BF16)
    ar = lax.broadcasted_iota(jnp.int32, (C, 2 * C), 0)
    ac = lax.broadcasted_iota(jnp.int32, (C, 2 * C), 1) & (C - 1)
    causal = ac <= ar
    lane128 = lax.broadcasted_iota(jnp.int32, (C, 128), 1)
    lane256 = lax.broadcasted_iota(jnp.int32, (C, 256), 1)
    sr = lax.broadcasted_iota(jnp.int32, (128, 256), 0)
    sc = lax.broadcasted_iota(jnp.int32, (128, 256), 1)
    blockdiag = (sr >= GLA_DK) == (sc >= GLA_DV)
    mid = C // 2 - 1

    n_chunks = T // C
    pairs = [(c, p) for c in range(n_chunks) for p in range(2)]
    ks = [slice(128 * p, 128 * (p + 1)) for p in range(2)]
    vs = [slice(256 * p, 256 * (p + 1)) for p in range(2)]

    bcs = []
    for c in range(n_chunks):
        la_hi, la_lo = _split_bf16(loga[c * C:(c + 1) * C])
        bb = _dot(tri, jnp.concatenate([la_hi, la_lo], axis=1))
        bcs.append(bb[:, :D_GLA_K] + bb[:, D_GLA_K:])
    yield

    q_in, k_in, q_st, k_st, dec_t, vc = [], [], [], [], [], []
    for c in range(n_chunks):
        bc = bcs[c]
        bmid = bc[mid:mid + 1]
        bend = bc[C - 1:C]
        qc = q[c * C:(c + 1) * C]
        kc = k[c * C:(c + 1) * C]
        q_in.append((qc * jnp.exp(bc - bmid)).astype(BF16))
        k_in.append((kc * jnp.exp(bmid - bc)).astype(BF16))
        q_st.append((qc * jnp.exp(bc)).astype(BF16))
        k_st.append((kc * jnp.exp(bend - bc)).astype(BF16))
        dec_t.append(jnp.transpose(jnp.broadcast_to(jnp.exp(bend), (128, D_GLA_K))))
        vc.append(v[c * C:(c + 1) * C].astype(BF16))
    yield

    attn, upd = {}, {}
    for c, p in pairs:
        k_in_p = k_in[c][:, ks[p]]
        zk = jnp.zeros_like(k_in_p)
        kexp = jnp.concatenate([jnp.where(lane128 < GLA_DK, k_in_p, zk),
                                jnp.where(lane128 >= GLA_DK, k_in_p, zk)], axis=0)
        a = _dot_nt(q_in[c][:, ks[p]], kexp)
        attn[c, p] = jnp.where(causal, a, 0.0).astype(BF16)
    for c, p in pairs:
        u_cp = _dot_tn(k_st[c][:, ks[p]], vc[c][:, vs[p]])
        upd[c, p] = jnp.where(blockdiag, u_cp, 0.0)
    yield

    s_vals = [sbd[p] for p in range(2)]
    s_start = {}
    for c, p in pairs:
        s_start[c, p] = s_vals[p].astype(BF16)
        dec_p = dec_t[c][ks[p], :]
        s_vals[p] = jnp.concatenate([dec_p, dec_p], axis=1) * s_vals[p] + upd[c, p]
    for p in range(2):
        sbd[p] = s_vals[p]

    o_rows = [[], []]
    for c, p in pairs:
        v_p = vc[c][:, vs[p]]
        zv = jnp.zeros_like(v_p)
        vblk = jnp.concatenate([jnp.where(lane256 < GLA_DV, v_p, zv),
                                jnp.where(lane256 >= GLA_DV, v_p, zv)], axis=0)
        lhs = jnp.concatenate([attn[c, p], q_st[c][:, ks[p]]], axis=1)
        rhs = jnp.concatenate([vblk, s_start[c, p]], axis=0)
        o_rows[p].append(_dot(lhs, rhs))
    o = jnp.concatenate([jnp.concatenate(o_rows[p], axis=0) for p in range(2)], axis=1)
    yield

    y_gla = _gla_output_gate(o, r, gnorm_ref[...])
    x1_ref[...] = _mix_out(x, y_pool, y_gla, w_out_ref, g_ref, b_ref)


def _layer_spec(shape, layer):
    nd = len(shape)
    return pl.BlockSpec((None,) + shape, lambda *_: (layer,) + (0,) * nd, pipeline_mode=pl.Buffered(1))


def _mixer_weight_specs(layer):
    shapes = [(D_MODEL, C_END), (GATE_RANK, D_GLA_K), (1, D_GLA_K), (4, POOL_GROUP, POOL_GROUP), (1, D_POOL),
              (1, GLA_DV), (D_MODEL, D_MODEL), (1, D_MODEL), (1, D_MODEL)]
    return [_layer_spec(s, layer) for s in shapes]


def _mixer_long(x, pprev, s0, wts, layer):
    B, L, _ = x.shape
    T = min(LONG_TILE, L)
    P = LONG_PAR if B % LONG_PAR == 0 else 1
    assert L % T == 0 and T % LONG_CHUNK == 0
    kern = functools.partial(_mixer_long_kernel, n_par=P, tile=T, chunk=LONG_CHUNK)
    return pl.pallas_call(
        kern,
        grid=(B // P, L // T),
        in_specs=[pl.BlockSpec((P, T, D_MODEL), lambda b, t: (b, t, 0)),
                  pl.BlockSpec((1, POOL_PAD, D_POOL), lambda b, t: (0, 0, 0)),
                  pl.BlockSpec((1, GLA_HEADS, GLA_DK, GLA_DV), lambda b, t: (0, 0, 0, 0))]
        + _mixer_weight_specs(layer),
        out_specs=[pl.BlockSpec((P, T, D_MODEL), lambda b, t: (b, t, 0)),
                   pl.BlockSpec((P, POOL_PAD, D_POOL), lambda b, t: (b, 0, 0)),
                   pl.BlockSpec((P, GLA_HEADS, GLA_DK, GLA_DV), lambda b, t: (b, 0, 0, 0))],
        out_shape=[jax.ShapeDtypeStruct((B, L, D_MODEL), F32),
                   jax.ShapeDtypeStruct((B, POOL_PAD, D_POOL), F32),
                   jax.ShapeDtypeStruct((B, GLA_HEADS, GLA_DK, GLA_DV), F32)],
        scratch_shapes=[pltpu.VMEM((P, T + POOL_PAD, D_POOL), F32),
                        pltpu.VMEM((P, 2, 2 * GLA_DK, 2 * GLA_DV), F32)],
        compiler_params=pltpu.CompilerParams(dimension_semantics=("arbitrary", "arbitrary"),
                                             vmem_limit_bytes=VMEM_LIMIT),
        name="mixer_long",
    )(x, pprev, s0, *wts)


def _mixer_short_kernel(x_ref, pprev_ref, s0_ref, w_in_ref, w_a2_ref, b_a_ref, w_pool_ref,
                        pscale_ref, gnorm_ref, w_out_ref, g_ref, b_ref,
                        x1_ref, u_ref, snew_ref, *, n_seq, seq_len, pos0):
    G, Ls = n_seq, seq_len
    R = G * Ls
    ls_shift = Ls.bit_length() - 1
    NP = G * POOL_PAD
    NS = G * GLA_DK
    x = x_ref[...]
    xb = x.astype(BF16)

    u = _dot(xb, w_in_ref[:, C_POOL:C_Q])
    u_ref[...] = u
    pr = lax.broadcasted_iota(jnp.int32, (R, NP + R), 0)
    pc = lax.broadcasted_iota(jnp.int32, (R, NP + R), 1)
    p_seq, p_tok = pr >> ls_shift, pr & (Ls - 1)
    hist_seq, hist_row = pc >> 4, pc & (POOL_PAD - 1)
    cc = pc - NP
    new_seq, new_tok = cc >> ls_shift, cc & (Ls - 1)
    tok1 = lax.broadcasted_iota(jnp.int32, (R, 1), 0) & (Ls - 1)
    d_groups = []
    for g, w in enumerate(POOL_WINDOWS):
        in_hist = (pc < NP) & (hist_seq == p_seq) & (hist_row >= POOL_PAD - (w - 1 - p_tok))
        in_new = (pc >= NP) & (new_seq == p_seq) & (new_tok <= p_tok) & (new_tok > p_tok - w)
        sel = jnp.where(in_hist | in_new, 1.0, 0.0).astype(BF16)
        ug = u[:, g * POOL_GROUP:(g + 1) * POOL_GROUP]
        z = jnp.concatenate([pprev_ref[:, g * POOL_GROUP:(g + 1) * POOL_GROUP], ug], axis=0)
        z_hi, z_lo = _split_bf16(z)
        wsum = _dot(sel, z_hi) + _dot(sel, z_lo)
        if pos0 >= POOL_BUF:
            mean = wsum * (1.0 / w)
        else:
            mean = wsum / jnp.minimum(w, pos0 + tok1 + 1).astype(F32)
        d_groups.append(mean - ug)
    y_pool = _pool_project(d_groups, w_pool_ref, pscale_ref)

    q = _dot(xb, w_in_ref[:, C_Q:C_K]) * (GLA_DK ** -0.5)
    k = _dot(xb, w_in_ref[:, C_K:C_V])
    v = _dot(xb, w_in_ref[:, C_V:C_R]).astype(BF16)
    zr = _dot(xb, w_in_ref[:, C_Z:C_END])
    loga = _gate_log_decay(zr, w_a2_ref, b_a_ref)

    tok = lax.broadcasted_iota(jnp.int32, (R, D_GLA_K), 0) & (Ls - 1)
    b = loga
    sh = 1
    while sh < Ls:
        b = b + jnp.where(tok >= sh, _roll_rows(b, sh), 0.0)
        sh *= 2
    bend = jnp.where(tok == Ls - 1, b, 0.0)
    sh = 1
    while sh < Ls:
        bend = bend + jnp.where(tok + sh <= Ls - 1, _roll_rows(bend, -sh), 0.0)
        sh *= 2

    q_in = (q * jnp.exp(b)).astype(BF16)
    k_in = (k * jnp.exp(-b)).astype(BF16)
    k_st = k * jnp.exp(bend - b)
    dec_hi, dec_lo = _split_bf16(jnp.exp(bend))
    dec_rows = jnp.where(tok == Ls - 1, dec_hi, jnp.where(tok == Ls - 2, dec_lo, jnp.zeros_like(dec_lo)))

    ar = lax.broadcasted_iota(jnp.int32, (R, 2 * R), 0)
    ac = lax.broadcasted_iota(jnp.int32, (R, 2 * R), 1) & (R - 1)
    same_seq_causal = ((ac >> ls_shift) == (ar >> ls_shift)) & ((ac & (Ls - 1)) <= (ar & (Ls - 1)))
    lane128 = lax.broadcasted_iota(jnp.int32, (R, 128), 1)
    lane256 = lax.broadcasted_iota(jnp.int32, (R, 256), 1)
    br = lax.broadcasted_iota(jnp.int32, (R, NS), 0)
    bcol = lax.broadcasted_iota(jnp.int32, (R, NS), 1)
    own_state = (bcol >> 6) == (br >> ls_shift)
    ones_blk = jnp.ones((R, GLA_DV), BF16)
    zeros_blk = jnp.zeros((R, GLA_DV), BF16)

    def expand(xp, first):
        sw = pltpu.roll(xp, GLA_DK, 1)
        two = jnp.where(lane128 < GLA_DK, xp, sw) if first else jnp.where(lane128 < GLA_DK, sw, xp)
        rep = jnp.concatenate([two] * (NS // 128), axis=1)
        return jnp.where(own_state, rep, 0.0).astype(BF16)

    o_parts = []
    for p in range(2):
        ks = slice(128 * p, 128 * (p + 1))
        vs = slice(256 * p, 256 * (p + 1))
        k_in_p = k_in[:, ks]
        zk = jnp.zeros_like(k_in_p)
        kexp = jnp.concatenate([jnp.where(lane128 < GLA_DK, k_in_p, zk),
                                jnp.where(lane128 >= GLA_DK, k_in_p, zk)], axis=0)
        a = _dot_nt(q_in[:, ks], kexp)
        a = jnp.where(same_seq_causal, a, 0.0).astype(BF16)
        v_p = v[:, vs]
        zv = jnp.zeros_like(v_p)
        vblk = jnp.concatenate([jnp.where(lane256 < GLA_DV, v_p, zv),
                                jnp.where(lane256 >= GLA_DV, v_p, zv)], axis=0)
        o_intra = _dot(a, vblk)
        q_f = q_in[:, ks].astype(F32)
        inter = []
        for j in range(2):
            h = 2 * p + j
            s_flat = s0_ref[:, h].reshape(NS, GLA_DV)
            inter.append(_dot(expand(q_f, j == 0), s_flat.astype(BF16)))
            lhs = jnp.concatenate([expand(k_st[:, ks], j == 0),
                                   expand(dec_rows[:, ks].astype(F32), j == 0)], axis=0)
            v_h = v[:, h * GLA_DV:(h + 1) * GLA_DV]
            rhs = jnp.concatenate([jnp.concatenate([v_h, zeros_blk], axis=1),
                                   jnp.concatenate([zeros_blk, ones_blk], axis=1)], axis=0)
            ud = _dot_tn(lhs, rhs)
            s_new = ud[:, GLA_DV:] * s_flat + ud[:, :GLA_DV]
            snew_ref[:, h] = s_new.reshape(G, GLA_DK, GLA_DV)
        o_parts.append(o_intra + jnp.concatenate(inter, axis=1))
    o = jnp.concatenate(o_parts, axis=1)

    r = _dot(xb, w_in_ref[:, C_R:C_Z])
    y_gla = _gla_output_gate(o, r, gnorm_ref[...])
    x1_ref[...] = _mix_out(x, y_pool, y_gla, w_out_ref, g_ref, b_ref)


def _mixer_short(x, pprev, s0, wts, layer, *, seq_len, pos0, state_layer, s_out=None):
    rows = x.shape[0]
    n_total = rows // seq_len
    G = SHORT_ROWS // seq_len
    assert n_total % G == 0
    sl = state_layer
    body = functools.partial(_mixer_short_kernel, n_seq=G, seq_len=seq_len, pos0=pos0)
    n_in = 12
    in_specs = ([pl.BlockSpec((SHORT_ROWS, D_MODEL), lambda i: (i, 0)),
                 pl.BlockSpec((None, G * POOL_PAD, D_POOL), lambda i: (sl, i, 0)),
                 pl.BlockSpec((None, G, GLA_HEADS, GLA_DK, GLA_DV), lambda i: (sl, i, 0, 0, 0))]
                + _mixer_weight_specs(layer))
    args = [x, pprev, s0, *wts]
    aliases = {}
    if s_out is not None:
        in_specs.append(pl.BlockSpec(memory_space=pl.ANY))
        args.append(s_out)
        aliases = {n_in: 2}

    def kern(*refs):
        body(*refs[:n_in], *refs[len(args):])

    return pl.pallas_call(
        kern,
        grid=(n_total // G,),
        in_specs=in_specs,
        out_specs=[pl.BlockSpec((SHORT_ROWS, D_MODEL), lambda i: (i, 0)),
                   pl.BlockSpec((SHORT_ROWS, D_POOL), lambda i: (i, 0)),
                   pl.BlockSpec((None, G, GLA_HEADS, GLA_DK, GLA_DV), lambda i: (sl, i, 0, 0, 0))],
        out_shape=[jax.ShapeDtypeStruct((rows, D_MODEL), F32),
                   jax.ShapeDtypeStruct((rows, D_POOL), F32),
                   jax.ShapeDtypeStruct(s0.shape, F32)],
        input_output_aliases=aliases,
        compiler_params=pltpu.CompilerParams(dimension_semantics=("arbitrary",),
                                             vmem_limit_bytes=VMEM_LIMIT),
        name=f"mixer_short_len{seq_len}",
    )(*args)


def _ffn_tile(x_ref, hist_ref, w_up_ref, w_gate_ref, cw_ref, cb_ref, w_down_ref, g_ref, b_ref,
              y_ref, gt_ref, gbuf, act, *, seq_len):
    T = x_ref.shape[0]
    x = x_ref[...]
    xb = x.astype(BF16)
    if seq_len is not None:
        tok = lax.broadcasted_iota(jnp.int32, (T, FF_CHUNK), 0) & (seq_len - 1)

    for j in range(D_FF // FF_CHUNK):
        cs = slice(j * FF_CHUNK, (j + 1) * FF_CHUNK)
        a = _dot(xb, w_up_ref[:, cs])
        gt = _dot(xb, w_gate_ref[:, cs])
        if seq_len is None:
            gbuf[CONV_PAD:CONV_PAD + T, cs] = gt
            g1 = gbuf[CONV_PAD - 1:CONV_PAD - 1 + T, cs]
            g2 = gbuf[CONV_PAD - 2:CONV_PAD - 2 + T, cs]
        else:
            gt_ref[:, cs] = gt
            hist = hist_ref[:, cs]
            g1 = jnp.where(tok >= 1, _roll_rows(gt, 1), _roll_rows(hist, -1))
            g2 = jnp.where(tok >= 2, _roll_rows(gt, 2), hist)
        gc = cb_ref[:, cs] + cw_ref[0:1, cs] * g2 + cw_ref[1:2, cs] * g1 + cw_ref[2:3, cs] * gt
        act[:, cs] = (a * _silu(gc)).astype(BF16)
    if seq_len is None:
        gbuf[0:CONV_PAD, :] = gbuf[T:T + CONV_PAD, :]
    yield

    f = _dot(act[...], w_down_ref[...])
    yield

    y_ref[...] = _layer_norm(ALPHA * x + f, g_ref[...], b_ref[...])


def _ffn_long_kernel(x_ref, cprev_ref, w_up_ref, w_gate_ref, cw_ref, cb_ref, w_down_ref, g_ref, b_ref,
                     y_ref, hist_out_ref, gbuf, act, *, n_par):
    t = pl.program_id(1)

    @pl.when(t == 0)
    def _init():
        for j in range(n_par):
            gbuf[j, 0:CONV_PAD, :] = cprev_ref[0]

    tiles = [_ffn_tile(x_ref.at[j], None, w_up_ref, w_gate_ref, cw_ref, cb_ref, w_down_ref, g_ref, b_ref,
                       y_ref.at[j], None, gbuf.at[j], act.at[j], seq_len=None) for j in range(n_par)]
    for _ in itertools.zip_longest(*tiles):
        pass

    @pl.when(t == pl.num_programs(1) - 1)
    def _final():
        for j in range(n_par):
            hist_out_ref[j] = gbuf[j, 0:CONV_PAD, :]


def _ffn_short_kernel(x_ref, cprev_ref, w_up_ref, w_gate_ref, cw_ref, cb_ref, w_down_ref, g_ref, b_ref,
                      y_ref, gt_ref, act, *, seq_len):
    for _ in _ffn_tile(x_ref, cprev_ref, w_up_ref, w_gate_ref, cw_ref, cb_ref, w_down_ref, g_ref, b_ref,
                       y_ref, gt_ref, None, act, seq_len=seq_len):
        pass


def _ffn_weight_specs(layer):
    shapes = [(D_MODEL, D_FF), (D_MODEL, D_FF), (3, D_FF), (1, D_FF), (D_FF, D_MODEL), (1, D_MODEL), (1, D_MODEL)]
    return [_layer_spec(s, layer) for s in shapes]


def _ffn_long(x, cprev, wts, layer):
    B, L, _ = x.shape
    T = min(FFN_TILE, L)
    P = LONG_PAR if B % LONG_PAR == 0 else 1
    assert L % T == 0
    kern = functools.partial(_ffn_long_kernel, n_par=P)
    return pl.pallas_call(
        kern,
        grid=(B // P, L // T),
        in_specs=[pl.BlockSpec((P, T, D_MODEL), lambda b, t: (b, t, 0)),
                  pl.BlockSpec((1, CONV_PAD, D_FF), lambda b, t: (0, 0, 0))] + _ffn_weight_specs(layer),
        out_specs=[pl.BlockSpec((P, T, D_MODEL), lambda b, t: (b, t, 0)),
                   pl.BlockSpec((P, CONV_PAD, D_FF), lambda b, t: (b, 0, 0))],
        out_shape=[jax.ShapeDtypeStruct((B, L, D_MODEL), F32),
                   jax.ShapeDtypeStruct((B, CONV_PAD, D_FF), F32)],
        scratch_shapes=[pltpu.VMEM((P, T + CONV_PAD, D_FF), F32),
                        pltpu.VMEM((P, T, D_FF), BF16)],
        compiler_params=pltpu.CompilerParams(dimension_semantics=("arbitrary", "arbitrary"),
                                             vmem_limit_bytes=VMEM_LIMIT),
        name="ffn_long",
    )(x, cprev, *wts)


def _ffn_short(x, cprev, wts, layer, *, seq_len, state_layer):
    rows = x.shape[0]
    kern = functools.partial(_ffn_short_kernel, seq_len=seq_len)
    return pl.pallas_call(
        kern,
        grid=(1,),
        in_specs=[pl.BlockSpec((rows, D_MODEL), lambda i: (0, 0)),
                  pl.BlockSpec((None, rows, D_FF), lambda i: (state_layer, 0, 0))] + _ffn_weight_specs(layer),
        out_specs=[pl.BlockSpec((rows, D_MODEL), lambda i: (0, 0)),
                   pl.BlockSpec((rows, D_FF), lambda i: (0, 0))],
        out_shape=[jax.ShapeDtypeStruct((rows, D_MODEL), F32),
                   jax.ShapeDtypeStruct((rows, D_FF), F32)],
        scratch_shapes=[pltpu.VMEM((rows, D_FF), BF16)],
        compiler_params=pltpu.CompilerParams(dimension_semantics=("arbitrary",),
                                             vmem_limit_bytes=VMEM_LIMIT),
        name=f"ffn_short_len{seq_len}",
    )(x, cprev, *wts)


def _conv_history_rows(state, seq_len):
    ls, n = state.shape[0], state.shape[1]
    return jnp.pad(state, ((0, 0), (0, 0), (0, seq_len - CONV_BUF), (0, 0))).reshape(ls, n * seq_len, D_FF)


def kernel(x_prompt, x_sample, state_pool, state_gla, state_conv, meta_tokens,
           w_in, w_a2, b_a, w_pool, pool_scale, gla_norm, w_out, ln1_g, ln1_b,
           w_up, w_gate, conv_w, conv_b, w_down, ln2_g, ln2_b):
    B = x_prompt.shape[0]
    NB, LS = x_sample.shape[0], x_sample.shape[1]
    n_meta_seq = SHORT_ROWS // N_META

    hm = jnp.pad(meta_tokens.astype(F32), ((0, SHORT_ROWS - N_META), (0, 0)))
    hp = x_prompt
    hs = x_sample.reshape(NB * LS, D_MODEL)
    meta_pool = jnp.zeros((1, n_meta_seq * POOL_PAD, D_POOL), F32)
    meta_gla = jnp.zeros((1, n_meta_seq, GLA_HEADS, GLA_DK, GLA_DV), F32)
    meta_conv = jnp.zeros((1, SHORT_ROWS, D_FF), F32)

    mix_w = (w_in.astype(BF16), w_a2.astype(BF16), b_a[:, None], w_pool.astype(BF16), pool_scale[:, None],
             gla_norm[:, None], w_out.astype(BF16), ln1_g[:, None], ln1_b[:, None])
    ffn_w = (w_up.astype(BF16), w_gate.astype(BF16), conv_w, conv_b[:, None], w_down.astype(BF16),
             ln2_g[:, None], ln2_b[:, None])
    pool_hist = jnp.pad(state_pool, ((0, 0), (0, 0), (1, 0), (0, 0))).reshape(DEPTH, NB * POOL_PAD, D_POOL)
    conv_hist = _conv_history_rows(state_conv, LS)

    pp, gp, cp, us_l, cs = [], [], [], [], []
    gs = None
    for l in range(DEPTH):
        hm1, um, sm = _mixer_short(hm, meta_pool, meta_gla, mix_w, l, seq_len=N_META, pos0=0, state_layer=0)
        hm, gtm = _ffn_short(hm1, meta_conv, ffn_w, l, seq_len=N_META, state_layer=0)

        cprev = jnp.pad(gtm[N_META - CONV_BUF:N_META], ((CONV_PAD - CONV_BUF, 0), (0, 0)))[None]
        hp1, pbuf, snew = _mixer_long(hp, um[None, 0:N_META], sm[0, 0:1], mix_w, l)
        hp, cbuf = _ffn_long(hp1, cprev, ffn_w, l)
        pp.append(pbuf[:, 1:])
        gp.append(snew)
        cp.append(cbuf[:, CONV_PAD - CONV_BUF:])

        hs1, us, gs = _mixer_short(hs, pool_hist, state_gla, mix_w, l, seq_len=LS, pos0=PAST_LEN,
                                   state_layer=l, s_out=gs)
        hs, gts = _ffn_short(hs1, conv_hist, ffn_w, l, seq_len=LS, state_layer=l)
        us_l.append(us.reshape(NB, LS, D_POOL))
        cs.append(gts.reshape(NB, LS, D_FF)[:, LS - CONV_BUF:])

    ps = jnp.concatenate([state_pool[:, :, LS:], jnp.stack(us_l)], axis=2)
    return (hp, hs.reshape(NB, LS, D_MODEL), jnp.stack(pp), jnp.stack(gp), jnp.stack(cp),
            ps, gs, jnp.stack(cs))
```

```python
import functools
import itertools

import jax
import jax.numpy as jnp
from jax import lax
from jax.experimental import pallas as pl
from jax.experimental.pallas import tpu as pltpu

F32 = jnp.float32
BF16 = jnp.bfloat16

D_MODEL = 1024
N_META = 16
D_POOL = 512
POOL_WINDOWS = (2, 4, 8, 16)
POOL_GROUP = 128
POOL_BUF = 15
POOL_PAD = 16
D_GLA = 512
GLA_HEADS = 4
GLA_DV = 128
GLA_DK = 64
D_GLA_K = 256
GATE_RANK = 16
GATE_TAU = 16.0
D_FF = 2816
CONV_BUF = 2
CONV_PAD = 8
DEPTH = 2
ALPHA = (2 * DEPTH) ** 0.25
LN_EPS = 1e-5
RMS_EPS = 1e-6
PAST_LEN = 16384

C_POOL, C_Q, C_K, C_V, C_R, C_MAIN = 0, 512, 768, 1024, 1536, 2048

LONG_TILE = 512
LONG_CHUNK = 64
LONG_PAR = 2
FFN_TILE = 256
SEQ_GROUP = 16
FF_CHUNK = 256
VMEM_LIMIT = 56 * 1024 * 1024

N_MIX_W = 10
N_FFN_W = 7


def _dot(a, b):
    return jnp.dot(a, b, preferred_element_type=F32)


def _dot_nt(a, b):
    return lax.dot_general(a, b, (((1,), (1,)), ((), ())), preferred_element_type=F32)


def _dot_tn(a, b):
    return lax.dot_general(a, b, (((0,), (0,)), ((), ())), preferred_element_type=F32)


def _layer_norm(y, g, b):
    mu = jnp.mean(y, axis=-1, keepdims=True)
    yc = y - mu
    var = jnp.mean(yc * yc, axis=-1, keepdims=True)
    return yc * lax.rsqrt(var + LN_EPS) * g + b


def _silu(x):
    return x * (1.0 / (1.0 + jnp.exp(-x)))


def _log_sigmoid(z):
    return jnp.minimum(z, 0.0) - jnp.log(1.0 + jnp.exp(-jnp.abs(z)))


def _roll_rows(x, shift):
    n = x.shape[0]
    return pltpu.roll(x, shift % n, 0)


def _split_bf16(x):
    hi = x.astype(BF16)
    lo = (x - hi.astype(F32)).astype(BF16)
    return hi, lo


def _project_in(xb, w_main_ref, w_zr_ref):
    u = _dot(xb, w_main_ref[:, C_POOL:C_Q])
    zr = _dot(xb, w_zr_ref[...])
    q = _dot(xb, w_main_ref[:, C_Q:C_K]) * (GLA_DK ** -0.5)
    k = _dot(xb, w_main_ref[:, C_K:C_V])
    v = _dot(xb, w_main_ref[:, C_V:C_R])
    r = _dot(xb, w_main_ref[:, C_R:C_MAIN])
    return u, zr, q, k, v, r


def _gate_log_decay(zr, w_a2_ref, b_a_ref):
    z = _dot(zr.astype(BF16), w_a2_ref[...]) + b_a_ref[...]
    return _log_sigmoid(z) * (1.0 / GATE_TAU)


def _gla_output_gate(o, r, gnorm):
    parts = []
    for h in range(GLA_HEADS):
        oh = o[:, h * GLA_DV:(h + 1) * GLA_DV]
        ms = jnp.mean(oh * oh, axis=-1, keepdims=True)
        parts.append(oh * lax.rsqrt(ms + RMS_EPS) * gnorm)
    return jnp.concatenate(parts, axis=1) * _silu(r)


def _pool_project(d_groups, w_pool_ref, pscale_ref):
    ys = [_dot(d.astype(BF16), w_pool_ref[g]) for g, d in enumerate(d_groups)]
    return jnp.concatenate(ys, axis=1) * pscale_ref[...]


def _mix_out(x, y_pool, y_gla, w_out_ref, g_ref, b_ref):
    mix = jnp.concatenate([y_pool, y_gla], axis=1).astype(BF16)
    return _layer_norm(ALPHA * x + _dot(mix, w_out_ref[...]), g_ref[...], b_ref[...])


def _head_pair_keys(k_p, lane128):
    zk = jnp.zeros_like(k_p)
    return jnp.concatenate([jnp.where(lane128 < GLA_DK, k_p, zk), jnp.where(lane128 >= GLA_DK, k_p, zk)], axis=0)


def _head_pair_values(v_p, lane256):
    zv = jnp.zeros_like(v_p)
    return jnp.concatenate([jnp.where(lane256 < GLA_DV, v_p, zv), jnp.where(lane256 >= GLA_DV, v_p, zv)], axis=0)


def _mixer_long_kernel(*refs, n_par, tile, chunk):
    x_ref, pprev_ref, s0_ref = refs[:3]
    wts = refs[3:3 + N_MIX_W]
    x1_ref, pbuf_ref, snew_ref, ubuf, sbd = refs[3 + N_MIX_W:]
    t = pl.program_id(1)
    zero_blk = jnp.zeros((GLA_DK, GLA_DV), F32)

    @pl.when(t == 0)
    def _init():
        for j in range(n_par):
            ubuf[j, 0:POOL_PAD, :] = pprev_ref[...]
            for p in range(2):
                top = jnp.concatenate([s0_ref[2 * p], zero_blk], axis=1)
                bot = jnp.concatenate([zero_blk, s0_ref[2 * p + 1]], axis=1)
                sbd[j, p] = jnp.concatenate([top, bot], axis=0)

    tiles = [_mixer_long_tile(x_ref.at[j], wts, x1_ref.at[j], ubuf.at[j], sbd.at[j], tile=tile, chunk=chunk)
             for j in range(n_par)]
    for _ in itertools.zip_longest(*tiles):
        pass

    @pl.when(t == pl.num_programs(1) - 1)
    def _final():
        for j in range(n_par):
            pbuf_ref[j] = ubuf[j, 0:POOL_PAD, :]
            for p in range(2):
                s_p = sbd[j, p]
                snew_ref[j, 2 * p] = s_p[0:GLA_DK, 0:GLA_DV]
                snew_ref[j, 2 * p + 1] = s_p[GLA_DK:2 * GLA_DK, GLA_DV:2 * GLA_DV]


def _mixer_long_tile(x_ref, wts, x1_ref, ubuf, sbd, *, tile, chunk):
    w_main_ref, w_zr_ref, w_a2_ref, b_a_ref, w_pool_ref, pscale_ref, gnorm_ref, w_out_ref, g_ref, b_ref = wts
    T, C = tile, chunk
    x = x_ref[...]
    xb = x.astype(BF16)

    u, zr, q, k, v, r = _project_in(xb, w_main_ref, w_zr_ref)
    yield

    ubuf[POOL_PAD:POOL_PAD + T, :] = u
    d_groups = []
    for g, w in enumerate(POOL_WINDOWS):
        s = ubuf[:, g * POOL_GROUP:(g + 1) * POOL_GROUP]
        sh = 1
        while sh < w:
            s = s + _roll_rows(s, sh)
            sh *= 2
        d_groups.append(s[POOL_PAD:, :] * (1.0 / w) - u[:, g * POOL_GROUP:(g + 1) * POOL_GROUP])
    y_pool = _pool_project(d_groups, w_pool_ref, pscale_ref)
    ubuf[0:POOL_PAD, :] = ubuf[T:T + POOL_PAD, :]
    loga = _gate_log_decay(zr, w_a2_ref, b_a_ref)

    tr = lax.broadcasted_iota(jnp.int32, (C, C), 0)
    tc = lax.broadcasted_iota(jnp.int32, (C, C), 1)
    tri = jnp.where(tc <= tr, 1.0, 0.0).astype(BF16)
    ar = lax.broadcasted_iota(jnp.int32, (C, 2 * C), 0)
    ac = lax.broadcasted_iota(jnp.int32, (C, 2 * C), 1) & (C - 1)
    causal = ac <= ar
    lane128 = lax.broadcasted_iota(jnp.int32, (C, 128), 1)
    lane256 = lax.broadcasted_iota(jnp.int32, (C, 256), 1)
    sr = lax.broadcasted_iota(jnp.int32, (128, 256), 0)
    sc = lax.broadcasted_iota(jnp.int32, (128, 256), 1)
    blockdiag = (sr >= GLA_DK) == (sc >= GLA_DV)
    mid = C // 2 - 1

    n_chunks = T // C
    pairs = [(c, p) for c in range(n_chunks) for p in range(2)]
    ks = [slice(128 * p, 128 * (p + 1)) for p in range(2)]
    vs = [slice(256 * p, 256 * (p + 1)) for p in range(2)]

    bcs = []
    for c in range(n_chunks):
        la_hi, la_lo = _split_bf16(loga[c * C:(c + 1) * C])
        bb = _dot(tri, jnp.concatenate([la_hi, la_lo], axis=1))
        bcs.append(bb[:, :D_GLA_K] + bb[:, D_GLA_K:])
    yield

    q_in, k_in, q_st, k_st, dec_t, vc = [], [], [], [], [], []
    for c in range(n_chunks):
        bc = bcs[c]
        bmid = bc[mid:mid + 1]
        bend = bc[C - 1:C]
        qc = q[c * C:(c + 1) * C]
        kc = k[c * C:(c + 1) * C]
        q_in.append((qc * jnp.exp(bc - bmid)).astype(BF16))
        k_in.append((kc * jnp.exp(bmid - bc)).astype(BF16))
        q_st.append((qc * jnp.exp(bc)).astype(BF16))
        k_st.append((kc * jnp.exp(bend - bc)).astype(BF16))
        dec_t.append(jnp.transpose(jnp.broadcast_to(jnp.exp(bend), (128, D_GLA_K))))
        vc.append(v[c * C:(c + 1) * C].astype(BF16))
    yield

    attn, upd = {}, {}
    for c, p in pairs:
        a = _dot_nt(q_in[c][:, ks[p]], _head_pair_keys(k_in[c][:, ks[p]], lane128))
        attn[c, p] = jnp.where(causal, a, 0.0).astype(BF16)
    for c, p in pairs:
        u_cp = _dot_tn(k_st[c][:, ks[p]], vc[c][:, vs[p]])
        upd[c, p] = jnp.where(blockdiag, u_cp, 0.0)
    yield

    s_vals = [sbd[p] for p in range(2)]
    s_start = {}
    for c, p in pairs:
        s_start[c, p] = s_vals[p].astype(BF16)
        dec_p = dec_t[c][ks[p], :]
        s_vals[p] = jnp.concatenate([dec_p, dec_p], axis=1) * s_vals[p] + upd[c, p]
    for p in range(2):
        sbd[p] = s_vals[p]

    o_rows = [[], []]
    for c, p in pairs:
        vblk = _head_pair_values(vc[c][:, vs[p]], lane256)
        lhs = jnp.concatenate([attn[c, p], q_st[c][:, ks[p]]], axis=1)
        rhs = jnp.concatenate([vblk, s_start[c, p]], axis=0)
        o_rows[p].append(_dot(lhs, rhs))
    o = jnp.concatenate([jnp.concatenate(o_rows[p], axis=0) for p in range(2)], axis=1)
    yield

    y_gla = _gla_output_gate(o, r, gnorm_ref[...])
    x1_ref[...] = _mix_out(x, y_pool, y_gla, w_out_ref, g_ref, b_ref)


def _layer_spec(shape, layer):
    nd = len(shape)
    return pl.BlockSpec((None,) + shape, lambda *_: (layer,) + (0,) * nd, pipeline_mode=pl.Buffered(1))


def _mixer_weight_specs(layer):
    shapes = [(D_MODEL, C_MAIN), (D_MODEL, GATE_RANK), (GATE_RANK, D_GLA_K), (1, D_GLA_K),
              (4, POOL_GROUP, POOL_GROUP), (1, D_POOL), (1, GLA_DV), (D_MODEL, D_MODEL), (1, D_MODEL), (1, D_MODEL)]
    assert len(shapes) == N_MIX_W
    return [_layer_spec(s, layer) for s in shapes]


def _mixer_long(x, pprev, s0, wts, layer):
    B, L, _ = x.shape
    T = min(LONG_TILE, L)
    P = LONG_PAR if B % LONG_PAR == 0 else 1
    assert L % T == 0 and T % LONG_CHUNK == 0
    kern = functools.partial(_mixer_long_kernel, n_par=P, tile=T, chunk=LONG_CHUNK)
    return pl.pallas_call(
        kern,
        grid=(B // P, L // T),
        in_specs=[pl.BlockSpec((P, T, D_MODEL), lambda b, t: (b, t, 0)),
                  pl.BlockSpec((POOL_PAD, D_POOL), lambda b, t: (0, 0)),
                  pl.BlockSpec((GLA_HEADS, GLA_DK, GLA_DV), lambda b, t: (0, 0, 0))]
        + _mixer_weight_specs(layer),
        out_specs=[pl.BlockSpec((P, T, D_MODEL), lambda b, t: (b, t, 0)),
                   pl.BlockSpec((P, POOL_PAD, D_POOL), lambda b, t: (b, 0, 0)),
                   pl.BlockSpec((P, GLA_HEADS, GLA_DK, GLA_DV), lambda b, t: (b, 0, 0, 0))],
        out_shape=[jax.ShapeDtypeStruct((B, L, D_MODEL), F32),
                   jax.ShapeDtypeStruct((B, POOL_PAD, D_POOL), F32),
                   jax.ShapeDtypeStruct((B, GLA_HEADS, GLA_DK, GLA_DV), F32)],
        scratch_shapes=[pltpu.VMEM((P, T + POOL_PAD, D_POOL), F32),
                        pltpu.VMEM((P, 2, 2 * GLA_DK, 2 * GLA_DV), F32)],
        compiler_params=pltpu.CompilerParams(dimension_semantics=("arbitrary", "arbitrary"),
                                             vmem_limit_bytes=VMEM_LIMIT),
        name="mixer_long",
    )(x, pprev, s0, *wts)


def _mixer_meta(x_ref, wts, x1_ref, u_ref, s_ref):
    w_main_ref, w_zr_ref, w_a2_ref, b_a_ref, w_pool_ref, pscale_ref, gnorm_ref, w_out_ref, g_ref, b_ref = wts
    L = N_META
    x = x_ref[...]
    xb = x.astype(BF16)
    u, zr, q, k, v, r = _project_in(xb, w_main_ref, w_zr_ref)
    u_ref[...] = u

    row128 = lax.broadcasted_iota(jnp.int32, (L, POOL_GROUP), 0)
    pos1 = lax.broadcasted_iota(jnp.int32, (L, 1), 0)
    d_groups = []
    for g, w in enumerate(POOL_WINDOWS):
        ug = u[:, g * POOL_GROUP:(g + 1) * POOL_GROUP]
        s = ug
        sh = 1
        while sh < w:
            s = s + jnp.where(row128 >= sh, _roll_rows(s, sh), 0.0)
            sh *= 2
        d_groups.append(s / jnp.minimum(w, pos1 + 1).astype(F32) - ug)
    y_pool = _pool_project(d_groups, w_pool_ref, pscale_ref)

    loga = _gate_log_decay(zr, w_a2_ref, b_a_ref)
    row256 = lax.broadcasted_iota(jnp.int32, (L, D_GLA_K), 0)
    b = loga
    sh = 1
    while sh < L:
        b = b + jnp.where(row256 >= sh, _roll_rows(b, sh), 0.0)
        sh *= 2
    bend = b[L - 1:L]
    q_in = (q * jnp.exp(b)).astype(BF16)
    k_in = (k * jnp.exp(-b)).astype(BF16)
    k_st = (k * jnp.exp(bend - b)).astype(BF16)
    vb = v.astype(BF16)
    ar = lax.broadcasted_iota(jnp.int32, (L, 2 * L), 0)
    ac = lax.broadcasted_iota(jnp.int32, (L, 2 * L), 1) & (L - 1)
    causal = ac <= ar
    lane128 = lax.broadcasted_iota(jnp.int32, (L, 128), 1)
    lane256 = lax.broadcasted_iota(jnp.int32, (L, 256), 1)
    attn = []
    for p in range(2):
        ks = slice(128 * p, 128 * (p + 1))
        a = _dot_nt(q_in[:, ks], _head_pair_keys(k_in[:, ks], lane128))
        attn.append(jnp.where(causal, a, 0.0).astype(BF16))
    o_parts = []
    for p in range(2):
        ks = slice(128 * p, 128 * (p + 1))
        vs = slice(256 * p, 256 * (p + 1))
        s_p = _dot_tn(k_st[:, ks], vb[:, vs])
        s_ref[2 * p] = s_p[0:GLA_DK, 0:GLA_DV]
        s_ref[2 * p + 1] = s_p[GLA_DK:2 * GLA_DK, GLA_DV:2 * GLA_DV]
        o_parts.append(_dot(attn[p], _head_pair_values(vb[:, vs], lane256)))
    o = jnp.concatenate(o_parts, axis=1)

    y_gla = _gla_output_gate(o, r, gnorm_ref[...])
    x1_ref[...] = _mix_out(x, y_pool, y_gla, w_out_ref, g_ref, b_ref)


def _mixer_sample(x_ref, hist_ref, s0_ref, wts, x1_ref, u_ref, snew_ref, *, seq_len):
    w_main_ref, w_zr_ref, w_a2_ref, b_a_ref, w_pool_ref, pscale_ref, gnorm_ref, w_out_ref, g_ref, b_ref = wts
    G, Ls = SEQ_GROUP, seq_len
    R = G * Ls
    NS = G * GLA_DK
    g_shift = G.bit_length() - 1
    x = x_ref[...]
    xb = x.astype(BF16)
    u, zr, q, k, v, r = _project_in(xb, w_main_ref, w_zr_ref)
    u_ref[...] = u

    def blk(a, t):
        return a[t * G:(t + 1) * G]

    d_groups = []
    for g, w in enumerate(POOL_WINDOWS):
        cols = slice(g * POOL_GROUP, (g + 1) * POOL_GROUP)
        ug = u[:, cols]
        suffix = [None]
        acc = None
        for m in range(1, min(w - 1, POOL_BUF) + 1):
            h = hist_ref[(POOL_BUF - m) * G:(POOL_BUF - m + 1) * G, cols]
            acc = h if acc is None else acc + h
            suffix.append(acc)
        parts = []
        for t in range(Ls):
            wsum = blk(ug, t)
            for j in range(max(0, t - w + 1), t):
                wsum = wsum + blk(ug, j)
            m = w - 1 - t
            if m > 0:
                wsum = wsum + suffix[m]
            parts.append(wsum * (1.0 / w) - blk(ug, t))
        d_groups.append(jnp.concatenate(parts, axis=0))
    y_pool = _pool_project(d_groups, w_pool_ref, pscale_ref)

    loga = _gate_log_decay(zr, w_a2_ref, b_a_ref)
    b_t = [blk(loga, 0)]
    for t in range(1, Ls):
        b_t.append(b_t[-1] + blk(loga, t))
    b = jnp.concatenate(b_t, axis=0)
    bend = jnp.concatenate([b_t[-1]] * Ls, axis=0)
    q_in = (q * jnp.exp(b)).astype(BF16)
    k_in = (k * jnp.exp(-b)).astype(BF16)
    k_st = k * jnp.exp(bend - b)
    dec_hi, dec_lo = _split_bf16(jnp.exp(bend))
    tok = lax.broadcasted_iota(jnp.int32, (R, D_GLA_K), 0) >> g_shift
    dec_rows = jnp.where(tok == Ls - 1, dec_hi, jnp.where(tok == Ls - 2, dec_lo, jnp.zeros_like(dec_lo)))
    vb = v.astype(BF16)

    ar = lax.broadcasted_iota(jnp.int32, (R, 2 * R), 0)
    ac = lax.broadcasted_iota(jnp.int32, (R, 2 * R), 1) & (R - 1)
    same_seq_causal = ((ac & (G - 1)) == (ar & (G - 1))) & ((ac >> g_shift) <= (ar >> g_shift))
    lane128 = lax.broadcasted_iota(jnp.int32, (R, 128), 1)
    lane256 = lax.broadcasted_iota(jnp.int32, (R, 256), 1)
    br = lax.broadcasted_iota(jnp.int32, (R, NS), 0)
    bcol = lax.broadcasted_iota(jnp.int32, (R, NS), 1)
    own_state = (bcol >> 6) == (br & (G - 1))
    ones_blk = jnp.ones((R, GLA_DV), BF16)
    zeros_blk = jnp.zeros((R, GLA_DV), BF16)

    def expand(xp, first):
        sw = pltpu.roll(xp, GLA_DK, 1)
        two = jnp.where(lane128 < GLA_DK, xp, sw) if first else jnp.where(lane128 < GLA_DK, sw, xp)
        rep = jnp.concatenate([two] * (NS // 128), axis=1)
        return jnp.where(own_state, rep, 0.0).astype(BF16)

    ks = [slice(128 * p, 128 * (p + 1)) for p in range(2)]
    vs = [slice(256 * p, 256 * (p + 1)) for p in range(2)]
    attn = []
    for p in range(2):
        a = _dot_nt(q_in[:, ks[p]], _head_pair_keys(k_in[:, ks[p]], lane128))
        attn.append(jnp.where(same_seq_causal, a, 0.0).astype(BF16))
    inter = []
    for h in range(GLA_HEADS):
        p, first = h // 2, h % 2 == 0
        s_flat = s0_ref[:, h].reshape(NS, GLA_DV)
        inter.append(_dot(expand(q_in[:, ks[p]].astype(F32), first), s_flat.astype(BF16)))
    for h in range(GLA_HEADS):
        p, first = h // 2, h % 2 == 0
        s_flat = s0_ref[:, h].reshape(NS, GLA_DV)
        lhs = jnp.concatenate([expand(k_st[:, ks[p]], first),
                               expand(dec_rows[:, ks[p]].astype(F32), first)], axis=0)
        v_h = vb[:, h * GLA_DV:(h + 1) * GLA_DV]
        rhs = jnp.concatenate([jnp.concatenate([v_h, zeros_blk], axis=1),
                               jnp.concatenate([zeros_blk, ones_blk], axis=1)], axis=0)
        ud = _dot_tn(lhs, rhs)
        s_new = ud[:, GLA_DV:] * s_flat + ud[:, :GLA_DV]
        snew_ref[:, h] = s_new.reshape(G, GLA_DK, GLA_DV)
    o_parts = []
    for p in range(2):
        o_intra = _dot(attn[p], _head_pair_values(vb[:, vs[p]], lane256))
        o_parts.append(o_intra + jnp.concatenate(inter[2 * p:2 * p + 2], axis=1))
    o = jnp.concatenate(o_parts, axis=1)

    y_gla = _gla_output_gate(o, r, gnorm_ref[...])
    x1_ref[...] = _mix_out(x, y_pool, y_gla, w_out_ref, g_ref, b_ref)


def _mixer_short_kernel(*refs, seq_len):
    xm_ref, xs_ref, hist_ref, s0_ref = refs[:4]
    wts = refs[4:4 + N_MIX_W]
    x1m_ref, um_ref, sm_ref, x1s_ref, us_ref, ss_ref = refs[-6:]
    i = pl.program_id(0)

    @pl.when(i == 0)
    def _meta():
        _mixer_meta(xm_ref, wts, x1m_ref, um_ref, sm_ref)

    @pl.when(i > 0)
    def _sample():
        _mixer_sample(xs_ref, hist_ref, s0_ref, wts, x1s_ref, us_ref, ss_ref, seq_len=seq_len)


def _mixer_short(x_meta, x_samp, hist, s0, wts, layer, *, seq_len, s_out=None):
    rows = x_samp.shape[0]
    G = SEQ_GROUP
    R = G * seq_len
    n_tiles = rows // R
    assert rows % R == 0
    clamp = lambda i: jnp.maximum(i - 1, 0)
    n_in = 4 + N_MIX_W
    in_specs = ([pl.BlockSpec((N_META, D_MODEL), lambda i: (0, 0)),
                 pl.BlockSpec((R, D_MODEL), lambda i: (clamp(i), 0)),
                 pl.BlockSpec((None, G * POOL_BUF, D_POOL), lambda i: (layer, clamp(i), 0)),
                 pl.BlockSpec((None, G, GLA_HEADS, GLA_DK, GLA_DV), lambda i: (layer, clamp(i), 0, 0, 0))]
                + _mixer_weight_specs(layer))
    args = [x_meta, x_samp, hist, s0, *wts]
    aliases = {}
    if s_out is not None:
        in_specs.append(pl.BlockSpec(memory_space=pl.ANY))
        args.append(s_out)
        aliases = {n_in: 5}
    body = functools.partial(_mixer_short_kernel, seq_len=seq_len)

    def kern(*refs):
        body(*refs[:n_in], *refs[len(args):])

    return pl.pallas_call(
        kern,
        grid=(n_tiles + 1,),
        in_specs=in_specs,
        out_specs=[pl.BlockSpec((N_META, D_MODEL), lambda i: (0, 0)),
                   pl.BlockSpec((N_META, D_POOL), lambda i: (0, 0)),
                   pl.BlockSpec((GLA_HEADS, GLA_DK, GLA_DV), lambda i: (0, 0, 0)),
                   pl.BlockSpec((R, D_MODEL), lambda i: (clamp(i), 0)),
                   pl.BlockSpec((R, D_POOL), lambda i: (clamp(i), 0)),
                   pl.BlockSpec((None, G, GLA_HEADS, GLA_DK, GLA_DV), lambda i: (layer, clamp(i), 0, 0, 0))],
        out_shape=[jax.ShapeDtypeStruct((N_META, D_MODEL), F32),
                   jax.ShapeDtypeStruct((N_META, D_POOL), F32),
                   jax.ShapeDtypeStruct((GLA_HEADS, GLA_DK, GLA_DV), F32),
                   jax.ShapeDtypeStruct((rows, D_MODEL), F32),
                   jax.ShapeDtypeStruct((rows, D_POOL), F32),
                   jax.ShapeDtypeStruct(s0.shape, F32)],
        input_output_aliases=aliases,
        compiler_params=pltpu.CompilerParams(dimension_semantics=("arbitrary",),
                                             vmem_limit_bytes=VMEM_LIMIT),
        name="mixer_short",
    )(*args)


def _ffn_tile(x_ref, wts, y_ref, conv_inputs, store_gate):
    w_up_ref, w_gate_ref, cw_ref, cb_ref, w_down_ref, g_ref, b_ref = wts
    x = x_ref[...]
    xb = x.astype(BF16)

    acts = []
    for j in range(D_FF // FF_CHUNK):
        cs = slice(j * FF_CHUNK, (j + 1) * FF_CHUNK)
        a = _dot(xb, w_up_ref[:, cs])
        gt = _dot(xb, w_gate_ref[:, cs])
        g1, g2 = conv_inputs(gt, cs)
        store_gate(gt, cs)
        gc = cb_ref[:, cs] + cw_ref[0:1, cs] * g2 + cw_ref[1:2, cs] * g1 + cw_ref[2:3, cs] * gt
        acts.append((a * _silu(gc)).astype(BF16))
    yield

    f = _dot(jnp.concatenate(acts, axis=1), w_down_ref[...])
    yield

    y_ref[...] = _layer_norm(ALPHA * x + f, g_ref[...], b_ref[...])


def _ffn_long_kernel(*refs, n_par):
    x_ref, cprev_ref = refs[:2]
    wts = refs[2:2 + N_FFN_W]
    y_ref, hist_out_ref, gbuf = refs[2 + N_FFN_W:]
    t = pl.program_id(1)
    T = x_ref.shape[1]

    @pl.when(t == 0)
    def _init():
        for j in range(n_par):
            gbuf[j, 0:CONV_PAD, :] = cprev_ref[...]

    def make_tile(j):
        def conv_inputs(gt, cs):
            gbuf[j, CONV_PAD:CONV_PAD + T, cs] = gt
            return gbuf[j, CONV_PAD - 1:CONV_PAD - 1 + T, cs], gbuf[j, CONV_PAD - 2:CONV_PAD - 2 + T, cs]

        def store_gate(gt, cs):
            gbuf[j, 0:CONV_PAD, cs] = gbuf[j, T:T + CONV_PAD, cs]

        return _ffn_tile(x_ref.at[j], wts, y_ref.at[j], conv_inputs, store_gate)

    for _ in itertools.zip_longest(*[make_tile(j) for j in range(n_par)]):
        pass

    @pl.when(t == pl.num_programs(1) - 1)
    def _final():
        for j in range(n_par):
            hist_out_ref[j] = gbuf[j, 0:CONV_PAD, :]


def _ffn_short_kernel(*refs, seq_len):
    xm_ref, xs_ref, hist_ref = refs[:3]
    wts = refs[3:3 + N_FFN_W]
    ym_ref, cm_ref, ys_ref, cs_ref = refs[3 + N_FFN_W:]
    G, Ls = SEQ_GROUP, seq_len
    R = G * Ls
    rows = xs_ref.shape[0]
    n_tiles = rows // R

    rowm = lax.broadcasted_iota(jnp.int32, (N_META, FF_CHUNK), 0)

    def meta_conv(gt, cs):
        return (jnp.where(rowm >= 1, _roll_rows(gt, 1), 0.0), jnp.where(rowm >= 2, _roll_rows(gt, 2), 0.0))

    def meta_store(gt, cs):
        cm_ref[:, cs] = gt[N_META - CONV_PAD:N_META]

    tok = (lax.broadcasted_iota(jnp.int32, (rows, FF_CHUNK), 0) & (R - 1)) >> (G.bit_length() - 1)
    zeros_tail = jnp.zeros((R - CONV_BUF * G, FF_CHUNK), F32)

    def sample_conv(gt, cs):
        hx = jnp.concatenate([piece for n in range(n_tiles)
                              for piece in (hist_ref[n * CONV_BUF * G:(n + 1) * CONV_BUF * G, cs], zeros_tail)], axis=0)
        g1 = jnp.where(tok >= 1, _roll_rows(gt, G), _roll_rows(hx, -G))
        g2 = jnp.where(tok >= 2, _roll_rows(gt, 2 * G), hx)
        return g1, g2

    def sample_store(gt, cs):
        for n in range(n_tiles):
            cs_ref[n * CONV_BUF * G:(n + 1) * CONV_BUF * G, cs] = gt[n * R + (Ls - CONV_BUF) * G:(n + 1) * R]

    tiles = [_ffn_tile(xs_ref, wts, ys_ref, sample_conv, sample_store),
             _ffn_tile(xm_ref, wts, ym_ref, meta_conv, meta_store)]
    for _ in itertools.zip_longest(*tiles):
        pass


def _ffn_weight_specs(layer):
    shapes = [(D_MODEL, D_FF), (D_MODEL, D_FF), (3, D_FF), (1, D_FF), (D_FF, D_MODEL), (1, D_MODEL), (1, D_MODEL)]
    assert len(shapes) == N_FFN_W
    return [_layer_spec(s, layer) for s in shapes]


def _ffn_long(x, cprev, wts, layer):
    B, L, _ = x.shape
    T = min(FFN_TILE, L)
    P = LONG_PAR if B % LONG_PAR == 0 else 1
    assert L % T == 0
    kern = functools.partial(_ffn_long_kernel, n_par=P)
    return pl.pallas_call(
        kern,
        grid=(B // P, L // T),
        in_specs=[pl.BlockSpec((P, T, D_MODEL), lambda b, t: (b, t, 0)),
                  pl.BlockSpec((CONV_PAD, D_FF), lambda b, t: (0, 0))] + _ffn_weight_specs(layer),
        out_specs=[pl.BlockSpec((P, T, D_MODEL), lambda b, t: (b, t, 0)),
                   pl.BlockSpec((P, CONV_PAD, D_FF), lambda b, t: (b, 0, 0))],
        out_shape=[jax.ShapeDtypeStruct((B, L, D_MODEL), F32),
                   jax.ShapeDtypeStruct((B, CONV_PAD, D_FF), F32)],
        scratch_shapes=[pltpu.VMEM((P, T + CONV_PAD, D_FF), F32)],
        compiler_params=pltpu.CompilerParams(dimension_semantics=("arbitrary", "arbitrary"),
                                             vmem_limit_bytes=VMEM_LIMIT),
        name="ffn_long",
    )(x, cprev, *wts)


def _ffn_short(x_meta, x_samp, hist, wts, layer, *, seq_len):
    rows = x_samp.shape[0]
    n_hist = hist.shape[1]
    kern = functools.partial(_ffn_short_kernel, seq_len=seq_len)
    return pl.pallas_call(
        kern,
        grid=(1,),
        in_specs=[pl.BlockSpec((N_META, D_MODEL), lambda i: (0, 0)),
                  pl.BlockSpec((rows, D_MODEL), lambda i: (0, 0)),
                  pl.BlockSpec((None, n_hist, D_FF), lambda i: (layer, 0, 0))] + _ffn_weight_specs(layer),
        out_specs=[pl.BlockSpec((N_META, D_MODEL), lambda i: (0, 0)),
                   pl.BlockSpec((CONV_PAD, D_FF), lambda i: (0, 0)),
                   pl.BlockSpec((rows, D_MODEL), lambda i: (0, 0)),
                   pl.BlockSpec((n_hist, D_FF), lambda i: (0, 0))],
        out_shape=[jax.ShapeDtypeStruct((N_META, D_MODEL), F32),
                   jax.ShapeDtypeStruct((CONV_PAD, D_FF), F32),
                   jax.ShapeDtypeStruct((rows, D_MODEL), F32),
                   jax.ShapeDtypeStruct((n_hist, D_FF), F32)],
        compiler_params=pltpu.CompilerParams(dimension_semantics=("arbitrary",),
                                             vmem_limit_bytes=VMEM_LIMIT),
        name="ffn_short",
    )(x_meta, x_samp, hist, *wts)


def _to_group_major(a, axis):
    n, j = a.shape[axis], a.shape[axis + 1]
    lead, tail = a.shape[:axis], a.shape[axis + 2:]
    a = a.reshape(*lead, n // SEQ_GROUP, SEQ_GROUP, j, *tail)
    a = jnp.swapaxes(a, axis + 1, axis + 2)
    return a.reshape(*lead, n * j, *tail)


def _from_group_major(a, axis, j):
    rows = a.shape[axis]
    n = rows // j
    lead, tail = a.shape[:axis], a.shape[axis + 1:]
    a = a.reshape(*lead, n // SEQ_GROUP, j, SEQ_GROUP, *tail)
    a = jnp.swapaxes(a, axis + 1, axis + 2)
    return a.reshape(*lead, n, j, *tail)


def kernel(x_prompt, x_sample, state_pool, state_gla, state_conv, meta_tokens,
           w_in, w_a2, b_a, w_pool, pool_scale, gla_norm, w_out, ln1_g, ln1_b,
           w_up, w_gate, conv_w, conv_b, w_down, ln2_g, ln2_b):
    NB, LS = x_sample.shape[0], x_sample.shape[1]
    assert NB % SEQ_GROUP == 0 and LS & (LS - 1) == 0 and CONV_BUF <= LS <= POOL_BUF

    mix_w = (w_in[:, :, :C_MAIN].astype(BF16), w_in[:, :, C_MAIN:].astype(BF16), w_a2.astype(BF16), b_a[:, None],
             w_pool.astype(BF16), pool_scale[:, None], gla_norm[:, None], w_out.astype(BF16),
             ln1_g[:, None], ln1_b[:, None])
    ffn_w = (w_up.astype(BF16), w_gate.astype(BF16), conv_w, conv_b[:, None], w_down.astype(BF16),
             ln2_g[:, None], ln2_b[:, None])

    hm = meta_tokens.astype(F32)
    hp = x_prompt
    hs = _to_group_major(x_sample, 0)
    pool_hist = _to_group_major(state_pool, 1)
    conv_hist = _to_group_major(state_conv, 1)

    pp, gp, cp, us_l, cs_l = [], [], [], [], []
    gs = None
    for l in range(DEPTH):
        hm1, um, sm, hs1, us, gs = _mixer_short(hm, hs, pool_hist, state_gla, mix_w, l, seq_len=LS, s_out=gs)
        hm, cm, hs, cs_new = _ffn_short(hm1, hs1, conv_hist, ffn_w, l, seq_len=LS)
        us_l.append(us)
        cs_l.append(cs_new)

        hp1, pbuf, snew = _mixer_long(hp, um, sm, mix_w, l)
        hp, cbuf = _ffn_long(hp1, cm, ffn_w, l)
        pp.append(pbuf[:, 1:])
        gp.append(snew)
        cp.append(cbuf[:, CONV_PAD - CONV_BUF:])

    ps = jnp.concatenate([state_pool[:, :, LS:], _from_group_major(jnp.stack(us_l), 1, LS)], axis=2)
    cs = _from_group_major(jnp.stack(cs_l), 1, CONV_BUF)
    return (hp, _from_group_major(hs, 0, LS), jnp.stack(pp), jnp.stack(gp), jnp.stack(cp), ps, gs, cs)
```

```python
import functools
import itertools

import jax
import jax.numpy as jnp
from jax import lax
from jax.experimental import pallas as pl
from jax.experimental.pallas import tpu as pltpu

F32 = jnp.float32
BF16 = jnp.bfloat16

D_MODEL = 1024
N_META = 16
D_POOL = 512
POOL_WINDOWS = (2, 4, 8, 16)
POOL_GROUP = 128
POOL_BUF = 15
POOL_PAD = 16
D_GLA = 512
GLA_HEADS = 4
GLA_DV = 128
GLA_DK = 64
D_GLA_K = 256
GATE_RANK = 16
GATE_TAU = 16.0
D_FF = 2816
CONV_BUF = 2
CONV_PAD = 8
DEPTH = 2
ALPHA = (2 * DEPTH) ** 0.25
LN_EPS = 1e-5
RMS_EPS = 1e-6
PAST_LEN = 16384

C_POOL, C_Q, C_K, C_V, C_R, C_Z, C_END = 0, 512, 768, 1024, 1536, 2048, 2064

LONG_TILE = 512
LONG_CHUNK = 64
LONG_PAR = 2
FFN_TILE = 512
SEQ_GROUP = 16
FF_CHUNK = 256
VMEM_LIMIT = 56 * 1024 * 1024

N_MIX_W = 9
N_FFN_W = 7


def _dot(a, b):
    return jnp.dot(a, b, preferred_element_type=F32)


def _dot_nt(a, b):
    return lax.dot_general(a, b, (((1,), (1,)), ((), ())), preferred_element_type=F32)


def _dot_tn(a, b):
    return lax.dot_general(a, b, (((0,), (0,)), ((), ())), preferred_element_type=F32)


def _layer_norm(y, g, b):
    mu = jnp.mean(y, axis=-1, keepdims=True)
    yc = y - mu
    var = jnp.mean(yc * yc, axis=-1, keepdims=True)
    return yc * lax.rsqrt(var + LN_EPS) * g + b


def _silu(x):
    return x * (1.0 / (1.0 + jnp.exp(-x)))


def _log_sigmoid(z):
    return jnp.minimum(z, 0.0) - jnp.log(1.0 + jnp.exp(-jnp.abs(z)))


def _roll_rows(x, shift):
    n = x.shape[0]
    return pltpu.roll(x, shift % n, 0)


def _split_bf16(x):
    hi = x.astype(BF16)
    lo = (x - hi.astype(F32)).astype(BF16)
    return hi, lo


def _project_in(xb, w_in_ref):
    u = _dot(xb, w_in_ref[:, C_POOL:C_Q])
    zr = _dot(xb, w_in_ref[:, C_Z:C_END])
    q = _dot(xb, w_in_ref[:, C_Q:C_K]) * (GLA_DK ** -0.5)
    k = _dot(xb, w_in_ref[:, C_K:C_V])
    v = _dot(xb, w_in_ref[:, C_V:C_R])
    r = _dot(xb, w_in_ref[:, C_R:C_Z])
    return u, zr, q, k, v, r


def _gate_log_decay(zr, w_a2_ref, b_a_ref):
    z = _dot(zr.astype(BF16), w_a2_ref[...]) + b_a_ref[...]
    return _log_sigmoid(z) * (1.0 / GATE_TAU)


def _gla_output_gate(o, r, gnorm):
    parts = []
    for h in range(GLA_HEADS):
        oh = o[:, h * GLA_DV:(h + 1) * GLA_DV]
        ms = jnp.mean(oh * oh, axis=-1, keepdims=True)
        parts.append(oh * lax.rsqrt(ms + RMS_EPS) * gnorm)
    return jnp.concatenate(parts, axis=1) * _silu(r)


def _pool_project(d_groups, w_pool_ref, pscale_ref):
    ys = [_dot(d.astype(BF16), w_pool_ref[g]) for g, d in enumerate(d_groups)]
    return jnp.concatenate(ys, axis=1) * pscale_ref[...]


def _mix_out(x, y_pool, y_gla, w_out_ref, g_ref, b_ref):
    mix = jnp.concatenate([y_pool, y_gla], axis=1).astype(BF16)
    return _layer_norm(ALPHA * x + _dot(mix, w_out_ref[...]), g_ref[...], b_ref[...])


def _head_pair_keys(k_p, lane128):
    zk = jnp.zeros_like(k_p)
    return jnp.concatenate([jnp.where(lane128 < GLA_DK, k_p, zk), jnp.where(lane128 >= GLA_DK, k_p, zk)], axis=0)


def _head_pair_values(v_p, lane256):
    zv = jnp.zeros_like(v_p)
    return jnp.concatenate([jnp.where(lane256 < GLA_DV, v_p, zv), jnp.where(lane256 >= GLA_DV, v_p, zv)], axis=0)


def _mixer_long_kernel(*refs, n_par, tile, chunk):
    x_ref, pprev_ref, s0_ref = refs[:3]
    wts = refs[3:3 + N_MIX_W]
    x1_ref, pbuf_ref, snew_ref, ubuf, sbd = refs[3 + N_MIX_W:]
    t = pl.program_id(1)
    zero_blk = jnp.zeros((GLA_DK, GLA_DV), F32)

    @pl.when(t == 0)
    def _init():
        for j in range(n_par):
            ubuf[j, 0:POOL_PAD, :] = pprev_ref[...]
            for p in range(2):
                top = jnp.concatenate([s0_ref[2 * p], zero_blk], axis=1)
                bot = jnp.concatenate([zero_blk, s0_ref[2 * p + 1]], axis=1)
                sbd[j, p] = jnp.concatenate([top, bot], axis=0)

    tiles = [_mixer_long_tile(x_ref.at[j], wts, x1_ref.at[j], ubuf.at[j], sbd.at[j], tile=tile, chunk=chunk)
             for j in range(n_par)]
    for _ in itertools.zip_longest(*tiles):
        pass

    @pl.when(t == pl.num_programs(1) - 1)
    def _final():
        for j in range(n_par):
            pbuf_ref[j] = ubuf[j, 0:POOL_PAD, :]
            for p in range(2):
                s_p = sbd[j, p]
                snew_ref[j, 2 * p] = s_p[0:GLA_DK, 0:GLA_DV]
                snew_ref[j, 2 * p + 1] = s_p[GLA_DK:2 * GLA_DK, GLA_DV:2 * GLA_DV]


def _mixer_long_tile(x_ref, wts, x1_ref, ubuf, sbd, *, tile, chunk):
    w_in_ref, w_a2_ref, b_a_ref, w_pool_ref, pscale_ref, gnorm_ref, w_out_ref, g_ref, b_ref = wts
    T, C = tile, chunk
    x = x_ref[...]
    xb = x.astype(BF16)

    u, zr, q, k, v, r = _project_in(xb, w_in_ref)
    yield

    ubuf[POOL_PAD:POOL_PAD + T, :] = u
    d_groups = []
    for g, w in enumerate(POOL_WINDOWS):
        s = ubuf[:, g * POOL_GROUP:(g + 1) * POOL_GROUP]
        sh = 1
        while sh < w:
            s = s + _roll_rows(s, sh)
            sh *= 2
        d_groups.append(s[POOL_PAD:, :] * (1.0 / w) - u[:, g * POOL_GROUP:(g + 1) * POOL_GROUP])
    y_pool = _pool_project(d_groups, w_pool_ref, pscale_ref)
    ubuf[0:POOL_PAD, :] = ubuf[T:T + POOL_PAD, :]
    loga = _gate_log_decay(zr, w_a2_ref, b_a_ref)

    tr = lax.broadcasted_iota(jnp.int32, (C, C), 0)
    tc = lax.broadcasted_iota(jnp.int32, (C, C), 1)
    tri = jnp.where(tc <= tr, 1.0, 0.0).astype(BF16)
    ar = lax.broadcasted_iota(jnp.int32, (C, 2 * C), 0)
    ac = lax.broadcasted_iota(jnp.int32, (C, 2 * C), 1) & (C - 1)
    causal = ac <= ar
    lane128 = lax.broadcasted_iota(jnp.int32, (C, 128), 1)
    lane256 = lax.broadcasted_iota(jnp.int32, (C, 256), 1)
    sr = lax.broadcasted_iota(jnp.int32, (128, 256), 0)
    sc = lax.broadcasted_iota(jnp.int32, (128, 256), 1)
    blockdiag = (sr >= GLA_DK) == (sc >= GLA_DV)
    mid = C // 2 - 1

    n_chunks = T // C
    pairs = [(c, p) for c in range(n_chunks) for p in range(2)]
    ks = [slice(128 * p, 128 * (p + 1)) for p in range(2)]
    vs = [slice(256 * p, 256 * (p + 1)) for p in range(2)]

    bcs = []
    for c in range(n_chunks):
        la_hi, la_lo = _split_bf16(loga[c * C:(c + 1) * C])
        bb = _dot(tri, jnp.concatenate([la_hi, la_lo], axis=1))
        bcs.append(bb[:, :D_GLA_K] + bb[:, D_GLA_K:])
    yield

    q_in, k_in, q_st, k_st, dec_t, vc = [], [], [], [], [], []
    for c in range(n_chunks):
        bc = bcs[c]
        bmid = bc[mid:mid + 1]
        bend = bc[C - 1:C]
        qc = q[c * C:(c + 1) * C]
        kc = k[c * C:(c + 1) * C]
        q_in.append((qc * jnp.exp(bc - bmid)).astype(BF16))
        k_in.append((kc * jnp.exp(bmid - bc)).astype(BF16))
        q_st.append((qc * jnp.exp(bc)).astype(BF16))
        k_st.append((kc * jnp.exp(bend - bc)).astype(BF16))
        dec_t.append(jnp.transpose(jnp.broadcast_to(jnp.exp(bend), (128, D_GLA_K))))
        vc.append(v[c * C:(c + 1) * C].astype(BF16))
    yield

    attn, upd = {}, {}
    for c, p in pairs:
        a = _dot_nt(q_in[c][:, ks[p]], _head_pair_keys(k_in[c][:, ks[p]], lane128))
        attn[c, p] = jnp.where(causal, a, 0.0).astype(BF16)
    for c, p in pairs:
        u_cp = _dot_tn(k_st[c][:, ks[p]], vc[c][:, vs[p]])
        upd[c, p] = jnp.where(blockdiag, u_cp, 0.0)
    yield

    s_vals = [sbd[p] for p in range(2)]
    s_start = {}
    for c, p in pairs:
        s_start[c, p] = s_vals[p].astype(BF16)
        dec_p = dec_t[c][ks[p], :]
        s_vals[p] = jnp.concatenate([dec_p, dec_p], axis=1) * s_vals[p] + upd[c, p]
    for p in range(2):
        sbd[p] = s_vals[p]

    o_rows = [[], []]
    for c, p in pairs:
        vblk = _head_pair_values(vc[c][:, vs[p]], lane256)
        lhs = jnp.concatenate([attn[c, p], q_st[c][:, ks[p]]], axis=1)
        rhs = jnp.concatenate([vblk, s_start[c, p]], axis=0)
        o_rows[p].append(_dot(lhs, rhs))
    o = jnp.concatenate([jnp.concatenate(o_rows[p], axis=0) for p in range(2)], axis=1)
    yield

    y_gla = _gla_output_gate(o, r, gnorm_ref[...])
    x1_ref[...] = _mix_out(x, y_pool, y_gla, w_out_ref, g_ref, b_ref)


def _layer_spec(shape, layer):
    nd = len(shape)
    return pl.BlockSpec((None,) + shape, lambda *_: (layer,) + (0,) * nd, pipeline_mode=pl.Buffered(1))


def _mixer_weight_specs(layer):
    shapes = [(D_MODEL, C_END), (GATE_RANK, D_GLA_K), (1, D_GLA_K),
              (4, POOL_GROUP, POOL_GROUP), (1, D_POOL), (1, GLA_DV), (D_MODEL, D_MODEL), (1, D_MODEL), (1, D_MODEL)]
    assert len(shapes) == N_MIX_W
    return [_layer_spec(s, layer) for s in shapes]


def _mixer_long(x, pprev, s0, wts, layer):
    B, L, _ = x.shape
    T = min(LONG_TILE, L)
    P = LONG_PAR if B % LONG_PAR == 0 else 1
    assert L % T == 0 and T % LONG_CHUNK == 0
    kern = functools.partial(_mixer_long_kernel, n_par=P, tile=T, chunk=LONG_CHUNK)
    return pl.pallas_call(
        kern,
        grid=(B // P, L // T),
        in_specs=[pl.BlockSpec((P, T, D_MODEL), lambda b, t: (b, t, 0)),
                  pl.BlockSpec((POOL_PAD, D_POOL), lambda b, t: (0, 0)),
                  pl.BlockSpec((GLA_HEADS, GLA_DK, GLA_DV), lambda b, t: (0, 0, 0))]
        + _mixer_weight_specs(layer),
        out_specs=[pl.BlockSpec((P, T, D_MODEL), lambda b, t: (b, t, 0)),
                   pl.BlockSpec((P, POOL_PAD, D_POOL), lambda b, t: (b, 0, 0)),
                   pl.BlockSpec((P, GLA_HEADS, GLA_DK, GLA_DV), lambda b, t: (b, 0, 0, 0))],
        out_shape=[jax.ShapeDtypeStruct((B, L, D_MODEL), F32),
                   jax.ShapeDtypeStruct((B, POOL_PAD, D_POOL), F32),
                   jax.ShapeDtypeStruct((B, GLA_HEADS, GLA_DK, GLA_DV), F32)],
        scratch_shapes=[pltpu.VMEM((P, T + POOL_PAD, D_POOL), F32),
                        pltpu.VMEM((P, 2, 2 * GLA_DK, 2 * GLA_DV), F32)],
        compiler_params=pltpu.CompilerParams(dimension_semantics=("arbitrary", "arbitrary"),
                                             vmem_limit_bytes=VMEM_LIMIT),
        name="mixer_long",
    )(x, pprev, s0, *wts)


def _mixer_meta(x_ref, wts, x1_ref, u_ref, s_ref):
    w_in_ref, w_a2_ref, b_a_ref, w_pool_ref, pscale_ref, gnorm_ref, w_out_ref, g_ref, b_ref = wts
    L = N_META
    x = x_ref[...]
    xb = x.astype(BF16)
    u, zr, q, k, v, r = _project_in(xb, w_in_ref)
    u_ref[...] = u

    row128 = lax.broadcasted_iota(jnp.int32, (L, POOL_GROUP), 0)
    pos1 = lax.broadcasted_iota(jnp.int32, (L, 1), 0)
    d_groups = []
    for g, w in enumerate(POOL_WINDOWS):
        ug = u[:, g * POOL_GROUP:(g + 1) * POOL_GROUP]
        s = ug
        sh = 1
        while sh < w:
            s = s + jnp.where(row128 >= sh, _roll_rows(s, sh), 0.0)
            sh *= 2
        d_groups.append(s / jnp.minimum(w, pos1 + 1).astype(F32) - ug)
    y_pool = _pool_project(d_groups, w_pool_ref, pscale_ref)

    loga = _gate_log_decay(zr, w_a2_ref, b_a_ref)
    row256 = lax.broadcasted_iota(jnp.int32, (L, D_GLA_K), 0)
    b = loga
    sh = 1
    while sh < L:
        b = b + jnp.where(row256 >= sh, _roll_rows(b, sh), 0.0)
        sh *= 2
    bend = b[L - 1:L]
    q_in = (q * jnp.exp(b)).astype(BF16)
    k_in = (k * jnp.exp(-b)).astype(BF16)
    k_st = (k * jnp.exp(bend - b)).astype(BF16)
    vb = v.astype(BF16)
    ar = lax.broadcasted_iota(jnp.int32, (L, 2 * L), 0)
    ac = lax.broadcasted_iota(jnp.int32, (L, 2 * L), 1) & (L - 1)
    causal = ac <= ar
    lane128 = lax.broadcasted_iota(jnp.int32, (L, 128), 1)
    lane256 = lax.broadcasted_iota(jnp.int32, (L, 256), 1)
    attn = []
    for p in range(2):
        ks = slice(128 * p, 128 * (p + 1))
        a = _dot_nt(q_in[:, ks], _head_pair_keys(k_in[:, ks], lane128))
        attn.append(jnp.where(causal, a, 0.0).astype(BF16))
    o_parts = []
    for p in range(2):
        ks = slice(128 * p, 128 * (p + 1))
        vs = slice(256 * p, 256 * (p + 1))
        s_p = _dot_tn(k_st[:, ks], vb[:, vs])
        s_ref[2 * p] = s_p[0:GLA_DK, 0:GLA_DV]
        s_ref[2 * p + 1] = s_p[GLA_DK:2 * GLA_DK, GLA_DV:2 * GLA_DV]
        o_parts.append(_dot(attn[p], _head_pair_values(vb[:, vs], lane256)))
    o = jnp.concatenate(o_parts, axis=1)

    y_gla = _gla_output_gate(o, r, gnorm_ref[...])
    x1_ref[...] = _mix_out(x, y_pool, y_gla, w_out_ref, g_ref, b_ref)


def _mixer_sample(x_ref, hist_ref, s0_ref, wts, x1_ref, hist_out_ref, snew_ref, *, seq_len):
    w_in_ref, w_a2_ref, b_a_ref, w_pool_ref, pscale_ref, gnorm_ref, w_out_ref, g_ref, b_ref = wts
    G, Ls = SEQ_GROUP, seq_len
    R = G * Ls
    NS = G * GLA_DK
    g_shift = G.bit_length() - 1
    x = x_ref[...]
    xb = x.astype(BF16)
    u, zr, q, k, v, r = _project_in(xb, w_in_ref)
    hist_out_ref[0:(POOL_BUF - Ls) * G, :] = hist_ref[R:POOL_BUF * G, :]
    hist_out_ref[(POOL_BUF - Ls) * G:POOL_BUF * G, :] = u

    def blk(a, t):
        return a[t * G:(t + 1) * G]

    d_groups = []
    for g, w in enumerate(POOL_WINDOWS):
        cols = slice(g * POOL_GROUP, (g + 1) * POOL_GROUP)
        ug = u[:, cols]
        suffix = [None]
        acc = None
        for m in range(1, min(w - 1, POOL_BUF) + 1):
            h = hist_ref[(POOL_BUF - m) * G:(POOL_BUF - m + 1) * G, cols]
            acc = h if acc is None else acc + h
            suffix.append(acc)
        parts = []
        for t in range(Ls):
            wsum = blk(ug, t)
            for j in range(max(0, t - w + 1), t):
                wsum = wsum + blk(ug, j)
            m = w - 1 - t
            if m > 0:
                wsum = wsum + suffix[m]
            parts.append(wsum * (1.0 / w) - blk(ug, t))
        d_groups.append(jnp.concatenate(parts, axis=0))
    y_pool = _pool_project(d_groups, w_pool_ref, pscale_ref)

    loga = _gate_log_decay(zr, w_a2_ref, b_a_ref)
    b_t = [blk(loga, 0)]
    for t in range(1, Ls):
        b_t.append(b_t[-1] + blk(loga, t))
    b = jnp.concatenate(b_t, axis=0)
    bend = jnp.concatenate([b_t[-1]] * Ls, axis=0)
    q_in = (q * jnp.exp(b)).astype(BF16)
    k_in = (k * jnp.exp(-b)).astype(BF16)
    k_st = k * jnp.exp(bend - b)
    dec_hi, dec_lo = _split_bf16(jnp.exp(bend))
    tok = lax.broadcasted_iota(jnp.int32, (R, D_GLA_K), 0) >> g_shift
    dec_rows = jnp.where(tok == Ls - 1, dec_hi, jnp.where(tok == Ls - 2, dec_lo, jnp.zeros_like(dec_lo)))
    vb = v.astype(BF16)

    ar = lax.broadcasted_iota(jnp.int32, (R, 2 * R), 0)
    ac = lax.broadcasted_iota(jnp.int32, (R, 2 * R), 1) & (R - 1)
    same_seq_causal = ((ac & (G - 1)) == (ar & (G - 1))) & ((ac >> g_shift) <= (ar >> g_shift))
    lane128 = lax.broadcasted_iota(jnp.int32, (R, 128), 1)
    lane256 = lax.broadcasted_iota(jnp.int32, (R, 256), 1)
    br = lax.broadcasted_iota(jnp.int32, (R, NS), 0)
    bcol = lax.broadcasted_iota(jnp.int32, (R, NS), 1)
    own_state = (bcol >> 6) == (br & (G - 1))
    ones_blk = jnp.ones((R, GLA_DV), BF16)
    zeros_blk = jnp.zeros((R, GLA_DV), BF16)

    def expand(xp, first):
        sw = pltpu.roll(xp, GLA_DK, 1)
        two = jnp.where(lane128 < GLA_DK, xp, sw) if first else jnp.where(lane128 < GLA_DK, sw, xp)
        rep = jnp.concatenate([two] * (NS // 128), axis=1)
        return jnp.where(own_state, rep, 0.0).astype(BF16)

    ks = [slice(128 * p, 128 * (p + 1)) for p in range(2)]
    vs = [slice(256 * p, 256 * (p + 1)) for p in range(2)]
    attn = []
    for p in range(2):
        a = _dot_nt(q_in[:, ks[p]], _head_pair_keys(k_in[:, ks[p]], lane128))
        attn.append(jnp.where(same_seq_causal, a, 0.0).astype(BF16))
    inter = []
    for h in range(GLA_HEADS):
        p, first = h // 2, h % 2 == 0
        s_flat = s0_ref[:, h].reshape(NS, GLA_DV)
        inter.append(_dot(expand(q_in[:, ks[p]].astype(F32), first), s_flat.astype(BF16)))
    for h in range(GLA_HEADS):
        p, first = h // 2, h % 2 == 0
        s_flat = s0_ref[:, h].reshape(NS, GLA_DV)
        lhs = jnp.concatenate([expand(k_st[:, ks[p]], first),
                               expand(dec_rows[:, ks[p]].astype(F32), first)], axis=0)
        v_h = vb[:, h * GLA_DV:(h + 1) * GLA_DV]
        rhs = jnp.concatenate([jnp.concatenate([v_h, zeros_blk], axis=1),
                               jnp.concatenate([zeros_blk, ones_blk], axis=1)], axis=0)
        ud = _dot_tn(lhs, rhs)
        s_new = ud[:, GLA_DV:] * s_flat + ud[:, :GLA_DV]
        snew_ref[:, h] = s_new.reshape(G, GLA_DK, GLA_DV)
    o_parts = []
    for p in range(2):
        o_intra = _dot(attn[p], _head_pair_values(vb[:, vs[p]], lane256))
        o_parts.append(o_intra + jnp.concatenate(inter[2 * p:2 * p + 2], axis=1))
    o = jnp.concatenate(o_parts, axis=1)

    y_gla = _gla_output_gate(o, r, gnorm_ref[...])
    x1_ref[...] = _mix_out(x, y_pool, y_gla, w_out_ref, g_ref, b_ref)


def _mixer_short_kernel(*refs, seq_len):
    xm_ref, xs_ref, hist_ref, s0_ref = refs[:4]
    wts = refs[4:4 + N_MIX_W]
    x1m_ref, um_ref, sm_ref, x1s_ref, us_ref, ss_ref = refs[-6:]
    i = pl.program_id(0)

    @pl.when(i == 0)
    def _meta():
        _mixer_meta(xm_ref, wts, x1m_ref, um_ref, sm_ref)

    @pl.when(i > 0)
    def _sample():
        _mixer_sample(xs_ref, hist_ref, s0_ref, wts, x1s_ref, us_ref, ss_ref, seq_len=seq_len)


def _mixer_short(x_meta, x_samp, hist, s0, wts, layer, *, seq_len, s_out=None):
    rows = x_samp.shape[0]
    G = SEQ_GROUP
    R = G * seq_len
    n_tiles = rows // R
    assert rows % R == 0
    clamp = lambda i: jnp.maximum(i - 1, 0)
    n_in = 4 + N_MIX_W
    in_specs = ([pl.BlockSpec((N_META, D_MODEL), lambda i: (0, 0)),
                 pl.BlockSpec((R, D_MODEL), lambda i: (clamp(i), 0)),
                 pl.BlockSpec((None, G * POOL_BUF, D_POOL), lambda i: (layer, clamp(i), 0)),
                 pl.BlockSpec((None, G, GLA_HEADS, GLA_DK, GLA_DV), lambda i: (layer, clamp(i), 0, 0, 0))]
                + _mixer_weight_specs(layer))
    args = [x_meta, x_samp, hist, s0, *wts]
    aliases = {}
    if s_out is not None:
        in_specs.append(pl.BlockSpec(memory_space=pl.ANY))
        args.append(s_out)
        aliases = {n_in: 5}
    body = functools.partial(_mixer_short_kernel, seq_len=seq_len)

    def kern(*refs):
        body(*refs[:n_in], *refs[len(args):])

    return pl.pallas_call(
        kern,
        grid=(n_tiles + 1,),
        in_specs=in_specs,
        out_specs=[pl.BlockSpec((N_META, D_MODEL), lambda i: (0, 0)),
                   pl.BlockSpec((N_META, D_POOL), lambda i: (0, 0)),
                   pl.BlockSpec((GLA_HEADS, GLA_DK, GLA_DV), lambda i: (0, 0, 0)),
                   pl.BlockSpec((R, D_MODEL), lambda i: (clamp(i), 0)),
                   pl.BlockSpec((G * POOL_BUF, D_POOL), lambda i: (clamp(i), 0)),
                   pl.BlockSpec((None, G, GLA_HEADS, GLA_DK, GLA_DV), lambda i: (layer, clamp(i), 0, 0, 0))],
        out_shape=[jax.ShapeDtypeStruct((N_META, D_MODEL), F32),
                   jax.ShapeDtypeStruct((N_META, D_POOL), F32),
                   jax.ShapeDtypeStruct((GLA_HEADS, GLA_DK, GLA_DV), F32),
                   jax.ShapeDtypeStruct((rows, D_MODEL), F32),
                   jax.ShapeDtypeStruct((n_tiles * G * POOL_BUF, D_POOL), F32),
                   jax.ShapeDtypeStruct(s0.shape, F32)],
        input_output_aliases=aliases,
        compiler_params=pltpu.CompilerParams(dimension_semantics=("arbitrary",),
                                             vmem_limit_bytes=VMEM_LIMIT),
        name="mixer_short",
    )(*args)


def _ffn_tile(x_ref, wts, y_ref, conv_inputs, store_gate):
    w_up_ref, w_gate_ref, cw_ref, cb_ref, w_down_ref, g_ref, b_ref = wts
    x = x_ref[...]
    xb = x.astype(BF16)

    acts = []
    for j in range(D_FF // FF_CHUNK):
        cs = slice(j * FF_CHUNK, (j + 1) * FF_CHUNK)
        a = _dot(xb, w_up_ref[:, cs])
        gt = _dot(xb, w_gate_ref[:, cs])
        g1, g2 = conv_inputs(gt, cs)
        store_gate(gt, cs)
        gc = cb_ref[:, cs] + cw_ref[0:1, cs] * g2 + cw_ref[1:2, cs] * g1 + cw_ref[2:3, cs] * gt
        acts.append((a * _silu(gc)).astype(BF16))
    yield

    f = _dot(jnp.concatenate(acts, axis=1), w_down_ref[...])
    yield

    y_ref[...] = _layer_norm(ALPHA * x + f, g_ref[...], b_ref[...])


def _ffn_long_kernel(*refs, n_par):
    x_ref, cprev_ref = refs[:2]
    wts = refs[2:2 + N_FFN_W]
    y_ref, hist_out_ref, gbuf = refs[2 + N_FFN_W:]
    t = pl.program_id(1)
    T = x_ref.shape[1]

    @pl.when(t == 0)
    def _init():
        for j in range(n_par):
            gbuf[j, 0:CONV_PAD, :] = cprev_ref[...]

    def make_tile(j):
        def conv_inputs(gt, cs):
            gbuf[j, CONV_PAD:CONV_PAD + T, cs] = gt
            return gbuf[j, CONV_PAD - 1:CONV_PAD - 1 + T, cs], gbuf[j, CONV_PAD - 2:CONV_PAD - 2 + T, cs]

        def store_gate(gt, cs):
            gbuf[j, 0:CONV_PAD, cs] = gbuf[j, T:T + CONV_PAD, cs]

        return _ffn_tile(x_ref.at[j], wts, y_ref.at[j], conv_inputs, store_gate)

    for _ in itertools.zip_longest(*[make_tile(j) for j in range(n_par)]):
        pass

    @pl.when(t == pl.num_programs(1) - 1)
    def _final():
        for j in range(n_par):
            hist_out_ref[j] = gbuf[j, 0:CONV_PAD, :]


def _ffn_short_kernel(*refs, seq_len):
    xm_ref, xs_ref, hist_ref = refs[:3]
    wts = refs[3:3 + N_FFN_W]
    ym_ref, cm_ref, ys_ref, cs_ref = refs[3 + N_FFN_W:]
    G, Ls = SEQ_GROUP, seq_len
    R = G * Ls
    rows = xs_ref.shape[0]
    n_tiles = rows // R

    rowm = lax.broadcasted_iota(jnp.int32, (N_META, FF_CHUNK), 0)

    def meta_conv(gt, cs):
        return (jnp.where(rowm >= 1, _roll_rows(gt, 1), 0.0), jnp.where(rowm >= 2, _roll_rows(gt, 2), 0.0))

    def meta_store(gt, cs):
        cm_ref[:, cs] = gt[N_META - CONV_PAD:N_META]

    tok = (lax.broadcasted_iota(jnp.int32, (rows, FF_CHUNK), 0) & (R - 1)) >> (G.bit_length() - 1)
    zeros_tail = jnp.zeros((R - CONV_BUF * G, FF_CHUNK), F32)

    def sample_conv(gt, cs):
        hx = jnp.concatenate([piece for n in range(n_tiles)
                              for piece in (hist_ref[n * CONV_BUF * G:(n + 1) * CONV_BUF * G, cs], zeros_tail)], axis=0)
        g1 = jnp.where(tok >= 1, _roll_rows(gt, G), _roll_rows(hx, -G))
        g2 = jnp.where(tok >= 2, _roll_rows(gt, 2 * G), hx)
        return g1, g2

    def sample_store(gt, cs):
        for n in range(n_tiles):
            cs_ref[n * CONV_BUF * G:(n + 1) * CONV_BUF * G, cs] = gt[n * R + (Ls - CONV_BUF) * G:(n + 1) * R]

    tiles = [_ffn_tile(xs_ref, wts, ys_ref, sample_conv, sample_store),
             _ffn_tile(xm_ref, wts, ym_ref, meta_conv, meta_store)]
    for _ in itertools.zip_longest(*tiles):
        pass


def _ffn_weight_specs(layer):
    shapes = [(D_MODEL, D_FF), (D_MODEL, D_FF), (3, D_FF), (1, D_FF), (D_FF, D_MODEL), (1, D_MODEL), (1, D_MODEL)]
    assert len(shapes) == N_FFN_W
    return [_layer_spec(s, layer) for s in shapes]


def _ffn_long(x, cprev, wts, layer):
    B, L, _ = x.shape
    T = min(FFN_TILE, L)
    P = LONG_PAR if B % LONG_PAR == 0 else 1
    assert L % T == 0
    kern = functools.partial(_ffn_long_kernel, n_par=P)
    return pl.pallas_call(
        kern,
        grid=(B // P, L // T),
        in_specs=[pl.BlockSpec((P, T, D_MODEL), lambda b, t: (b, t, 0)),
                  pl.BlockSpec((CONV_PAD, D_FF), lambda b, t: (0, 0))] + _ffn_weight_specs(layer),
        out_specs=[pl.BlockSpec((P, T, D_MODEL), lambda b, t: (b, t, 0)),
                   pl.BlockSpec((P, CONV_PAD, D_FF), lambda b, t: (b, 0, 0))],
        out_shape=[jax.ShapeDtypeStruct((B, L, D_MODEL), F32),
                   jax.ShapeDtypeStruct((B, CONV_PAD, D_FF), F32)],
        scratch_shapes=[pltpu.VMEM((P, T + CONV_PAD, D_FF), F32)],
        compiler_params=pltpu.CompilerParams(dimension_semantics=("arbitrary", "arbitrary"),
                                             vmem_limit_bytes=VMEM_LIMIT),
        name="ffn_long",
    )(x, cprev, *wts)


def _ffn_short(x_meta, x_samp, hist, wts, layer, *, seq_len):
    rows = x_samp.shape[0]
    n_hist = hist.shape[1]
    kern = functools.partial(_ffn_short_kernel, seq_len=seq_len)
    return pl.pallas_call(
        kern,
        grid=(1,),
        in_specs=[pl.BlockSpec((N_META, D_MODEL), lambda i: (0, 0)),
                  pl.BlockSpec((rows, D_MODEL), lambda i: (0, 0)),
                  pl.BlockSpec((None, n_hist, D_FF), lambda i: (layer, 0, 0))] + _ffn_weight_specs(layer),
        out_specs=[pl.BlockSpec((N_META, D_MODEL), lambda i: (0, 0)),
                   pl.BlockSpec((CONV_PAD, D_FF), lambda i: (0, 0)),
                   pl.BlockSpec((rows, D_MODEL), lambda i: (0, 0)),
                   pl.BlockSpec((n_hist, D_FF), lambda i: (0, 0))],
        out_shape=[jax.ShapeDtypeStruct((N_META, D_MODEL), F32),
                   jax.ShapeDtypeStruct((CONV_PAD, D_FF), F32),
                   jax.ShapeDtypeStruct((rows, D_MODEL), F32),
                   jax.ShapeDtypeStruct((n_hist, D_FF), F32)],
        compiler_params=pltpu.CompilerParams(dimension_semantics=("arbitrary",),
                                             vmem_limit_bytes=VMEM_LIMIT),
        name="ffn_short",
    )(x_meta, x_samp, hist, *wts)


def _to_group_major(a, axis):
    n, j = a.shape[axis], a.shape[axis + 1]
    lead, tail = a.shape[:axis], a.shape[axis + 2:]
    a = a.reshape(*lead, n // SEQ_GROUP, SEQ_GROUP, j, *tail)
    a = jnp.swapaxes(a, axis + 1, axis + 2)
    return a.reshape(*lead, n * j, *tail)


def _from_group_major(a, axis, j):
    rows = a.shape[axis]
    n = rows // j
    lead, tail = a.shape[:axis], a.shape[axis + 1:]
    a = a.reshape(*lead, n // SEQ_GROUP, j, SEQ_GROUP, *tail)
    a = jnp.swapaxes(a, axis + 1, axis + 2)
    return a.reshape(*lead, n, j, *tail)


def kernel(x_prompt, x_sample, state_pool, state_gla, state_conv, meta_tokens,
           w_in, w_a2, b_a, w_pool, pool_scale, gla_norm, w_out, ln1_g, ln1_b,
           w_up, w_gate, conv_w, conv_b, w_down, ln2_g, ln2_b):
    NB, LS = x_sample.shape[0], x_sample.shape[1]
    assert NB % SEQ_GROUP == 0 and LS & (LS - 1) == 0 and CONV_BUF <= LS <= POOL_BUF

    mix_w = (w_in.astype(BF16), w_a2.astype(BF16), b_a[:, None],
             w_pool.astype(BF16), pool_scale[:, None], gla_norm[:, None], w_out.astype(BF16),
             ln1_g[:, None], ln1_b[:, None])
    ffn_w = (w_up.astype(BF16), w_gate.astype(BF16), conv_w, conv_b[:, None], w_down.astype(BF16),
             ln2_g[:, None], ln2_b[:, None])

    hm = meta_tokens.astype(F32)
    hp = x_prompt
    hs = _to_group_major(x_sample, 0)
    pool_hist = _to_group_major(state_pool, 1)
    conv_hist = _to_group_major(state_conv, 1)

    pp, gp, cp, ps_l, cs_l = [], [], [], [], []
    gs = None
    for l in range(DEPTH):
        hm1, um, sm, hs1, ps_new, gs = _mixer_short(hm, hs, pool_hist, state_gla, mix_w, l, seq_len=LS, s_out=gs)
        hm, cm, hs, cs_new = _ffn_short(hm1, hs1, conv_hist, ffn_w, l, seq_len=LS)
        ps_l.append(ps_new)
        cs_l.append(cs_new)

        hp1, pbuf, snew = _mixer_long(hp, um, sm, mix_w, l)
        hp, cbuf = _ffn_long(hp1, cm, ffn_w, l)
        pp.append(pbuf[:, 1:])
        gp.append(snew)
        cp.append(cbuf[:, CONV_PAD - CONV_BUF:])

    ps = _from_group_major(jnp.stack(ps_l), 1, POOL_BUF)
    cs = _from_group_major(jnp.stack(cs_l), 1, CONV_BUF)
    return (hp, _from_group_major(hs, 0, LS), jnp.stack(pp), jnp.stack(gp), jnp.stack(cp), ps, gs, cs)
```

```python
import functools
import itertools

import jax
import jax.numpy as jnp
from jax import lax
from jax.experimental import pallas as pl
from jax.experimental.pallas import tpu as pltpu

F32 = jnp.float32
BF16 = jnp.bfloat16

D_MODEL = 1024
N_META = 16
D_POOL = 512
POOL_WINDOWS = (2, 4, 8, 16)
POOL_GROUP = 128
POOL_BUF = 15
POOL_PAD = 16
D_GLA = 512
GLA_HEADS = 4
GLA_DV = 128
GLA_DK = 64
D_GLA_K = 256
GATE_RANK = 16
GATE_TAU = 16.0
D_FF = 2816
CONV_BUF = 2
CONV_PAD = 8
DEPTH = 2
ALPHA = (2 * DEPTH) ** 0.25
LN_EPS = 1e-5
RMS_EPS = 1e-6
PAST_LEN = 16384

C_POOL, C_Q, C_K, C_V, C_R, C_Z, C_END = 0, 512, 768, 1024, 1536, 2048, 2064

LONG_TILE = 512
LONG_CHUNK = 64
LONG_PAR = 2
MIXER_PAR = 2
MIXER_SKEW = 1
FFN_TILE = 512
SEQ_GROUP = 16
FF_CHUNK = 256
VMEM_LIMIT = 56 * 1024 * 1024

N_MIX_W = 9
N_FFN_W = 7


def _dot(a, b):
    return jnp.dot(a, b, preferred_element_type=F32)


def _dot_nt(a, b):
    return lax.dot_general(a, b, (((1,), (1,)), ((), ())), preferred_element_type=F32)


def _dot_tn(a, b):
    return lax.dot_general(a, b, (((0,), (0,)), ((), ())), preferred_element_type=F32)


def _layer_norm(y, g, b):
    mu = jnp.mean(y, axis=-1, keepdims=True)
    yc = y - mu
    var = jnp.mean(yc * yc, axis=-1, keepdims=True)
    return yc * lax.rsqrt(var + LN_EPS) * g + b


def _silu(x):
    return x * (1.0 / (1.0 + jnp.exp(-x)))


def _log_sigmoid(z):
    return jnp.minimum(z, 0.0) - jnp.log(1.0 + jnp.exp(-jnp.abs(z)))


def _roll_rows(x, shift):
    n = x.shape[0]
    return pltpu.roll(x, shift % n, 0)


def _split_bf16(x):
    hi = x.astype(BF16)
    lo = (x - hi.astype(F32)).astype(BF16)
    return hi, lo


def _project_in(xb, w_in_ref):
    u = _dot(xb, w_in_ref[:, C_POOL:C_Q])
    zr = _dot(xb, w_in_ref[:, C_Z:C_END])
    q = _dot(xb, w_in_ref[:, C_Q:C_K]) * (GLA_DK ** -0.5)
    k = _dot(xb, w_in_ref[:, C_K:C_V])
    v = _dot(xb, w_in_ref[:, C_V:C_R])
    r = _dot(xb, w_in_ref[:, C_R:C_Z])
    return u, zr, q, k, v, r


def _gate_log_decay(zr, w_a2_ref, b_a_ref):
    z = _dot(zr.astype(BF16), w_a2_ref[...]) + b_a_ref[...]
    return _log_sigmoid(z) * (1.0 / GATE_TAU)


def _gla_output_gate(o, r, gnorm):
    parts = []
    for h in range(GLA_HEADS):
        oh = o[:, h * GLA_DV:(h + 1) * GLA_DV]
        ms = jnp.mean(oh * oh, axis=-1, keepdims=True)
        parts.append(oh * lax.rsqrt(ms + RMS_EPS) * gnorm)
    return jnp.concatenate(parts, axis=1) * _silu(r)


def _pool_project(d_groups, w_pool_ref, pscale_ref):
    ys = [_dot(d.astype(BF16), w_pool_ref[g]) for g, d in enumerate(d_groups)]
    return jnp.concatenate(ys, axis=1) * pscale_ref[...]


def _mix_out(x, y_pool, y_gla, w_out_ref, g_ref, b_ref):
    mix = jnp.concatenate([y_pool, y_gla], axis=1).astype(BF16)
    return _layer_norm(ALPHA * x + _dot(mix, w_out_ref[...]), g_ref[...], b_ref[...])


def _emit_staggered(stage_gens, first_round):
    live = list(zip(first_round, stage_gens))
    rnd = 0
    while live:
        for start, g in list(live):
            if rnd >= start and next(g, StopIteration) is StopIteration:
                live.remove((start, g))
        rnd += 1


def _head_pair_keys(k_p, lane128):
    zk = jnp.zeros_like(k_p)
    return jnp.concatenate([jnp.where(lane128 < GLA_DK, k_p, zk), jnp.where(lane128 >= GLA_DK, k_p, zk)], axis=0)


def _head_pair_values(v_p, lane256):
    zv = jnp.zeros_like(v_p)
    return jnp.concatenate([jnp.where(lane256 < GLA_DV, v_p, zv), jnp.where(lane256 >= GLA_DV, v_p, zv)], axis=0)


def _mixer_long_kernel(*refs, n_par, tile, chunk):
    x_ref, pprev_ref, s0_ref = refs[:3]
    wts = refs[3:3 + N_MIX_W]
    x1_ref, pbuf_ref, snew_ref, ubuf, sbd = refs[3 + N_MIX_W:]
    t = pl.program_id(1)
    zero_blk = jnp.zeros((GLA_DK, GLA_DV), F32)

    @pl.when(t == 0)
    def _init():
        for j in range(n_par):
            ubuf[j, 0:POOL_PAD, :] = pprev_ref[...]
            for p in range(2):
                top = jnp.concatenate([s0_ref[2 * p], zero_blk], axis=1)
                bot = jnp.concatenate([zero_blk, s0_ref[2 * p + 1]], axis=1)
                sbd[j, p] = jnp.concatenate([top, bot], axis=0)

    tiles = [_mixer_long_tile(x_ref.at[j], wts, x1_ref.at[j], ubuf.at[j], sbd.at[j], tile=tile, chunk=chunk)
             for j in range(n_par)]
    _emit_staggered(tiles, [MIXER_SKEW * j for j in range(n_par)])

    @pl.when(t == pl.num_programs(1) - 1)
    def _final():
        for j in range(n_par):
            pbuf_ref[j] = ubuf[j, 0:POOL_PAD, :]
            for p in range(2):
                s_p = sbd[j, p]
                snew_ref[j, 2 * p] = s_p[0:GLA_DK, 0:GLA_DV]
                snew_ref[j, 2 * p + 1] = s_p[GLA_DK:2 * GLA_DK, GLA_DV:2 * GLA_DV]


def _mixer_long_tile(x_ref, wts, x1_ref, ubuf, sbd, *, tile, chunk):
    w_in_ref, w_a2_ref, b_a_ref, w_pool_ref, pscale_ref, gnorm_ref, w_out_ref, g_ref, b_ref = wts
    T, C = tile, chunk
    x = x_ref[...]
    xb = x.astype(BF16)

    u, zr, q, k, v, r = _project_in(xb, w_in_ref)
    yield

    ubuf[POOL_PAD:POOL_PAD + T, :] = u
    d_groups = []
    for g, w in enumerate(POOL_WINDOWS):
        s = ubuf[:, g * POOL_GROUP:(g + 1) * POOL_GROUP]
        sh = 1
        while sh < w:
            s = s + _roll_rows(s, sh)
            sh *= 2
        d_groups.append(s[POOL_PAD:, :] * (1.0 / w) - u[:, g * POOL_GROUP:(g + 1) * POOL_GROUP])
    y_pool = _pool_project(d_groups, w_pool_ref, pscale_ref)
    ubuf[0:POOL_PAD, :] = ubuf[T:T + POOL_PAD, :]
    loga = _gate_log_decay(zr, w_a2_ref, b_a_ref)

    tr = lax.broadcasted_iota(jnp.int32, (C, C), 0)
    tc = lax.broadcasted_iota(jnp.int32, (C, C), 1)
    tri = jnp.where(tc <= tr, 1.0, 0.0).astype(BF16)
    ar = lax.broadcasted_iota(jnp.int32, (C, 2 * C), 0)
    ac = lax.broadcasted_iota(jnp.int32, (C, 2 * C), 1) & (C - 1)
    causal = ac <= ar
    lane128 = lax.broadcasted_iota(jnp.int32, (C, 128), 1)
    lane256 = lax.broadcasted_iota(jnp.int32, (C, 256), 1)
    sr = lax.broadcasted_iota(jnp.int32, (128, 256), 0)
    sc = lax.broadcasted_iota(jnp.int32, (128, 256), 1)
    blockdiag = (sr >= GLA_DK) == (sc >= GLA_DV)
    mid = C // 2 - 1

    n_chunks = T // C
    pairs = [(c, p) for c in range(n_chunks) for p in range(2)]
    ks = [slice(128 * p, 128 * (p + 1)) for p in range(2)]
    vs = [slice(256 * p, 256 * (p + 1)) for p in range(2)]

    bcs = []
    for c in range(n_chunks):
        la_hi, la_lo = _split_bf16(loga[c * C:(c + 1) * C])
        bb = _dot(tri, jnp.concatenate([la_hi, la_lo], axis=1))
        bcs.append(bb[:, :D_GLA_K] + bb[:, D_GLA_K:])
    yield

    q_in, k_in, q_st, k_st, dec_t, vc = [], [], [], [], [], []
    for c in range(n_chunks):
        bc = bcs[c]
        bmid = bc[mid:mid + 1]
        bend = bc[C - 1:C]
        qc = q[c * C:(c + 1) * C]
        kc = k[c * C:(c + 1) * C]
        q_in.append((qc * jnp.exp(bc - bmid)).astype(BF16))
        k_in.append((kc * jnp.exp(bmid - bc)).astype(BF16))
        q_st.append((qc * jnp.exp(bc)).astype(BF16))
        k_st.append((kc * jnp.exp(bend - bc)).astype(BF16))
        dec_t.append(jnp.transpose(jnp.broadcast_to(jnp.exp(bend), (128, D_GLA_K))))
        vc.append(v[c * C:(c + 1) * C].astype(BF16))
    yield

    attn, upd = {}, {}
    for c, p in pairs:
        a = _dot_nt(q_in[c][:, ks[p]], _head_pair_keys(k_in[c][:, ks[p]], lane128))
        attn[c, p] = jnp.where(causal, a, 0.0).astype(BF16)
    for c, p in pairs:
        u_cp = _dot_tn(k_st[c][:, ks[p]], vc[c][:, vs[p]])
        upd[c, p] = jnp.where(blockdiag, u_cp, 0.0)
    yield

    s_vals = [sbd[p] for p in range(2)]
    s_start = {}
    for c, p in pairs:
        s_start[c, p] = s_vals[p].astype(BF16)
        dec_p = dec_t[c][ks[p], :]
        s_vals[p] = jnp.concatenate([dec_p, dec_p], axis=1) * s_vals[p] + upd[c, p]
    for p in range(2):
        sbd[p] = s_vals[p]

    o_rows = [[], []]
    for c, p in pairs:
        vblk = _head_pair_values(vc[c][:, vs[p]], lane256)
        lhs = jnp.concatenate([attn[c, p], q_st[c][:, ks[p]]], axis=1)
        rhs = jnp.concatenate([vblk, s_start[c, p]], axis=0)
        o_rows[p].append(_dot(lhs, rhs))
    o = jnp.concatenate([jnp.concatenate(o_rows[p], axis=0) for p in range(2)], axis=1)
    yield

    y_gla = _gla_output_gate(o, r, gnorm_ref[...])
    x1_ref[...] = _mix_out(x, y_pool, y_gla, w_out_ref, g_ref, b_ref)


def _layer_spec(shape, layer):
    nd = len(shape)
    return pl.BlockSpec((None,) + shape, lambda *_: (layer,) + (0,) * nd, pipeline_mode=pl.Buffered(1))


def _mixer_weight_specs(layer):
    shapes = [(D_MODEL, C_END), (GATE_RANK, D_GLA_K), (1, D_GLA_K),
              (4, POOL_GROUP, POOL_GROUP), (1, D_POOL), (1, GLA_DV), (D_MODEL, D_MODEL), (1, D_MODEL), (1, D_MODEL)]
    assert len(shapes) == N_MIX_W
    return [_layer_spec(s, layer) for s in shapes]


def _mixer_long(x, pprev, s0, wts, layer):
    B, L, _ = x.shape
    T = min(LONG_TILE, L)
    P = MIXER_PAR if B % MIXER_PAR == 0 else 1
    assert L % T == 0 and T % LONG_CHUNK == 0
    kern = functools.partial(_mixer_long_kernel, n_par=P, tile=T, chunk=LONG_CHUNK)
    return pl.pallas_call(
        kern,
        grid=(B // P, L // T),
        in_specs=[pl.BlockSpec((P, T, D_MODEL), lambda b, t: (b, t, 0)),
                  pl.BlockSpec((POOL_PAD, D_POOL), lambda b, t: (0, 0)),
                  pl.BlockSpec((GLA_HEADS, GLA_DK, GLA_DV), lambda b, t: (0, 0, 0))]
        + _mixer_weight_specs(layer),
        out_specs=[pl.BlockSpec((P, T, D_MODEL), lambda b, t: (b, t, 0)),
                   pl.BlockSpec((P, POOL_PAD, D_POOL), lambda b, t: (b, 0, 0)),
                   pl.BlockSpec((P, GLA_HEADS, GLA_DK, GLA_DV), lambda b, t: (b, 0, 0, 0))],
        out_shape=[jax.ShapeDtypeStruct((B, L, D_MODEL), F32),
                   jax.ShapeDtypeStruct((B, POOL_PAD, D_POOL), F32),
                   jax.ShapeDtypeStruct((B, GLA_HEADS, GLA_DK, GLA_DV), F32)],
        scratch_shapes=[pltpu.VMEM((P, T + POOL_PAD, D_POOL), F32),
                        pltpu.VMEM((P, 2, 2 * GLA_DK, 2 * GLA_DV), F32)],
        compiler_params=pltpu.CompilerParams(dimension_semantics=("arbitrary", "arbitrary"),
                                             vmem_limit_bytes=VMEM_LIMIT),
        name="mixer_long",
    )(x, pprev, s0, *wts)


def _mixer_meta(x_ref, wts, x1_ref, u_ref, s_ref):
    w_in_ref, w_a2_ref, b_a_ref, w_pool_ref, pscale_ref, gnorm_ref, w_out_ref, g_ref, b_ref = wts
    L = N_META
    x = x_ref[...]
    xb = x.astype(BF16)
    u, zr, q, k, v, r = _project_in(xb, w_in_ref)
    u_ref[...] = u

    row128 = lax.broadcasted_iota(jnp.int32, (L, POOL_GROUP), 0)
    pos1 = lax.broadcasted_iota(jnp.int32, (L, 1), 0)
    d_groups = []
    for g, w in enumerate(POOL_WINDOWS):
        ug = u[:, g * POOL_GROUP:(g + 1) * POOL_GROUP]
        s = ug
        sh = 1
        while sh < w:
            s = s + jnp.where(row128 >= sh, _roll_rows(s, sh), 0.0)
            sh *= 2
        d_groups.append(s / jnp.minimum(w, pos1 + 1).astype(F32) - ug)
    y_pool = _pool_project(d_groups, w_pool_ref, pscale_ref)

    loga = _gate_log_decay(zr, w_a2_ref, b_a_ref)
    row256 = lax.broadcasted_iota(jnp.int32, (L, D_GLA_K), 0)
    b = loga
    sh = 1
    while sh < L:
        b = b + jnp.where(row256 >= sh, _roll_rows(b, sh), 0.0)
        sh *= 2
    bend = b[L - 1:L]
    q_in = (q * jnp.exp(b)).astype(BF16)
    k_in = (k * jnp.exp(-b)).astype(BF16)
    k_st = (k * jnp.exp(bend - b)).astype(BF16)
    vb = v.astype(BF16)
    ar = lax.broadcasted_iota(jnp.int32, (L, 2 * L), 0)
    ac = lax.broadcasted_iota(jnp.int32, (L, 2 * L), 1) & (L - 1)
    causal = ac <= ar
    lane128 = lax.broadcasted_iota(jnp.int32, (L, 128), 1)
    lane256 = lax.broadcasted_iota(jnp.int32, (L, 256), 1)
    attn = []
    for p in range(2):
        ks = slice(128 * p, 128 * (p + 1))
        a = _dot_nt(q_in[:, ks], _head_pair_keys(k_in[:, ks], lane128))
        attn.append(jnp.where(causal, a, 0.0).astype(BF16))
    o_parts = []
    for p in range(2):
        ks = slice(128 * p, 128 * (p + 1))
        vs = slice(256 * p, 256 * (p + 1))
        s_p = _dot_tn(k_st[:, ks], vb[:, vs])
        s_ref[2 * p] = s_p[0:GLA_DK, 0:GLA_DV]
        s_ref[2 * p + 1] = s_p[GLA_DK:2 * GLA_DK, GLA_DV:2 * GLA_DV]
        o_parts.append(_dot(attn[p], _head_pair_values(vb[:, vs], lane256)))
    o = jnp.concatenate(o_parts, axis=1)

    y_gla = _gla_output_gate(o, r, gnorm_ref[...])
    x1_ref[...] = _mix_out(x, y_pool, y_gla, w_out_ref, g_ref, b_ref)


def _mixer_sample(x_ref, hist_ref, s0_ref, wts, x1_ref, hist_out_ref, snew_ref, *, seq_len):
    w_in_ref, w_a2_ref, b_a_ref, w_pool_ref, pscale_ref, gnorm_ref, w_out_ref, g_ref, b_ref = wts
    G, Ls = SEQ_GROUP, seq_len
    R = G * Ls
    NS = G * GLA_DK
    g_shift = G.bit_length() - 1
    x = x_ref[...]
    xb = x.astype(BF16)
    u, zr, q, k, v, r = _project_in(xb, w_in_ref)
    hist_out_ref[0:(POOL_BUF - Ls) * G, :] = hist_ref[R:POOL_BUF * G, :]
    hist_out_ref[(POOL_BUF - Ls) * G:POOL_BUF * G, :] = u

    def blk(a, t):
        return a[t * G:(t + 1) * G]

    d_groups = []
    for g, w in enumerate(POOL_WINDOWS):
        cols = slice(g * POOL_GROUP, (g + 1) * POOL_GROUP)
        ug = u[:, cols]
        suffix = [None]
        acc = None
        for m in range(1, min(w - 1, POOL_BUF) + 1):
            h = hist_ref[(POOL_BUF - m) * G:(POOL_BUF - m + 1) * G, cols]
            acc = h if acc is None else acc + h
            suffix.append(acc)
        parts = []
        for t in range(Ls):
            wsum = blk(ug, t)
            for j in range(max(0, t - w + 1), t):
                wsum = wsum + blk(ug, j)
            m = w - 1 - t
            if m > 0:
                wsum = wsum + suffix[m]
            parts.append(wsum * (1.0 / w) - blk(ug, t))
        d_groups.append(jnp.concatenate(parts, axis=0))
    y_pool = _pool_project(d_groups, w_pool_ref, pscale_ref)

    loga = _gate_log_decay(zr, w_a2_ref, b_a_ref)
    b_t = [blk(loga, 0)]
    for t in range(1, Ls):
        b_t.append(b_t[-1] + blk(loga, t))
    b = jnp.concatenate(b_t, axis=0)
    bend = jnp.concatenate([b_t[-1]] * Ls, axis=0)
    q_in = (q * jnp.exp(b)).astype(BF16)
    k_in = (k * jnp.exp(-b)).astype(BF16)
    k_st = k * jnp.exp(bend - b)
    dec_hi, dec_lo = _split_bf16(jnp.exp(bend))
    tok = lax.broadcasted_iota(jnp.int32, (R, D_GLA_K), 0) >> g_shift
    dec_rows = jnp.where(tok == Ls - 1, dec_hi, jnp.where(tok == Ls - 2, dec_lo, jnp.zeros_like(dec_lo)))
    vb = v.astype(BF16)

    ar = lax.broadcasted_iota(jnp.int32, (R, 2 * R), 0)
    ac = lax.broadcasted_iota(jnp.int32, (R, 2 * R), 1) & (R - 1)
    same_seq_causal = ((ac & (G - 1)) == (ar & (G - 1))) & ((ac >> g_shift) <= (ar >> g_shift))
    lane128 = lax.broadcasted_iota(jnp.int32, (R, 128), 1)
    lane256 = lax.broadcasted_iota(jnp.int32, (R, 256), 1)
    br = lax.broadcasted_iota(jnp.int32, (R, NS), 0)
    bcol = lax.broadcasted_iota(jnp.int32, (R, NS), 1)
    own_state = (bcol >> 6) == (br & (G - 1))
    ones_blk = jnp.ones((R, GLA_DV), BF16)
    zeros_blk = jnp.zeros((R, GLA_DV), BF16)

    def expand(xp, first):
        sw = pltpu.roll(xp, GLA_DK, 1)
        two = jnp.where(lane128 < GLA_DK, xp, sw) if first else jnp.where(lane128 < GLA_DK, sw, xp)
        rep = jnp.concatenate([two] * (NS // 128), axis=1)
        return jnp.where(own_state, rep, 0.0).astype(BF16)

    ks = [slice(128 * p, 128 * (p + 1)) for p in range(2)]
    vs = [slice(256 * p, 256 * (p + 1)) for p in range(2)]
    attn = []
    for p in range(2):
        a = _dot_nt(q_in[:, ks[p]], _head_pair_keys(k_in[:, ks[p]], lane128))
        attn.append(jnp.where(same_seq_causal, a, 0.0).astype(BF16))
    inter = []
    for h in range(GLA_HEADS):
        p, first = h // 2, h % 2 == 0
        s_flat = s0_ref[:, h].reshape(NS, GLA_DV)
        inter.append(_dot(expand(q_in[:, ks[p]].astype(F32), first), s_flat.astype(BF16)))
    for h in range(GLA_HEADS):
        p, first = h // 2, h % 2 == 0
        s_flat = s0_ref[:, h].reshape(NS, GLA_DV)
        lhs = jnp.concatenate([expand(k_st[:, ks[p]], first),
                               expand(dec_rows[:, ks[p]].astype(F32), first)], axis=0)
        v_h = vb[:, h * GLA_DV:(h + 1) * GLA_DV]
        rhs = jnp.concatenate([jnp.concatenate([v_h, zeros_blk], axis=1),
                               jnp.concatenate([zeros_blk, ones_blk], axis=1)], axis=0)
        ud = _dot_tn(lhs, rhs)
        s_new = ud[:, GLA_DV:] * s_flat + ud[:, :GLA_DV]
        snew_ref[:, h] = s_new.reshape(G, GLA_DK, GLA_DV)
    o_parts = []
    for p in range(2):
        o_intra = _dot(attn[p], _head_pair_values(vb[:, vs[p]], lane256))
        o_parts.append(o_intra + jnp.concatenate(inter[2 * p:2 * p + 2], axis=1))
    o = jnp.concatenate(o_parts, axis=1)

    y_gla = _gla_output_gate(o, r, gnorm_ref[...])
    x1_ref[...] = _mix_out(x, y_pool, y_gla, w_out_ref, g_ref, b_ref)


def _mixer_short_kernel(*refs, seq_len):
    xm_ref, xs_ref, hist_ref, s0_ref = refs[:4]
    wts = refs[4:4 + N_MIX_W]
    x1m_ref, um_ref, sm_ref, x1s_ref, us_ref, ss_ref = refs[-6:]
    i = pl.program_id(0)

    @pl.when(i == 0)
    def _meta():
        _mixer_meta(xm_ref, wts, x1m_ref, um_ref, sm_ref)

    @pl.when(i > 0)
    def _sample():
        _mixer_sample(xs_ref, hist_ref, s0_ref, wts, x1s_ref, us_ref, ss_ref, seq_len=seq_len)


def _mixer_short(x_meta, x_samp, hist, s0, wts, layer, *, seq_len, s_out=None):
    rows = x_samp.shape[0]
    G = SEQ_GROUP
    R = G * seq_len
    n_tiles = rows // R
    assert rows % R == 0
    clamp = lambda i: jnp.maximum(i - 1, 0)
    n_in = 4 + N_MIX_W
    in_specs = ([pl.BlockSpec((N_META, D_MODEL), lambda i: (0, 0)),
                 pl.BlockSpec((R, D_MODEL), lambda i: (clamp(i), 0)),
                 pl.BlockSpec((None, G * POOL_BUF, D_POOL), lambda i: (layer, clamp(i), 0)),
                 pl.BlockSpec((None, G, GLA_HEADS, GLA_DK, GLA_DV), lambda i: (layer, clamp(i), 0, 0, 0))]
                + _mixer_weight_specs(layer))
    args = [x_meta, x_samp, hist, s0, *wts]
    aliases = {}
    if s_out is not None:
        in_specs.append(pl.BlockSpec(memory_space=pl.ANY))
        args.append(s_out)
        aliases = {n_in: 5}
    body = functools.partial(_mixer_short_kernel, seq_len=seq_len)

    def kern(*refs):
        body(*refs[:n_in], *refs[len(args):])

    return pl.pallas_call(
        kern,
        grid=(n_tiles + 1,),
        in_specs=in_specs,
        out_specs=[pl.BlockSpec((N_META, D_MODEL), lambda i: (0, 0)),
                   pl.BlockSpec((N_META, D_POOL), lambda i: (0, 0)),
                   pl.BlockSpec((GLA_HEADS, GLA_DK, GLA_DV), lambda i: (0, 0, 0)),
                   pl.BlockSpec((R, D_MODEL), lambda i: (clamp(i), 0)),
                   pl.BlockSpec((G * POOL_BUF, D_POOL), lambda i: (clamp(i), 0)),
                   pl.BlockSpec((None, G, GLA_HEADS, GLA_DK, GLA_DV), lambda i: (layer, clamp(i), 0, 0, 0))],
        out_shape=[jax.ShapeDtypeStruct((N_META, D_MODEL), F32),
                   jax.ShapeDtypeStruct((N_META, D_POOL), F32),
                   jax.ShapeDtypeStruct((GLA_HEADS, GLA_DK, GLA_DV), F32),
                   jax.ShapeDtypeStruct((rows, D_MODEL), F32),
                   jax.ShapeDtypeStruct((n_tiles * G * POOL_BUF, D_POOL), F32),
                   jax.ShapeDtypeStruct(s0.shape, F32)],
        input_output_aliases=aliases,
        compiler_params=pltpu.CompilerParams(dimension_semantics=("arbitrary",),
                                             vmem_limit_bytes=VMEM_LIMIT),
        name="mixer_short",
    )(*args)


def _ffn_tile(x_ref, wts, y_ref, conv_inputs, store_gate):
    w_up_ref, w_gate_ref, cw_ref, cb_ref, w_down_ref, g_ref, b_ref = wts
    x = x_ref[...]
    xb = x.astype(BF16)

    acts = []
    for j in range(D_FF // FF_CHUNK):
        cs = slice(j * FF_CHUNK, (j + 1) * FF_CHUNK)
        a = _dot(xb, w_up_ref[:, cs])
        gt = _dot(xb, w_gate_ref[:, cs])
        g1, g2 = conv_inputs(gt, cs)
        store_gate(gt, cs)
        gc = cb_ref[:, cs] + cw_ref[0:1, cs] * g2 + cw_ref[1:2, cs] * g1 + cw_ref[2:3, cs] * gt
        acts.append((a * _silu(gc)).astype(BF16))
    yield

    f = _dot(jnp.concatenate(acts, axis=1), w_down_ref[...])
    yield

    y_ref[...] = _layer_norm(ALPHA * x + f, g_ref[...], b_ref[...])


def _ffn_long_kernel(*refs, n_par):
    x_ref, cprev_ref = refs[:2]
    wts = refs[2:2 + N_FFN_W]
    y_ref, hist_out_ref, gbuf = refs[2 + N_FFN_W:]
    t = pl.program_id(1)
    T = x_ref.shape[1]

    @pl.when(t == 0)
    def _init():
        for j in range(n_par):
            gbuf[j, 0:CONV_PAD, :] = cprev_ref[...]

    def make_tile(j):
        def conv_inputs(gt, cs):
            gbuf[j, CONV_PAD:CONV_PAD + T, cs] = gt
            return gbuf[j, CONV_PAD - 1:CONV_PAD - 1 + T, cs], gbuf[j, CONV_PAD - 2:CONV_PAD - 2 + T, cs]

        def store_gate(gt, cs):
            gbuf[j, 0:CONV_PAD, cs] = gbuf[j, T:T + CONV_PAD, cs]

        return _ffn_tile(x_ref.at[j], wts, y_ref.at[j], conv_inputs, store_gate)

    for _ in itertools.zip_longest(*[make_tile(j) for j in range(n_par)]):
        pass

    @pl.when(t == pl.num_programs(1) - 1)
    def _final():
        for j in range(n_par):
            hist_out_ref[j] = gbuf[j, 0:CONV_PAD, :]


def _ffn_short_kernel(*refs, seq_len):
    xm_ref, xs_ref, hist_ref = refs[:3]
    wts = refs[3:3 + N_FFN_W]
    ym_ref, cm_ref, ys_ref, cs_ref = refs[3 + N_FFN_W:]
    G, Ls = SEQ_GROUP, seq_len
    R = G * Ls
    rows = xs_ref.shape[0]
    n_tiles = rows // R

    rowm = lax.broadcasted_iota(jnp.int32, (N_META, FF_CHUNK), 0)

    def meta_conv(gt, cs):
        return (jnp.where(rowm >= 1, _roll_rows(gt, 1), 0.0), jnp.where(rowm >= 2, _roll_rows(gt, 2), 0.0))

    def meta_store(gt, cs):
        cm_ref[:, cs] = gt[N_META - CONV_PAD:N_META]

    tok = (lax.broadcasted_iota(jnp.int32, (rows, FF_CHUNK), 0) & (R - 1)) >> (G.bit_length() - 1)
    zeros_tail = jnp.zeros((R - CONV_BUF * G, FF_CHUNK), F32)

    def sample_conv(gt, cs):
        hx = jnp.concatenate([piece for n in range(n_tiles)
                              for piece in (hist_ref[n * CONV_BUF * G:(n + 1) * CONV_BUF * G, cs], zeros_tail)], axis=0)
        g1 = jnp.where(tok >= 1, _roll_rows(gt, G), _roll_rows(hx, -G))
        g2 = jnp.where(tok >= 2, _roll_rows(gt, 2 * G), hx)
        return g1, g2

    def sample_store(gt, cs):
        for n in range(n_tiles):
            cs_ref[n * CONV_BUF * G:(n + 1) * CONV_BUF * G, cs] = gt[n * R + (Ls - CONV_BUF) * G:(n + 1) * R]

    tiles = [_ffn_tile(xs_ref, wts, ys_ref, sample_conv, sample_store),
             _ffn_tile(xm_ref, wts, ym_ref, meta_conv, meta_store)]
    for _ in itertools.zip_longest(*tiles):
        pass


def _ffn_weight_specs(layer):
    shapes = [(D_MODEL, D_FF), (D_MODEL, D_FF), (3, D_FF), (1, D_FF), (D_FF, D_MODEL), (1, D_MODEL), (1, D_MODEL)]
    assert len(shapes) == N_FFN_W
    return [_layer_spec(s, layer) for s in shapes]


def _ffn_long(x, cprev, wts, layer):
    B, L, _ = x.shape
    T = min(FFN_TILE, L)
    P = LONG_PAR if B % LONG_PAR == 0 else 1
    assert L % T == 0
    kern = functools.partial(_ffn_long_kernel, n_par=P)
    return pl.pallas_call(
        kern,
        grid=(B // P, L // T),
        in_specs=[pl.BlockSpec((P, T, D_MODEL), lambda b, t: (b, t, 0)),
                  pl.BlockSpec((CONV_PAD, D_FF), lambda b, t: (0, 0))] + _ffn_weight_specs(layer),
        out_specs=[pl.BlockSpec((P, T, D_MODEL), lambda b, t: (b, t, 0)),
                   pl.BlockSpec((P, CONV_PAD, D_FF), lambda b, t: (b, 0, 0))],
        out_shape=[jax.ShapeDtypeStruct((B, L, D_MODEL), F32),
                   jax.ShapeDtypeStruct((B, CONV_PAD, D_FF), F32)],
        scratch_shapes=[pltpu.VMEM((P, T + CONV_PAD, D_FF), F32)],
        compiler_params=pltpu.CompilerParams(dimension_semantics=("arbitrary", "arbitrary"),
                                             vmem_limit_bytes=VMEM_LIMIT),
        name="ffn_long",
    )(x, cprev, *wts)


def _ffn_short(x_meta, x_samp, hist, wts, layer, *, seq_len):
    rows = x_samp.shape[0]
    n_hist = hist.shape[1]
    kern = functools.partial(_ffn_short_kernel, seq_len=seq_len)
    return pl.pallas_call(
        kern,
        grid=(1,),
        in_specs=[pl.BlockSpec((N_META, D_MODEL), lambda i: (0, 0)),
                  pl.BlockSpec((rows, D_MODEL), lambda i: (0, 0)),
                  pl.BlockSpec((None, n_hist, D_FF), lambda i: (layer, 0, 0))] + _ffn_weight_specs(layer),
        out_specs=[pl.BlockSpec((N_META, D_MODEL), lambda i: (0, 0)),
                   pl.BlockSpec((CONV_PAD, D_FF), lambda i: (0, 0)),
                   pl.BlockSpec((rows, D_MODEL), lambda i: (0, 0)),
                   pl.BlockSpec((n_hist, D_FF), lambda i: (0, 0))],
        out_shape=[jax.ShapeDtypeStruct((N_META, D_MODEL), F32),
                   jax.ShapeDtypeStruct((CONV_PAD, D_FF), F32),
                   jax.ShapeDtypeStruct((rows, D_MODEL), F32),
                   jax.ShapeDtypeStruct((n_hist, D_FF), F32)],
        compiler_params=pltpu.CompilerParams(dimension_semantics=("arbitrary",),
                                             vmem_limit_bytes=VMEM_LIMIT),
        name="ffn_short",
    )(x_meta, x_samp, hist, *wts)


def _to_group_major(a, axis):
    n, j = a.shape[axis], a.shape[axis + 1]
    lead, tail = a.shape[:axis], a.shape[axis + 2:]
    a = a.reshape(*lead, n // SEQ_GROUP, SEQ_GROUP, j, *tail)
    a = jnp.swapaxes(a, axis + 1, axis + 2)
    return a.reshape(*lead, n * j, *tail)


def _from_group_major(a, axis, j):
    rows = a.shape[axis]
    n = rows // j
    lead, tail = a.shape[:axis], a.shape[axis + 1:]
    a = a.reshape(*lead, n // SEQ_GROUP, j, SEQ_GROUP, *tail)
    a = jnp.swapaxes(a, axis + 1, axis + 2)
    return a.reshape(*lead, n, j, *tail)


def kernel(x_prompt, x_sample, state_pool, state_gla, state_conv, meta_tokens,
           w_in, w_a2, b_a, w_pool, pool_scale, gla_norm, w_out, ln1_g, ln1_b,
           w_up, w_gate, conv_w, conv_b, w_down, ln2_g, ln2_b):
    NB, LS = x_sample.shape[0], x_sample.shape[1]
    assert NB % SEQ_GROUP == 0 and LS & (LS - 1) == 0 and CONV_BUF <= LS <= POOL_BUF

    mix_w = (w_in.astype(BF16), w_a2.astype(BF16), b_a[:, None],
             w_pool.astype(BF16), pool_scale[:, None], gla_norm[:, None], w_out.astype(BF16),
             ln1_g[:, None], ln1_b[:, None])
    ffn_w = (w_up.astype(BF16), w_gate.astype(BF16), conv_w, conv_b[:, None], w_down.astype(BF16),
             ln2_g[:, None], ln2_b[:, None])

    hm = meta_tokens.astype(F32)
    hp = x_prompt
    hs = _to_group_major(x_sample, 0)
    pool_hist = _to_group_major(state_pool, 1)
    conv_hist = _to_group_major(state_conv, 1)

    pp, gp, cp, ps_l, cs_l = [], [], [], [], []
    gs = None
    for l in range(DEPTH):
        hm1, um, sm, hs1, ps_new, gs = _mixer_short(hm, hs, pool_hist, state_gla, mix_w, l, seq_len=LS, s_out=gs)
        hm, cm, hs, cs_new = _ffn_short(hm1, hs1, conv_hist, ffn_w, l, seq_len=LS)
        ps_l.append(ps_new)
        cs_l.append(cs_new)

        hp1, pbuf, snew = _mixer_long(hp, um, sm, mix_w, l)
        hp, cbuf = _ffn_long(hp1, cm, ffn_w, l)
        pp.append(pbuf[:, 1:])
        gp.append(snew)
        cp.append(cbuf[:, CONV_PAD - CONV_BUF:])

    ps = _from_group_major(jnp.stack(ps_l), 1, POOL_BUF)
    cs = _from_group_major(jnp.stack(cs_l), 1, CONV_BUF)
    return (hp, _from_group_major(hs, 0, LS), jnp.stack(pp), jnp.stack(gp), jnp.stack(cp), ps, gs, cs)
```

```python
import functools
import itertools

import jax
import jax.numpy as jnp
from jax import lax
from jax.experimental import pallas as pl
from jax.experimental.pallas import tpu as pltpu

F32 = jnp.float32
BF16 = jnp.bfloat16

D_MODEL = 1024
N_META = 16
D_POOL = 512
POOL_WINDOWS = (2, 4, 8, 16)
POOL_GROUP = 128
POOL_BUF = 15
POOL_PAD = 16
D_GLA = 512
GLA_HEADS = 4
GLA_DV = 128
GLA_DK = 64
D_GLA_K = 256
GATE_RANK = 16
GATE_TAU = 16.0
D_FF = 2816
CONV_BUF = 2
CONV_PAD = 8
DEPTH = 2
ALPHA = (2 * DEPTH) ** 0.25
LN_EPS = 1e-5
RMS_EPS = 1e-6
PAST_LEN = 16384

C_POOL, C_Q, C_K, C_V, C_R, C_Z, C_END = 0, 512, 768, 1024, 1536, 2048, 2064

LONG_TILE = 512
LONG_CHUNK = 64
LONG_PAR = 2
MIXER_PAR = 2
MIXER_SKEW = 1
FFN_TILE = 512
SEQ_GROUP = 16
FF_CHUNK = 256
VMEM_LIMIT = 56 * 1024 * 1024

N_MIX_W = 9
N_FFN_W = 7


def _dot(a, b):
    return jnp.dot(a, b, preferred_element_type=F32)


def _dot_nt(a, b):
    return lax.dot_general(a, b, (((1,), (1,)), ((), ())), preferred_element_type=F32)


def _dot_tn(a, b):
    return lax.dot_general(a, b, (((0,), (0,)), ((), ())), preferred_element_type=F32)


def _layer_norm(y, g, b):
    mu = jnp.mean(y, axis=-1, keepdims=True)
    yc = y - mu
    var = jnp.mean(yc * yc, axis=-1, keepdims=True)
    return yc * lax.rsqrt(var + LN_EPS) * g + b


def _silu(x):
    h = 0.5 * x
    return h + h * jnp.tanh(h)


def _log_sigmoid(z):
    return jnp.minimum(z, 0.0) - jnp.log(1.0 + jnp.exp(-jnp.abs(z)))


def _roll_rows(x, shift):
    n = x.shape[0]
    return pltpu.roll(x, shift % n, 0)


def _split_bf16(x):
    hi = x.astype(BF16)
    lo = (x - hi.astype(F32)).astype(BF16)
    return hi, lo


def _project_in(xb, w_in_ref):
    u = _dot(xb, w_in_ref[:, C_POOL:C_Q])
    zr = _dot(xb, w_in_ref[:, C_Z:C_END])
    q = _dot(xb, w_in_ref[:, C_Q:C_K]) * (GLA_DK ** -0.5)
    k = _dot(xb, w_in_ref[:, C_K:C_V])
    v = _dot(xb, w_in_ref[:, C_V:C_R])
    r = _dot(xb, w_in_ref[:, C_R:C_Z])
    return u, zr, q, k, v, r


def _gate_log_decay(zr, w_a2_ref, b_a_ref):
    z = _dot(zr.astype(BF16), w_a2_ref[...]) + b_a_ref[...]
    return _log_sigmoid(z) * (1.0 / GATE_TAU)


def _gla_output_gate(o, r, gnorm):
    parts = []
    for h in range(GLA_HEADS):
        oh = o[:, h * GLA_DV:(h + 1) * GLA_DV]
        ms = jnp.mean(oh * oh, axis=-1, keepdims=True)
        parts.append(oh * lax.rsqrt(ms + RMS_EPS) * gnorm)
    return jnp.concatenate(parts, axis=1) * _silu(r)


def _pool_project(d_groups, w_pool_ref, pscale_ref):
    ys = [_dot(d.astype(BF16), w_pool_ref[g]) for g, d in enumerate(d_groups)]
    return jnp.concatenate(ys, axis=1) * pscale_ref[...]


def _mix_out(x, y_pool, y_gla, w_out_ref, g_ref, b_ref):
    mix = jnp.concatenate([y_pool, y_gla], axis=1).astype(BF16)
    return _layer_norm(ALPHA * x + _dot(mix, w_out_ref[...]), g_ref[...], b_ref[...])


def _emit_staggered(stage_gens, first_round):
    live = list(zip(first_round, stage_gens))
    rnd = 0
    while live:
        for start, g in list(live):
            if rnd >= start and next(g, StopIteration) is StopIteration:
                live.remove((start, g))
        rnd += 1


def _head_pair_keys(k_p, lane128):
    zk = jnp.zeros_like(k_p)
    return jnp.concatenate([jnp.where(lane128 < GLA_DK, k_p, zk), jnp.where(lane128 >= GLA_DK, k_p, zk)], axis=0)


def _head_pair_values(v_p, lane256):
    zv = jnp.zeros_like(v_p)
    return jnp.concatenate([jnp.where(lane256 < GLA_DV, v_p, zv), jnp.where(lane256 >= GLA_DV, v_p, zv)], axis=0)


def _mixer_long_kernel(*refs, n_par, tile, chunk):
    x_ref, pprev_ref, s0_ref = refs[:3]
    wts = refs[3:3 + N_MIX_W]
    x1_ref, pbuf_ref, snew_ref, ubuf, sbd = refs[3 + N_MIX_W:]
    t = pl.program_id(1)
    zero_blk = jnp.zeros((GLA_DK, GLA_DV), F32)

    @pl.when(t == 0)
    def _init():
        for j in range(n_par):
            ubuf[j, 0:POOL_PAD, :] = pprev_ref[...]
            for p in range(2):
                top = jnp.concatenate([s0_ref[2 * p], zero_blk], axis=1)
                bot = jnp.concatenate([zero_blk, s0_ref[2 * p + 1]], axis=1)
                sbd[j, p] = jnp.concatenate([top, bot], axis=0)

    tiles = [_mixer_long_tile(x_ref.at[j], wts, x1_ref.at[j], ubuf.at[j], sbd.at[j], tile=tile, chunk=chunk)
             for j in range(n_par)]
    _emit_staggered(tiles, [MIXER_SKEW * j for j in range(n_par)])

    @pl.when(t == pl.num_programs(1) - 1)
    def _final():
        for j in range(n_par):
            pbuf_ref[j] = ubuf[j, 0:POOL_PAD, :]
            for p in range(2):
                s_p = sbd[j, p]
                snew_ref[j, 2 * p] = s_p[0:GLA_DK, 0:GLA_DV]
                snew_ref[j, 2 * p + 1] = s_p[GLA_DK:2 * GLA_DK, GLA_DV:2 * GLA_DV]


def _mixer_long_tile(x_ref, wts, x1_ref, ubuf, sbd, *, tile, chunk):
    w_in_ref, w_a2_ref, b_a_ref, w_pool_ref, pscale_ref, gnorm_ref, w_out_ref, g_ref, b_ref = wts
    T, C = tile, chunk
    x = x_ref[...]
    xb = x.astype(BF16)

    u, zr, q, k, v, r = _project_in(xb, w_in_ref)
    yield

    ubuf[POOL_PAD:POOL_PAD + T, :] = u
    d_groups = []
    for g, w in enumerate(POOL_WINDOWS):
        s = ubuf[:, g * POOL_GROUP:(g + 1) * POOL_GROUP]
        sh = 1
        while sh < w:
            s = s + _roll_rows(s, sh)
            sh *= 2
        d_groups.append(s[POOL_PAD:, :] * (1.0 / w) - u[:, g * POOL_GROUP:(g + 1) * POOL_GROUP])
    y_pool = _pool_project(d_groups, w_pool_ref, pscale_ref)
    ubuf[0:POOL_PAD, :] = ubuf[T:T + POOL_PAD, :]
    loga = _gate_log_decay(zr, w_a2_ref, b_a_ref)

    tr = lax.broadcasted_iota(jnp.int32, (C, C), 0)
    tc = lax.broadcasted_iota(jnp.int32, (C, C), 1)
    tri = jnp.where(tc <= tr, 1.0, 0.0).astype(BF16)
    ar = lax.broadcasted_iota(jnp.int32, (C, 2 * C), 0)
    ac = lax.broadcasted_iota(jnp.int32, (C, 2 * C), 1) & (C - 1)
    causal = ac <= ar
    lane128 = lax.broadcasted_iota(jnp.int32, (C, 128), 1)
    lane256 = lax.broadcasted_iota(jnp.int32, (C, 256), 1)
    sr = lax.broadcasted_iota(jnp.int32, (128, 256), 0)
    sc = lax.broadcasted_iota(jnp.int32, (128, 256), 1)
    blockdiag = (sr >= GLA_DK) == (sc >= GLA_DV)
    mid = C // 2 - 1

    n_chunks = T // C
    pairs = [(c, p) for c in range(n_chunks) for p in range(2)]
    ks = [slice(128 * p, 128 * (p + 1)) for p in range(2)]
    vs = [slice(256 * p, 256 * (p + 1)) for p in range(2)]

    bcs = []
    for c in range(n_chunks):
        la_hi, la_lo = _split_bf16(loga[c * C:(c + 1) * C])
        bb = _dot(tri, jnp.concatenate([la_hi, la_lo], axis=1))
        bcs.append(bb[:, :D_GLA_K] + bb[:, D_GLA_K:])
    yield

    q_in, k_in, q_st, k_st, dec_t, vc = [], [], [], [], [], []
    for c in range(n_chunks):
        bc = bcs[c]
        bmid = bc[mid:mid + 1]
        bend = bc[C - 1:C]
        qc = q[c * C:(c + 1) * C]
        kc = k[c * C:(c + 1) * C]
        q_in.append((qc * jnp.exp(bc - bmid)).astype(BF16))
        k_in.append((kc * jnp.exp(bmid - bc)).astype(BF16))
        q_st.append((qc * jnp.exp(bc)).astype(BF16))
        k_st.append((kc * jnp.exp(bend - bc)).astype(BF16))
        dec_t.append(jnp.transpose(jnp.broadcast_to(jnp.exp(bend), (128, D_GLA_K))))
        vc.append(v[c * C:(c + 1) * C].astype(BF16))
    yield

    attn, upd = {}, {}
    for c, p in pairs:
        a = _dot_nt(q_in[c][:, ks[p]], _head_pair_keys(k_in[c][:, ks[p]], lane128))
        attn[c, p] = jnp.where(causal, a, 0.0).astype(BF16)
    for c, p in pairs:
        u_cp = _dot_tn(k_st[c][:, ks[p]], vc[c][:, vs[p]])
        upd[c, p] = jnp.where(blockdiag, u_cp, 0.0)
    yield

    s_vals = [sbd[p] for p in range(2)]
    s_start = {}
    for c, p in pairs:
        s_start[c, p] = s_vals[p].astype(BF16)
        dec_p = dec_t[c][ks[p], :]
        s_vals[p] = jnp.concatenate([dec_p, dec_p], axis=1) * s_vals[p] + upd[c, p]
    for p in range(2):
        sbd[p] = s_vals[p]

    o_rows = [[], []]
    for c, p in pairs:
        vblk = _head_pair_values(vc[c][:, vs[p]], lane256)
        lhs = jnp.concatenate([attn[c, p], q_st[c][:, ks[p]]], axis=1)
        rhs = jnp.concatenate([vblk, s_start[c, p]], axis=0)
        o_rows[p].append(_dot(lhs, rhs))
    o = jnp.concatenate([jnp.concatenate(o_rows[p], axis=0) for p in range(2)], axis=1)
    yield

    y_gla = _gla_output_gate(o, r, gnorm_ref[...])
    x1_ref[...] = _mix_out(x, y_pool, y_gla, w_out_ref, g_ref, b_ref)


def _layer_spec(shape, layer):
    nd = len(shape)
    return pl.BlockSpec((None,) + shape, lambda *_: (layer,) + (0,) * nd, pipeline_mode=pl.Buffered(1))


def _mixer_weight_specs(layer):
    shapes = [(D_MODEL, C_END), (GATE_RANK, D_GLA_K), (1, D_GLA_K),
              (4, POOL_GROUP, POOL_GROUP), (1, D_POOL), (1, GLA_DV), (D_MODEL, D_MODEL), (1, D_MODEL), (1, D_MODEL)]
    assert len(shapes) == N_MIX_W
    return [_layer_spec(s, layer) for s in shapes]


def _mixer_long(x, pprev, s0, wts, layer):
    B, L, _ = x.shape
    T = min(LONG_TILE, L)
    P = MIXER_PAR if B % MIXER_PAR == 0 else 1
    assert L % T == 0 and T % LONG_CHUNK == 0
    kern = functools.partial(_mixer_long_kernel, n_par=P, tile=T, chunk=LONG_CHUNK)
    return pl.pallas_call(
        kern,
        grid=(B // P, L // T),
        in_specs=[pl.BlockSpec((P, T, D_MODEL), lambda b, t: (b, t, 0)),
                  pl.BlockSpec((POOL_PAD, D_POOL), lambda b, t: (0, 0)),
                  pl.BlockSpec((GLA_HEADS, GLA_DK, GLA_DV), lambda b, t: (0, 0, 0))]
        + _mixer_weight_specs(layer),
        out_specs=[pl.BlockSpec((P, T, D_MODEL), lambda b, t: (b, t, 0)),
                   pl.BlockSpec((P, POOL_PAD, D_POOL), lambda b, t: (b, 0, 0)),
                   pl.BlockSpec((P, GLA_HEADS, GLA_DK, GLA_DV), lambda b, t: (b, 0, 0, 0))],
        out_shape=[jax.ShapeDtypeStruct((B, L, D_MODEL), F32),
                   jax.ShapeDtypeStruct((B, POOL_PAD, D_POOL), F32),
                   jax.ShapeDtypeStruct((B, GLA_HEADS, GLA_DK, GLA_DV), F32)],
        scratch_shapes=[pltpu.VMEM((P, T + POOL_PAD, D_POOL), F32),
                        pltpu.VMEM((P, 2, 2 * GLA_DK, 2 * GLA_DV), F32)],
        compiler_params=pltpu.CompilerParams(dimension_semantics=("arbitrary", "arbitrary"),
                                             vmem_limit_bytes=VMEM_LIMIT),
        name="mixer_long",
    )(x, pprev, s0, *wts)


def _mixer_meta(x_ref, wts, x1_ref, u_ref, s_ref):
    w_in_ref, w_a2_ref, b_a_ref, w_pool_ref, pscale_ref, gnorm_ref, w_out_ref, g_ref, b_ref = wts
    L = N_META
    x = x_ref[...]
    xb = x.astype(BF16)
    u, zr, q, k, v, r = _project_in(xb, w_in_ref)
    u_ref[...] = u

    row128 = lax.broadcasted_iota(jnp.int32, (L, POOL_GROUP), 0)
    pos1 = lax.broadcasted_iota(jnp.int32, (L, 1), 0)
    d_groups = []
    for g, w in enumerate(POOL_WINDOWS):
        ug = u[:, g * POOL_GROUP:(g + 1) * POOL_GROUP]
        s = ug
        sh = 1
        while sh < w:
            s = s + jnp.where(row128 >= sh, _roll_rows(s, sh), 0.0)
            sh *= 2
        d_groups.append(s / jnp.minimum(w, pos1 + 1).astype(F32) - ug)
    y_pool = _pool_project(d_groups, w_pool_ref, pscale_ref)

    loga = _gate_log_decay(zr, w_a2_ref, b_a_ref)
    row256 = lax.broadcasted_iota(jnp.int32, (L, D_GLA_K), 0)
    b = loga
    sh = 1
    while sh < L:
        b = b + jnp.where(row256 >= sh, _roll_rows(b, sh), 0.0)
        sh *= 2
    bend = b[L - 1:L]
    q_in = (q * jnp.exp(b)).astype(BF16)
    k_in = (k * jnp.exp(-b)).astype(BF16)
    k_st = (k * jnp.exp(bend - b)).astype(BF16)
    vb = v.astype(BF16)
    ar = lax.broadcasted_iota(jnp.int32, (L, 2 * L), 0)
    ac = lax.broadcasted_iota(jnp.int32, (L, 2 * L), 1) & (L - 1)
    causal = ac <= ar
    lane128 = lax.broadcasted_iota(jnp.int32, (L, 128), 1)
    lane256 = lax.broadcasted_iota(jnp.int32, (L, 256), 1)
    attn = []
    for p in range(2):
        ks = slice(128 * p, 128 * (p + 1))
        a = _dot_nt(q_in[:, ks], _head_pair_keys(k_in[:, ks], lane128))
        attn.append(jnp.where(causal, a, 0.0).astype(BF16))
    o_parts = []
    for p in range(2):
        ks = slice(128 * p, 128 * (p + 1))
        vs = slice(256 * p, 256 * (p + 1))
        s_p = _dot_tn(k_st[:, ks], vb[:, vs])
        s_ref[2 * p] = s_p[0:GLA_DK, 0:GLA_DV]
        s_ref[2 * p + 1] = s_p[GLA_DK:2 * GLA_DK, GLA_DV:2 * GLA_DV]
        o_parts.append(_dot(attn[p], _head_pair_values(vb[:, vs], lane256)))
    o = jnp.concatenate(o_parts, axis=1)

    y_gla = _gla_output_gate(o, r, gnorm_ref[...])
    x1_ref[...] = _mix_out(x, y_pool, y_gla, w_out_ref, g_ref, b_ref)


def _mixer_sample(x_ref, hist_ref, s0_ref, wts, x1_ref, hist_out_ref, snew_ref, *, seq_len):
    w_in_ref, w_a2_ref, b_a_ref, w_pool_ref, pscale_ref, gnorm_ref, w_out_ref, g_ref, b_ref = wts
    G, Ls = SEQ_GROUP, seq_len
    R = G * Ls
    NS = G * GLA_DK
    g_shift = G.bit_length() - 1
    x = x_ref[...]
    xb = x.astype(BF16)
    u, zr, q, k, v, r = _project_in(xb, w_in_ref)
    hist_out_ref[0:(POOL_BUF - Ls) * G, :] = hist_ref[R:POOL_BUF * G, :]
    hist_out_ref[(POOL_BUF - Ls) * G:POOL_BUF * G, :] = u

    def blk(a, t):
        return a[t * G:(t + 1) * G]

    d_groups = []
    for g, w in enumerate(POOL_WINDOWS):
        cols = slice(g * POOL_GROUP, (g + 1) * POOL_GROUP)
        ug = u[:, cols]
        suffix = [None]
        acc = None
        for m in range(1, min(w - 1, POOL_BUF) + 1):
            h = hist_ref[(POOL_BUF - m) * G:(POOL_BUF - m + 1) * G, cols]
            acc = h if acc is None else acc + h
            suffix.append(acc)
        parts = []
        for t in range(Ls):
            wsum = blk(ug, t)
            for j in range(max(0, t - w + 1), t):
                wsum = wsum + blk(ug, j)
            m = w - 1 - t
            if m > 0:
                wsum = wsum + suffix[m]
            parts.append(wsum * (1.0 / w) - blk(ug, t))
        d_groups.append(jnp.concatenate(parts, axis=0))
    y_pool = _pool_project(d_groups, w_pool_ref, pscale_ref)

    loga = _gate_log_decay(zr, w_a2_ref, b_a_ref)
    b_t = [blk(loga, 0)]
    for t in range(1, Ls):
        b_t.append(b_t[-1] + blk(loga, t))
    b = jnp.concatenate(b_t, axis=0)
    bend = jnp.concatenate([b_t[-1]] * Ls, axis=0)
    q_in = (q * jnp.exp(b)).astype(BF16)
    k_in = (k * jnp.exp(-b)).astype(BF16)
    k_st = k * jnp.exp(bend - b)
    dec_hi, dec_lo = _split_bf16(jnp.exp(bend))
    tok = lax.broadcasted_iota(jnp.int32, (R, D_GLA_K), 0) >> g_shift
    dec_rows = jnp.where(tok == Ls - 1, dec_hi, jnp.where(tok == Ls - 2, dec_lo, jnp.zeros_like(dec_lo)))
    vb = v.astype(BF16)

    ar = lax.broadcasted_iota(jnp.int32, (R, 2 * R), 0)
    ac = lax.broadcasted_iota(jnp.int32, (R, 2 * R), 1) & (R - 1)
    same_seq_causal = ((ac & (G - 1)) == (ar & (G - 1))) & ((ac >> g_shift) <= (ar >> g_shift))
    lane128 = lax.broadcasted_iota(jnp.int32, (R, 128), 1)
    lane256 = lax.broadcasted_iota(jnp.int32, (R, 256), 1)
    br = lax.broadcasted_iota(jnp.int32, (R, NS), 0)
    bcol = lax.broadcasted_iota(jnp.int32, (R, NS), 1)
    own_state = (bcol >> 6) == (br & (G - 1))
    ones_blk = jnp.ones((R, GLA_DV), BF16)
    zeros_blk = jnp.zeros((R, GLA_DV), BF16)

    def expand(xp, first):
        sw = pltpu.roll(xp, GLA_DK, 1)
        two = jnp.where(lane128 < GLA_DK, xp, sw) if first else jnp.where(lane128 < GLA_DK, sw, xp)
        rep = jnp.concatenate([two] * (NS // 128), axis=1)
        return jnp.where(own_state, rep, 0.0).astype(BF16)

    ks = [slice(128 * p, 128 * (p + 1)) for p in range(2)]
    vs = [slice(256 * p, 256 * (p + 1)) for p in range(2)]
    attn = []
    for p in range(2):
        a = _dot_nt(q_in[:, ks[p]], _head_pair_keys(k_in[:, ks[p]], lane128))
        attn.append(jnp.where(same_seq_causal, a, 0.0).astype(BF16))
    inter = []
    for h in range(GLA_HEADS):
        p, first = h // 2, h % 2 == 0
        s_flat = s0_ref[:, h].reshape(NS, GLA_DV)
        inter.append(_dot(expand(q_in[:, ks[p]].astype(F32), first), s_flat.astype(BF16)))
    for h in range(GLA_HEADS):
        p, first = h // 2, h % 2 == 0
        s_flat = s0_ref[:, h].reshape(NS, GLA_DV)
        lhs = jnp.concatenate([expand(k_st[:, ks[p]], first),
                               expand(dec_rows[:, ks[p]].astype(F32), first)], axis=0)
        v_h = vb[:, h * GLA_DV:(h + 1) * GLA_DV]
        rhs = jnp.concatenate([jnp.concatenate([v_h, zeros_blk], axis=1),
                               jnp.concatenate([zeros_blk, ones_blk], axis=1)], axis=0)
        ud = _dot_tn(lhs, rhs)
        s_new = ud[:, GLA_DV:] * s_flat + ud[:, :GLA_DV]
        snew_ref[:, h] = s_new.reshape(G, GLA_DK, GLA_DV)
    o_parts = []
    for p in range(2):
        o_intra = _dot(attn[p], _head_pair_values(vb[:, vs[p]], lane256))
        o_parts.append(o_intra + jnp.concatenate(inter[2 * p:2 * p + 2], axis=1))
    o = jnp.concatenate(o_parts, axis=1)

    y_gla = _gla_output_gate(o, r, gnorm_ref[...])
    x1_ref[...] = _mix_out(x, y_pool, y_gla, w_out_ref, g_ref, b_ref)


def _mixer_short_kernel(*refs, seq_len, n_prev):
    xm_ref, xs_ref, hist_ref, s0_ref = refs[:4]
    wts = refs[4:4 + N_MIX_W]
    x1m_ref, um_ref, sm_ref, x1s_ref, us_ref, ss_ref = refs[-6:]
    i = pl.program_id(0)

    @pl.when(i == 0)
    def _meta():
        _mixer_meta(xm_ref, wts, x1m_ref, um_ref, sm_ref)

    @pl.when(i > 0)
    def _sample():
        for a in range(n_prev):
            ss_ref[a] = refs[4 + N_MIX_W][a]
        _mixer_sample(xs_ref, hist_ref, s0_ref, wts, x1s_ref, us_ref, ss_ref.at[n_prev], seq_len=seq_len)


def _mixer_short(x_meta, x_samp, hist, s0, wts, layer, *, seq_len, s_prev=None):
    rows = x_samp.shape[0]
    G = SEQ_GROUP
    R = G * seq_len
    n_tiles = rows // R
    assert rows % R == 0
    clamp = lambda i: jnp.maximum(i - 1, 0)
    n_prev = 0 if s_prev is None else s_prev.shape[0]
    state_blk = (G, GLA_HEADS, GLA_DK, GLA_DV)
    in_specs = ([pl.BlockSpec((N_META, D_MODEL), lambda i: (0, 0)),
                 pl.BlockSpec((R, D_MODEL), lambda i: (clamp(i), 0)),
                 pl.BlockSpec((None, G * POOL_BUF, D_POOL), lambda i: (layer, clamp(i), 0)),
                 pl.BlockSpec((None,) + state_blk, lambda i: (layer, clamp(i), 0, 0, 0))]
                + _mixer_weight_specs(layer))
    args = [x_meta, x_samp, hist, s0, *wts]
    if n_prev:
        in_specs.append(pl.BlockSpec((n_prev,) + state_blk, lambda i: (0, clamp(i), 0, 0, 0)))
        args.append(s_prev)
    kern = functools.partial(_mixer_short_kernel, seq_len=seq_len, n_prev=n_prev)

    return pl.pallas_call(
        kern,
        grid=(n_tiles + 1,),
        in_specs=in_specs,
        out_specs=[pl.BlockSpec((N_META, D_MODEL), lambda i: (0, 0)),
                   pl.BlockSpec((N_META, D_POOL), lambda i: (0, 0)),
                   pl.BlockSpec((GLA_HEADS, GLA_DK, GLA_DV), lambda i: (0, 0, 0)),
                   pl.BlockSpec((R, D_MODEL), lambda i: (clamp(i), 0)),
                   pl.BlockSpec((G * POOL_BUF, D_POOL), lambda i: (clamp(i), 0)),
                   pl.BlockSpec((n_prev + 1,) + state_blk, lambda i: (0, clamp(i), 0, 0, 0))],
        out_shape=[jax.ShapeDtypeStruct((N_META, D_MODEL), F32),
                   jax.ShapeDtypeStruct((N_META, D_POOL), F32),
                   jax.ShapeDtypeStruct((GLA_HEADS, GLA_DK, GLA_DV), F32),
                   jax.ShapeDtypeStruct((rows, D_MODEL), F32),
                   jax.ShapeDtypeStruct((n_tiles * G * POOL_BUF, D_POOL), F32),
                   jax.ShapeDtypeStruct((n_prev + 1,) + s0.shape[1:], F32)],
        compiler_params=pltpu.CompilerParams(dimension_semantics=("arbitrary",),
                                             vmem_limit_bytes=VMEM_LIMIT),
        name="mixer_short",
    )(*args)


def _ffn_tile(x_ref, wts, y_ref, conv_inputs, store_gate):
    w_up_ref, w_gate_ref, cw_ref, cb_ref, w_down_ref, g_ref, b_ref = wts
    x = x_ref[...]
    xb = x.astype(BF16)

    acts = []
    for j in range(D_FF // FF_CHUNK):
        cs = slice(j * FF_CHUNK, (j + 1) * FF_CHUNK)
        a = _dot(xb, w_up_ref[:, cs])
        gt = _dot(xb, w_gate_ref[:, cs])
        g1, g2 = conv_inputs(gt, cs)
        store_gate(gt, cs)
        gc = cb_ref[:, cs] + cw_ref[0:1, cs] * g2 + cw_ref[1:2, cs] * g1 + cw_ref[2:3, cs] * gt
        acts.append((a * _silu(gc)).astype(BF16))
    yield

    f = _dot(jnp.concatenate(acts, axis=1), w_down_ref[...])
    yield

    y_ref[...] = _layer_norm(ALPHA * x + f, g_ref[...], b_ref[...])


def _ffn_long_kernel(*refs, n_par):
    x_ref, cprev_ref = refs[:2]
    wts = refs[2:2 + N_FFN_W]
    y_ref, hist_out_ref, gbuf = refs[2 + N_FFN_W:]
    t = pl.program_id(1)
    T = x_ref.shape[1]

    @pl.when(t == 0)
    def _init():
        for j in range(n_par):
            gbuf[j, 0:CONV_PAD, :] = cprev_ref[...]

    def make_tile(j):
        def conv_inputs(gt, cs):
            gbuf[j, CONV_PAD:CONV_PAD + T, cs] = gt
            return gbuf[j, CONV_PAD - 1:CONV_PAD - 1 + T, cs], gbuf[j, CONV_PAD - 2:CONV_PAD - 2 + T, cs]

        def store_gate(gt, cs):
            gbuf[j, 0:CONV_PAD, cs] = gbuf[j, T:T + CONV_PAD, cs]

        return _ffn_tile(x_ref.at[j], wts, y_ref.at[j], conv_inputs, store_gate)

    for _ in itertools.zip_longest(*[make_tile(j) for j in range(n_par)]):
        pass

    @pl.when(t == pl.num_programs(1) - 1)
    def _final():
        for j in range(n_par):
            hist_out_ref[j] = gbuf[j, 0:CONV_PAD, :]


def _ffn_short_kernel(*refs, seq_len):
    xm_ref, xs_ref, hist_ref = refs[:3]
    wts = refs[3:3 + N_FFN_W]
    ym_ref, cm_ref, ys_ref, cs_ref = refs[3 + N_FFN_W:]
    G, Ls = SEQ_GROUP, seq_len
    R = G * Ls
    rows = xs_ref.shape[0]
    n_tiles = rows // R

    rowm = lax.broadcasted_iota(jnp.int32, (N_META, FF_CHUNK), 0)

    def meta_conv(gt, cs):
        return (jnp.where(rowm >= 1, _roll_rows(gt, 1), 0.0), jnp.where(rowm >= 2, _roll_rows(gt, 2), 0.0))

    def meta_store(gt, cs):
        cm_ref[:, cs] = gt[N_META - CONV_PAD:N_META]

    tok = (lax.broadcasted_iota(jnp.int32, (rows, FF_CHUNK), 0) & (R - 1)) >> (G.bit_length() - 1)
    zeros_tail = jnp.zeros((R - CONV_BUF * G, FF_CHUNK), F32)

    def sample_conv(gt, cs):
        hx = jnp.concatenate([piece for n in range(n_tiles)
                              for piece in (hist_ref[n * CONV_BUF * G:(n + 1) * CONV_BUF * G, cs], zeros_tail)], axis=0)
        g1 = jnp.where(tok >= 1, _roll_rows(gt, G), _roll_rows(hx, -G))
        g2 = jnp.where(tok >= 2, _roll_rows(gt, 2 * G), hx)
        return g1, g2

    def sample_store(gt, cs):
        for n in range(n_tiles):
            cs_ref[n * CONV_BUF * G:(n + 1) * CONV_BUF * G, cs] = gt[n * R + (Ls - CONV_BUF) * G:(n + 1) * R]

    tiles = [_ffn_tile(xs_ref, wts, ys_ref, sample_conv, sample_store),
             _ffn_tile(xm_ref, wts, ym_ref, meta_conv, meta_store)]
    for _ in itertools.zip_longest(*tiles):
        pass


def _ffn_weight_specs(layer):
    shapes = [(D_MODEL, D_FF), (D_MODEL, D_FF), (3, D_FF), (1, D_FF), (D_FF, D_MODEL), (1, D_MODEL), (1, D_MODEL)]
    assert len(shapes) == N_FFN_W
    return [_layer_spec(s, layer) for s in shapes]


def _ffn_long(x, cprev, wts, layer):
    B, L, _ = x.shape
    T = min(FFN_TILE, L)
    P = LONG_PAR if B % LONG_PAR == 0 else 1
    assert L % T == 0
    kern = functools.partial(_ffn_long_kernel, n_par=P)
    return pl.pallas_call(
        kern,
        grid=(B // P, L // T),
        in_specs=[pl.BlockSpec((P, T, D_MODEL), lambda b, t: (b, t, 0)),
                  pl.BlockSpec((CONV_PAD, D_FF), lambda b, t: (0, 0))] + _ffn_weight_specs(layer),
        out_specs=[pl.BlockSpec((P, T, D_MODEL), lambda b, t: (b, t, 0)),
                   pl.BlockSpec((P, CONV_PAD, D_FF), lambda b, t: (b, 0, 0))],
        out_shape=[jax.ShapeDtypeStruct((B, L, D_MODEL), F32),
                   jax.ShapeDtypeStruct((B, CONV_PAD, D_FF), F32)],
        scratch_shapes=[pltpu.VMEM((P, T + CONV_PAD, D_FF), F32)],
        compiler_params=pltpu.CompilerParams(dimension_semantics=("arbitrary", "arbitrary"),
                                             vmem_limit_bytes=VMEM_LIMIT),
        name="ffn_long",
    )(x, cprev, *wts)


def _ffn_short(x_meta, x_samp, hist, wts, layer, *, seq_len):
    rows = x_samp.shape[0]
    n_hist = hist.shape[1]
    kern = functools.partial(_ffn_short_kernel, seq_len=seq_len)
    return pl.pallas_call(
        kern,
        grid=(1,),
        in_specs=[pl.BlockSpec((N_META, D_MODEL), lambda i: (0, 0)),
                  pl.BlockSpec((rows, D_MODEL), lambda i: (0, 0)),
                  pl.BlockSpec((None, n_hist, D_FF), lambda i: (layer, 0, 0))] + _ffn_weight_specs(layer),
        out_specs=[pl.BlockSpec((N_META, D_MODEL), lambda i: (0, 0)),
                   pl.BlockSpec((CONV_PAD, D_FF), lambda i: (0, 0)),
                   pl.BlockSpec((rows, D_MODEL), lambda i: (0, 0)),
                   pl.BlockSpec((n_hist, D_FF), lambda i: (0, 0))],
        out_shape=[jax.ShapeDtypeStruct((N_META, D_MODEL), F32),
                   jax.ShapeDtypeStruct((CONV_PAD, D_FF), F32),
                   jax.ShapeDtypeStruct((rows, D_MODEL), F32),
                   jax.ShapeDtypeStruct((n_hist, D_FF), F32)],
        compiler_params=pltpu.CompilerParams(dimension_semantics=("arbitrary",),
                                             vmem_limit_bytes=VMEM_LIMIT),
        name="ffn_short",
    )(x_meta, x_samp, hist, *wts)


def _to_group_major(a, axis):
    n, j = a.shape[axis], a.shape[axis + 1]
    lead, tail = a.shape[:axis], a.shape[axis + 2:]
    a = a.reshape(*lead, n // SEQ_GROUP, SEQ_GROUP, j, *tail)
    a = jnp.swapaxes(a, axis + 1, axis + 2)
    return a.reshape(*lead, n * j, *tail)


def _from_group_major(a, axis, j):
    rows = a.shape[axis]
    n = rows // j
    lead, tail = a.shape[:axis], a.shape[axis + 1:]
    a = a.reshape(*lead, n // SEQ_GROUP, j, SEQ_GROUP, *tail)
    a = jnp.swapaxes(a, axis + 1, axis + 2)
    return a.reshape(*lead, n, j, *tail)


def kernel(x_prompt, x_sample, state_pool, state_gla, state_conv, meta_tokens,
           w_in, w_a2, b_a, w_pool, pool_scale, gla_norm, w_out, ln1_g, ln1_b,
           w_up, w_gate, conv_w, conv_b, w_down, ln2_g, ln2_b):
    NB, LS = x_sample.shape[0], x_sample.shape[1]
    assert NB % SEQ_GROUP == 0 and LS & (LS - 1) == 0 and CONV_BUF <= LS <= POOL_BUF

    mix_w = (w_in.astype(BF16), w_a2.astype(BF16), b_a[:, None],
             w_pool.astype(BF16), pool_scale[:, None], gla_norm[:, None], w_out.astype(BF16),
             ln1_g[:, None], ln1_b[:, None])
    ffn_w = (w_up.astype(BF16), w_gate.astype(BF16), conv_w, conv_b[:, None], w_down.astype(BF16),
             ln2_g[:, None], ln2_b[:, None])

    hm = meta_tokens.astype(F32)
    hp = x_prompt
    hs = _to_group_major(x_sample, 0)
    pool_hist = _to_group_major(state_pool, 1)
    conv_hist = _to_group_major(state_conv, 1)

    pp, gp, cp, ps_l, cs_l = [], [], [], [], []
    gs = None
    for l in range(DEPTH):
        hm1, um, sm, hs1, ps_new, gs = _mixer_short(hm, hs, pool_hist, state_gla, mix_w, l, seq_len=LS, s_prev=gs)
        hm, cm, hs, cs_new = _ffn_short(hm1, hs1, conv_hist, ffn_w, l, seq_len=LS)
        ps_l.append(ps_new)
        cs_l.append(cs_new)

        hp1, pbuf, snew = _mixer_long(hp, um, sm, mix_w, l)
        hp, cbuf = _ffn_long(hp1, cm, ffn_w, l)
        pp.append(pbuf[:, 1:])
        gp.append(snew)
        cp.append(cbuf[:, CONV_PAD - CONV_BUF:])

    ps = _from_group_major(jnp.stack(ps_l), 1, POOL_BUF)
    cs = _from_group_major(jnp.stack(cs_l), 1, CONV_BUF)
    return (hp, _from_group_major(hs, 0, LS), jnp.stack(pp), jnp.stack(gp), jnp.stack(cp), ps, gs, cs)
```

```python
import functools
import itertools

import jax
import jax.numpy as jnp
from jax import lax
from jax.experimental import pallas as pl
from jax.experimental.pallas import tpu as pltpu

F32 = jnp.float32
BF16 = jnp.bfloat16

D_MODEL = 1024
N_META = 16
D_POOL = 512
POOL_WINDOWS = (2, 4, 8, 16)
POOL_GROUP = 128
POOL_BUF = 15
POOL_PAD = 16
D_GLA = 512
GLA_HEADS = 4
GLA_DV = 128
GLA_DK = 64
D_GLA_K = 256
GATE_RANK = 16
GATE_TAU = 16.0
D_FF = 2816
CONV_BUF = 2
CONV_PAD = 8
DEPTH = 2
ALPHA = (2 * DEPTH) ** 0.25
LN_EPS = 1e-5
RMS_EPS = 1e-6
PAST_LEN = 16384

C_POOL, C_Q, C_K, C_V, C_R, C_Z, C_END = 0, 512, 768, 1024, 1536, 2048, 2064

LONG_TILE = 512
LONG_CHUNK = 64
LONG_PAR = 2
MIXER_PAR = 2
MIXER_SKEW = 1
FFN_TILE = 512
SEQ_GROUP = 16
GROUPS_PER_STEP = 2
FF_CHUNK = 256
DOWN_ROWS = 256
VMEM_LIMIT = 56 * 1024 * 1024

N_MIX_W = 9
N_FFN_W = 7


def _dot(a, b):
    return jnp.dot(a, b, preferred_element_type=F32)


def _dot_nt(a, b):
    return lax.dot_general(a, b, (((1,), (1,)), ((), ())), preferred_element_type=F32)


def _dot_tn(a, b):
    return lax.dot_general(a, b, (((0,), (0,)), ((), ())), preferred_element_type=F32)


def _layer_norm(y, g, b):
    mu = jnp.mean(y, axis=-1, keepdims=True)
    yc = y - mu
    var = jnp.mean(yc * yc, axis=-1, keepdims=True)
    return yc * lax.rsqrt(var + LN_EPS) * g + b


def _silu(x):
    h = 0.5 * x
    return h + h * jnp.tanh(h)


def _log_sigmoid(z):
    return jnp.minimum(z, 0.0) - jnp.log(1.0 + jnp.exp(-jnp.abs(z)))


def _roll_rows(x, shift):
    n = x.shape[0]
    return pltpu.roll(x, shift % n, 0)


def _split_bf16(x):
    hi = x.astype(BF16)
    lo = (x - hi.astype(F32)).astype(BF16)
    return hi, lo


def _project_in(xb, w_in_ref):
    u = _dot(xb, w_in_ref[:, C_POOL:C_Q])
    zr = _dot(xb, w_in_ref[:, C_Z:C_END])
    q = _dot(xb, w_in_ref[:, C_Q:C_K]) * (GLA_DK ** -0.5)
    k = _dot(xb, w_in_ref[:, C_K:C_V])
    v = _dot(xb, w_in_ref[:, C_V:C_R])
    r = _dot(xb, w_in_ref[:, C_R:C_Z])
    return u, zr, q, k, v, r


def _gate_log_decay(zr, w_a2_ref, b_a_ref):
    z = _dot(zr.astype(BF16), w_a2_ref[...]) + b_a_ref[...]
    return _log_sigmoid(z) * (1.0 / GATE_TAU)


def _gla_output_gate(o, r, gnorm):
    parts = []
    for h in range(GLA_HEADS):
        oh = o[:, h * GLA_DV:(h + 1) * GLA_DV]
        ms = jnp.mean(oh * oh, axis=-1, keepdims=True)
        parts.append(oh * lax.rsqrt(ms + RMS_EPS) * gnorm)
    return jnp.concatenate(parts, axis=1) * _silu(r)


def _pool_project(d_groups, w_pool_ref, pscale_ref):
    ys = [_dot(d.astype(BF16), w_pool_ref[g]) for g, d in enumerate(d_groups)]
    return jnp.concatenate(ys, axis=1) * pscale_ref[...]


def _mix_out(x, y_pool, y_gla, w_out_ref, g_ref, b_ref):
    mix = jnp.concatenate([y_pool, y_gla], axis=1).astype(BF16)
    return _layer_norm(ALPHA * x + _dot(mix, w_out_ref[...]), g_ref[...], b_ref[...])


def _emit_staggered(stage_gens, first_round):
    live = list(zip(first_round, stage_gens))
    rnd = 0
    while live:
        for start, g in list(live):
            if rnd >= start and next(g, StopIteration) is StopIteration:
                live.remove((start, g))
        rnd += 1


def _head_pair_keys(k_p, lane128):
    zk = jnp.zeros_like(k_p)
    return jnp.concatenate([jnp.where(lane128 < GLA_DK, k_p, zk), jnp.where(lane128 >= GLA_DK, k_p, zk)], axis=0)


def _head_pair_values(v_p, lane256):
    zv = jnp.zeros_like(v_p)
    return jnp.concatenate([jnp.where(lane256 < GLA_DV, v_p, zv), jnp.where(lane256 >= GLA_DV, v_p, zv)], axis=0)


def _mixer_long_kernel(*refs, n_par, tile, chunk):
    x_ref, pprev_ref, s0_ref = refs[:3]
    wts = refs[3:3 + N_MIX_W]
    x1_ref, pbuf_ref, snew_ref, ubuf, sbd = refs[3 + N_MIX_W:]
    t = pl.program_id(1)
    zero_blk = jnp.zeros((GLA_DK, GLA_DV), F32)

    @pl.when(t == 0)
    def _init():
        for j in range(n_par):
            ubuf[j, 0:POOL_PAD, :] = pprev_ref[...]
            for p in range(2):
                top = jnp.concatenate([s0_ref[2 * p], zero_blk], axis=1)
                bot = jnp.concatenate([zero_blk, s0_ref[2 * p + 1]], axis=1)
                sbd[j, p] = jnp.concatenate([top, bot], axis=0)

    tiles = [_mixer_long_tile(x_ref.at[j], wts, x1_ref.at[j], ubuf.at[j], sbd.at[j], tile=tile, chunk=chunk)
             for j in range(n_par)]
    _emit_staggered(tiles, [MIXER_SKEW * j for j in range(n_par)])

    @pl.when(t == pl.num_programs(1) - 1)
    def _final():
        for j in range(n_par):
            pbuf_ref[j] = ubuf[j, 0:POOL_PAD, :]
            for p in range(2):
                s_p = sbd[j, p]
                snew_ref[j, 2 * p] = s_p[0:GLA_DK, 0:GLA_DV]
                snew_ref[j, 2 * p + 1] = s_p[GLA_DK:2 * GLA_DK, GLA_DV:2 * GLA_DV]


def _mixer_long_tile(x_ref, wts, x1_ref, ubuf, sbd, *, tile, chunk):
    w_in_ref, w_a2_ref, b_a_ref, w_pool_ref, pscale_ref, gnorm_ref, w_out_ref, g_ref, b_ref = wts
    T, C = tile, chunk
    x = x_ref[...]
    xb = x.astype(BF16)

    u, zr, q, k, v, r = _project_in(xb, w_in_ref)
    yield

    ubuf[POOL_PAD:POOL_PAD + T, :] = u
    d_groups = []
    for g, w in enumerate(POOL_WINDOWS):
        s = ubuf[:, g * POOL_GROUP:(g + 1) * POOL_GROUP]
        sh = 1
        while sh < w:
            s = s + _roll_rows(s, sh)
            sh *= 2
        d_groups.append(s[POOL_PAD:, :] * (1.0 / w) - u[:, g * POOL_GROUP:(g + 1) * POOL_GROUP])
    y_pool = _pool_project(d_groups, w_pool_ref, pscale_ref)
    ubuf[0:POOL_PAD, :] = ubuf[T:T + POOL_PAD, :]
    loga = _gate_log_decay(zr, w_a2_ref, b_a_ref)

    tr = lax.broadcasted_iota(jnp.int32, (C, C), 0)
    tc = lax.broadcasted_iota(jnp.int32, (C, C), 1)
    tri = jnp.where(tc <= tr, 1.0, 0.0).astype(BF16)
    ar = lax.broadcasted_iota(jnp.int32, (C, 2 * C), 0)
    ac = lax.broadcasted_iota(jnp.int32, (C, 2 * C), 1) & (C - 1)
    causal = ac <= ar
    lane128 = lax.broadcasted_iota(jnp.int32, (C, 128), 1)
    lane256 = lax.broadcasted_iota(jnp.int32, (C, 256), 1)
    sr = lax.broadcasted_iota(jnp.int32, (128, 256), 0)
    sc = lax.broadcasted_iota(jnp.int32, (128, 256), 1)
    blockdiag = (sr >= GLA_DK) == (sc >= GLA_DV)
    mid = C // 2 - 1

    n_chunks = T // C
    pairs = [(c, p) for c in range(n_chunks) for p in range(2)]
    ks = [slice(128 * p, 128 * (p + 1)) for p in range(2)]
    vs = [slice(256 * p, 256 * (p + 1)) for p in range(2)]

    bcs = []
    for c in range(n_chunks):
        la_hi, la_lo = _split_bf16(loga[c * C:(c + 1) * C])
        bb = _dot(tri, jnp.concatenate([la_hi, la_lo], axis=1))
        bcs.append(bb[:, :D_GLA_K] + bb[:, D_GLA_K:])
    yield

    q_in, k_in, q_st, k_st, dec_t, vc = [], [], [], [], [], []
    for c in range(n_chunks):
        bc = bcs[c]
        bmid = bc[mid:mid + 1]
        bend = bc[C - 1:C]
        qc = q[c * C:(c + 1) * C]
        kc = k[c * C:(c + 1) * C]
        q_in.append((qc * jnp.exp(bc - bmid)).astype(BF16))
        k_in.append((kc * jnp.exp(bmid - bc)).astype(BF16))
        q_st.append((qc * jnp.exp(bc)).astype(BF16))
        k_st.append((kc * jnp.exp(bend - bc)).astype(BF16))
        dec_t.append(jnp.transpose(jnp.broadcast_to(jnp.exp(bend), (128, D_GLA_K))))
        vc.append(v[c * C:(c + 1) * C].astype(BF16))
    yield

    attn, upd = {}, {}
    for c, p in pairs:
        a = _dot_nt(q_in[c][:, ks[p]], _head_pair_keys(k_in[c][:, ks[p]], lane128))
        attn[c, p] = jnp.where(causal, a, 0.0).astype(BF16)
    for c, p in pairs:
        u_cp = _dot_tn(k_st[c][:, ks[p]], vc[c][:, vs[p]])
        upd[c, p] = jnp.where(blockdiag, u_cp, 0.0)
    yield

    s_vals = [sbd[p] for p in range(2)]
    s_start = {}
    for c, p in pairs:
        s_start[c, p] = s_vals[p].astype(BF16)
        dec_p = dec_t[c][ks[p], :]
        s_vals[p] = jnp.concatenate([dec_p, dec_p], axis=1) * s_vals[p] + upd[c, p]
    for p in range(2):
        sbd[p] = s_vals[p]

    o_rows = [[], []]
    for c, p in pairs:
        vblk = _head_pair_values(vc[c][:, vs[p]], lane256)
        lhs = jnp.concatenate([attn[c, p], q_st[c][:, ks[p]]], axis=1)
        rhs = jnp.concatenate([vblk, s_start[c, p]], axis=0)
        o_rows[p].append(_dot(lhs, rhs))
    o = jnp.concatenate([jnp.concatenate(o_rows[p], axis=0) for p in range(2)], axis=1)
    yield

    y_gla = _gla_output_gate(o, r, gnorm_ref[...])
    x1_ref[...] = _mix_out(x, y_pool, y_gla, w_out_ref, g_ref, b_ref)


def _layer_spec(shape, layer):
    nd = len(shape)
    return pl.BlockSpec((None,) + shape, lambda *_: (layer,) + (0,) * nd, pipeline_mode=pl.Buffered(1))


def _mixer_weight_specs(layer):
    shapes = [(D_MODEL, C_END), (GATE_RANK, D_GLA_K), (1, D_GLA_K),
              (4, POOL_GROUP, POOL_GROUP), (1, D_POOL), (1, GLA_DV), (D_MODEL, D_MODEL), (1, D_MODEL), (1, D_MODEL)]
    assert len(shapes) == N_MIX_W
    return [_layer_spec(s, layer) for s in shapes]


def _mixer_long(x, pprev, s0, wts, layer):
    B, L, _ = x.shape
    T = min(LONG_TILE, L)
    P = MIXER_PAR if B % MIXER_PAR == 0 else 1
    assert L % T == 0 and T % LONG_CHUNK == 0
    kern = functools.partial(_mixer_long_kernel, n_par=P, tile=T, chunk=LONG_CHUNK)
    return pl.pallas_call(
        kern,
        grid=(B // P, L // T),
        in_specs=[pl.BlockSpec((P, T, D_MODEL), lambda b, t: (b, t, 0)),
                  pl.BlockSpec((POOL_PAD, D_POOL), lambda b, t: (0, 0)),
                  pl.BlockSpec((GLA_HEADS, GLA_DK, GLA_DV), lambda b, t: (0, 0, 0))]
        + _mixer_weight_specs(layer),
        out_specs=[pl.BlockSpec((P, T, D_MODEL), lambda b, t: (b, t, 0)),
                   pl.BlockSpec((P, POOL_PAD, D_POOL), lambda b, t: (b, 0, 0)),
                   pl.BlockSpec((P, GLA_HEADS, GLA_DK, GLA_DV), lambda b, t: (b, 0, 0, 0))],
        out_shape=[jax.ShapeDtypeStruct((B, L, D_MODEL), F32),
                   jax.ShapeDtypeStruct((B, POOL_PAD, D_POOL), F32),
                   jax.ShapeDtypeStruct((B, GLA_HEADS, GLA_DK, GLA_DV), F32)],
        scratch_shapes=[pltpu.VMEM((P, T + POOL_PAD, D_POOL), F32),
                        pltpu.VMEM((P, 2, 2 * GLA_DK, 2 * GLA_DV), F32)],
        compiler_params=pltpu.CompilerParams(dimension_semantics=("arbitrary", "arbitrary"),
                                             vmem_limit_bytes=VMEM_LIMIT),
        name="mixer_long",
    )(x, pprev, s0, *wts)


def _mixer_meta(x_ref, wts, x1_ref, u_ref, s_ref):
    w_in_ref, w_a2_ref, b_a_ref, w_pool_ref, pscale_ref, gnorm_ref, w_out_ref, g_ref, b_ref = wts
    L = N_META
    x = x_ref[...]
    xb = x.astype(BF16)
    u, zr, q, k, v, r = _project_in(xb, w_in_ref)
    u_ref[...] = u

    row128 = lax.broadcasted_iota(jnp.int32, (L, POOL_GROUP), 0)
    pos1 = lax.broadcasted_iota(jnp.int32, (L, 1), 0)
    d_groups = []
    for g, w in enumerate(POOL_WINDOWS):
        ug = u[:, g * POOL_GROUP:(g + 1) * POOL_GROUP]
        s = ug
        sh = 1
        while sh < w:
            s = s + jnp.where(row128 >= sh, _roll_rows(s, sh), 0.0)
            sh *= 2
        d_groups.append(s / jnp.minimum(w, pos1 + 1).astype(F32) - ug)
    y_pool = _pool_project(d_groups, w_pool_ref, pscale_ref)

    loga = _gate_log_decay(zr, w_a2_ref, b_a_ref)
    row256 = lax.broadcasted_iota(jnp.int32, (L, D_GLA_K), 0)
    b = loga
    sh = 1
    while sh < L:
        b = b + jnp.where(row256 >= sh, _roll_rows(b, sh), 0.0)
        sh *= 2
    bend = b[L - 1:L]
    q_in = (q * jnp.exp(b)).astype(BF16)
    k_in = (k * jnp.exp(-b)).astype(BF16)
    k_st = (k * jnp.exp(bend - b)).astype(BF16)
    vb = v.astype(BF16)
    ar = lax.broadcasted_iota(jnp.int32, (L, 2 * L), 0)
    ac = lax.broadcasted_iota(jnp.int32, (L, 2 * L), 1) & (L - 1)
    causal = ac <= ar
    lane128 = lax.broadcasted_iota(jnp.int32, (L, 128), 1)
    lane256 = lax.broadcasted_iota(jnp.int32, (L, 256), 1)
    attn = []
    for p in range(2):
        ks = slice(128 * p, 128 * (p + 1))
        a = _dot_nt(q_in[:, ks], _head_pair_keys(k_in[:, ks], lane128))
        attn.append(jnp.where(causal, a, 0.0).astype(BF16))
    o_parts = []
    for p in range(2):
        ks = slice(128 * p, 128 * (p + 1))
        vs = slice(256 * p, 256 * (p + 1))
        s_p = _dot_tn(k_st[:, ks], vb[:, vs])
        s_ref[2 * p] = s_p[0:GLA_DK, 0:GLA_DV]
        s_ref[2 * p + 1] = s_p[GLA_DK:2 * GLA_DK, GLA_DV:2 * GLA_DV]
        o_parts.append(_dot(attn[p], _head_pair_values(vb[:, vs], lane256)))
    o = jnp.concatenate(o_parts, axis=1)

    y_gla = _gla_output_gate(o, r, gnorm_ref[...])
    x1_ref[...] = _mix_out(x, y_pool, y_gla, w_out_ref, g_ref, b_ref)


def _mixer_sample(x_ref, hist_ref, s0_ref, wts, x1_ref, hist_out_ref, snew_ref, *, seq_len):
    w_in_ref, w_a2_ref, b_a_ref, w_pool_ref, pscale_ref, gnorm_ref, w_out_ref, g_ref, b_ref = wts
    G = SEQ_GROUP
    R = G * seq_len
    n_groups = x_ref.shape[0] // R
    x = x_ref[...]
    xb = x.astype(BF16)
    u, zr, q, k, v, r = _project_in(xb, w_in_ref)
    loga = _gate_log_decay(zr, w_a2_ref, b_a_ref)
    results = []
    gens = []
    for gi in range(n_groups):
        rs = slice(gi * R, (gi + 1) * R)
        hist_rows = pl.ds(gi * G * POOL_BUF, G * POOL_BUF)
        seqs = pl.ds(gi * G, G)
        gens.append(_sample_group(u[rs], q[rs], k[rs], v[rs], loga[rs], hist_ref.at[hist_rows], s0_ref.at[seqs],
                                  hist_out_ref.at[hist_rows], snew_ref.at[seqs], results, seq_len=seq_len))
    for _ in itertools.zip_longest(*gens):
        pass
    d_groups = [jnp.concatenate([res[0][g] for res in results], axis=0) for g in range(len(POOL_WINDOWS))]
    o = jnp.concatenate([res[1] for res in results], axis=0)
    y_pool = _pool_project(d_groups, w_pool_ref, pscale_ref)
    y_gla = _gla_output_gate(o, r, gnorm_ref[...])
    x1_ref[...] = _mix_out(x, y_pool, y_gla, w_out_ref, g_ref, b_ref)


def _sample_group(u, q, k, v, loga, hist_ref, s0_ref, hist_out_ref, snew_ref, results, *, seq_len):
    G, Ls = SEQ_GROUP, seq_len
    R = G * Ls
    NS = G * GLA_DK
    g_shift = G.bit_length() - 1
    hist_out_ref[0:(POOL_BUF - Ls) * G, :] = hist_ref[R:POOL_BUF * G, :]
    hist_out_ref[(POOL_BUF - Ls) * G:POOL_BUF * G, :] = u

    def blk(a, t):
        return a[t * G:(t + 1) * G]

    d_groups = []
    for g, w in enumerate(POOL_WINDOWS):
        cols = slice(g * POOL_GROUP, (g + 1) * POOL_GROUP)
        ug = u[:, cols]
        suffix = [None]
        acc = None
        for m in range(1, min(w - 1, POOL_BUF) + 1):
            h = hist_ref[(POOL_BUF - m) * G:(POOL_BUF - m + 1) * G, cols]
            acc = h if acc is None else acc + h
            suffix.append(acc)
        parts = []
        for t in range(Ls):
            wsum = blk(ug, t)
            for j in range(max(0, t - w + 1), t):
                wsum = wsum + blk(ug, j)
            m = w - 1 - t
            if m > 0:
                wsum = wsum + suffix[m]
            parts.append(wsum * (1.0 / w) - blk(ug, t))
        d_groups.append(jnp.concatenate(parts, axis=0))

    b_t = [blk(loga, 0)]
    for t in range(1, Ls):
        b_t.append(b_t[-1] + blk(loga, t))
    b = jnp.concatenate(b_t, axis=0)
    bend = jnp.concatenate([b_t[-1]] * Ls, axis=0)
    q_in = (q * jnp.exp(b)).astype(BF16)
    k_in = (k * jnp.exp(-b)).astype(BF16)
    k_st = k * jnp.exp(bend - b)
    dec_hi, dec_lo = _split_bf16(jnp.exp(bend))
    tok = lax.broadcasted_iota(jnp.int32, (R, D_GLA_K), 0) >> g_shift
    dec_rows = jnp.where(tok == Ls - 1, dec_hi, jnp.where(tok == Ls - 2, dec_lo, jnp.zeros_like(dec_lo)))
    vb = v.astype(BF16)

    ar = lax.broadcasted_iota(jnp.int32, (R, 2 * R), 0)
    ac = lax.broadcasted_iota(jnp.int32, (R, 2 * R), 1) & (R - 1)
    same_seq_causal = ((ac & (G - 1)) == (ar & (G - 1))) & ((ac >> g_shift) <= (ar >> g_shift))
    lane128 = lax.broadcasted_iota(jnp.int32, (R, 128), 1)
    lane256 = lax.broadcasted_iota(jnp.int32, (R, 256), 1)
    br = lax.broadcasted_iota(jnp.int32, (R, NS), 0)
    bcol = lax.broadcasted_iota(jnp.int32, (R, NS), 1)
    own_state = (bcol >> 6) == (br & (G - 1))
    ones_blk = jnp.ones((R, GLA_DV), BF16)
    zeros_blk = jnp.zeros((R, GLA_DV), BF16)

    def expand(xp, first):
        sw = pltpu.roll(xp, GLA_DK, 1)
        two = jnp.where(lane128 < GLA_DK, xp, sw) if first else jnp.where(lane128 < GLA_DK, sw, xp)
        rep = jnp.concatenate([two] * (NS // 128), axis=1)
        return jnp.where(own_state, rep, 0.0).astype(BF16)

    ks = [slice(128 * p, 128 * (p + 1)) for p in range(2)]
    vs = [slice(256 * p, 256 * (p + 1)) for p in range(2)]
    attn = []
    for p in range(2):
        a = _dot_nt(q_in[:, ks[p]], _head_pair_keys(k_in[:, ks[p]], lane128))
        attn.append(jnp.where(same_seq_causal, a, 0.0).astype(BF16))
    yield
    inter = []
    for h in range(GLA_HEADS):
        p, first = h // 2, h % 2 == 0
        s_flat = s0_ref[:, h].reshape(NS, GLA_DV)
        inter.append(_dot(expand(q_in[:, ks[p]].astype(F32), first), s_flat.astype(BF16)))
    for h in range(GLA_HEADS):
        p, first = h // 2, h % 2 == 0
        s_flat = s0_ref[:, h].reshape(NS, GLA_DV)
        lhs = jnp.concatenate([expand(k_st[:, ks[p]], first),
                               expand(dec_rows[:, ks[p]].astype(F32), first)], axis=0)
        v_h = vb[:, h * GLA_DV:(h + 1) * GLA_DV]
        rhs = jnp.concatenate([jnp.concatenate([v_h, zeros_blk], axis=1),
                               jnp.concatenate([zeros_blk, ones_blk], axis=1)], axis=0)
        ud = _dot_tn(lhs, rhs)
        s_new = ud[:, GLA_DV:] * s_flat + ud[:, :GLA_DV]
        snew_ref[:, h] = s_new.reshape(G, GLA_DK, GLA_DV)
    yield
    o_parts = []
    for p in range(2):
        o_intra = _dot(attn[p], _head_pair_values(vb[:, vs[p]], lane256))
        o_parts.append(o_intra + jnp.concatenate(inter[2 * p:2 * p + 2], axis=1))
    results.append((d_groups, jnp.concatenate(o_parts, axis=1)))


def _mixer_short_kernel(*refs, seq_len, n_prev):
    xm_ref, xs_ref, hist_ref, s0_ref = refs[:4]
    wts = refs[4:4 + N_MIX_W]
    x1m_ref, um_ref, sm_ref, x1s_ref, us_ref, ss_ref = refs[-6:]
    i = pl.program_id(0)

    @pl.when(i == 0)
    def _meta():
        _mixer_meta(xm_ref, wts, x1m_ref, um_ref, sm_ref)

    @pl.when(i > 0)
    def _sample():
        for a in range(n_prev):
            ss_ref[a] = refs[4 + N_MIX_W][a]
        _mixer_sample(xs_ref, hist_ref, s0_ref, wts, x1s_ref, us_ref, ss_ref.at[n_prev], seq_len=seq_len)


def _mixer_short(x_meta, x_samp, hist, s0, wts, layer, *, seq_len, s_prev=None):
    rows = x_samp.shape[0]
    n_groups = rows // (SEQ_GROUP * seq_len)
    per_step = GROUPS_PER_STEP if n_groups % GROUPS_PER_STEP == 0 else 1
    G = SEQ_GROUP * per_step
    R = G * seq_len
    n_tiles = rows // R
    assert rows % R == 0
    clamp = lambda i: jnp.maximum(i - 1, 0)
    n_prev = 0 if s_prev is None else s_prev.shape[0]
    state_blk = (G, GLA_HEADS, GLA_DK, GLA_DV)
    in_specs = ([pl.BlockSpec((N_META, D_MODEL), lambda i: (0, 0)),
                 pl.BlockSpec((R, D_MODEL), lambda i: (clamp(i), 0)),
                 pl.BlockSpec((None, G * POOL_BUF, D_POOL), lambda i: (layer, clamp(i), 0)),
                 pl.BlockSpec((None,) + state_blk, lambda i: (layer, clamp(i), 0, 0, 0))]
                + _mixer_weight_specs(layer))
    args = [x_meta, x_samp, hist, s0, *wts]
    if n_prev:
        in_specs.append(pl.BlockSpec((n_prev,) + state_blk, lambda i: (0, clamp(i), 0, 0, 0)))
        args.append(s_prev)
    kern = functools.partial(_mixer_short_kernel, seq_len=seq_len, n_prev=n_prev)

    return pl.pallas_call(
        kern,
        grid=(n_tiles + 1,),
        in_specs=in_specs,
        out_specs=[pl.BlockSpec((N_META, D_MODEL), lambda i: (0, 0)),
                   pl.BlockSpec((N_META, D_POOL), lambda i: (0, 0)),
                   pl.BlockSpec((GLA_HEADS, GLA_DK, GLA_DV), lambda i: (0, 0, 0)),
                   pl.BlockSpec((R, D_MODEL), lambda i: (clamp(i), 0)),
                   pl.BlockSpec((G * POOL_BUF, D_POOL), lambda i: (clamp(i), 0)),
                   pl.BlockSpec((n_prev + 1,) + state_blk, lambda i: (0, clamp(i), 0, 0, 0))],
        out_shape=[jax.ShapeDtypeStruct((N_META, D_MODEL), F32),
                   jax.ShapeDtypeStruct((N_META, D_POOL), F32),
                   jax.ShapeDtypeStruct((GLA_HEADS, GLA_DK, GLA_DV), F32),
                   jax.ShapeDtypeStruct((rows, D_MODEL), F32),
                   jax.ShapeDtypeStruct((n_tiles * G * POOL_BUF, D_POOL), F32),
                   jax.ShapeDtypeStruct((n_prev + 1,) + s0.shape[1:], F32)],
        compiler_params=pltpu.CompilerParams(dimension_semantics=("arbitrary",),
                                             vmem_limit_bytes=VMEM_LIMIT),
        name="mixer_short",
    )(*args)


def _ffn_tile(x_ref, wts, y_ref, conv_inputs, store_gate):
    w_up_ref, w_gate_ref, cw_ref, cb_ref, w_down_ref, g_ref, b_ref = wts
    x = x_ref[...]
    xb = x.astype(BF16)

    acts = []
    for j in range(D_FF // FF_CHUNK):
        cs = slice(j * FF_CHUNK, (j + 1) * FF_CHUNK)
        a = _dot(xb, w_up_ref[:, cs])
        gt = _dot(xb, w_gate_ref[:, cs])
        g1, g2 = conv_inputs(gt, cs)
        store_gate(gt, cs)
        gc = cb_ref[:, cs] + cw_ref[0:1, cs] * g2 + cw_ref[1:2, cs] * g1 + cw_ref[2:3, cs] * gt
        acts.append((a * _silu(gc)).astype(BF16))
    yield

    act = jnp.concatenate(acts, axis=1)
    rows = x.shape[0]
    rb = min(rows, DOWN_ROWS)
    for r0 in range(0, rows, rb):
        f = _dot(act[r0:r0 + rb], w_down_ref[...])
        y_ref[r0:r0 + rb, :] = _layer_norm(ALPHA * x[r0:r0 + rb] + f, g_ref[...], b_ref[...])


def _ffn_long_kernel(*refs, n_par):
    x_ref, cprev_ref = refs[:2]
    wts = refs[2:2 + N_FFN_W]
    y_ref, hist_out_ref, gbuf = refs[2 + N_FFN_W:]
    t = pl.program_id(1)
    T = x_ref.shape[1]

    @pl.when(t == 0)
    def _init():
        for j in range(n_par):
            gbuf[j, 0:CONV_PAD, :] = cprev_ref[...]

    def make_tile(j):
        def conv_inputs(gt, cs):
            gbuf[j, CONV_PAD:CONV_PAD + T, cs] = gt
            return gbuf[j, CONV_PAD - 1:CONV_PAD - 1 + T, cs], gbuf[j, CONV_PAD - 2:CONV_PAD - 2 + T, cs]

        def store_gate(gt, cs):
            gbuf[j, 0:CONV_PAD, cs] = gbuf[j, T:T + CONV_PAD, cs]

        return _ffn_tile(x_ref.at[j], wts, y_ref.at[j], conv_inputs, store_gate)

    for _ in itertools.zip_longest(*[make_tile(j) for j in range(n_par)]):
        pass

    @pl.when(t == pl.num_programs(1) - 1)
    def _final():
        for j in range(n_par):
            hist_out_ref[j] = gbuf[j, 0:CONV_PAD, :]


def _ffn_short_kernel(*refs, seq_len):
    xm_ref, xs_ref, hist_ref = refs[:3]
    wts = refs[3:3 + N_FFN_W]
    ym_ref, cm_ref, ys_ref, cs_ref = refs[3 + N_FFN_W:]
    G, Ls = SEQ_GROUP, seq_len
    R = G * Ls
    rows = xs_ref.shape[0]
    n_tiles = rows // R

    rowm = lax.broadcasted_iota(jnp.int32, (N_META, FF_CHUNK), 0)

    def meta_conv(gt, cs):
        return (jnp.where(rowm >= 1, _roll_rows(gt, 1), 0.0), jnp.where(rowm >= 2, _roll_rows(gt, 2), 0.0))

    def meta_store(gt, cs):
        cm_ref[:, cs] = gt[N_META - CONV_PAD:N_META]

    tok = (lax.broadcasted_iota(jnp.int32, (rows, FF_CHUNK), 0) & (R - 1)) >> (G.bit_length() - 1)
    zeros_tail = jnp.zeros((R - CONV_BUF * G, FF_CHUNK), F32)

    def sample_conv(gt, cs):
        hx = jnp.concatenate([piece for n in range(n_tiles)
                              for piece in (hist_ref[n * CONV_BUF * G:(n + 1) * CONV_BUF * G, cs], zeros_tail)], axis=0)
        g1 = jnp.where(tok >= 1, _roll_rows(gt, G), _roll_rows(hx, -G))
        g2 = jnp.where(tok >= 2, _roll_rows(gt, 2 * G), hx)
        return g1, g2

    def sample_store(gt, cs):
        for n in range(n_tiles):
            cs_ref[n * CONV_BUF * G:(n + 1) * CONV_BUF * G, cs] = gt[n * R + (Ls - CONV_BUF) * G:(n + 1) * R]

    tiles = [_ffn_tile(xs_ref, wts, ys_ref, sample_conv, sample_store),
             _ffn_tile(xm_ref, wts, ym_ref, meta_conv, meta_store)]
    for _ in itertools.zip_longest(*tiles):
        pass


def _ffn_weight_specs(layer):
    shapes = [(D_MODEL, D_FF), (D_MODEL, D_FF), (3, D_FF), (1, D_FF), (D_FF, D_MODEL), (1, D_MODEL), (1, D_MODEL)]
    assert len(shapes) == N_FFN_W
    return [_layer_spec(s, layer) for s in shapes]


def _ffn_long(x, cprev, wts, layer):
    B, L, _ = x.shape
    T = min(FFN_TILE, L)
    P = LONG_PAR if B % LONG_PAR == 0 else 1
    assert L % T == 0
    kern = functools.partial(_ffn_long_kernel, n_par=P)
    return pl.pallas_call(
        kern,
        grid=(B // P, L // T),
        in_specs=[pl.BlockSpec((P, T, D_MODEL), lambda b, t: (b, t, 0)),
                  pl.BlockSpec((CONV_PAD, D_FF), lambda b, t: (0, 0))] + _ffn_weight_specs(layer),
        out_specs=[pl.BlockSpec((P, T, D_MODEL), lambda b, t: (b, t, 0)),
                   pl.BlockSpec((P, CONV_PAD, D_FF), lambda b, t: (b, 0, 0))],
        out_shape=[jax.ShapeDtypeStruct((B, L, D_MODEL), F32),
                   jax.ShapeDtypeStruct((B, CONV_PAD, D_FF), F32)],
        scratch_shapes=[pltpu.VMEM((P, T + CONV_PAD, D_FF), F32)],
        compiler_params=pltpu.CompilerParams(dimension_semantics=("arbitrary", "arbitrary"),
                                             vmem_limit_bytes=VMEM_LIMIT),
        name="ffn_long",
    )(x, cprev, *wts)


def _ffn_short(x_meta, x_samp, hist, wts, layer, *, seq_len):
    rows = x_samp.shape[0]
    n_hist = hist.shape[1]
    kern = functools.partial(_ffn_short_kernel, seq_len=seq_len)
    return pl.pallas_call(
        kern,
        grid=(1,),
        in_specs=[pl.BlockSpec((N_META, D_MODEL), lambda i: (0, 0)),
                  pl.BlockSpec((rows, D_MODEL), lambda i: (0, 0)),
                  pl.BlockSpec((None, n_hist, D_FF), lambda i: (layer, 0, 0))] + _ffn_weight_specs(layer),
        out_specs=[pl.BlockSpec((N_META, D_MODEL), lambda i: (0, 0)),
                   pl.BlockSpec((CONV_PAD, D_FF), lambda i: (0, 0)),
                   pl.BlockSpec((rows, D_MODEL), lambda i: (0, 0)),
                   pl.BlockSpec((n_hist, D_FF), lambda i: (0, 0))],
        out_shape=[jax.ShapeDtypeStruct((N_META, D_MODEL), F32),
                   jax.ShapeDtypeStruct((CONV_PAD, D_FF), F32),
                   jax.ShapeDtypeStruct((rows, D_MODEL), F32),
                   jax.ShapeDtypeStruct((n_hist, D_FF), F32)],
        compiler_params=pltpu.CompilerParams(dimension_semantics=("arbitrary",),
                                             vmem_limit_bytes=VMEM_LIMIT),
        name="ffn_short",
    )(x_meta, x_samp, hist, *wts)


def _to_group_major(a, axis):
    n, j = a.shape[axis], a.shape[axis + 1]
    lead, tail = a.shape[:axis], a.shape[axis + 2:]
    a = a.reshape(*lead, n // SEQ_GROUP, SEQ_GROUP, j, *tail)
    a = jnp.swapaxes(a, axis + 1, axis + 2)
    return a.reshape(*lead, n * j, *tail)


def _from_group_major(a, axis, j):
    rows = a.shape[axis]
    n = rows // j
    lead, tail = a.shape[:axis], a.shape[axis + 1:]
    a = a.reshape(*lead, n // SEQ_GROUP, j, SEQ_GROUP, *tail)
    a = jnp.swapaxes(a, axis + 1, axis + 2)
    return a.reshape(*lead, n, j, *tail)


def kernel(x_prompt, x_sample, state_pool, state_gla, state_conv, meta_tokens,
           w_in, w_a2, b_a, w_pool, pool_scale, gla_norm, w_out, ln1_g, ln1_b,
           w_up, w_gate, conv_w, conv_b, w_down, ln2_g, ln2_b):
    NB, LS = x_sample.shape[0], x_sample.shape[1]
    assert NB % SEQ_GROUP == 0 and LS & (LS - 1) == 0 and CONV_BUF <= LS <= POOL_BUF

    mix_w = (w_in.astype(BF16), w_a2.astype(BF16), b_a[:, None],
             w_pool.astype(BF16), pool_scale[:, None], gla_norm[:, None], w_out.astype(BF16),
             ln1_g[:, None], ln1_b[:, None])
    ffn_w = (w_up.astype(BF16), w_gate.astype(BF16), conv_w, conv_b[:, None], w_down.astype(BF16),
             ln2_g[:, None], ln2_b[:, None])

    hm = meta_tokens.astype(F32)
    hp = x_prompt
    hs = _to_group_major(x_sample, 0)
    pool_hist = _to_group_major(state_pool, 1)
    conv_hist = _to_group_major(state_conv, 1)

    pp, gp, cp, ps_l, cs_l = [], [], [], [], []
    gs = None
    for l in range(DEPTH):
        hm1, um, sm, hs1, ps_new, gs = _mixer_short(hm, hs, pool_hist, state_gla, mix_w, l, seq_len=LS, s_prev=gs)
        hm, cm, hs, cs_new = _ffn_short(hm1, hs1, conv_hist, ffn_w, l, seq_len=LS)
        ps_l.append(ps_new)
        cs_l.append(cs_new)

        hp1, pbuf, snew = _mixer_long(hp, um, sm, mix_w, l)
        hp, cbuf = _ffn_long(hp1, cm, ffn_w, l)
        pp.append(pbuf[:, 1:])
        gp.append(snew)
        cp.append(cbuf[:, CONV_PAD - CONV_BUF:])

    ps = _from_group_major(jnp.stack(ps_l), 1, POOL_BUF)
    cs = _from_group_major(jnp.stack(cs_l), 1, CONV_BUF)
    return (hp, _from_group_major(hs, 0, LS), jnp.stack(pp), jnp.stack(gp), jnp.stack(cp), ps, gs, cs)
```

```python
import functools
import itertools

import jax
import jax.numpy as jnp
from jax import lax
from jax.experimental import pallas as pl
from jax.experimental.pallas import tpu as pltpu

F32 = jnp.float32
BF16 = jnp.bfloat16

D_MODEL = 1024
N_META = 16
D_POOL = 512
POOL_WINDOWS = (2, 4, 8, 16)
POOL_GROUP = 128
POOL_BUF = 15
POOL_PAD = 16
D_GLA = 512
GLA_HEADS = 4
GLA_DV = 128
GLA_DK = 64
D_GLA_K = 256
GATE_RANK = 16
GATE_TAU = 16.0
D_FF = 2816
CONV_BUF = 2
CONV_PAD = 8
DEPTH = 2
ALPHA = (2 * DEPTH) ** 0.25
LN_EPS = 1e-5
RMS_EPS = 1e-6
PAST_LEN = 16384

C_POOL, C_Q, C_K, C_V, C_R, C_Z, C_END = 0, 512, 768, 1024, 1536, 2048, 2064

LONG_TILE = 512
LONG_CHUNK = 64
LONG_PAR = 2
MIXER_PAR = 2
MIXER_SKEW = 1
FFN_TILE = 512
SEQ_GROUP = 16
GROUPS_PER_STEP = 2
FF_CHUNK = 256
DOWN_ROWS = 256
VMEM_LIMIT = 56 * 1024 * 1024

N_MIX_W = 9
N_FFN_W = 7


def _dot(a, b):
    return jnp.dot(a, b, preferred_element_type=F32)


def _dot_nt(a, b):
    return lax.dot_general(a, b, (((1,), (1,)), ((), ())), preferred_element_type=F32)


def _dot_tn(a, b):
    return lax.dot_general(a, b, (((0,), (0,)), ((), ())), preferred_element_type=F32)


def _layer_norm(y, g, b):
    mu = jnp.mean(y, axis=-1, keepdims=True)
    yc = y - mu
    var = jnp.mean(yc * yc, axis=-1, keepdims=True)
    return yc * lax.rsqrt(var + LN_EPS) * g + b


def _silu(x):
    h = 0.5 * x
    return h + h * jnp.tanh(h)


def _log_sigmoid(z):
    return jnp.minimum(z, 0.0) - jnp.log(1.0 + jnp.exp(-jnp.abs(z)))


def _roll_rows(x, shift):
    n = x.shape[0]
    return pltpu.roll(x, shift % n, 0)


def _split_bf16(x):
    hi = x.astype(BF16)
    lo = (x - hi.astype(F32)).astype(BF16)
    return hi, lo


def _project_in(xb, w_in_ref):
    u = _dot(xb, w_in_ref[:, C_POOL:C_Q])
    zr = _dot(xb, w_in_ref[:, C_Z:C_END])
    q = _dot(xb, w_in_ref[:, C_Q:C_K]) * (GLA_DK ** -0.5)
    k = _dot(xb, w_in_ref[:, C_K:C_V])
    v = _dot(xb, w_in_ref[:, C_V:C_R])
    r = _dot(xb, w_in_ref[:, C_R:C_Z])
    return u, zr, q, k, v, r


def _gate_log_decay(zr, w_a2_ref, b_a_ref):
    z = _dot(zr.astype(BF16), w_a2_ref[...]) + b_a_ref[...]
    return _log_sigmoid(z) * (1.0 / GATE_TAU)


def _gla_output_gate(o, r, gnorm):
    parts = []
    for h in range(GLA_HEADS):
        oh = o[:, h * GLA_DV:(h + 1) * GLA_DV]
        ms = jnp.mean(oh * oh, axis=-1, keepdims=True)
        parts.append(oh * lax.rsqrt(ms + RMS_EPS) * gnorm)
    return jnp.concatenate(parts, axis=1) * _silu(r)


def _pool_project(d_groups, w_pool_ref, pscale_ref):
    ys = [_dot(d.astype(BF16), w_pool_ref[g]) for g, d in enumerate(d_groups)]
    return jnp.concatenate(ys, axis=1) * pscale_ref[...]


def _mix_out(x, y_pool, y_gla, w_out_ref, g_ref, b_ref):
    mix = jnp.concatenate([y_pool, y_gla], axis=1).astype(BF16)
    return _layer_norm(ALPHA * x + _dot(mix, w_out_ref[...]), g_ref[...], b_ref[...])


def _emit_staggered(stage_gens, first_round):
    live = list(zip(first_round, stage_gens))
    rnd = 0
    while live:
        for start, g in list(live):
            if rnd >= start and next(g, StopIteration) is StopIteration:
                live.remove((start, g))
        rnd += 1


def _head_pair_keys(k_p, lane128):
    zk = jnp.zeros_like(k_p)
    return jnp.concatenate([jnp.where(lane128 < GLA_DK, k_p, zk), jnp.where(lane128 >= GLA_DK, k_p, zk)], axis=0)


def _head_pair_values(v_p, lane256):
    zv = jnp.zeros_like(v_p)
    return jnp.concatenate([jnp.where(lane256 < GLA_DV, v_p, zv), jnp.where(lane256 >= GLA_DV, v_p, zv)], axis=0)


def _mixer_long_kernel(*refs, n_par, tile, chunk):
    x_ref, pprev_ref, s0_ref = refs[:3]
    wts = refs[3:3 + N_MIX_W]
    x1_ref, pbuf_ref, snew_ref, ubuf, sbd = refs[3 + N_MIX_W:]
    t = pl.program_id(1)
    zero_blk = jnp.zeros((GLA_DK, GLA_DV), F32)

    @pl.when(t == 0)
    def _init():
        for j in range(n_par):
            ubuf[j, 0:POOL_PAD, :] = pprev_ref[...]
            for p in range(2):
                top = jnp.concatenate([s0_ref[2 * p], zero_blk], axis=1)
                bot = jnp.concatenate([zero_blk, s0_ref[2 * p + 1]], axis=1)
                sbd[j, p] = jnp.concatenate([top, bot], axis=0)

    tiles = [_mixer_long_tile(x_ref.at[j], wts, x1_ref.at[j], ubuf.at[j], sbd.at[j], tile=tile, chunk=chunk)
             for j in range(n_par)]
    _emit_staggered(tiles, [MIXER_SKEW * j for j in range(n_par)])

    @pl.when(t == pl.num_programs(1) - 1)
    def _final():
        for j in range(n_par):
            pbuf_ref[j] = ubuf[j, 0:POOL_PAD, :]
            for p in range(2):
                s_p = sbd[j, p]
                snew_ref[j, 2 * p] = s_p[0:GLA_DK, 0:GLA_DV]
                snew_ref[j, 2 * p + 1] = s_p[GLA_DK:2 * GLA_DK, GLA_DV:2 * GLA_DV]


def _mixer_long_tile(x_ref, wts, x1_ref, ubuf, sbd, *, tile, chunk):
    w_in_ref, w_a2_ref, b_a_ref, w_pool_ref, pscale_ref, gnorm_ref, w_out_ref, g_ref, b_ref = wts
    T, C = tile, chunk
    x = x_ref[...]
    xb = x.astype(BF16)

    u, zr, q, k, v, r = _project_in(xb, w_in_ref)
    yield

    ubuf[POOL_PAD:POOL_PAD + T, :] = u
    d_groups = []
    for g, w in enumerate(POOL_WINDOWS):
        s = ubuf[:, g * POOL_GROUP:(g + 1) * POOL_GROUP]
        sh = 1
        while sh < w:
            s = s + _roll_rows(s, sh)
            sh *= 2
        d_groups.append(s[POOL_PAD:, :] * (1.0 / w) - u[:, g * POOL_GROUP:(g + 1) * POOL_GROUP])
    y_pool = _pool_project(d_groups, w_pool_ref, pscale_ref)
    ubuf[0:POOL_PAD, :] = ubuf[T:T + POOL_PAD, :]
    loga = _gate_log_decay(zr, w_a2_ref, b_a_ref)

    tr = lax.broadcasted_iota(jnp.int32, (C, C), 0)
    tc = lax.broadcasted_iota(jnp.int32, (C, C), 1)
    tri = jnp.where(tc <= tr, 1.0, 0.0).astype(BF16)
    ar = lax.broadcasted_iota(jnp.int32, (C, 2 * C), 0)
    ac = lax.broadcasted_iota(jnp.int32, (C, 2 * C), 1) & (C - 1)
    causal = ac <= ar
    lane128 = lax.broadcasted_iota(jnp.int32, (C, 128), 1)
    lane256 = lax.broadcasted_iota(jnp.int32, (C, 256), 1)
    sr = lax.broadcasted_iota(jnp.int32, (128, 256), 0)
    sc = lax.broadcasted_iota(jnp.int32, (128, 256), 1)
    blockdiag = (sr >= GLA_DK) == (sc >= GLA_DV)
    mid = C // 2 - 1

    n_chunks = T // C
    pairs = [(c, p) for c in range(n_chunks) for p in range(2)]
    ks = [slice(128 * p, 128 * (p + 1)) for p in range(2)]
    vs = [slice(256 * p, 256 * (p + 1)) for p in range(2)]

    bcs = []
    for c in range(n_chunks):
        la_hi, la_lo = _split_bf16(loga[c * C:(c + 1) * C])
        bb = _dot(tri, jnp.concatenate([la_hi, la_lo], axis=1))
        bcs.append(bb[:, :D_GLA_K] + bb[:, D_GLA_K:])
    yield

    q_in, k_in, q_st, k_st, dec_t, vc = [], [], [], [], [], []
    for c in range(n_chunks):
        bc = bcs[c]
        bmid = bc[mid:mid + 1]
        bend = bc[C - 1:C]
        qc = q[c * C:(c + 1) * C]
        kc = k[c * C:(c + 1) * C]
        q_in.append((qc * jnp.exp(bc - bmid)).astype(BF16))
        k_in.append((kc * jnp.exp(bmid - bc)).astype(BF16))
        q_st.append((qc * jnp.exp(bc)).astype(BF16))
        k_st.append((kc * jnp.exp(bend - bc)).astype(BF16))
        dec_t.append(jnp.transpose(jnp.broadcast_to(jnp.exp(bend), (128, D_GLA_K))))
        vc.append(v[c * C:(c + 1) * C].astype(BF16))
    yield

    attn, upd = {}, {}
    for c, p in pairs:
        a = _dot_nt(q_in[c][:, ks[p]], _head_pair_keys(k_in[c][:, ks[p]], lane128))
        attn[c, p] = jnp.where(causal, a, 0.0).astype(BF16)
    for c, p in pairs:
        u_cp = _dot_tn(k_st[c][:, ks[p]], vc[c][:, vs[p]])
        upd[c, p] = jnp.where(blockdiag, u_cp, 0.0)
    yield

    s_vals = [sbd[p] for p in range(2)]
    s_start = {}
    for c, p in pairs:
        s_start[c, p] = s_vals[p].astype(BF16)
        dec_p = dec_t[c][ks[p], :]
        s_vals[p] = jnp.concatenate([dec_p, dec_p], axis=1) * s_vals[p] + upd[c, p]
    for p in range(2):
        sbd[p] = s_vals[p]

    o_rows = [[], []]
    for c, p in pairs:
        vblk = _head_pair_values(vc[c][:, vs[p]], lane256)
        lhs = jnp.concatenate([attn[c, p], q_st[c][:, ks[p]]], axis=1)
        rhs = jnp.concatenate([vblk, s_start[c, p]], axis=0)
        o_rows[p].append(_dot(lhs, rhs))
    o = jnp.concatenate([jnp.concatenate(o_rows[p], axis=0) for p in range(2)], axis=1)
    yield

    y_gla = _gla_output_gate(o, r, gnorm_ref[...])
    x1_ref[...] = _mix_out(x, y_pool, y_gla, w_out_ref, g_ref, b_ref)


def _layer_spec(shape, layer):
    nd = len(shape)
    return pl.BlockSpec((None,) + shape, lambda *_: (layer,) + (0,) * nd, pipeline_mode=pl.Buffered(1))


def _mixer_weight_specs(layer):
    shapes = [(D_MODEL, C_END), (GATE_RANK, D_GLA_K), (1, D_GLA_K),
              (4, POOL_GROUP, POOL_GROUP), (1, D_POOL), (1, GLA_DV), (D_MODEL, D_MODEL), (1, D_MODEL), (1, D_MODEL)]
    assert len(shapes) == N_MIX_W
    return [_layer_spec(s, layer) for s in shapes]


def _mixer_long(x, pprev, s0, wts, layer):
    B, L, _ = x.shape
    T = min(LONG_TILE, L)
    P = MIXER_PAR if B % MIXER_PAR == 0 else 1
    assert L % T == 0 and T % LONG_CHUNK == 0
    kern = functools.partial(_mixer_long_kernel, n_par=P, tile=T, chunk=LONG_CHUNK)
    return pl.pallas_call(
        kern,
        grid=(B // P, L // T),
        in_specs=[pl.BlockSpec((P, T, D_MODEL), lambda b, t: (b, t, 0)),
                  pl.BlockSpec((POOL_PAD, D_POOL), lambda b, t: (0, 0)),
                  pl.BlockSpec((GLA_HEADS, GLA_DK, GLA_DV), lambda b, t: (0, 0, 0))]
        + _mixer_weight_specs(layer),
        out_specs=[pl.BlockSpec((P, T, D_MODEL), lambda b, t: (b, t, 0)),
                   pl.BlockSpec((P, POOL_PAD, D_POOL), lambda b, t: (b, 0, 0)),
                   pl.BlockSpec((P, GLA_HEADS, GLA_DK, GLA_DV), lambda b, t: (b, 0, 0, 0))],
        out_shape=[jax.ShapeDtypeStruct((B, L, D_MODEL), F32),
                   jax.ShapeDtypeStruct((B, POOL_PAD, D_POOL), F32),
                   jax.ShapeDtypeStruct((B, GLA_HEADS, GLA_DK, GLA_DV), F32)],
        scratch_shapes=[pltpu.VMEM((P, T + POOL_PAD, D_POOL), F32),
                        pltpu.VMEM((P, 2, 2 * GLA_DK, 2 * GLA_DV), F32)],
        compiler_params=pltpu.CompilerParams(dimension_semantics=("arbitrary", "arbitrary"),
                                             vmem_limit_bytes=VMEM_LIMIT),
        name="mixer_long",
    )(x, pprev, s0, *wts)


def _mixer_meta(x_ref, wts, x1_ref, u_ref, s_ref):
    w_in_ref, w_a2_ref, b_a_ref, w_pool_ref, pscale_ref, gnorm_ref, w_out_ref, g_ref, b_ref = wts
    L = N_META
    x = x_ref[...]
    xb = x.astype(BF16)
    u, zr, q, k, v, r = _project_in(xb, w_in_ref)
    u_ref[...] = u

    row128 = lax.broadcasted_iota(jnp.int32, (L, POOL_GROUP), 0)
    pos1 = lax.broadcasted_iota(jnp.int32, (L, 1), 0)
    d_groups = []
    for g, w in enumerate(POOL_WINDOWS):
        ug = u[:, g * POOL_GROUP:(g + 1) * POOL_GROUP]
        s = ug
        sh = 1
        while sh < w:
            s = s + jnp.where(row128 >= sh, _roll_rows(s, sh), 0.0)
            sh *= 2
        d_groups.append(s / jnp.minimum(w, pos1 + 1).astype(F32) - ug)
    y_pool = _pool_project(d_groups, w_pool_ref, pscale_ref)

    loga = _gate_log_decay(zr, w_a2_ref, b_a_ref)
    row256 = lax.broadcasted_iota(jnp.int32, (L, D_GLA_K), 0)
    b = loga
    sh = 1
    while sh < L:
        b = b + jnp.where(row256 >= sh, _roll_rows(b, sh), 0.0)
        sh *= 2
    bend = b[L - 1:L]
    q_in = (q * jnp.exp(b)).astype(BF16)
    k_in = (k * jnp.exp(-b)).astype(BF16)
    k_st = (k * jnp.exp(bend - b)).astype(BF16)
    vb = v.astype(BF16)
    ar = lax.broadcasted_iota(jnp.int32, (L, 2 * L), 0)
    ac = lax.broadcasted_iota(jnp.int32, (L, 2 * L), 1) & (L - 1)
    causal = ac <= ar
    lane128 = lax.broadcasted_iota(jnp.int32, (L, 128), 1)
    lane256 = lax.broadcasted_iota(jnp.int32, (L, 256), 1)
    attn = []
    for p in range(2):
        ks = slice(128 * p, 128 * (p + 1))
        a = _dot_nt(q_in[:, ks], _head_pair_keys(k_in[:, ks], lane128))
        attn.append(jnp.where(causal, a, 0.0).astype(BF16))
    o_parts = []
    for p in range(2):
        ks = slice(128 * p, 128 * (p + 1))
        vs = slice(256 * p, 256 * (p + 1))
        s_p = _dot_tn(k_st[:, ks], vb[:, vs])
        s_ref[2 * p] = s_p[0:GLA_DK, 0:GLA_DV]
        s_ref[2 * p + 1] = s_p[GLA_DK:2 * GLA_DK, GLA_DV:2 * GLA_DV]
        o_parts.append(_dot(attn[p], _head_pair_values(vb[:, vs], lane256)))
    o = jnp.concatenate(o_parts, axis=1)

    y_gla = _gla_output_gate(o, r, gnorm_ref[...])
    x1_ref[...] = _mix_out(x, y_pool, y_gla, w_out_ref, g_ref, b_ref)


def _mixer_sample(x_ref, hist_ref, s0_ref, wts, x1_ref, hist_out_ref, snew_ref, *, seq_len):
    w_in_ref, w_a2_ref, b_a_ref, w_pool_ref, pscale_ref, gnorm_ref, w_out_ref, g_ref, b_ref = wts
    G = SEQ_GROUP
    R = G * seq_len
    n_groups = x_ref.shape[0] // R
    x = x_ref[...]
    xb = x.astype(BF16)
    u, zr, q, k, v, r = _project_in(xb, w_in_ref)
    loga = _gate_log_decay(zr, w_a2_ref, b_a_ref)
    results = []
    gens = []
    for gi in range(n_groups):
        rs = slice(gi * R, (gi + 1) * R)
        hist_rows = pl.ds(gi * G * POOL_BUF, G * POOL_BUF)
        seqs = pl.ds(gi * G, G)
        gens.append(_sample_group(u[rs], q[rs], k[rs], v[rs], loga[rs], hist_ref.at[hist_rows], s0_ref.at[seqs],
                                  hist_out_ref.at[hist_rows], snew_ref.at[seqs], results, seq_len=seq_len))
    for _ in itertools.zip_longest(*gens):
        pass
    d_groups = [jnp.concatenate([res[0][g] for res in results], axis=0) for g in range(len(POOL_WINDOWS))]
    o = jnp.concatenate([res[1] for res in results], axis=0)
    y_pool = _pool_project(d_groups, w_pool_ref, pscale_ref)
    y_gla = _gla_output_gate(o, r, gnorm_ref[...])
    x1_ref[...] = _mix_out(x, y_pool, y_gla, w_out_ref, g_ref, b_ref)


def _sample_group(u, q, k, v, loga, hist_ref, s0_ref, hist_out_ref, snew_ref, results, *, seq_len):
    G, Ls = SEQ_GROUP, seq_len
    R = G * Ls
    NS = G * GLA_DK
    g_shift = G.bit_length() - 1
    hist_out_ref[0:(POOL_BUF - Ls) * G, :] = hist_ref[R:POOL_BUF * G, :]
    hist_out_ref[(POOL_BUF - Ls) * G:POOL_BUF * G, :] = u

    def blk(a, t):
        return a[t * G:(t + 1) * G]

    d_groups = []
    for g, w in enumerate(POOL_WINDOWS):
        cols = slice(g * POOL_GROUP, (g + 1) * POOL_GROUP)
        ug = u[:, cols]
        suffix = [None]
        acc = None
        for m in range(1, min(w - 1, POOL_BUF) + 1):
            h = hist_ref[(POOL_BUF - m) * G:(POOL_BUF - m + 1) * G, cols]
            acc = h if acc is None else acc + h
            suffix.append(acc)
        parts = []
        for t in range(Ls):
            wsum = blk(ug, t)
            for j in range(max(0, t - w + 1), t):
                wsum = wsum + blk(ug, j)
            m = w - 1 - t
            if m > 0:
                wsum = wsum + suffix[m]
            parts.append(wsum * (1.0 / w) - blk(ug, t))
        d_groups.append(jnp.concatenate(parts, axis=0))

    b_t = [blk(loga, 0)]
    for t in range(1, Ls):
        b_t.append(b_t[-1] + blk(loga, t))
    b = jnp.concatenate(b_t, axis=0)
    bend = jnp.concatenate([b_t[-1]] * Ls, axis=0)
    q_in = (q * jnp.exp(b)).astype(BF16)
    k_in = (k * jnp.exp(-b)).astype(BF16)
    k_st = k * jnp.exp(bend - b)
    dec_hi, dec_lo = _split_bf16(jnp.exp(bend))
    tok = lax.broadcasted_iota(jnp.int32, (R, D_GLA_K), 0) >> g_shift
    dec_rows = jnp.where(tok == Ls - 1, dec_hi, jnp.where(tok == Ls - 2, dec_lo, jnp.zeros_like(dec_lo)))
    vb = v.astype(BF16)

    ar = lax.broadcasted_iota(jnp.int32, (R, 2 * R), 0)
    ac = lax.broadcasted_iota(jnp.int32, (R, 2 * R), 1) & (R - 1)
    same_seq_causal = ((ac & (G - 1)) == (ar & (G - 1))) & ((ac >> g_shift) <= (ar >> g_shift))
    lane128 = lax.broadcasted_iota(jnp.int32, (R, 128), 1)
    lane256 = lax.broadcasted_iota(jnp.int32, (R, 256), 1)
    br = lax.broadcasted_iota(jnp.int32, (R, NS), 0)
    bcol = lax.broadcasted_iota(jnp.int32, (R, NS), 1)
    own_state = (bcol >> 6) == (br & (G - 1))
    ones_blk = jnp.ones((R, GLA_DV), BF16)
    zeros_blk = jnp.zeros((R, GLA_DV), BF16)

    def expand(xp, first):
        sw = pltpu.roll(xp, GLA_DK, 1)
        two = jnp.where(lane128 < GLA_DK, xp, sw) if first else jnp.where(lane128 < GLA_DK, sw, xp)
        rep = jnp.concatenate([two] * (NS // 128), axis=1)
        return jnp.where(own_state, rep, 0.0).astype(BF16)

    ks = [slice(128 * p, 128 * (p + 1)) for p in range(2)]
    vs = [slice(256 * p, 256 * (p + 1)) for p in range(2)]
    attn = []
    for p in range(2):
        a = _dot_nt(q_in[:, ks[p]], _head_pair_keys(k_in[:, ks[p]], lane128))
        attn.append(jnp.where(same_seq_causal, a, 0.0).astype(BF16))
    yield
    inter = []
    for h in range(GLA_HEADS):
        p, first = h // 2, h % 2 == 0
        s_flat = s0_ref[:, h].reshape(NS, GLA_DV)
        inter.append(_dot(expand(q_in[:, ks[p]].astype(F32), first), s_flat.astype(BF16)))
    for h in range(GLA_HEADS):
        p, first = h // 2, h % 2 == 0
        s_flat = s0_ref[:, h].reshape(NS, GLA_DV)
        lhs = jnp.concatenate([expand(k_st[:, ks[p]], first),
                               expand(dec_rows[:, ks[p]].astype(F32), first)], axis=0)
        v_h = vb[:, h * GLA_DV:(h + 1) * GLA_DV]
        rhs = jnp.concatenate([jnp.concatenate([v_h, zeros_blk], axis=1),
                               jnp.concatenate([zeros_blk, ones_blk], axis=1)], axis=0)
        ud = _dot_tn(lhs, rhs)
        s_new = ud[:, GLA_DV:] * s_flat + ud[:, :GLA_DV]
        snew_ref[:, h] = s_new.reshape(G, GLA_DK, GLA_DV)
    yield
    o_parts = []
    for p in range(2):
        o_intra = _dot(attn[p], _head_pair_values(vb[:, vs[p]], lane256))
        o_parts.append(o_intra + jnp.concatenate(inter[2 * p:2 * p + 2], axis=1))
    results.append((d_groups, jnp.concatenate(o_parts, axis=1)))


def _mixer_short_kernel(*refs, seq_len, n_prev):
    xm_ref, xs_ref, hist_ref, s0_ref = refs[:4]
    wts = refs[4:4 + N_MIX_W]
    x1m_ref, um_ref, sm_ref, x1s_ref, us_ref, ss_ref = refs[-6:]
    i = pl.program_id(0)

    @pl.when(i == 0)
    def _meta():
        _mixer_meta(xm_ref, wts, x1m_ref, um_ref, sm_ref)

    @pl.when(i > 0)
    def _sample():
        for a in range(n_prev):
            ss_ref[a] = refs[4 + N_MIX_W][a]
        _mixer_sample(xs_ref, hist_ref, s0_ref, wts, x1s_ref, us_ref, ss_ref.at[n_prev], seq_len=seq_len)


def _mixer_short(x_meta, x_samp, hist, s0, wts, layer, *, seq_len, s_prev=None):
    rows = x_samp.shape[0]
    n_groups = rows // (SEQ_GROUP * seq_len)
    per_step = GROUPS_PER_STEP if n_groups % GROUPS_PER_STEP == 0 else 1
    G = SEQ_GROUP * per_step
    R = G * seq_len
    n_tiles = rows // R
    assert rows % R == 0
    clamp = lambda i: jnp.maximum(i - 1, 0)
    n_prev = 0 if s_prev is None else s_prev.shape[0]
    state_blk = (G, GLA_HEADS, GLA_DK, GLA_DV)
    in_specs = ([pl.BlockSpec((N_META, D_MODEL), lambda i: (0, 0)),
                 pl.BlockSpec((R, D_MODEL), lambda i: (clamp(i), 0)),
                 pl.BlockSpec((None, G * POOL_BUF, D_POOL), lambda i: (layer, clamp(i), 0)),
                 pl.BlockSpec((None,) + state_blk, lambda i: (layer, clamp(i), 0, 0, 0))]
                + _mixer_weight_specs(layer))
    args = [x_meta, x_samp, hist, s0, *wts]
    if n_prev:
        in_specs.append(pl.BlockSpec((n_prev,) + state_blk, lambda i: (0, clamp(i), 0, 0, 0)))
        args.append(s_prev)
    kern = functools.partial(_mixer_short_kernel, seq_len=seq_len, n_prev=n_prev)

    return pl.pallas_call(
        kern,
        grid=(n_tiles + 1,),
        in_specs=in_specs,
        out_specs=[pl.BlockSpec((N_META, D_MODEL), lambda i: (0, 0)),
                   pl.BlockSpec((N_META, D_POOL), lambda i: (0, 0)),
                   pl.BlockSpec((GLA_HEADS, GLA_DK, GLA_DV), lambda i: (0, 0, 0)),
                   pl.BlockSpec((R, D_MODEL), lambda i: (clamp(i), 0)),
                   pl.BlockSpec((G * POOL_BUF, D_POOL), lambda i: (clamp(i), 0)),
                   pl.BlockSpec((n_prev + 1,) + state_blk, lambda i: (0, clamp(i), 0, 0, 0))],
        out_shape=[jax.ShapeDtypeStruct((N_META, D_MODEL), F32),
                   jax.ShapeDtypeStruct((N_META, D_POOL), F32),
                   jax.ShapeDtypeStruct((GLA_HEADS, GLA_DK, GLA_DV), F32),
                   jax.ShapeDtypeStruct((rows, D_MODEL), F32),
                   jax.ShapeDtypeStruct((n_tiles * G * POOL_BUF, D_POOL), F32),
                   jax.ShapeDtypeStruct((n_prev + 1,) + s0.shape[1:], F32)],
        compiler_params=pltpu.CompilerParams(dimension_semantics=("arbitrary",),
                                             vmem_limit_bytes=VMEM_LIMIT),
        name="mixer_short",
    )(*args)


def _ffn_tile(x_ref, wts, y_ref, conv_inputs, store_gate):
    w_up_ref, w_gate_ref, cw_ref, cb_ref, w_down_ref, g_ref, b_ref = wts
    x = x_ref[...]
    xb = x.astype(BF16)

    acts = []
    for j in range(D_FF // FF_CHUNK):
        cs = slice(j * FF_CHUNK, (j + 1) * FF_CHUNK)
        a = _dot(xb, w_up_ref[:, cs])
        gt = _dot(xb, w_gate_ref[:, cs])
        g1, g2 = conv_inputs(gt, cs)
        store_gate(gt, cs)
        gc = cb_ref[:, cs] + cw_ref[0:1, cs] * g2 + cw_ref[1:2, cs] * g1 + cw_ref[2:3, cs] * gt
        acts.append((a * _silu(gc)).astype(BF16))
    yield

    act = jnp.concatenate(acts, axis=1)
    rows = x.shape[0]
    rb = min(rows, DOWN_ROWS)
    for r0 in range(0, rows, rb):
        f = _dot(act[r0:r0 + rb], w_down_ref[...])
        y_ref[r0:r0 + rb, :] = _layer_norm(ALPHA * x[r0:r0 + rb] + f, g_ref[...], b_ref[...])


def _ffn_long_kernel(*refs, n_par):
    x_ref, cprev_ref = refs[:2]
    wts = refs[2:2 + N_FFN_W]
    y_ref, hist_out_ref, gbuf = refs[2 + N_FFN_W:]
    t = pl.program_id(1)
    T = x_ref.shape[1]

    @pl.when(t == 0)
    def _init():
        for j in range(n_par):
            gbuf[j, 0:CONV_PAD, :] = cprev_ref[...]

    def make_tile(j):
        def conv_inputs(gt, cs):
            gbuf[j, CONV_PAD:CONV_PAD + T, cs] = gt
            return gbuf[j, CONV_PAD - 1:CONV_PAD - 1 + T, cs], gbuf[j, CONV_PAD - 2:CONV_PAD - 2 + T, cs]

        def store_gate(gt, cs):
            gbuf[j, 0:CONV_PAD, cs] = gbuf[j, T:T + CONV_PAD, cs]

        return _ffn_tile(x_ref.at[j], wts, y_ref.at[j], conv_inputs, store_gate)

    for _ in itertools.zip_longest(*[make_tile(j) for j in range(n_par)]):
        pass

    @pl.when(t == pl.num_programs(1) - 1)
    def _final():
        for j in range(n_par):
            hist_out_ref[j] = gbuf[j, 0:CONV_PAD, :]


def _ffn_short_kernel(xm_ref, xs_ref, hist_ref, w_up_ref, w_gate_ref, cw_ref, cb_ref, w_down_ref, g_ref, b_ref,
                      ym_ref, cm_ref, ys_ref, cs_ref, w_up_bf_ref, w_gate_bf_ref, w_down_bf_ref,
                      fm_ref, fs_ref, *, seq_len):
    j = pl.program_id(0)
    G, Ls = SEQ_GROUP, seq_len
    R = G * Ls
    rows = xs_ref.shape[0]
    n_tiles = rows // R

    w_up = w_up_ref[...].astype(BF16)
    w_gate = w_gate_ref[...].astype(BF16)
    w_down = w_down_ref[...].astype(BF16)
    w_up_bf_ref[...] = w_up
    w_gate_bf_ref[...] = w_gate
    w_down_bf_ref[...] = w_down

    @pl.when(j == 0)
    def _first():
        fm_ref[...] = jnp.zeros_like(fm_ref)
        fs_ref[...] = jnp.zeros_like(fs_ref)

    def block(x_ref, f_ref, conv_inputs, store_gate):
        xb = x_ref[...].astype(BF16)
        a = _dot(xb, w_up)
        gt = _dot(xb, w_gate)
        g1, g2 = conv_inputs(gt)
        store_gate(gt)
        gc = cb_ref[...] + cw_ref[0:1, :] * g2 + cw_ref[1:2, :] * g1 + cw_ref[2:3, :] * gt
        f_ref[...] += _dot((a * _silu(gc)).astype(BF16), w_down)

    tok = (lax.broadcasted_iota(jnp.int32, (rows, FF_CHUNK), 0) & (R - 1)) >> (G.bit_length() - 1)
    zeros_tail = jnp.zeros((R - CONV_BUF * G, FF_CHUNK), F32)

    def sample_conv(gt):
        hx = jnp.concatenate([piece for n in range(n_tiles)
                              for piece in (hist_ref[n * CONV_BUF * G:(n + 1) * CONV_BUF * G, :], zeros_tail)], axis=0)
        g1 = jnp.where(tok >= 1, _roll_rows(gt, G), _roll_rows(hx, -G))
        g2 = jnp.where(tok >= 2, _roll_rows(gt, 2 * G), hx)
        return g1, g2

    def sample_store(gt):
        for n in range(n_tiles):
            cs_ref[n * CONV_BUF * G:(n + 1) * CONV_BUF * G, :] = gt[n * R + (Ls - CONV_BUF) * G:(n + 1) * R]

    rowm = lax.broadcasted_iota(jnp.int32, (N_META, FF_CHUNK), 0)

    def meta_conv(gt):
        return (jnp.where(rowm >= 1, _roll_rows(gt, 1), 0.0), jnp.where(rowm >= 2, _roll_rows(gt, 2), 0.0))

    def meta_store(gt):
        cm_ref[...] = gt[N_META - CONV_PAD:N_META]

    block(xs_ref, fs_ref, sample_conv, sample_store)
    block(xm_ref, fm_ref, meta_conv, meta_store)

    @pl.when(j == pl.num_programs(0) - 1)
    def _last():
        ys_ref[...] = _layer_norm(ALPHA * xs_ref[...] + fs_ref[...], g_ref[...], b_ref[...])
        ym_ref[...] = _layer_norm(ALPHA * xm_ref[...] + fm_ref[...], g_ref[...], b_ref[...])


def _ffn_weight_specs(layer):
    whole = lambda shape: pl.BlockSpec(shape, lambda *_: (0,) * len(shape), pipeline_mode=pl.Buffered(1))
    specs = [whole((D_MODEL, D_FF)), whole((D_MODEL, D_FF)), _layer_spec((3, D_FF), layer), _layer_spec((1, D_FF), layer),
             whole((D_FF, D_MODEL)), _layer_spec((1, D_MODEL), layer), _layer_spec((1, D_MODEL), layer)]
    assert len(specs) == N_FFN_W
    return specs


def _ffn_long(x, cprev, wts, layer):
    B, L, _ = x.shape
    T = min(FFN_TILE, L)
    P = LONG_PAR if B % LONG_PAR == 0 else 1
    assert L % T == 0
    kern = functools.partial(_ffn_long_kernel, n_par=P)
    return pl.pallas_call(
        kern,
        grid=(B // P, L // T),
        in_specs=[pl.BlockSpec((P, T, D_MODEL), lambda b, t: (b, t, 0)),
                  pl.BlockSpec((CONV_PAD, D_FF), lambda b, t: (0, 0))] + _ffn_weight_specs(layer),
        out_specs=[pl.BlockSpec((P, T, D_MODEL), lambda b, t: (b, t, 0)),
                   pl.BlockSpec((P, CONV_PAD, D_FF), lambda b, t: (b, 0, 0))],
        out_shape=[jax.ShapeDtypeStruct((B, L, D_MODEL), F32),
                   jax.ShapeDtypeStruct((B, CONV_PAD, D_FF), F32)],
        scratch_shapes=[pltpu.VMEM((P, T + CONV_PAD, D_FF), F32)],
        compiler_params=pltpu.CompilerParams(dimension_semantics=("arbitrary", "arbitrary"),
                                             vmem_limit_bytes=VMEM_LIMIT),
        name="ffn_long",
    )(x, cprev, *wts)


def _ffn_short(x_meta, x_samp, hist, wts, layer, *, seq_len):
    rows = x_samp.shape[0]
    n_hist = hist.shape[1]
    C = FF_CHUNK
    const = lambda shape: pl.BlockSpec(shape, lambda j: (0,) * len(shape))
    kern = functools.partial(_ffn_short_kernel, seq_len=seq_len)
    return pl.pallas_call(
        kern,
        grid=(D_FF // C,),
        in_specs=[const((N_META, D_MODEL)),
                  const((rows, D_MODEL)),
                  pl.BlockSpec((None, n_hist, C), lambda j: (layer, 0, j)),
                  pl.BlockSpec((None, D_MODEL, C), lambda j: (layer, 0, j)),
                  pl.BlockSpec((None, D_MODEL, C), lambda j: (layer, 0, j)),
                  pl.BlockSpec((None, 3, C), lambda j: (layer, 0, j)),
                  pl.BlockSpec((None, 1, C), lambda j: (layer, 0, j)),
                  pl.BlockSpec((None, C, D_MODEL), lambda j: (layer, j, 0)),
                  pl.BlockSpec((None, 1, D_MODEL), lambda j: (layer, 0, 0)),
                  pl.BlockSpec((None, 1, D_MODEL), lambda j: (layer, 0, 0))],
        out_specs=[const((N_META, D_MODEL)),
                   pl.BlockSpec((CONV_PAD, C), lambda j: (0, j)),
                   const((rows, D_MODEL)),
                   pl.BlockSpec((n_hist, C), lambda j: (0, j)),
                   pl.BlockSpec((D_MODEL, C), lambda j: (0, j)),
                   pl.BlockSpec((D_MODEL, C), lambda j: (0, j)),
                   pl.BlockSpec((C, D_MODEL), lambda j: (j, 0))],
        out_shape=[jax.ShapeDtypeStruct((N_META, D_MODEL), F32),
                   jax.ShapeDtypeStruct((CONV_PAD, D_FF), F32),
                   jax.ShapeDtypeStruct((rows, D_MODEL), F32),
                   jax.ShapeDtypeStruct((n_hist, D_FF), F32),
                   jax.ShapeDtypeStruct((D_MODEL, D_FF), BF16),
                   jax.ShapeDtypeStruct((D_MODEL, D_FF), BF16),
                   jax.ShapeDtypeStruct((D_FF, D_MODEL), BF16)],
        scratch_shapes=[pltpu.VMEM((N_META, D_MODEL), F32),
                        pltpu.VMEM((rows, D_MODEL), F32)],
        compiler_params=pltpu.CompilerParams(dimension_semantics=("arbitrary",),
                                             vmem_limit_bytes=VMEM_LIMIT),
        name="ffn_short",
    )(x_meta, x_samp, hist, *wts)


def _to_group_major(a, axis):
    n, j = a.shape[axis], a.shape[axis + 1]
    lead, tail = a.shape[:axis], a.shape[axis + 2:]
    a = a.reshape(*lead, n // SEQ_GROUP, SEQ_GROUP, j, *tail)
    a = jnp.swapaxes(a, axis + 1, axis + 2)
    return a.reshape(*lead, n * j, *tail)


def _from_group_major(a, axis, j):
    rows = a.shape[axis]
    n = rows // j
    lead, tail = a.shape[:axis], a.shape[axis + 1:]
    a = a.reshape(*lead, n // SEQ_GROUP, j, SEQ_GROUP, *tail)
    a = jnp.swapaxes(a, axis + 1, axis + 2)
    return a.reshape(*lead, n, j, *tail)


def kernel(x_prompt, x_sample, state_pool, state_gla, state_conv, meta_tokens,
           w_in, w_a2, b_a, w_pool, pool_scale, gla_norm, w_out, ln1_g, ln1_b,
           w_up, w_gate, conv_w, conv_b, w_down, ln2_g, ln2_b):
    NB, LS = x_sample.shape[0], x_sample.shape[1]
    assert NB % SEQ_GROUP == 0 and LS & (LS - 1) == 0 and CONV_BUF <= LS <= POOL_BUF

    mix_w = (w_in.astype(BF16), w_a2.astype(BF16), b_a[:, None],
             w_pool.astype(BF16), pool_scale[:, None], gla_norm[:, None], w_out.astype(BF16),
             ln1_g[:, None], ln1_b[:, None])
    ffn_w = (w_up, w_gate, conv_w, conv_b[:, None], w_down, ln2_g[:, None], ln2_b[:, None])

    hm = meta_tokens.astype(F32)
    hp = x_prompt
    hs = _to_group_major(x_sample, 0)
    pool_hist = _to_group_major(state_pool, 1)
    conv_hist = _to_group_major(state_conv, 1)

    pp, gp, cp, ps_l, cs_l = [], [], [], [], []
    gs = None
    for l in range(DEPTH):
        hm1, um, sm, hs1, ps_new, gs = _mixer_short(hm, hs, pool_hist, state_gla, mix_w, l, seq_len=LS, s_prev=gs)
        hm, cm, hs, cs_new, w_up_bf, w_gate_bf, w_down_bf = _ffn_short(hm1, hs1, conv_hist, ffn_w, l, seq_len=LS)
        ps_l.append(ps_new)
        cs_l.append(cs_new)

        hp1, pbuf, snew = _mixer_long(hp, um, sm, mix_w, l)
        hp, cbuf = _ffn_long(hp1, cm, (w_up_bf, w_gate_bf) + ffn_w[2:4] + (w_down_bf,) + ffn_w[5:], l)
        pp.append(pbuf[:, 1:])
        gp.append(snew)
        cp.append(cbuf[:, CONV_PAD - CONV_BUF:])

    ps = _from_group_major(jnp.stack(ps_l), 1, POOL_BUF)
    cs = _from_group_major(jnp.stack(cs_l), 1, CONV_BUF)
    return (hp, _from_group_major(hs, 0, LS), jnp.stack(pp), jnp.stack(gp), jnp.stack(cp), ps, gs, cs)
```

```python
import functools
import itertools

import jax
import jax.numpy as jnp
from jax import lax
from jax.experimental import pallas as pl
from jax.experimental.pallas import tpu as pltpu

F32 = jnp.float32
BF16 = jnp.bfloat16

D_MODEL = 1024
N_META = 16
D_POOL = 512
POOL_WINDOWS = (2, 4, 8, 16)
POOL_GROUP = 128
POOL_BUF = 15
POOL_PAD = 16
D_GLA = 512
GLA_HEADS = 4
GLA_DV = 128
GLA_DK = 64
D_GLA_K = 256
GATE_RANK = 16
GATE_TAU = 16.0
D_FF = 2816
CONV_BUF = 2
CONV_PAD = 8
DEPTH = 2
ALPHA = (2 * DEPTH) ** 0.25
LN_EPS = 1e-5
RMS_EPS = 1e-6
PAST_LEN = 16384

C_POOL, C_Q, C_K, C_V, C_R, C_Z, C_END = 0, 512, 768, 1024, 1536, 2048, 2064

LONG_TILE = 512
LONG_CHUNK = 64
LONG_PAR = 2
MIXER_PAR = 2
MIXER_SKEW = 1
FFN_TILE = 512
SEQ_GROUP = 16
GROUPS_PER_STEP = 2
FF_CHUNK = 256
DOWN_ROWS = 256
SHORT_FFN_PARTS = 4
VMEM_LIMIT = 56 * 1024 * 1024

N_MIX_W = 9
N_FFN_W = 7


def _dot(a, b):
    return jnp.dot(a, b, preferred_element_type=F32)


def _dot_nt(a, b):
    return lax.dot_general(a, b, (((1,), (1,)), ((), ())), preferred_element_type=F32)


def _dot_tn(a, b):
    return lax.dot_general(a, b, (((0,), (0,)), ((), ())), preferred_element_type=F32)


def _layer_norm(y, g, b):
    mu = jnp.mean(y, axis=-1, keepdims=True)
    yc = y - mu
    var = jnp.mean(yc * yc, axis=-1, keepdims=True)
    return yc * lax.rsqrt(var + LN_EPS) * g + b


def _silu(x):
    h = 0.5 * x
    return h + h * jnp.tanh(h)


def _log_sigmoid(z):
    return jnp.minimum(z, 0.0) - jnp.log(1.0 + jnp.exp(-jnp.abs(z)))


def _roll_rows(x, shift):
    n = x.shape[0]
    return pltpu.roll(x, shift % n, 0)


def _split_bf16(x):
    hi = x.astype(BF16)
    lo = (x - hi.astype(F32)).astype(BF16)
    return hi, lo


def _project_in(xb, w_in_ref):
    u = _dot(xb, w_in_ref[:, C_POOL:C_Q])
    zr = _dot(xb, w_in_ref[:, C_Z:C_END])
    q = _dot(xb, w_in_ref[:, C_Q:C_K]) * (GLA_DK ** -0.5)
    k = _dot(xb, w_in_ref[:, C_K:C_V])
    v = _dot(xb, w_in_ref[:, C_V:C_R])
    r = _dot(xb, w_in_ref[:, C_R:C_Z])
    return u, zr, q, k, v, r


def _gate_log_decay(zr, w_a2_ref, b_a_ref):
    z = _dot(zr.astype(BF16), w_a2_ref[...]) + b_a_ref[...]
    return _log_sigmoid(z) * (1.0 / GATE_TAU)


def _gla_output_gate(o, r, gnorm):
    parts = []
    for h in range(GLA_HEADS):
        oh = o[:, h * GLA_DV:(h + 1) * GLA_DV]
        ms = jnp.mean(oh * oh, axis=-1, keepdims=True)
        parts.append(oh * lax.rsqrt(ms + RMS_EPS) * gnorm)
    return jnp.concatenate(parts, axis=1) * _silu(r)


def _pool_project(d_groups, w_pool_ref, pscale_ref):
    ys = [_dot(d.astype(BF16), w_pool_ref[g]) for g, d in enumerate(d_groups)]
    return jnp.concatenate(ys, axis=1) * pscale_ref[...]


def _mix_out(x, y_pool, y_gla, w_out_ref, g_ref, b_ref):
    mix = jnp.concatenate([y_pool, y_gla], axis=1).astype(BF16)
    return _layer_norm(ALPHA * x + _dot(mix, w_out_ref[...]), g_ref[...], b_ref[...])


def _emit_staggered(stage_gens, first_round):
    live = list(zip(first_round, stage_gens))
    rnd = 0
    while live:
        for start, g in list(live):
            if rnd >= start and next(g, StopIteration) is StopIteration:
                live.remove((start, g))
        rnd += 1


def _head_pair_keys(k_p, lane128):
    zk = jnp.zeros_like(k_p)
    return jnp.concatenate([jnp.where(lane128 < GLA_DK, k_p, zk), jnp.where(lane128 >= GLA_DK, k_p, zk)], axis=0)


def _head_pair_values(v_p, lane256):
    zv = jnp.zeros_like(v_p)
    return jnp.concatenate([jnp.where(lane256 < GLA_DV, v_p, zv), jnp.where(lane256 >= GLA_DV, v_p, zv)], axis=0)


def _mixer_long_kernel(*refs, n_par, tile, chunk):
    x_ref, pprev_ref, s0_ref = refs[:3]
    wts = refs[3:3 + N_MIX_W]
    x1_ref, pbuf_ref, snew_ref, ubuf, sbd = refs[3 + N_MIX_W:]
    t = pl.program_id(1)
    zero_blk = jnp.zeros((GLA_DK, GLA_DV), F32)

    @pl.when(t == 0)
    def _init():
        for j in range(n_par):
            ubuf[j, 0:POOL_PAD, :] = pprev_ref[...]
            for p in range(2):
                top = jnp.concatenate([s0_ref[2 * p], zero_blk], axis=1)
                bot = jnp.concatenate([zero_blk, s0_ref[2 * p + 1]], axis=1)
                sbd[j, p] = jnp.concatenate([top, bot], axis=0)

    tiles = [_mixer_long_tile(x_ref.at[j], wts, x1_ref.at[j], ubuf.at[j], sbd.at[j], tile=tile, chunk=chunk)
             for j in range(n_par)]
    _emit_staggered(tiles, [MIXER_SKEW * j for j in range(n_par)])

    @pl.when(t == pl.num_programs(1) - 1)
    def _final():
        for j in range(n_par):
            pbuf_ref[j] = ubuf[j, 0:POOL_PAD, :]
            for p in range(2):
                s_p = sbd[j, p]
                snew_ref[j, 2 * p] = s_p[0:GLA_DK, 0:GLA_DV]
                snew_ref[j, 2 * p + 1] = s_p[GLA_DK:2 * GLA_DK, GLA_DV:2 * GLA_DV]


def _mixer_long_tile(x_ref, wts, x1_ref, ubuf, sbd, *, tile, chunk):
    w_in_ref, w_a2_ref, b_a_ref, w_pool_ref, pscale_ref, gnorm_ref, w_out_ref, g_ref, b_ref = wts
    T, C = tile, chunk
    x = x_ref[...]
    xb = x.astype(BF16)

    u, zr, q, k, v, r = _project_in(xb, w_in_ref)
    yield

    ubuf[POOL_PAD:POOL_PAD + T, :] = u
    d_groups = []
    for g, w in enumerate(POOL_WINDOWS):
        s = ubuf[:, g * POOL_GROUP:(g + 1) * POOL_GROUP]
        sh = 1
        while sh < w:
            s = s + _roll_rows(s, sh)
            sh *= 2
        d_groups.append(s[POOL_PAD:, :] * (1.0 / w) - u[:, g * POOL_GROUP:(g + 1) * POOL_GROUP])
    y_pool = _pool_project(d_groups, w_pool_ref, pscale_ref)
    ubuf[0:POOL_PAD, :] = ubuf[T:T + POOL_PAD, :]
    loga = _gate_log_decay(zr, w_a2_ref, b_a_ref)

    tr = lax.broadcasted_iota(jnp.int32, (C, C), 0)
    tc = lax.broadcasted_iota(jnp.int32, (C, C), 1)
    tri = jnp.where(tc <= tr, 1.0, 0.0).astype(BF16)
    ar = lax.broadcasted_iota(jnp.int32, (C, 2 * C), 0)
    ac = lax.broadcasted_iota(jnp.int32, (C, 2 * C), 1) & (C - 1)
    causal = ac <= ar
    lane128 = lax.broadcasted_iota(jnp.int32, (C, 128), 1)
    lane256 = lax.broadcasted_iota(jnp.int32, (C, 256), 1)
    sr = lax.broadcasted_iota(jnp.int32, (128, 256), 0)
    sc = lax.broadcasted_iota(jnp.int32, (128, 256), 1)
    blockdiag = (sr >= GLA_DK) == (sc >= GLA_DV)
    mid = C // 2 - 1

    n_chunks = T // C
    pairs = [(c, p) for c in range(n_chunks) for p in range(2)]
    ks = [slice(128 * p, 128 * (p + 1)) for p in range(2)]
    vs = [slice(256 * p, 256 * (p + 1)) for p in range(2)]

    bcs = []
    for c in range(n_chunks):
        la_hi, la_lo = _split_bf16(loga[c * C:(c + 1) * C])
        bb = _dot(tri, jnp.concatenate([la_hi, la_lo], axis=1))
        bcs.append(bb[:, :D_GLA_K] + bb[:, D_GLA_K:])
    yield

    q_in, k_in, q_st, k_st, dec_t, vc = [], [], [], [], [], []
    for c in range(n_chunks):
        bc = bcs[c]
        bmid = bc[mid:mid + 1]
        bend = bc[C - 1:C]
        qc = q[c * C:(c + 1) * C]
        kc = k[c * C:(c + 1) * C]
        q_in.append((qc * jnp.exp(bc - bmid)).astype(BF16))
        k_in.append((kc * jnp.exp(bmid - bc)).astype(BF16))
        q_st.append((qc * jnp.exp(bc)).astype(BF16))
        k_st.append((kc * jnp.exp(bend - bc)).astype(BF16))
        dec_t.append(jnp.transpose(jnp.broadcast_to(jnp.exp(bend), (128, D_GLA_K))))
        vc.append(v[c * C:(c + 1) * C].astype(BF16))
    yield

    attn, upd = {}, {}
    for c, p in pairs:
        a = _dot_nt(q_in[c][:, ks[p]], _head_pair_keys(k_in[c][:, ks[p]], lane128))
        attn[c, p] = jnp.where(causal, a, 0.0).astype(BF16)
    for c, p in pairs:
        u_cp = _dot_tn(k_st[c][:, ks[p]], vc[c][:, vs[p]])
        upd[c, p] = jnp.where(blockdiag, u_cp, 0.0)
    yield

    s_vals = [sbd[p] for p in range(2)]
    s_start = {}
    for c, p in pairs:
        s_start[c, p] = s_vals[p].astype(BF16)
        dec_p = dec_t[c][ks[p], :]
        s_vals[p] = jnp.concatenate([dec_p, dec_p], axis=1) * s_vals[p] + upd[c, p]
    for p in range(2):
        sbd[p] = s_vals[p]

    o_rows = [[], []]
    for c, p in pairs:
        vblk = _head_pair_values(vc[c][:, vs[p]], lane256)
        lhs = jnp.concatenate([attn[c, p], q_st[c][:, ks[p]]], axis=1)
        rhs = jnp.concatenate([vblk, s_start[c, p]], axis=0)
        o_rows[p].append(_dot(lhs, rhs))
    o = jnp.concatenate([jnp.concatenate(o_rows[p], axis=0) for p in range(2)], axis=1)
    yield

    y_gla = _gla_output_gate(o, r, gnorm_ref[...])
    x1_ref[...] = _mix_out(x, y_pool, y_gla, w_out_ref, g_ref, b_ref)


def _layer_spec(shape, layer):
    nd = len(shape)
    return pl.BlockSpec((None,) + shape, lambda *_: (layer,) + (0,) * nd, pipeline_mode=pl.Buffered(1))


def _mixer_weight_specs(layer):
    shapes = [(D_MODEL, C_END), (GATE_RANK, D_GLA_K), (1, D_GLA_K),
              (4, POOL_GROUP, POOL_GROUP), (1, D_POOL), (1, GLA_DV), (D_MODEL, D_MODEL), (1, D_MODEL), (1, D_MODEL)]
    assert len(shapes) == N_MIX_W
    return [_layer_spec(s, layer) for s in shapes]


def _mixer_long(x, pprev, s0, wts, layer):
    B, L, _ = x.shape
    T = min(LONG_TILE, L)
    P = MIXER_PAR if B % MIXER_PAR == 0 else 1
    assert L % T == 0 and T % LONG_CHUNK == 0
    kern = functools.partial(_mixer_long_kernel, n_par=P, tile=T, chunk=LONG_CHUNK)
    return pl.pallas_call(
        kern,
        grid=(B // P, L // T),
        in_specs=[pl.BlockSpec((P, T, D_MODEL), lambda b, t: (b, t, 0)),
                  pl.BlockSpec((POOL_PAD, D_POOL), lambda b, t: (0, 0)),
                  pl.BlockSpec((GLA_HEADS, GLA_DK, GLA_DV), lambda b, t: (0, 0, 0))]
        + _mixer_weight_specs(layer),
        out_specs=[pl.BlockSpec((P, T, D_MODEL), lambda b, t: (b, t, 0)),
                   pl.BlockSpec((P, POOL_PAD, D_POOL), lambda b, t: (b, 0, 0)),
                   pl.BlockSpec((P, GLA_HEADS, GLA_DK, GLA_DV), lambda b, t: (b, 0, 0, 0))],
        out_shape=[jax.ShapeDtypeStruct((B, L, D_MODEL), F32),
                   jax.ShapeDtypeStruct((B, POOL_PAD, D_POOL), F32),
                   jax.ShapeDtypeStruct((B, GLA_HEADS, GLA_DK, GLA_DV), F32)],
        scratch_shapes=[pltpu.VMEM((P, T + POOL_PAD, D_POOL), F32),
                        pltpu.VMEM((P, 2, 2 * GLA_DK, 2 * GLA_DV), F32)],
        compiler_params=pltpu.CompilerParams(dimension_semantics=("arbitrary", "arbitrary"),
                                             vmem_limit_bytes=VMEM_LIMIT),
        name="mixer_long",
    )(x, pprev, s0, *wts)


def _mixer_meta(x_ref, wts, x1_ref, u_ref, s_ref):
    w_in_ref, w_a2_ref, b_a_ref, w_pool_ref, pscale_ref, gnorm_ref, w_out_ref, g_ref, b_ref = wts
    L = N_META
    x = x_ref[...]
    xb = x.astype(BF16)
    u, zr, q, k, v, r = _project_in(xb, w_in_ref)
    u_ref[...] = u

    row128 = lax.broadcasted_iota(jnp.int32, (L, POOL_GROUP), 0)
    pos1 = lax.broadcasted_iota(jnp.int32, (L, 1), 0)
    d_groups = []
    for g, w in enumerate(POOL_WINDOWS):
        ug = u[:, g * POOL_GROUP:(g + 1) * POOL_GROUP]
        s = ug
        sh = 1
        while sh < w:
            s = s + jnp.where(row128 >= sh, _roll_rows(s, sh), 0.0)
            sh *= 2
        d_groups.append(s / jnp.minimum(w, pos1 + 1).astype(F32) - ug)
    y_pool = _pool_project(d_groups, w_pool_ref, pscale_ref)

    loga = _gate_log_decay(zr, w_a2_ref, b_a_ref)
    row256 = lax.broadcasted_iota(jnp.int32, (L, D_GLA_K), 0)
    b = loga
    sh = 1
    while sh < L:
        b = b + jnp.where(row256 >= sh, _roll_rows(b, sh), 0.0)
        sh *= 2
    bend = b[L - 1:L]
    q_in = (q * jnp.exp(b)).astype(BF16)
    k_in = (k * jnp.exp(-b)).astype(BF16)
    k_st = (k * jnp.exp(bend - b)).astype(BF16)
    vb = v.astype(BF16)
    ar = lax.broadcasted_iota(jnp.int32, (L, 2 * L), 0)
    ac = lax.broadcasted_iota(jnp.int32, (L, 2 * L), 1) & (L - 1)
    causal = ac <= ar
    lane128 = lax.broadcasted_iota(jnp.int32, (L, 128), 1)
    lane256 = lax.broadcasted_iota(jnp.int32, (L, 256), 1)
    attn = []
    for p in range(2):
        ks = slice(128 * p, 128 * (p + 1))
        a = _dot_nt(q_in[:, ks], _head_pair_keys(k_in[:, ks], lane128))
        attn.append(jnp.where(causal, a, 0.0).astype(BF16))
    o_parts = []
    for p in range(2):
        ks = slice(128 * p, 128 * (p + 1))
        vs = slice(256 * p, 256 * (p + 1))
        s_p = _dot_tn(k_st[:, ks], vb[:, vs])
        s_ref[2 * p] = s_p[0:GLA_DK, 0:GLA_DV]
        s_ref[2 * p + 1] = s_p[GLA_DK:2 * GLA_DK, GLA_DV:2 * GLA_DV]
        o_parts.append(_dot(attn[p], _head_pair_values(vb[:, vs], lane256)))
    o = jnp.concatenate(o_parts, axis=1)

    y_gla = _gla_output_gate(o, r, gnorm_ref[...])
    x1_ref[...] = _mix_out(x, y_pool, y_gla, w_out_ref, g_ref, b_ref)


def _mixer_sample(x_ref, hist_ref, s0_ref, wts, x1_ref, hist_out_ref, snew_ref, *, seq_len):
    w_in_ref, w_a2_ref, b_a_ref, w_pool_ref, pscale_ref, gnorm_ref, w_out_ref, g_ref, b_ref = wts
    G = SEQ_GROUP
    R = G * seq_len
    n_groups = x_ref.shape[0] // R
    x = x_ref[...]
    xb = x.astype(BF16)
    u, zr, q, k, v, r = _project_in(xb, w_in_ref)
    loga = _gate_log_decay(zr, w_a2_ref, b_a_ref)
    results = []
    gens = []
    for gi in range(n_groups):
        rs = slice(gi * R, (gi + 1) * R)
        hist_rows = pl.ds(gi * G * POOL_BUF, G * POOL_BUF)
        seqs = pl.ds(gi * G, G)
        gens.append(_sample_group(u[rs], q[rs], k[rs], v[rs], loga[rs], hist_ref.at[hist_rows], s0_ref.at[seqs],
                                  hist_out_ref.at[hist_rows], snew_ref.at[seqs], results, seq_len=seq_len))
    for _ in itertools.zip_longest(*gens):
        pass
    d_groups = [jnp.concatenate([res[0][g] for res in results], axis=0) for g in range(len(POOL_WINDOWS))]
    o = jnp.concatenate([res[1] for res in results], axis=0)
    y_pool = _pool_project(d_groups, w_pool_ref, pscale_ref)
    y_gla = _gla_output_gate(o, r, gnorm_ref[...])
    x1_ref[...] = _mix_out(x, y_pool, y_gla, w_out_ref, g_ref, b_ref)


def _sample_group(u, q, k, v, loga, hist_ref, s0_ref, hist_out_ref, snew_ref, results, *, seq_len):
    G, Ls = SEQ_GROUP, seq_len
    R = G * Ls
    NS = G * GLA_DK
    g_shift = G.bit_length() - 1
    hist_out_ref[0:(POOL_BUF - Ls) * G, :] = hist_ref[R:POOL_BUF * G, :]
    hist_out_ref[(POOL_BUF - Ls) * G:POOL_BUF * G, :] = u

    def blk(a, t):
        return a[t * G:(t + 1) * G]

    d_groups = []
    for g, w in enumerate(POOL_WINDOWS):
        cols = slice(g * POOL_GROUP, (g + 1) * POOL_GROUP)
        ug = u[:, cols]
        suffix = [None]
        acc = None
        for m in range(1, min(w - 1, POOL_BUF) + 1):
            h = hist_ref[(POOL_BUF - m) * G:(POOL_BUF - m + 1) * G, cols]
            acc = h if acc is None else acc + h
            suffix.append(acc)
        parts = []
        for t in range(Ls):
            wsum = blk(ug, t)
            for j in range(max(0, t - w + 1), t):
                wsum = wsum + blk(ug, j)
            m = w - 1 - t
            if m > 0:
                wsum = wsum + suffix[m]
            parts.append(wsum * (1.0 / w) - blk(ug, t))
        d_groups.append(jnp.concatenate(parts, axis=0))

    b_t = [blk(loga, 0)]
    for t in range(1, Ls):
        b_t.append(b_t[-1] + blk(loga, t))
    b = jnp.concatenate(b_t, axis=0)
    bend = jnp.concatenate([b_t[-1]] * Ls, axis=0)
    q_in = (q * jnp.exp(b)).astype(BF16)
    k_in = (k * jnp.exp(-b)).astype(BF16)
    k_st = k * jnp.exp(bend - b)
    dec_hi, dec_lo = _split_bf16(jnp.exp(bend))
    tok = lax.broadcasted_iota(jnp.int32, (R, D_GLA_K), 0) >> g_shift
    dec_rows = jnp.where(tok == Ls - 1, dec_hi, jnp.where(tok == Ls - 2, dec_lo, jnp.zeros_like(dec_lo)))
    vb = v.astype(BF16)

    ar = lax.broadcasted_iota(jnp.int32, (R, 2 * R), 0)
    ac = lax.broadcasted_iota(jnp.int32, (R, 2 * R), 1) & (R - 1)
    same_seq_causal = ((ac & (G - 1)) == (ar & (G - 1))) & ((ac >> g_shift) <= (ar >> g_shift))
    lane128 = lax.broadcasted_iota(jnp.int32, (R, 128), 1)
    lane256 = lax.broadcasted_iota(jnp.int32, (R, 256), 1)
    br = lax.broadcasted_iota(jnp.int32, (R, NS), 0)
    bcol = lax.broadcasted_iota(jnp.int32, (R, NS), 1)
    own_state = (bcol >> 6) == (br & (G - 1))
    ones_blk = jnp.ones((R, GLA_DV), BF16)
    zeros_blk = jnp.zeros((R, GLA_DV), BF16)

    def expand(xp, first):
        sw = pltpu.roll(xp, GLA_DK, 1)
        two = jnp.where(lane128 < GLA_DK, xp, sw) if first else jnp.where(lane128 < GLA_DK, sw, xp)
        rep = jnp.concatenate([two] * (NS // 128), axis=1)
        return jnp.where(own_state, rep, 0.0).astype(BF16)

    ks = [slice(128 * p, 128 * (p + 1)) for p in range(2)]
    vs = [slice(256 * p, 256 * (p + 1)) for p in range(2)]
    attn = []
    for p in range(2):
        a = _dot_nt(q_in[:, ks[p]], _head_pair_keys(k_in[:, ks[p]], lane128))
        attn.append(jnp.where(same_seq_causal, a, 0.0).astype(BF16))
    yield
    inter = []
    for h in range(GLA_HEADS):
        p, first = h // 2, h % 2 == 0
        s_flat = s0_ref[:, h].reshape(NS, GLA_DV)
        inter.append(_dot(expand(q_in[:, ks[p]].astype(F32), first), s_flat.astype(BF16)))
    for h in range(GLA_HEADS):
        p, first = h // 2, h % 2 == 0
        s_flat = s0_ref[:, h].reshape(NS, GLA_DV)
        lhs = jnp.concatenate([expand(k_st[:, ks[p]], first),
                               expand(dec_rows[:, ks[p]].astype(F32), first)], axis=0)
        v_h = vb[:, h * GLA_DV:(h + 1) * GLA_DV]
        rhs = jnp.concatenate([jnp.concatenate([v_h, zeros_blk], axis=1),
                               jnp.concatenate([zeros_blk, ones_blk], axis=1)], axis=0)
        ud = _dot_tn(lhs, rhs)
        s_new = ud[:, GLA_DV:] * s_flat + ud[:, :GLA_DV]
        snew_ref[:, h] = s_new.reshape(G, GLA_DK, GLA_DV)
    yield
    o_parts = []
    for p in range(2):
        o_intra = _dot(attn[p], _head_pair_values(vb[:, vs[p]], lane256))
        o_parts.append(o_intra + jnp.concatenate(inter[2 * p:2 * p + 2], axis=1))
    results.append((d_groups, jnp.concatenate(o_parts, axis=1)))


def _mixer_short_kernel(*refs, seq_len, n_prev):
    xm_ref, xs_ref, hist_ref, s0_ref = refs[:4]
    wts = refs[4:4 + N_MIX_W]
    x1m_ref, um_ref, sm_ref, x1s_ref, us_ref, ss_ref = refs[-6:]
    i = pl.program_id(0)

    @pl.when(i == 0)
    def _meta():
        _mixer_meta(xm_ref, wts, x1m_ref, um_ref, sm_ref)

    @pl.when(i > 0)
    def _sample():
        for a in range(n_prev):
            ss_ref[a] = refs[4 + N_MIX_W][a]
        _mixer_sample(xs_ref, hist_ref, s0_ref, wts, x1s_ref, us_ref, ss_ref.at[n_prev], seq_len=seq_len)


def _mixer_short(x_meta, x_samp, hist, s0, wts, layer, *, seq_len, s_prev=None):
    rows = x_samp.shape[0]
    n_groups = rows // (SEQ_GROUP * seq_len)
    per_step = GROUPS_PER_STEP if n_groups % GROUPS_PER_STEP == 0 else 1
    G = SEQ_GROUP * per_step
    R = G * seq_len
    n_tiles = rows // R
    assert rows % R == 0
    clamp = lambda i: jnp.maximum(i - 1, 0)
    n_prev = 0 if s_prev is None else s_prev.shape[0]
    state_blk = (G, GLA_HEADS, GLA_DK, GLA_DV)
    in_specs = ([pl.BlockSpec((N_META, D_MODEL), lambda i: (0, 0)),
                 pl.BlockSpec((R, D_MODEL), lambda i: (clamp(i), 0)),
                 pl.BlockSpec((None, G * POOL_BUF, D_POOL), lambda i: (layer, clamp(i), 0)),
                 pl.BlockSpec((None,) + state_blk, lambda i: (layer, clamp(i), 0, 0, 0))]
                + _mixer_weight_specs(layer))
    args = [x_meta, x_samp, hist, s0, *wts]
    if n_prev:
        in_specs.append(pl.BlockSpec((n_prev,) + state_blk, lambda i: (0, clamp(i), 0, 0, 0)))
        args.append(s_prev)
    kern = functools.partial(_mixer_short_kernel, seq_len=seq_len, n_prev=n_prev)

    return pl.pallas_call(
        kern,
        grid=(n_tiles + 1,),
        in_specs=in_specs,
        out_specs=[pl.BlockSpec((N_META, D_MODEL), lambda i: (0, 0)),
                   pl.BlockSpec((N_META, D_POOL), lambda i: (0, 0)),
                   pl.BlockSpec((GLA_HEADS, GLA_DK, GLA_DV), lambda i: (0, 0, 0)),
                   pl.BlockSpec((R, D_MODEL), lambda i: (clamp(i), 0)),
                   pl.BlockSpec((G * POOL_BUF, D_POOL), lambda i: (clamp(i), 0)),
                   pl.BlockSpec((n_prev + 1,) + state_blk, lambda i: (0, clamp(i), 0, 0, 0))],
        out_shape=[jax.ShapeDtypeStruct((N_META, D_MODEL), F32),
                   jax.ShapeDtypeStruct((N_META, D_POOL), F32),
                   jax.ShapeDtypeStruct((GLA_HEADS, GLA_DK, GLA_DV), F32),
                   jax.ShapeDtypeStruct((rows, D_MODEL), F32),
                   jax.ShapeDtypeStruct((n_tiles * G * POOL_BUF, D_POOL), F32),
                   jax.ShapeDtypeStruct((n_prev + 1,) + s0.shape[1:], F32)],
        compiler_params=pltpu.CompilerParams(dimension_semantics=("arbitrary",),
                                             vmem_limit_bytes=VMEM_LIMIT),
        name="mixer_short",
    )(*args)


def _ffn_tile(x_ref, wts, y_ref, conv_inputs, store_gate):
    w_up_ref, w_gate_ref, cw_ref, cb_ref, w_down_ref, g_ref, b_ref = wts
    x = x_ref[...]
    xb = x.astype(BF16)

    acts = []
    for j in range(D_FF // FF_CHUNK):
        cs = slice(j * FF_CHUNK, (j + 1) * FF_CHUNK)
        a = _dot(xb, w_up_ref[:, cs])
        gt = _dot(xb, w_gate_ref[:, cs])
        g1, g2 = conv_inputs(gt, cs)
        store_gate(gt, cs)
        gc = cb_ref[:, cs] + cw_ref[0:1, cs] * g2 + cw_ref[1:2, cs] * g1 + cw_ref[2:3, cs] * gt
        acts.append((a * _silu(gc)).astype(BF16))
    yield

    act = jnp.concatenate(acts, axis=1)
    rows = x.shape[0]
    rb = min(rows, DOWN_ROWS)
    for r0 in range(0, rows, rb):
        f = _dot(act[r0:r0 + rb], w_down_ref[...])
        y_ref[r0:r0 + rb, :] = _layer_norm(ALPHA * x[r0:r0 + rb] + f, g_ref[...], b_ref[...])


def _ffn_long_kernel(*refs, n_par):
    x_ref, cprev_ref = refs[:2]
    wts = refs[2:2 + N_FFN_W]
    y_ref, hist_out_ref, gbuf = refs[2 + N_FFN_W:]
    t = pl.program_id(1)
    T = x_ref.shape[1]

    @pl.when(t == 0)
    def _init():
        for j in range(n_par):
            gbuf[j, 0:CONV_PAD, :] = cprev_ref[...]

    def make_tile(j):
        def conv_inputs(gt, cs):
            gbuf[j, CONV_PAD:CONV_PAD + T, cs] = gt
            return gbuf[j, CONV_PAD - 1:CONV_PAD - 1 + T, cs], gbuf[j, CONV_PAD - 2:CONV_PAD - 2 + T, cs]

        def store_gate(gt, cs):
            gbuf[j, 0:CONV_PAD, cs] = gbuf[j, T:T + CONV_PAD, cs]

        return _ffn_tile(x_ref.at[j], wts, y_ref.at[j], conv_inputs, store_gate)

    for _ in itertools.zip_longest(*[make_tile(j) for j in range(n_par)]):
        pass

    @pl.when(t == pl.num_programs(1) - 1)
    def _final():
        for j in range(n_par):
            hist_out_ref[j] = gbuf[j, 0:CONV_PAD, :]


def _ffn_short_kernel(xm_ref, xs_ref, hist_ref, w_up_ref, w_gate_ref, cw_ref, cb_ref, w_down_ref, g_ref, b_ref,
                      ym_ref, cm_ref, ys_ref, cs_ref, w_up_bf_ref, w_gate_bf_ref, w_down_bf_ref,
                      fm_ref, fs_ref, *, seq_len):
    j = pl.program_id(0)
    G, Ls = SEQ_GROUP, seq_len
    R = G * Ls
    rows = xs_ref.shape[0]
    n_tiles = rows // R

    w_up = w_up_ref[...].astype(BF16)
    w_gate = w_gate_ref[...].astype(BF16)
    w_down = w_down_ref[...].astype(BF16)
    w_up_bf_ref[...] = w_up
    w_gate_bf_ref[...] = w_gate
    w_down_bf_ref[...] = w_down

    @pl.when(j == 0)
    def _first():
        fm_ref[...] = jnp.zeros_like(fm_ref)
        fs_ref[...] = jnp.zeros_like(fs_ref)

    n_parts = SHORT_FFN_PARTS if n_tiles % SHORT_FFN_PARTS == 0 else 1
    part_tiles = n_tiles // n_parts
    part_rows = part_tiles * R
    hist_rows = CONV_BUF * G
    tok = (lax.broadcasted_iota(jnp.int32, (part_rows, FF_CHUNK), 0) & (R - 1)) >> (G.bit_length() - 1)
    zeros_tail = jnp.zeros((R - hist_rows, FF_CHUNK), F32)

    def sample_conv(gt, part):
        first = part * part_tiles
        hx = jnp.concatenate([piece for n in range(first, first + part_tiles)
                              for piece in (hist_ref[n * hist_rows:(n + 1) * hist_rows, :], zeros_tail)], axis=0)
        g1 = jnp.where(tok >= 1, _roll_rows(gt, G), _roll_rows(hx, -G))
        g2 = jnp.where(tok >= 2, _roll_rows(gt, 2 * G), hx)
        return g1, g2

    def sample_store(gt, part):
        for n in range(part_tiles):
            dst = (part * part_tiles + n) * hist_rows
            cs_ref[dst:dst + hist_rows, :] = gt[n * R + (Ls - CONV_BUF) * G:(n + 1) * R]

    rowm = lax.broadcasted_iota(jnp.int32, (N_META, FF_CHUNK), 0)

    def meta_conv(gt):
        return (jnp.where(rowm >= 1, _roll_rows(gt, 1), 0.0), jnp.where(rowm >= 2, _roll_rows(gt, 2), 0.0))

    def meta_store(gt):
        cm_ref[...] = gt[N_META - CONV_PAD:N_META]

    parts = [(xs_ref.at[pl.ds(p * part_rows, part_rows)], fs_ref.at[pl.ds(p * part_rows, part_rows)],
              functools.partial(sample_conv, part=p), functools.partial(sample_store, part=p))
             for p in range(n_parts)]
    parts.append((xm_ref, fm_ref, meta_conv, meta_store))

    proj = []
    for x_ref, _, _, _ in parts:
        xb = x_ref[...].astype(BF16)
        proj.append((_dot(xb, w_up), _dot(xb, w_gate)))
    acts = []
    for (a, gt), (_, _, conv_inputs, store_gate) in zip(proj, parts):
        g1, g2 = conv_inputs(gt)
        store_gate(gt)
        gc = cb_ref[...] + cw_ref[0:1, :] * g2 + cw_ref[1:2, :] * g1 + cw_ref[2:3, :] * gt
        acts.append((a * _silu(gc)).astype(BF16))
    for act, (_, f_ref, _, _) in zip(acts, parts):
        f_ref[...] += _dot(act, w_down)

    @pl.when(j == pl.num_programs(0) - 1)
    def _last():
        ys_ref[...] = _layer_norm(ALPHA * xs_ref[...] + fs_ref[...], g_ref[...], b_ref[...])
        ym_ref[...] = _layer_norm(ALPHA * xm_ref[...] + fm_ref[...], g_ref[...], b_ref[...])


def _ffn_weight_specs(layer):
    whole = lambda shape: pl.BlockSpec(shape, lambda *_: (0,) * len(shape), pipeline_mode=pl.Buffered(1))
    specs = [whole((D_MODEL, D_FF)), whole((D_MODEL, D_FF)), _layer_spec((3, D_FF), layer), _layer_spec((1, D_FF), layer),
             whole((D_FF, D_MODEL)), _layer_spec((1, D_MODEL), layer), _layer_spec((1, D_MODEL), layer)]
    assert len(specs) == N_FFN_W
    return specs


def _ffn_long(x, cprev, wts, layer):
    B, L, _ = x.shape
    T = min(FFN_TILE, L)
    P = LONG_PAR if B % LONG_PAR == 0 else 1
    assert L % T == 0
    kern = functools.partial(_ffn_long_kernel, n_par=P)
    return pl.pallas_call(
        kern,
        grid=(B // P, L // T),
        in_specs=[pl.BlockSpec((P, T, D_MODEL), lambda b, t: (b, t, 0)),
                  pl.BlockSpec((CONV_PAD, D_FF), lambda b, t: (0, 0))] + _ffn_weight_specs(layer),
        out_specs=[pl.BlockSpec((P, T, D_MODEL), lambda b, t: (b, t, 0)),
                   pl.BlockSpec((P, CONV_PAD, D_FF), lambda b, t: (b, 0, 0))],
        out_shape=[jax.ShapeDtypeStruct((B, L, D_MODEL), F32),
                   jax.ShapeDtypeStruct((B, CONV_PAD, D_FF), F32)],
        scratch_shapes=[pltpu.VMEM((P, T + CONV_PAD, D_FF), F32)],
        compiler_params=pltpu.CompilerParams(dimension_semantics=("arbitrary", "arbitrary"),
                                             vmem_limit_bytes=VMEM_LIMIT),
        name="ffn_long",
    )(x, cprev, *wts)


def _ffn_short(x_meta, x_samp, hist, wts, layer, *, seq_len):
    rows = x_samp.shape[0]
    n_hist = hist.shape[1]
    C = FF_CHUNK
    const = lambda shape: pl.BlockSpec(shape, lambda j: (0,) * len(shape))
    kern = functools.partial(_ffn_short_kernel, seq_len=seq_len)
    return pl.pallas_call(
        kern,
        grid=(D_FF // C,),
        in_specs=[const((N_META, D_MODEL)),
                  const((rows, D_MODEL)),
                  pl.BlockSpec((None, n_hist, C), lambda j: (layer, 0, j)),
                  pl.BlockSpec((None, D_MODEL, C), lambda j: (layer, 0, j)),
                  pl.BlockSpec((None, D_MODEL, C), lambda j: (layer, 0, j)),
                  pl.BlockSpec((None, 3, C), lambda j: (layer, 0, j)),
                  pl.BlockSpec((None, 1, C), lambda j: (layer, 0, j)),
                  pl.BlockSpec((None, C, D_MODEL), lambda j: (layer, j, 0)),
                  pl.BlockSpec((None, 1, D_MODEL), lambda j: (layer, 0, 0)),
                  pl.BlockSpec((None, 1, D_MODEL), lambda j: (layer, 0, 0))],
        out_specs=[const((N_META, D_MODEL)),
                   pl.BlockSpec((CONV_PAD, C), lambda j: (0, j)),
                   const((rows, D_MODEL)),
                   pl.BlockSpec((n_hist, C), lambda j: (0, j)),
                   pl.BlockSpec((D_MODEL, C), lambda j: (0, j)),
                   pl.BlockSpec((D_MODEL, C), lambda j: (0, j)),
                   pl.BlockSpec((C, D_MODEL), lambda j: (j, 0))],
        out_shape=[jax.ShapeDtypeStruct((N_META, D_MODEL), F32),
                   jax.ShapeDtypeStruct((CONV_PAD, D_FF), F32),
                   jax.ShapeDtypeStruct((rows, D_MODEL), F32),
                   jax.ShapeDtypeStruct((n_hist, D_FF), F32),
                   jax.ShapeDtypeStruct((D_MODEL, D_FF), BF16),
                   jax.ShapeDtypeStruct((D_MODEL, D_FF), BF16),
                   jax.ShapeDtypeStruct((D_FF, D_MODEL), BF16)],
        scratch_shapes=[pltpu.VMEM((N_META, D_MODEL), F32),
                        pltpu.VMEM((rows, D_MODEL), F32)],
        compiler_params=pltpu.CompilerParams(dimension_semantics=("arbitrary",),
                                             vmem_limit_bytes=VMEM_LIMIT),
        name="ffn_short",
    )(x_meta, x_samp, hist, *wts)


def _to_group_major(a, axis):
    n, j = a.shape[axis], a.shape[axis + 1]
    lead, tail = a.shape[:axis], a.shape[axis + 2:]
    a = a.reshape(*lead, n // SEQ_GROUP, SEQ_GROUP, j, *tail)
    a = jnp.swapaxes(a, axis + 1, axis + 2)
    return a.reshape(*lead, n * j, *tail)


def _from_group_major(a, axis, j):
    rows = a.shape[axis]
    n = rows // j
    lead, tail = a.shape[:axis], a.shape[axis + 1:]
    a = a.reshape(*lead, n // SEQ_GROUP, j, SEQ_GROUP, *tail)
    a = jnp.swapaxes(a, axis + 1, axis + 2)
    return a.reshape(*lead, n, j, *tail)


def kernel(x_prompt, x_sample, state_pool, state_gla, state_conv, meta_tokens,
           w_in, w_a2, b_a, w_pool, pool_scale, gla_norm, w_out, ln1_g, ln1_b,
           w_up, w_gate, conv_w, conv_b, w_down, ln2_g, ln2_b):
    NB, LS = x_sample.shape[0], x_sample.shape[1]
    assert NB % SEQ_GROUP == 0 and LS & (LS - 1) == 0 and CONV_BUF <= LS <= POOL_BUF

    mix_w = (w_in.astype(BF16), w_a2.astype(BF16), b_a[:, None],
             w_pool.astype(BF16), pool_scale[:, None], gla_norm[:, None], w_out.astype(BF16),
             ln1_g[:, None], ln1_b[:, None])
    ffn_w = (w_up, w_gate, conv_w, conv_b[:, None], w_down, ln2_g[:, None], ln2_b[:, None])

    hm = meta_tokens.astype(F32)
    hp = x_prompt
    hs = _to_group_major(x_sample, 0)
    pool_hist = _to_group_major(state_pool, 1)
    conv_hist = _to_group_major(state_conv, 1)

    pp, gp, cp, ps_l, cs_l = [], [], [], [], []
    gs = None
    for l in range(DEPTH):
        hm1, um, sm, hs1, ps_new, gs = _mixer_short(hm, hs, pool_hist, state_gla, mix_w, l, seq_len=LS, s_prev=gs)
        hm, cm, hs, cs_new, w_up_bf, w_gate_bf, w_down_bf = _ffn_short(hm1, hs1, conv_hist, ffn_w, l, seq_len=LS)
        ps_l.append(ps_new)
        cs_l.append(cs_new)

        hp1, pbuf, snew = _mixer_long(hp, um, sm, mix_w, l)
        hp, cbuf = _ffn_long(hp1, cm, (w_up_bf, w_gate_bf) + ffn_w[2:4] + (w_down_bf,) + ffn_w[5:], l)
        pp.append(pbuf[:, 1:])
        gp.append(snew)
        cp.append(cbuf[:, CONV_PAD - CONV_BUF:])

    ps = _from_group_major(jnp.stack(ps_l), 1, POOL_BUF)
    cs = _from_group_major(jnp.stack(cs_l), 1, CONV_BUF)
    return (hp, _from_group_major(hs, 0, LS), jnp.stack(pp), jnp.stack(gp), jnp.stack(cp), ps, gs, cs)
```

```python
import functools
import itertools

import jax
import jax.numpy as jnp
from jax import lax
from jax.experimental import pallas as pl
from jax.experimental.pallas import tpu as pltpu

F32 = jnp.float32
BF16 = jnp.bfloat16

D_MODEL = 1024
N_META = 16
D_POOL = 512
POOL_WINDOWS = (2, 4, 8, 16)
POOL_GROUP = 128
POOL_BUF = 15
POOL_PAD = 16
D_GLA = 512
GLA_HEADS = 4
GLA_DV = 128
GLA_DK = 64
D_GLA_K = 256
GATE_RANK = 16
GATE_TAU = 16.0
D_FF = 2816
CONV_BUF = 2
CONV_PAD = 8
DEPTH = 2
ALPHA = (2 * DEPTH) ** 0.25
LN_EPS = 1e-5
RMS_EPS = 1e-6
PAST_LEN = 16384

C_POOL, C_Q, C_K, C_V, C_R, C_Z, C_END = 0, 512, 768, 1024, 1536, 2048, 2064

LONG_TILE = 512
LONG_CHUNK = 64
LONG_PAR = 2
MIXER_PAR = 2
MIXER_SKEW = 1
FFN_TILE = 512
SEQ_GROUP = 16
GROUPS_PER_STEP = 2
FF_CHUNK = 256
DOWN_ROWS = 256
SHORT_FFN_PARTS = 4
VMEM_LIMIT = 56 * 1024 * 1024

N_MIX_W = 9
N_FFN_W = 7


def _dot(a, b):
    return jnp.dot(a, b, preferred_element_type=F32)


def _dot_nt(a, b):
    return lax.dot_general(a, b, (((1,), (1,)), ((), ())), preferred_element_type=F32)


def _dot_tn(a, b):
    return lax.dot_general(a, b, (((0,), (0,)), ((), ())), preferred_element_type=F32)


def _layer_norm(y, g, b):
    mu = jnp.mean(y, axis=-1, keepdims=True)
    yc = y - mu
    var = jnp.mean(yc * yc, axis=-1, keepdims=True)
    return yc * lax.rsqrt(var + LN_EPS) * g + b


def _silu(x):
    h = 0.5 * x
    return h + h * jnp.tanh(h)


def _log_sigmoid(z):
    return jnp.minimum(z, 0.0) - jnp.log(1.0 + jnp.exp(-jnp.abs(z)))


def _roll_rows(x, shift):
    n = x.shape[0]
    return pltpu.roll(x, shift % n, 0)


def _split_bf16(x):
    hi = x.astype(BF16)
    lo = (x - hi.astype(F32)).astype(BF16)
    return hi, lo


def _project_in(xb, w_in_ref):
    u = _dot(xb, w_in_ref[:, C_POOL:C_Q])
    zr = _dot(xb, w_in_ref[:, C_Z:C_END])
    q = _dot(xb, w_in_ref[:, C_Q:C_K]) * (GLA_DK ** -0.5)
    k = _dot(xb, w_in_ref[:, C_K:C_V])
    v = _dot(xb, w_in_ref[:, C_V:C_R])
    r = _dot(xb, w_in_ref[:, C_R:C_Z])
    return u, zr, q, k, v, r


def _gate_log_decay(zr, w_a2_ref, b_a_ref):
    z = _dot(zr.astype(BF16), w_a2_ref[...]) + b_a_ref[...]
    return _log_sigmoid(z) * (1.0 / GATE_TAU)


def _gla_output_gate(o, r, gnorm):
    parts = []
    for h in range(GLA_HEADS):
        oh = o[:, h * GLA_DV:(h + 1) * GLA_DV]
        ms = jnp.mean(oh * oh, axis=-1, keepdims=True)
        parts.append(oh * lax.rsqrt(ms + RMS_EPS) * gnorm)
    return jnp.concatenate(parts, axis=1) * _silu(r)


def _pool_project(d_groups, w_pool_ref, pscale_ref):
    ys = [_dot(d.astype(BF16), w_pool_ref[g]) for g, d in enumerate(d_groups)]
    return jnp.concatenate(ys, axis=1) * pscale_ref[...]


def _mix_out(x, y_pool, y_gla, w_out_ref, g_ref, b_ref):
    mix = jnp.concatenate([y_pool, y_gla], axis=1).astype(BF16)
    return _layer_norm(ALPHA * x + _dot(mix, w_out_ref[...]), g_ref[...], b_ref[...])


def _emit_staggered(stage_gens, first_round):
    live = list(zip(first_round, stage_gens))
    rnd = 0
    while live:
        for start, g in list(live):
            if rnd >= start and next(g, StopIteration) is StopIteration:
                live.remove((start, g))
        rnd += 1


def _head_pair_keys(k_p, lane128):
    zk = jnp.zeros_like(k_p)
    return jnp.concatenate([jnp.where(lane128 < GLA_DK, k_p, zk), jnp.where(lane128 >= GLA_DK, k_p, zk)], axis=0)


def _head_pair_values(v_p, lane256):
    zv = jnp.zeros_like(v_p)
    return jnp.concatenate([jnp.where(lane256 < GLA_DV, v_p, zv), jnp.where(lane256 >= GLA_DV, v_p, zv)], axis=0)


def _mixer_long_kernel(*refs, n_par, tile, chunk):
    x_ref, pprev_ref, s0_ref = refs[:3]
    wts = refs[3:3 + N_MIX_W]
    x1_ref, pbuf_ref, snew_ref, ubuf, sbd = refs[3 + N_MIX_W:]
    t = pl.program_id(1)
    zero_blk = jnp.zeros((GLA_DK, GLA_DV), F32)

    @pl.when(t == 0)
    def _init():
        for j in range(n_par):
            ubuf[j, 0:POOL_PAD, :] = pprev_ref[...]
            for p in range(2):
                top = jnp.concatenate([s0_ref[2 * p], zero_blk], axis=1)
                bot = jnp.concatenate([zero_blk, s0_ref[2 * p + 1]], axis=1)
                sbd[j, p] = jnp.concatenate([top, bot], axis=0)

    tiles = [_mixer_long_tile(x_ref.at[j], wts, x1_ref.at[j], ubuf.at[j], sbd.at[j], tile=tile, chunk=chunk)
             for j in range(n_par)]
    _emit_staggered(tiles, [MIXER_SKEW * j for j in range(n_par)])

    @pl.when(t == pl.num_programs(1) - 1)
    def _final():
        for j in range(n_par):
            pbuf_ref[j] = ubuf[j, 0:POOL_PAD, :]
            for p in range(2):
                s_p = sbd[j, p]
                snew_ref[j, 2 * p] = s_p[0:GLA_DK, 0:GLA_DV]
                snew_ref[j, 2 * p + 1] = s_p[GLA_DK:2 * GLA_DK, GLA_DV:2 * GLA_DV]


def _mixer_long_tile(x_ref, wts, x1_ref, ubuf, sbd, *, tile, chunk):
    w_in_ref, w_a2_ref, b_a_ref, w_pool_ref, pscale_ref, gnorm_ref, w_out_ref, g_ref, b_ref = wts
    T, C = tile, chunk
    x = x_ref[...]
    xb = x.astype(BF16)

    u, zr, q, k, v, r = _project_in(xb, w_in_ref)
    yield

    ubuf[POOL_PAD:POOL_PAD + T, :] = u
    d_groups = []
    for g, w in enumerate(POOL_WINDOWS):
        s = ubuf[:, g * POOL_GROUP:(g + 1) * POOL_GROUP]
        sh = 1
        while sh < w:
            s = s + _roll_rows(s, sh)
            sh *= 2
        d_groups.append(s[POOL_PAD:, :] * (1.0 / w) - u[:, g * POOL_GROUP:(g + 1) * POOL_GROUP])
    y_pool = _pool_project(d_groups, w_pool_ref, pscale_ref)
    ubuf[0:POOL_PAD, :] = ubuf[T:T + POOL_PAD, :]
    loga = _gate_log_decay(zr, w_a2_ref, b_a_ref)

    tr = lax.broadcasted_iota(jnp.int32, (C, C), 0)
    tc = lax.broadcasted_iota(jnp.int32, (C, C), 1)
    tri = jnp.where(tc <= tr, 1.0, 0.0).astype(BF16)
    ar = lax.broadcasted_iota(jnp.int32, (C, 2 * C), 0)
    ac = lax.broadcasted_iota(jnp.int32, (C, 2 * C), 1) & (C - 1)
    causal = ac <= ar
    lane128 = lax.broadcasted_iota(jnp.int32, (C, 128), 1)
    lane256 = lax.broadcasted_iota(jnp.int32, (C, 256), 1)
    sr = lax.broadcasted_iota(jnp.int32, (128, 256), 0)
    sc = lax.broadcasted_iota(jnp.int32, (128, 256), 1)
    blockdiag = (sr >= GLA_DK) == (sc >= GLA_DV)
    mid = C // 2 - 1

    n_chunks = T // C
    pairs = [(c, p) for c in range(n_chunks) for p in range(2)]
    ks = [slice(128 * p, 128 * (p + 1)) for p in range(2)]
    vs = [slice(256 * p, 256 * (p + 1)) for p in range(2)]

    bcs = []
    for c in range(n_chunks):
        la_hi, la_lo = _split_bf16(loga[c * C:(c + 1) * C])
        bb = _dot(tri, jnp.concatenate([la_hi, la_lo], axis=1))
        bcs.append(bb[:, :D_GLA_K] + bb[:, D_GLA_K:])
    yield

    q_in, k_in, q_st, k_st, dec_t, vc = [], [], [], [], [], []
    for c in range(n_chunks):
        bc = bcs[c]
        bmid = bc[mid:mid + 1]
        bend = bc[C - 1:C]
        qc = q[c * C:(c + 1) * C]
        kc = k[c * C:(c + 1) * C]
        q_in.append((qc * jnp.exp(bc - bmid)).astype(BF16))
        k_in.append((kc * jnp.exp(bmid - bc)).astype(BF16))
        q_st.append((qc * jnp.exp(bc)).astype(BF16))
        k_st.append((kc * jnp.exp(bend - bc)).astype(BF16))
        dec_t.append(jnp.transpose(jnp.broadcast_to(jnp.exp(bend), (128, D_GLA_K))))
        vc.append(v[c * C:(c + 1) * C].astype(BF16))
    yield

    attn, upd = {}, {}
    for c, p in pairs:
        a = _dot_nt(q_in[c][:, ks[p]], _head_pair_keys(k_in[c][:, ks[p]], lane128))
        attn[c, p] = jnp.where(causal, a, 0.0).astype(BF16)
    for c, p in pairs:
        u_cp = _dot_tn(k_st[c][:, ks[p]], vc[c][:, vs[p]])
        upd[c, p] = jnp.where(blockdiag, u_cp, 0.0)
    yield

    s_vals = [sbd[p] for p in range(2)]
    s_start = {}
    for c, p in pairs:
        s_start[c, p] = s_vals[p].astype(BF16)
        dec_p = dec_t[c][ks[p], :]
        s_vals[p] = jnp.concatenate([dec_p, dec_p], axis=1) * s_vals[p] + upd[c, p]
    for p in range(2):
        sbd[p] = s_vals[p]

    o_rows = [[], []]
    for c, p in pairs:
        vblk = _head_pair_values(vc[c][:, vs[p]], lane256)
        lhs = jnp.concatenate([attn[c, p], q_st[c][:, ks[p]]], axis=1)
        rhs = jnp.concatenate([vblk, s_start[c, p]], axis=0)
        o_rows[p].append(_dot(lhs, rhs))
    o = jnp.concatenate([jnp.concatenate(o_rows[p], axis=0) for p in range(2)], axis=1)
    yield

    y_gla = _gla_output_gate(o, r, gnorm_ref[...])
    x1_ref[...] = _mix_out(x, y_pool, y_gla, w_out_ref, g_ref, b_ref)


def _layer_spec(shape, layer):
    nd = len(shape)
    return pl.BlockSpec((None,) + shape, lambda *_: (layer,) + (0,) * nd, pipeline_mode=pl.Buffered(1))


def _mixer_weight_specs(layer):
    shapes = [(D_MODEL, C_END), (GATE_RANK, D_GLA_K), (1, D_GLA_K),
              (4, POOL_GROUP, POOL_GROUP), (1, D_POOL), (1, GLA_DV), (D_MODEL, D_MODEL), (1, D_MODEL), (1, D_MODEL)]
    assert len(shapes) == N_MIX_W
    return [_layer_spec(s, layer) for s in shapes]


def _mixer_long(x, pprev, s0, wts, layer):
    B, L, _ = x.shape
    T = min(LONG_TILE, L)
    P = MIXER_PAR if B % MIXER_PAR == 0 else 1
    assert L % T == 0 and T % LONG_CHUNK == 0
    kern = functools.partial(_mixer_long_kernel, n_par=P, tile=T, chunk=LONG_CHUNK)
    return pl.pallas_call(
        kern,
        grid=(B // P, L // T),
        in_specs=[pl.BlockSpec((P, T, D_MODEL), lambda b, t: (b, t, 0)),
                  pl.BlockSpec((POOL_PAD, D_POOL), lambda b, t: (0, 0)),
                  pl.BlockSpec((GLA_HEADS, GLA_DK, GLA_DV), lambda b, t: (0, 0, 0))]
        + _mixer_weight_specs(layer),
        out_specs=[pl.BlockSpec((P, T, D_MODEL), lambda b, t: (b, t, 0)),
                   pl.BlockSpec((P, POOL_PAD, D_POOL), lambda b, t: (b, 0, 0)),
                   pl.BlockSpec((P, GLA_HEADS, GLA_DK, GLA_DV), lambda b, t: (b, 0, 0, 0))],
        out_shape=[jax.ShapeDtypeStruct((B, L, D_MODEL), F32),
                   jax.ShapeDtypeStruct((B, POOL_PAD, D_POOL), F32),
                   jax.ShapeDtypeStruct((B, GLA_HEADS, GLA_DK, GLA_DV), F32)],
        scratch_shapes=[pltpu.VMEM((P, T + POOL_PAD, D_POOL), F32),
                        pltpu.VMEM((P, 2, 2 * GLA_DK, 2 * GLA_DV), F32)],
        compiler_params=pltpu.CompilerParams(dimension_semantics=("arbitrary", "arbitrary"),
                                             vmem_limit_bytes=VMEM_LIMIT),
        name="mixer_long",
    )(x, pprev, s0, *wts)


def _mixer_meta(x_ref, wts, x1_ref, u_ref, s_ref):
    w_in_ref, w_a2_ref, b_a_ref, w_pool_ref, pscale_ref, gnorm_ref, w_out_ref, g_ref, b_ref = wts
    L = N_META
    x = x_ref[...]
    xb = x.astype(BF16)
    u, zr, q, k, v, r = _project_in(xb, w_in_ref)
    u_ref[...] = u

    row128 = lax.broadcasted_iota(jnp.int32, (L, POOL_GROUP), 0)
    pos1 = lax.broadcasted_iota(jnp.int32, (L, 1), 0)
    d_groups = []
    for g, w in enumerate(POOL_WINDOWS):
        ug = u[:, g * POOL_GROUP:(g + 1) * POOL_GROUP]
        s = ug
        sh = 1
        while sh < w:
            s = s + jnp.where(row128 >= sh, _roll_rows(s, sh), 0.0)
            sh *= 2
        d_groups.append(s / jnp.minimum(w, pos1 + 1).astype(F32) - ug)
    y_pool = _pool_project(d_groups, w_pool_ref, pscale_ref)

    loga = _gate_log_decay(zr, w_a2_ref, b_a_ref)
    row256 = lax.broadcasted_iota(jnp.int32, (L, D_GLA_K), 0)
    b = loga
    sh = 1
    while sh < L:
        b = b + jnp.where(row256 >= sh, _roll_rows(b, sh), 0.0)
        sh *= 2
    bend = b[L - 1:L]
    q_in = (q * jnp.exp(b)).astype(BF16)
    k_in = (k * jnp.exp(-b)).astype(BF16)
    k_st = (k * jnp.exp(bend - b)).astype(BF16)
    vb = v.astype(BF16)
    ar = lax.broadcasted_iota(jnp.int32, (L, 2 * L), 0)
    ac = lax.broadcasted_iota(jnp.int32, (L, 2 * L), 1) & (L - 1)
    causal = ac <= ar
    lane128 = lax.broadcasted_iota(jnp.int32, (L, 128), 1)
    lane256 = lax.broadcasted_iota(jnp.int32, (L, 256), 1)
    attn = []
    for p in range(2):
        ks = slice(128 * p, 128 * (p + 1))
        a = _dot_nt(q_in[:, ks], _head_pair_keys(k_in[:, ks], lane128))
        attn.append(jnp.where(causal, a, 0.0).astype(BF16))
    o_parts = []
    for p in range(2):
        ks = slice(128 * p, 128 * (p + 1))
        vs = slice(256 * p, 256 * (p + 1))
        s_p = _dot_tn(k_st[:, ks], vb[:, vs])
        s_ref[2 * p] = s_p[0:GLA_DK, 0:GLA_DV]
        s_ref[2 * p + 1] = s_p[GLA_DK:2 * GLA_DK, GLA_DV:2 * GLA_DV]
        o_parts.append(_dot(attn[p], _head_pair_values(vb[:, vs], lane256)))
    o = jnp.concatenate(o_parts, axis=1)

    y_gla = _gla_output_gate(o, r, gnorm_ref[...])
    x1_ref[...] = _mix_out(x, y_pool, y_gla, w_out_ref, g_ref, b_ref)


def _mixer_sample(x_ref, hist_ref, s0_ref, wts, x1_ref, hist_out_ref, snew_ref, *, seq_len):
    w_in_ref, w_a2_ref, b_a_ref, w_pool_ref, pscale_ref, gnorm_ref, w_out_ref, g_ref, b_ref = wts
    G = SEQ_GROUP
    R = G * seq_len
    n_groups = x_ref.shape[0] // R
    x = x_ref[...]
    xb = x.astype(BF16)
    u, zr, q, k, v, r = _project_in(xb, w_in_ref)
    loga = _gate_log_decay(zr, w_a2_ref, b_a_ref)
    results = []
    gens = []
    for gi in range(n_groups):
        rs = slice(gi * R, (gi + 1) * R)
        hist_rows = pl.ds(gi * G * POOL_BUF, G * POOL_BUF)
        seqs = pl.ds(gi * G, G)
        gens.append(_sample_group(u[rs], q[rs], k[rs], v[rs], loga[rs], hist_ref.at[hist_rows], s0_ref.at[seqs],
                                  hist_out_ref.at[hist_rows], snew_ref.at[seqs], results, seq_len=seq_len))
    for _ in itertools.zip_longest(*gens):
        pass
    d_groups = [jnp.concatenate([res[0][g] for res in results], axis=0) for g in range(len(POOL_WINDOWS))]
    o = jnp.concatenate([res[1] for res in results], axis=0)
    y_pool = _pool_project(d_groups, w_pool_ref, pscale_ref)
    y_gla = _gla_output_gate(o, r, gnorm_ref[...])
    x1_ref[...] = _mix_out(x, y_pool, y_gla, w_out_ref, g_ref, b_ref)


def _sample_group(u, q, k, v, loga, hist_ref, s0_ref, hist_out_ref, snew_ref, results, *, seq_len):
    G, Ls = SEQ_GROUP, seq_len
    R = G * Ls
    NS = G * GLA_DK
    g_shift = G.bit_length() - 1
    hist_out_ref[0:(POOL_BUF - Ls) * G, :] = hist_ref[R:POOL_BUF * G, :]
    hist_out_ref[(POOL_BUF - Ls) * G:POOL_BUF * G, :] = u

    def blk(a, t):
        return a[t * G:(t + 1) * G]

    d_groups = []
    for g, w in enumerate(POOL_WINDOWS):
        cols = slice(g * POOL_GROUP, (g + 1) * POOL_GROUP)
        ug = u[:, cols]
        suffix = [None]
        acc = None
        for m in range(1, min(w - 1, POOL_BUF) + 1):
            h = hist_ref[(POOL_BUF - m) * G:(POOL_BUF - m + 1) * G, cols]
            acc = h if acc is None else acc + h
            suffix.append(acc)
        parts = []
        for t in range(Ls):
            wsum = blk(ug, t)
            for j in range(max(0, t - w + 1), t):
                wsum = wsum + blk(ug, j)
            m = w - 1 - t
            if m > 0:
                wsum = wsum + suffix[m]
            parts.append(wsum * (1.0 / w) - blk(ug, t))
        d_groups.append(jnp.concatenate(parts, axis=0))

    b_t = [blk(loga, 0)]
    for t in range(1, Ls):
        b_t.append(b_t[-1] + blk(loga, t))
    b = jnp.concatenate(b_t, axis=0)
    bend = jnp.concatenate([b_t[-1]] * Ls, axis=0)
    q_in = (q * jnp.exp(b)).astype(BF16)
    k_in = (k * jnp.exp(-b)).astype(BF16)
    k_st = k * jnp.exp(bend - b)
    dec_hi, dec_lo = _split_bf16(jnp.exp(bend))
    tok = lax.broadcasted_iota(jnp.int32, (R, D_GLA_K), 0) >> g_shift
    dec_rows = jnp.where(tok == Ls - 1, dec_hi, jnp.where(tok == Ls - 2, dec_lo, jnp.zeros_like(dec_lo)))
    vb = v.astype(BF16)

    ar = lax.broadcasted_iota(jnp.int32, (R, 2 * R), 0)
    ac = lax.broadcasted_iota(jnp.int32, (R, 2 * R), 1) & (R - 1)
    same_seq_causal = ((ac & (G - 1)) == (ar & (G - 1))) & ((ac >> g_shift) <= (ar >> g_shift))
    lane128 = lax.broadcasted_iota(jnp.int32, (R, 128), 1)
    lane256 = lax.broadcasted_iota(jnp.int32, (R, 256), 1)
    br = lax.broadcasted_iota(jnp.int32, (R, NS), 0)
    bcol = lax.broadcasted_iota(jnp.int32, (R, NS), 1)
    own_state = (bcol >> 6) == (br & (G - 1))
    ones_blk = jnp.ones((R, GLA_DV), BF16)
    zeros_blk = jnp.zeros((R, GLA_DV), BF16)

    def expand(xp, first):
        sw = pltpu.roll(xp, GLA_DK, 1)
        two = jnp.where(lane128 < GLA_DK, xp, sw) if first else jnp.where(lane128 < GLA_DK, sw, xp)
        rep = jnp.concatenate([two] * (NS // 128), axis=1)
        return jnp.where(own_state, rep, 0.0).astype(BF16)

    ks = [slice(128 * p, 128 * (p + 1)) for p in range(2)]
    vs = [slice(256 * p, 256 * (p + 1)) for p in range(2)]
    attn = []
    for p in range(2):
        a = _dot_nt(q_in[:, ks[p]], _head_pair_keys(k_in[:, ks[p]], lane128))
        attn.append(jnp.where(same_seq_causal, a, 0.0).astype(BF16))
    yield
    inter = []
    for h in range(GLA_HEADS):
        p, first = h // 2, h % 2 == 0
        s_flat = s0_ref[:, h].reshape(NS, GLA_DV)
        inter.append(_dot(expand(q_in[:, ks[p]].astype(F32), first), s_flat.astype(BF16)))
    for h in range(GLA_HEADS):
        p, first = h // 2, h % 2 == 0
        s_flat = s0_ref[:, h].reshape(NS, GLA_DV)
        lhs = jnp.concatenate([expand(k_st[:, ks[p]], first),
                               expand(dec_rows[:, ks[p]].astype(F32), first)], axis=0)
        v_h = vb[:, h * GLA_DV:(h + 1) * GLA_DV]
        rhs = jnp.concatenate([jnp.concatenate([v_h, zeros_blk], axis=1),
                               jnp.concatenate([zeros_blk, ones_blk], axis=1)], axis=0)
        ud = _dot_tn(lhs, rhs)
        s_new = ud[:, GLA_DV:] * s_flat + ud[:, :GLA_DV]
        snew_ref[:, h] = s_new.reshape(G, GLA_DK, GLA_DV)
    yield
    o_parts = []
    for p in range(2):
        o_intra = _dot(attn[p], _head_pair_values(vb[:, vs[p]], lane256))
        o_parts.append(o_intra + jnp.concatenate(inter[2 * p:2 * p + 2], axis=1))
    results.append((d_groups, jnp.concatenate(o_parts, axis=1)))


def _mixer_short_kernel(*refs, seq_len, n_prev):
    xm_ref, xs_ref, hist_ref, s0_ref = refs[:4]
    wts = refs[4:4 + N_MIX_W]
    x1m_ref, um_ref, sm_ref, x1s_ref, us_ref, ss_ref = refs[-6:]
    i = pl.program_id(0)

    @pl.when(i == 0)
    def _meta():
        _mixer_meta(xm_ref, wts, x1m_ref, um_ref, sm_ref)

    @pl.when(i > 0)
    def _sample():
        for a in range(n_prev):
            ss_ref[a] = refs[4 + N_MIX_W][a]
        _mixer_sample(xs_ref, hist_ref, s0_ref, wts, x1s_ref, us_ref, ss_ref.at[n_prev], seq_len=seq_len)


def _mixer_short(x_meta, x_samp, hist, s0, wts, layer, *, seq_len, s_prev=None):
    rows = x_samp.shape[0]
    n_groups = rows // (SEQ_GROUP * seq_len)
    per_step = GROUPS_PER_STEP if n_groups % GROUPS_PER_STEP == 0 else 1
    G = SEQ_GROUP * per_step
    R = G * seq_len
    n_tiles = rows // R
    assert rows % R == 0
    clamp = lambda i: jnp.maximum(i - 1, 0)
    n_prev = 0 if s_prev is None else s_prev.shape[0]
    state_blk = (G, GLA_HEADS, GLA_DK, GLA_DV)
    in_specs = ([pl.BlockSpec((N_META, D_MODEL), lambda i: (0, 0)),
                 pl.BlockSpec((R, D_MODEL), lambda i: (clamp(i), 0)),
                 pl.BlockSpec((None, G * POOL_BUF, D_POOL), lambda i: (layer, clamp(i), 0)),
                 pl.BlockSpec((None,) + state_blk, lambda i: (layer, clamp(i), 0, 0, 0))]
                + _mixer_weight_specs(layer))
    args = [x_meta, x_samp, hist, s0, *wts]
    if n_prev:
        in_specs.append(pl.BlockSpec((n_prev,) + state_blk, lambda i: (0, clamp(i), 0, 0, 0)))
        args.append(s_prev)
    kern = functools.partial(_mixer_short_kernel, seq_len=seq_len, n_prev=n_prev)

    return pl.pallas_call(
        kern,
        grid=(n_tiles + 1,),
        in_specs=in_specs,
        out_specs=[pl.BlockSpec((N_META, D_MODEL), lambda i: (0, 0)),
                   pl.BlockSpec((N_META, D_POOL), lambda i: (0, 0)),
                   pl.BlockSpec((GLA_HEADS, GLA_DK, GLA_DV), lambda i: (0, 0, 0)),
                   pl.BlockSpec((R, D_MODEL), lambda i: (clamp(i), 0)),
                   pl.BlockSpec((G * POOL_BUF, D_POOL), lambda i: (clamp(i), 0)),
                   pl.BlockSpec((n_prev + 1,) + state_blk, lambda i: (0, clamp(i), 0, 0, 0))],
        out_shape=[jax.ShapeDtypeStruct((N_META, D_MODEL), F32),
                   jax.ShapeDtypeStruct((N_META, D_POOL), F32),
                   jax.ShapeDtypeStruct((GLA_HEADS, GLA_DK, GLA_DV), F32),
                   jax.ShapeDtypeStruct((rows, D_MODEL), F32),
                   jax.ShapeDtypeStruct((n_tiles * G * POOL_BUF, D_POOL), F32),
                   jax.ShapeDtypeStruct((n_prev + 1,) + s0.shape[1:], F32)],
        compiler_params=pltpu.CompilerParams(dimension_semantics=("arbitrary",),
                                             vmem_limit_bytes=VMEM_LIMIT),
        name="mixer_short",
    )(*args)


def _ffn_tile(x_ref, wts, y_ref, conv_inputs, store_gate):
    w_up_ref, w_gate_ref, cw_ref, cb_ref, w_down_ref, g_ref, b_ref = wts
    x = x_ref[...]
    xb = x.astype(BF16)

    acts = []
    for j in range(D_FF // FF_CHUNK):
        cs = slice(j * FF_CHUNK, (j + 1) * FF_CHUNK)
        a = _dot(xb, w_up_ref[j])
        gt = _dot(xb, w_gate_ref[j])
        g1, g2 = conv_inputs(gt, cs)
        store_gate(gt, cs)
        gc = cb_ref[:, cs] + cw_ref[0:1, cs] * g2 + cw_ref[1:2, cs] * g1 + cw_ref[2:3, cs] * gt
        acts.append((a * _silu(gc)).astype(BF16))
    yield

    act = jnp.concatenate(acts, axis=1)
    rows = x.shape[0]
    rb = min(rows, DOWN_ROWS)
    for r0 in range(0, rows, rb):
        f = _dot(act[r0:r0 + rb], w_down_ref[...])
        y_ref[r0:r0 + rb, :] = _layer_norm(ALPHA * x[r0:r0 + rb] + f, g_ref[...], b_ref[...])


def _ffn_long_kernel(*refs, n_par):
    x_ref, cprev_ref = refs[:2]
    wts = refs[2:2 + N_FFN_W]
    y_ref, hist_out_ref, gbuf = refs[2 + N_FFN_W:]
    t = pl.program_id(1)
    T = x_ref.shape[1]

    @pl.when(t == 0)
    def _init():
        for j in range(n_par):
            gbuf[j, 0:CONV_PAD, :] = cprev_ref[...]

    def make_tile(j):
        def conv_inputs(gt, cs):
            gbuf[j, CONV_PAD:CONV_PAD + T, cs] = gt
            return gbuf[j, CONV_PAD - 1:CONV_PAD - 1 + T, cs], gbuf[j, CONV_PAD - 2:CONV_PAD - 2 + T, cs]

        def store_gate(gt, cs):
            gbuf[j, 0:CONV_PAD, cs] = gbuf[j, T:T + CONV_PAD, cs]

        return _ffn_tile(x_ref.at[j], wts, y_ref.at[j], conv_inputs, store_gate)

    for _ in itertools.zip_longest(*[make_tile(j) for j in range(n_par)]):
        pass

    @pl.when(t == pl.num_programs(1) - 1)
    def _final():
        for j in range(n_par):
            hist_out_ref[j] = gbuf[j, 0:CONV_PAD, :]


def _ffn_short_kernel(xm_ref, xs_ref, hist_ref, w_up_ref, w_gate_ref, cw_ref, cb_ref, w_down_ref, g_ref, b_ref,
                      ym_ref, cm_ref, ys_ref, cs_ref, w_up_bf_ref, w_gate_bf_ref, w_down_bf_ref,
                      fm_ref, fs_ref, *, seq_len):
    j = pl.program_id(0)
    G, Ls = SEQ_GROUP, seq_len
    R = G * Ls
    rows = xs_ref.shape[0]
    n_tiles = rows // R

    w_up = w_up_ref[...].astype(BF16)
    w_gate = w_gate_ref[...].astype(BF16)
    w_down = w_down_ref[...].astype(BF16)
    w_up_bf_ref[...] = w_up
    w_gate_bf_ref[...] = w_gate
    w_down_bf_ref[...] = w_down

    @pl.when(j == 0)
    def _first():
        fm_ref[...] = jnp.zeros_like(fm_ref)
        fs_ref[...] = jnp.zeros_like(fs_ref)

    n_parts = SHORT_FFN_PARTS if n_tiles % SHORT_FFN_PARTS == 0 else 1
    part_tiles = n_tiles // n_parts
    part_rows = part_tiles * R
    hist_rows = CONV_BUF * G
    tok = (lax.broadcasted_iota(jnp.int32, (part_rows, FF_CHUNK), 0) & (R - 1)) >> (G.bit_length() - 1)
    zeros_tail = jnp.zeros((R - hist_rows, FF_CHUNK), F32)

    def sample_conv(gt, part):
        first = part * part_tiles
        hx = jnp.concatenate([piece for n in range(first, first + part_tiles)
                              for piece in (hist_ref[n * hist_rows:(n + 1) * hist_rows, :], zeros_tail)], axis=0)
        g1 = jnp.where(tok >= 1, _roll_rows(gt, G), _roll_rows(hx, -G))
        g2 = jnp.where(tok >= 2, _roll_rows(gt, 2 * G), hx)
        return g1, g2

    def sample_store(gt, part):
        for n in range(part_tiles):
            dst = (part * part_tiles + n) * hist_rows
            cs_ref[dst:dst + hist_rows, :] = gt[n * R + (Ls - CONV_BUF) * G:(n + 1) * R]

    rowm = lax.broadcasted_iota(jnp.int32, (N_META, FF_CHUNK), 0)

    def meta_conv(gt):
        return (jnp.where(rowm >= 1, _roll_rows(gt, 1), 0.0), jnp.where(rowm >= 2, _roll_rows(gt, 2), 0.0))

    def meta_store(gt):
        cm_ref[...] = gt[N_META - CONV_PAD:N_META]

    parts = [(xs_ref.at[pl.ds(p * part_rows, part_rows)], fs_ref.at[pl.ds(p * part_rows, part_rows)],
              functools.partial(sample_conv, part=p), functools.partial(sample_store, part=p))
             for p in range(n_parts)]
    parts.append((xm_ref, fm_ref, meta_conv, meta_store))

    proj = []
    for x_ref, _, _, _ in parts:
        xb = x_ref[...].astype(BF16)
        proj.append((_dot(xb, w_up), _dot(xb, w_gate)))
    acts = []
    for (a, gt), (_, _, conv_inputs, store_gate) in zip(proj, parts):
        g1, g2 = conv_inputs(gt)
        store_gate(gt)
        gc = cb_ref[...] + cw_ref[0:1, :] * g2 + cw_ref[1:2, :] * g1 + cw_ref[2:3, :] * gt
        acts.append((a * _silu(gc)).astype(BF16))
    for act, (_, f_ref, _, _) in zip(acts, parts):
        f_ref[...] += _dot(act, w_down)

    @pl.when(j == pl.num_programs(0) - 1)
    def _last():
        ys_ref[...] = _layer_norm(ALPHA * xs_ref[...] + fs_ref[...], g_ref[...], b_ref[...])
        ym_ref[...] = _layer_norm(ALPHA * xm_ref[...] + fm_ref[...], g_ref[...], b_ref[...])


def _ffn_weight_specs(layer):
    whole = lambda shape: pl.BlockSpec(shape, lambda *_: (0,) * len(shape), pipeline_mode=pl.Buffered(1))
    chunked = (D_FF // FF_CHUNK, D_MODEL, FF_CHUNK)
    specs = [whole(chunked), whole(chunked), _layer_spec((3, D_FF), layer), _layer_spec((1, D_FF), layer),
             whole((D_FF, D_MODEL)), _layer_spec((1, D_MODEL), layer), _layer_spec((1, D_MODEL), layer)]
    assert len(specs) == N_FFN_W
    return specs


def _ffn_long(x, cprev, wts, layer):
    B, L, _ = x.shape
    T = min(FFN_TILE, L)
    P = LONG_PAR if B % LONG_PAR == 0 else 1
    assert L % T == 0
    kern = functools.partial(_ffn_long_kernel, n_par=P)
    return pl.pallas_call(
        kern,
        grid=(B // P, L // T),
        in_specs=[pl.BlockSpec((P, T, D_MODEL), lambda b, t: (b, t, 0)),
                  pl.BlockSpec((CONV_PAD, D_FF), lambda b, t: (0, 0))] + _ffn_weight_specs(layer),
        out_specs=[pl.BlockSpec((P, T, D_MODEL), lambda b, t: (b, t, 0)),
                   pl.BlockSpec((P, CONV_PAD, D_FF), lambda b, t: (b, 0, 0))],
        out_shape=[jax.ShapeDtypeStruct((B, L, D_MODEL), F32),
                   jax.ShapeDtypeStruct((B, CONV_PAD, D_FF), F32)],
        scratch_shapes=[pltpu.VMEM((P, T + CONV_PAD, D_FF), F32)],
        compiler_params=pltpu.CompilerParams(dimension_semantics=("arbitrary", "arbitrary"),
                                             vmem_limit_bytes=VMEM_LIMIT),
        name="ffn_long",
    )(x, cprev, *wts)


def _ffn_short(x_meta, x_samp, hist, wts, layer, *, seq_len):
    rows = x_samp.shape[0]
    n_hist = hist.shape[1]
    C = FF_CHUNK
    const = lambda shape: pl.BlockSpec(shape, lambda j: (0,) * len(shape))
    kern = functools.partial(_ffn_short_kernel, seq_len=seq_len)
    return pl.pallas_call(
        kern,
        grid=(D_FF // C,),
        in_specs=[const((N_META, D_MODEL)),
                  const((rows, D_MODEL)),
                  pl.BlockSpec((None, n_hist, C), lambda j: (layer, 0, j)),
                  pl.BlockSpec((None, D_MODEL, C), lambda j: (layer, 0, j)),
                  pl.BlockSpec((None, D_MODEL, C), lambda j: (layer, 0, j)),
                  pl.BlockSpec((None, 3, C), lambda j: (layer, 0, j)),
                  pl.BlockSpec((None, 1, C), lambda j: (layer, 0, j)),
                  pl.BlockSpec((None, C, D_MODEL), lambda j: (layer, j, 0)),
                  pl.BlockSpec((None, 1, D_MODEL), lambda j: (layer, 0, 0)),
                  pl.BlockSpec((None, 1, D_MODEL), lambda j: (layer, 0, 0))],
        out_specs=[const((N_META, D_MODEL)),
                   pl.BlockSpec((CONV_PAD, C), lambda j: (0, j)),
                   const((rows, D_MODEL)),
                   pl.BlockSpec((n_hist, C), lambda j: (0, j)),
                   pl.BlockSpec((None, D_MODEL, C), lambda j: (j, 0, 0)),
                   pl.BlockSpec((None, D_MODEL, C), lambda j: (j, 0, 0)),
                   pl.BlockSpec((C, D_MODEL), lambda j: (j, 0))],
        out_shape=[jax.ShapeDtypeStruct((N_META, D_MODEL), F32),
                   jax.ShapeDtypeStruct((CONV_PAD, D_FF), F32),
                   jax.ShapeDtypeStruct((rows, D_MODEL), F32),
                   jax.ShapeDtypeStruct((n_hist, D_FF), F32),
                   jax.ShapeDtypeStruct((D_FF // C, D_MODEL, C), BF16),
                   jax.ShapeDtypeStruct((D_FF // C, D_MODEL, C), BF16),
                   jax.ShapeDtypeStruct((D_FF, D_MODEL), BF16)],
        scratch_shapes=[pltpu.VMEM((N_META, D_MODEL), F32),
                        pltpu.VMEM((rows, D_MODEL), F32)],
        compiler_params=pltpu.CompilerParams(dimension_semantics=("arbitrary",),
                                             vmem_limit_bytes=VMEM_LIMIT),
        name="ffn_short",
    )(x_meta, x_samp, hist, *wts)


def _to_group_major(a, axis):
    n, j = a.shape[axis], a.shape[axis + 1]
    lead, tail = a.shape[:axis], a.shape[axis + 2:]
    a = a.reshape(*lead, n // SEQ_GROUP, SEQ_GROUP, j, *tail)
    a = jnp.swapaxes(a, axis + 1, axis + 2)
    return a.reshape(*lead, n * j, *tail)


def _from_group_major(a, axis, j):
    rows = a.shape[axis]
    n = rows // j
    lead, tail = a.shape[:axis], a.shape[axis + 1:]
    a = a.reshape(*lead, n // SEQ_GROUP, j, SEQ_GROUP, *tail)
    a = jnp.swapaxes(a, axis + 1, axis + 2)
    return a.reshape(*lead, n, j, *tail)


def kernel(x_prompt, x_sample, state_pool, state_gla, state_conv, meta_tokens,
           w_in, w_a2, b_a, w_pool, pool_scale, gla_norm, w_out, ln1_g, ln1_b,
           w_up, w_gate, conv_w, conv_b, w_down, ln2_g, ln2_b):
    NB, LS = x_sample.shape[0], x_sample.shape[1]
    assert NB % SEQ_GROUP == 0 and LS & (LS - 1) == 0 and CONV_BUF <= LS <= POOL_BUF

    mix_w = (w_in.astype(BF16), w_a2.astype(BF16), b_a[:, None],
             w_pool.astype(BF16), pool_scale[:, None], gla_norm[:, None], w_out.astype(BF16),
             ln1_g[:, None], ln1_b[:, None])
    ffn_w = (w_up, w_gate, conv_w, conv_b[:, None], w_down, ln2_g[:, None], ln2_b[:, None])

    hm = meta_tokens.astype(F32)
    hp = x_prompt
    hs = _to_group_major(x_sample, 0)
    pool_hist = _to_group_major(state_pool, 1)
    conv_hist = _to_group_major(state_conv, 1)

    pp, gp, cp, ps_l, cs_l = [], [], [], [], []
    gs = None
    for l in range(DEPTH):
        hm1, um, sm, hs1, ps_new, gs = _mixer_short(hm, hs, pool_hist, state_gla, mix_w, l, seq_len=LS, s_prev=gs)
        hm, cm, hs, cs_new, w_up_bf, w_gate_bf, w_down_bf = _ffn_short(hm1, hs1, conv_hist, ffn_w, l, seq_len=LS)
        ps_l.append(ps_new)
        cs_l.append(cs_new)

        hp1, pbuf, snew = _mixer_long(hp, um, sm, mix_w, l)
        hp, cbuf = _ffn_long(hp1, cm, (w_up_bf, w_gate_bf) + ffn_w[2:4] + (w_down_bf,) + ffn_w[5:], l)
        pp.append(pbuf[:, 1:])
        gp.append(snew)
        cp.append(cbuf[:, CONV_PAD - CONV_BUF:])

    ps = _from_group_major(jnp.stack(ps_l), 1, POOL_BUF)
    cs = _from_group_major(jnp.stack(cs_l), 1, CONV_BUF)
    return (hp, _from_group_major(hs, 0, LS), jnp.stack(pp), jnp.stack(gp), jnp.stack(cp), ps, gs, cs)
```

```python
import functools
import itertools

import jax
import jax.numpy as jnp
from jax import lax
from jax.experimental import pallas as pl
from jax.experimental.pallas import tpu as pltpu

F32 = jnp.float32
BF16 = jnp.bfloat16

D_MODEL = 1024
N_META = 16
D_POOL = 512
POOL_WINDOWS = (2, 4, 8, 16)
POOL_GROUP = 128
POOL_BUF = 15
POOL_PAD = 16
D_GLA = 512
GLA_HEADS = 4
GLA_DV = 128
GLA_DK = 64
D_GLA_K = 256
GATE_RANK = 16
GATE_TAU = 16.0
D_FF = 2816
CONV_BUF = 2
CONV_PAD = 8
DEPTH = 2
ALPHA = (2 * DEPTH) ** 0.25
LN_EPS = 1e-5
RMS_EPS = 1e-6
PAST_LEN = 16384

C_POOL, C_Q, C_K, C_V, C_R, C_Z, C_END = 0, 512, 768, 1024, 1536, 2048, 2064

LONG_TILE = 512
LONG_CHUNK = 64
LONG_PAR = 2
MIXER_PAR = 2
MIXER_SKEW = 1
FFN_TILE = 512
SEQ_GROUP = 16
GROUPS_PER_STEP = 2
FF_CHUNK = 256
DOWN_ROWS = 256
SHORT_FFN_PARTS = 4
VMEM_LIMIT = 56 * 1024 * 1024

N_MIX_W = 9
N_FFN_W = 7


def _dot(a, b):
    return jnp.dot(a, b, preferred_element_type=F32)


def _dot_nt(a, b):
    return lax.dot_general(a, b, (((1,), (1,)), ((), ())), preferred_element_type=F32)


def _dot_tn(a, b):
    return lax.dot_general(a, b, (((0,), (0,)), ((), ())), preferred_element_type=F32)


def _layer_norm(y, g, b):
    mu = jnp.mean(y, axis=-1, keepdims=True)
    yc = y - mu
    var = jnp.mean(yc * yc, axis=-1, keepdims=True)
    return yc * lax.rsqrt(var + LN_EPS) * g + b


def _silu(x):
    h = 0.5 * x
    return h + h * jnp.tanh(h)


def _log_sigmoid(z):
    return jnp.minimum(z, 0.0) - jnp.log(1.0 + jnp.exp(-jnp.abs(z)))


def _roll_rows(x, shift):
    n = x.shape[0]
    return pltpu.roll(x, shift % n, 0)


def _split_bf16(x):
    hi = x.astype(BF16)
    lo = (x - hi.astype(F32)).astype(BF16)
    return hi, lo


def _project_in(xb, w_in_ref):
    u = _dot(xb, w_in_ref[:, C_POOL:C_Q])
    zr = _dot(xb, w_in_ref[:, C_Z:C_END])
    q = _dot(xb, w_in_ref[:, C_Q:C_K]) * (GLA_DK ** -0.5)
    k = _dot(xb, w_in_ref[:, C_K:C_V])
    v = _dot(xb, w_in_ref[:, C_V:C_R])
    r = _dot(xb, w_in_ref[:, C_R:C_Z])
    return u, zr, q, k, v, r


def _gate_log_decay(zr, w_a2_ref, b_a_ref):
    z = _dot(zr.astype(BF16), w_a2_ref[...]) + b_a_ref[...]
    return _log_sigmoid(z) * (1.0 / GATE_TAU)


def _gla_output_gate(o, r, gnorm):
    parts = []
    for h in range(GLA_HEADS):
        oh = o[:, h * GLA_DV:(h + 1) * GLA_DV]
        ms = jnp.mean(oh * oh, axis=-1, keepdims=True)
        parts.append(oh * lax.rsqrt(ms + RMS_EPS) * gnorm)
    return jnp.concatenate(parts, axis=1) * _silu(r)


def _pool_project(d_groups, w_pool_ref, pscale_ref):
    ys = [_dot(d.astype(BF16), w_pool_ref[g]) for g, d in enumerate(d_groups)]
    return jnp.concatenate(ys, axis=1) * pscale_ref[...]


def _mix_out(x, y_pool, y_gla, w_out_ref, g_ref, b_ref):
    mix = jnp.concatenate([y_pool, y_gla], axis=1).astype(BF16)
    return _layer_norm(ALPHA * x + _dot(mix, w_out_ref[...]), g_ref[...], b_ref[...])


def _emit_staggered(stage_gens, first_round):
    live = list(zip(first_round, stage_gens))
    rnd = 0
    while live:
        for start, g in list(live):
            if rnd >= start and next(g, StopIteration) is StopIteration:
                live.remove((start, g))
        rnd += 1


def _head_pair_keys(k_p, lane128):
    zk = jnp.zeros_like(k_p)
    return jnp.concatenate([jnp.where(lane128 < GLA_DK, k_p, zk), jnp.where(lane128 >= GLA_DK, k_p, zk)], axis=0)


def _head_pair_values(v_p, lane256):
    zv = jnp.zeros_like(v_p)
    return jnp.concatenate([jnp.where(lane256 < GLA_DV, v_p, zv), jnp.where(lane256 >= GLA_DV, v_p, zv)], axis=0)


def _mixer_long_kernel(*refs, n_par, tile, chunk):
    x_ref, pprev_ref, s0_ref = refs[:3]
    wts = refs[3:3 + N_MIX_W]
    x1_ref, pbuf_ref, snew_ref, ubuf, sbd = refs[3 + N_MIX_W:]
    t = pl.program_id(1)
    zero_blk = jnp.zeros((GLA_DK, GLA_DV), F32)

    @pl.when(t == 0)
    def _init():
        for j in range(n_par):
            ubuf[j, 0:POOL_PAD, :] = pprev_ref[...]
            for p in range(2):
                top = jnp.concatenate([s0_ref[2 * p], zero_blk], axis=1)
                bot = jnp.concatenate([zero_blk, s0_ref[2 * p + 1]], axis=1)
                sbd[j, p] = jnp.concatenate([top, bot], axis=0)

    tiles = [_mixer_long_tile(x_ref.at[j], wts, x1_ref.at[j], ubuf.at[j], sbd.at[j], tile=tile, chunk=chunk)
             for j in range(n_par)]
    _emit_staggered(tiles, [MIXER_SKEW * j for j in range(n_par)])

    @pl.when(t == pl.num_programs(1) - 1)
    def _final():
        for j in range(n_par):
            pbuf_ref[j] = ubuf[j, POOL_PAD - POOL_BUF:POOL_PAD, :]
            for p in range(2):
                s_p = sbd[j, p]
                snew_ref[j, 2 * p] = s_p[0:GLA_DK, 0:GLA_DV]
                snew_ref[j, 2 * p + 1] = s_p[GLA_DK:2 * GLA_DK, GLA_DV:2 * GLA_DV]


def _mixer_long_tile(x_ref, wts, x1_ref, ubuf, sbd, *, tile, chunk):
    w_in_ref, w_a2_ref, b_a_ref, w_pool_ref, pscale_ref, gnorm_ref, w_out_ref, g_ref, b_ref = wts
    T, C = tile, chunk
    x = x_ref[...]
    xb = x.astype(BF16)

    u, zr, q, k, v, r = _project_in(xb, w_in_ref)
    yield

    ubuf[POOL_PAD:POOL_PAD + T, :] = u
    d_groups = []
    for g, w in enumerate(POOL_WINDOWS):
        s = ubuf[:, g * POOL_GROUP:(g + 1) * POOL_GROUP]
        sh = 1
        while sh < w:
            s = s + _roll_rows(s, sh)
            sh *= 2
        d_groups.append(s[POOL_PAD:, :] * (1.0 / w) - u[:, g * POOL_GROUP:(g + 1) * POOL_GROUP])
    y_pool = _pool_project(d_groups, w_pool_ref, pscale_ref)
    ubuf[0:POOL_PAD, :] = ubuf[T:T + POOL_PAD, :]
    loga = _gate_log_decay(zr, w_a2_ref, b_a_ref)

    tr = lax.broadcasted_iota(jnp.int32, (C, C), 0)
    tc = lax.broadcasted_iota(jnp.int32, (C, C), 1)
    tri = jnp.where(tc <= tr, 1.0, 0.0).astype(BF16)
    ar = lax.broadcasted_iota(jnp.int32, (C, 2 * C), 0)
    ac = lax.broadcasted_iota(jnp.int32, (C, 2 * C), 1) & (C - 1)
    causal = ac <= ar
    lane128 = lax.broadcasted_iota(jnp.int32, (C, 128), 1)
    lane256 = lax.broadcasted_iota(jnp.int32, (C, 256), 1)
    sr = lax.broadcasted_iota(jnp.int32, (128, 256), 0)
    sc = lax.broadcasted_iota(jnp.int32, (128, 256), 1)
    blockdiag = (sr >= GLA_DK) == (sc >= GLA_DV)
    mid = C // 2 - 1

    n_chunks = T // C
    pairs = [(c, p) for c in range(n_chunks) for p in range(2)]
    ks = [slice(128 * p, 128 * (p + 1)) for p in range(2)]
    vs = [slice(256 * p, 256 * (p + 1)) for p in range(2)]

    bcs = []
    for c in range(n_chunks):
        la_hi, la_lo = _split_bf16(loga[c * C:(c + 1) * C])
        bb = _dot(tri, jnp.concatenate([la_hi, la_lo], axis=1))
        bcs.append(bb[:, :D_GLA_K] + bb[:, D_GLA_K:])
    yield

    q_in, k_in, q_st, k_st, dec_t, vc = [], [], [], [], [], []
    for c in range(n_chunks):
        bc = bcs[c]
        bmid = bc[mid:mid + 1]
        bend = bc[C - 1:C]
        qc = q[c * C:(c + 1) * C]
        kc = k[c * C:(c + 1) * C]
        q_in.append((qc * jnp.exp(bc - bmid)).astype(BF16))
        k_in.append((kc * jnp.exp(bmid - bc)).astype(BF16))
        q_st.append((qc * jnp.exp(bc)).astype(BF16))
        k_st.append((kc * jnp.exp(bend - bc)).astype(BF16))
        dec_t.append(jnp.transpose(jnp.broadcast_to(jnp.exp(bend), (128, D_GLA_K))))
        vc.append(v[c * C:(c + 1) * C].astype(BF16))
    yield

    attn, upd = {}, {}
    for c, p in pairs:
        a = _dot_nt(q_in[c][:, ks[p]], _head_pair_keys(k_in[c][:, ks[p]], lane128))
        attn[c, p] = jnp.where(causal, a, 0.0).astype(BF16)
    for c, p in pairs:
        u_cp = _dot_tn(k_st[c][:, ks[p]], vc[c][:, vs[p]])
        upd[c, p] = jnp.where(blockdiag, u_cp, 0.0)
    yield

    s_vals = [sbd[p] for p in range(2)]
    s_start = {}
    for c, p in pairs:
        s_start[c, p] = s_vals[p].astype(BF16)
        dec_p = dec_t[c][ks[p], :]
        s_vals[p] = jnp.concatenate([dec_p, dec_p], axis=1) * s_vals[p] + upd[c, p]
    for p in range(2):
        sbd[p] = s_vals[p]

    o_rows = [[], []]
    for c, p in pairs:
        vblk = _head_pair_values(vc[c][:, vs[p]], lane256)
        lhs = jnp.concatenate([attn[c, p], q_st[c][:, ks[p]]], axis=1)
        rhs = jnp.concatenate([vblk, s_start[c, p]], axis=0)
        o_rows[p].append(_dot(lhs, rhs))
    o = jnp.concatenate([jnp.concatenate(o_rows[p], axis=0) for p in range(2)], axis=1)
    yield

    y_gla = _gla_output_gate(o, r, gnorm_ref[...])
    x1_ref[...] = _mix_out(x, y_pool, y_gla, w_out_ref, g_ref, b_ref)


def _layer_spec(shape, layer):
    nd = len(shape)
    return pl.BlockSpec((None,) + shape, lambda *_: (layer,) + (0,) * nd, pipeline_mode=pl.Buffered(1))


def _mixer_weight_specs(layer):
    shapes = [(D_MODEL, C_END), (GATE_RANK, D_GLA_K), (1, D_GLA_K),
              (4, POOL_GROUP, POOL_GROUP), (1, D_POOL), (1, GLA_DV), (D_MODEL, D_MODEL), (1, D_MODEL), (1, D_MODEL)]
    assert len(shapes) == N_MIX_W
    return [_layer_spec(s, layer) for s in shapes]


def _mixer_long(x, pprev, s0, wts, layer):
    B, L, _ = x.shape
    T = min(LONG_TILE, L)
    P = MIXER_PAR if B % MIXER_PAR == 0 else 1
    assert L % T == 0 and T % LONG_CHUNK == 0
    kern = functools.partial(_mixer_long_kernel, n_par=P, tile=T, chunk=LONG_CHUNK)
    return pl.pallas_call(
        kern,
        grid=(B // P, L // T),
        in_specs=[pl.BlockSpec((P, T, D_MODEL), lambda b, t: (b, t, 0)),
                  pl.BlockSpec((POOL_PAD, D_POOL), lambda b, t: (0, 0)),
                  pl.BlockSpec((GLA_HEADS, GLA_DK, GLA_DV), lambda b, t: (0, 0, 0))]
        + _mixer_weight_specs(layer),
        out_specs=[pl.BlockSpec((P, T, D_MODEL), lambda b, t: (b, t, 0)),
                   pl.BlockSpec((P, POOL_BUF, D_POOL), lambda b, t: (b, 0, 0)),
                   pl.BlockSpec((P, GLA_HEADS, GLA_DK, GLA_DV), lambda b, t: (b, 0, 0, 0))],
        out_shape=[jax.ShapeDtypeStruct((B, L, D_MODEL), F32),
                   jax.ShapeDtypeStruct((B, POOL_BUF, D_POOL), F32),
                   jax.ShapeDtypeStruct((B, GLA_HEADS, GLA_DK, GLA_DV), F32)],
        scratch_shapes=[pltpu.VMEM((P, T + POOL_PAD, D_POOL), F32),
                        pltpu.VMEM((P, 2, 2 * GLA_DK, 2 * GLA_DV), F32)],
        compiler_params=pltpu.CompilerParams(dimension_semantics=("arbitrary", "arbitrary"),
                                             vmem_limit_bytes=VMEM_LIMIT),
        name="mixer_long",
    )(x, pprev, s0, *wts)


def _mixer_meta(x_ref, wts, x1_ref, u_ref, s_ref):
    w_in_ref, w_a2_ref, b_a_ref, w_pool_ref, pscale_ref, gnorm_ref, w_out_ref, g_ref, b_ref = wts
    L = N_META
    x = x_ref[...]
    xb = x.astype(BF16)
    u, zr, q, k, v, r = _project_in(xb, w_in_ref)
    u_ref[...] = u

    row128 = lax.broadcasted_iota(jnp.int32, (L, POOL_GROUP), 0)
    pos1 = lax.broadcasted_iota(jnp.int32, (L, 1), 0)
    d_groups = []
    for g, w in enumerate(POOL_WINDOWS):
        ug = u[:, g * POOL_GROUP:(g + 1) * POOL_GROUP]
        s = ug
        sh = 1
        while sh < w:
            s = s + jnp.where(row128 >= sh, _roll_rows(s, sh), 0.0)
            sh *= 2
        d_groups.append(s / jnp.minimum(w, pos1 + 1).astype(F32) - ug)
    y_pool = _pool_project(d_groups, w_pool_ref, pscale_ref)

    loga = _gate_log_decay(zr, w_a2_ref, b_a_ref)
    row256 = lax.broadcasted_iota(jnp.int32, (L, D_GLA_K), 0)
    b = loga
    sh = 1
    while sh < L:
        b = b + jnp.where(row256 >= sh, _roll_rows(b, sh), 0.0)
        sh *= 2
    bend = b[L - 1:L]
    q_in = (q * jnp.exp(b)).astype(BF16)
    k_in = (k * jnp.exp(-b)).astype(BF16)
    k_st = (k * jnp.exp(bend - b)).astype(BF16)
    vb = v.astype(BF16)
    ar = lax.broadcasted_iota(jnp.int32, (L, 2 * L), 0)
    ac = lax.broadcasted_iota(jnp.int32, (L, 2 * L), 1) & (L - 1)
    causal = ac <= ar
    lane128 = lax.broadcasted_iota(jnp.int32, (L, 128), 1)
    lane256 = lax.broadcasted_iota(jnp.int32, (L, 256), 1)
    attn = []
    for p in range(2):
        ks = slice(128 * p, 128 * (p + 1))
        a = _dot_nt(q_in[:, ks], _head_pair_keys(k_in[:, ks], lane128))
        attn.append(jnp.where(causal, a, 0.0).astype(BF16))
    o_parts = []
    for p in range(2):
        ks = slice(128 * p, 128 * (p + 1))
        vs = slice(256 * p, 256 * (p + 1))
        s_p = _dot_tn(k_st[:, ks], vb[:, vs])
        s_ref[2 * p] = s_p[0:GLA_DK, 0:GLA_DV]
        s_ref[2 * p + 1] = s_p[GLA_DK:2 * GLA_DK, GLA_DV:2 * GLA_DV]
        o_parts.append(_dot(attn[p], _head_pair_values(vb[:, vs], lane256)))
    o = jnp.concatenate(o_parts, axis=1)

    y_gla = _gla_output_gate(o, r, gnorm_ref[...])
    x1_ref[...] = _mix_out(x, y_pool, y_gla, w_out_ref, g_ref, b_ref)


def _mixer_sample(x_ref, hist_ref, s0_ref, wts, x1_ref, hist_out_ref, snew_ref, *, seq_len):
    w_in_ref, w_a2_ref, b_a_ref, w_pool_ref, pscale_ref, gnorm_ref, w_out_ref, g_ref, b_ref = wts
    G = SEQ_GROUP
    R = G * seq_len
    n_groups = x_ref.shape[0] // R
    x = x_ref[...]
    xb = x.astype(BF16)
    u, zr, q, k, v, r = _project_in(xb, w_in_ref)
    loga = _gate_log_decay(zr, w_a2_ref, b_a_ref)
    results = []
    gens = []
    for gi in range(n_groups):
        rs = slice(gi * R, (gi + 1) * R)
        hist_rows = pl.ds(gi * G * POOL_BUF, G * POOL_BUF)
        seqs = pl.ds(gi * G, G)
        gens.append(_sample_group(u[rs], q[rs], k[rs], v[rs], loga[rs], hist_ref.at[hist_rows], s0_ref.at[seqs],
                                  hist_out_ref.at[hist_rows], snew_ref.at[seqs], results, seq_len=seq_len))
    for _ in itertools.zip_longest(*gens):
        pass
    d_groups = [jnp.concatenate([res[0][g] for res in results], axis=0) for g in range(len(POOL_WINDOWS))]
    o = jnp.concatenate([res[1] for res in results], axis=0)
    y_pool = _pool_project(d_groups, w_pool_ref, pscale_ref)
    y_gla = _gla_output_gate(o, r, gnorm_ref[...])
    x1_ref[...] = _mix_out(x, y_pool, y_gla, w_out_ref, g_ref, b_ref)


def _sample_group(u, q, k, v, loga, hist_ref, s0_ref, hist_out_ref, snew_ref, results, *, seq_len):
    G, Ls = SEQ_GROUP, seq_len
    R = G * Ls
    NS = G * GLA_DK
    g_shift = G.bit_length() - 1
    hist_out_ref[0:(POOL_BUF - Ls) * G, :] = hist_ref[R:POOL_BUF * G, :]
    hist_out_ref[(POOL_BUF - Ls) * G:POOL_BUF * G, :] = u

    def blk(a, t):
        return a[t * G:(t + 1) * G]

    d_groups = []
    for g, w in enumerate(POOL_WINDOWS):
        cols = slice(g * POOL_GROUP, (g + 1) * POOL_GROUP)
        ug = u[:, cols]
        suffix = [None]
        acc = None
        for m in range(1, min(w - 1, POOL_BUF) + 1):
            h = hist_ref[(POOL_BUF - m) * G:(POOL_BUF - m + 1) * G, cols]
            acc = h if acc is None else acc + h
            suffix.append(acc)
        parts = []
        for t in range(Ls):
            wsum = blk(ug, t)
            for j in range(max(0, t - w + 1), t):
                wsum = wsum + blk(ug, j)
            m = w - 1 - t
            if m > 0:
                wsum = wsum + suffix[m]
            parts.append(wsum * (1.0 / w) - blk(ug, t))
        d_groups.append(jnp.concatenate(parts, axis=0))

    b_t = [blk(loga, 0)]
    for t in range(1, Ls):
        b_t.append(b_t[-1] + blk(loga, t))
    b = jnp.concatenate(b_t, axis=0)
    bend = jnp.concatenate([b_t[-1]] * Ls, axis=0)
    q_in = (q * jnp.exp(b)).astype(BF16)
    k_in = (k * jnp.exp(-b)).astype(BF16)
    k_st = k * jnp.exp(bend - b)
    dec_hi, dec_lo = _split_bf16(jnp.exp(bend))
    tok = lax.broadcasted_iota(jnp.int32, (R, D_GLA_K), 0) >> g_shift
    dec_rows = jnp.where(tok == Ls - 1, dec_hi, jnp.where(tok == Ls - 2, dec_lo, jnp.zeros_like(dec_lo)))
    vb = v.astype(BF16)

    ar = lax.broadcasted_iota(jnp.int32, (R, 2 * R), 0)
    ac = lax.broadcasted_iota(jnp.int32, (R, 2 * R), 1) & (R - 1)
    same_seq_causal = ((ac & (G - 1)) == (ar & (G - 1))) & ((ac >> g_shift) <= (ar >> g_shift))
    lane128 = lax.broadcasted_iota(jnp.int32, (R, 128), 1)
    lane256 = lax.broadcasted_iota(jnp.int32, (R, 256), 1)
    br = lax.broadcasted_iota(jnp.int32, (R, NS), 0)
    bcol = lax.broadcasted_iota(jnp.int32, (R, NS), 1)
    own_state = (bcol >> 6) == (br & (G - 1))
    ones_blk = jnp.ones((R, GLA_DV), BF16)
    zeros_blk = jnp.zeros((R, GLA_DV), BF16)

    def expand(xp, first):
        sw = pltpu.roll(xp, GLA_DK, 1)
        two = jnp.where(lane128 < GLA_DK, xp, sw) if first else jnp.where(lane128 < GLA_DK, sw, xp)
        rep = jnp.concatenate([two] * (NS // 128), axis=1)
        return jnp.where(own_state, rep, 0.0).astype(BF16)

    ks = [slice(128 * p, 128 * (p + 1)) for p in range(2)]
    vs = [slice(256 * p, 256 * (p + 1)) for p in range(2)]
    attn = []
    for p in range(2):
        a = _dot_nt(q_in[:, ks[p]], _head_pair_keys(k_in[:, ks[p]], lane128))
        attn.append(jnp.where(same_seq_causal, a, 0.0).astype(BF16))
    yield
    inter = []
    for h in range(GLA_HEADS):
        p, first = h // 2, h % 2 == 0
        s_flat = s0_ref[:, h].reshape(NS, GLA_DV)
        inter.append(_dot(expand(q_in[:, ks[p]].astype(F32), first), s_flat.astype(BF16)))
    for h in range(GLA_HEADS):
        p, first = h // 2, h % 2 == 0
        s_flat = s0_ref[:, h].reshape(NS, GLA_DV)
        lhs = jnp.concatenate([expand(k_st[:, ks[p]], first),
                               expand(dec_rows[:, ks[p]].astype(F32), first)], axis=0)
        v_h = vb[:, h * GLA_DV:(h + 1) * GLA_DV]
        rhs = jnp.concatenate([jnp.concatenate([v_h, zeros_blk], axis=1),
                               jnp.concatenate([zeros_blk, ones_blk], axis=1)], axis=0)
        ud = _dot_tn(lhs, rhs)
        s_new = ud[:, GLA_DV:] * s_flat + ud[:, :GLA_DV]
        snew_ref[:, h] = s_new.reshape(G, GLA_DK, GLA_DV)
    yield
    o_parts = []
    for p in range(2):
        o_intra = _dot(attn[p], _head_pair_values(vb[:, vs[p]], lane256))
        o_parts.append(o_intra + jnp.concatenate(inter[2 * p:2 * p + 2], axis=1))
    results.append((d_groups, jnp.concatenate(o_parts, axis=1)))


def _mixer_short_kernel(*refs, seq_len, n_prev):
    xm_ref, xs_ref, hist_ref, s0_ref = refs[:4]
    wts = refs[4:4 + N_MIX_W]
    x1m_ref, um_ref, sm_ref, x1s_ref, us_ref, ss_ref = refs[-6:]
    i = pl.program_id(0)

    @pl.when(i == 0)
    def _meta():
        _mixer_meta(xm_ref, wts, x1m_ref, um_ref, sm_ref)

    @pl.when(i > 0)
    def _sample():
        for a in range(n_prev):
            ss_ref[a] = refs[4 + N_MIX_W][a]
        _mixer_sample(xs_ref, hist_ref, s0_ref, wts, x1s_ref, us_ref, ss_ref.at[n_prev], seq_len=seq_len)


def _mixer_short(x_meta, x_samp, hist, s0, wts, layer, *, seq_len, s_prev=None):
    rows = x_samp.shape[0]
    n_groups = rows // (SEQ_GROUP * seq_len)
    per_step = GROUPS_PER_STEP if n_groups % GROUPS_PER_STEP == 0 else 1
    G = SEQ_GROUP * per_step
    R = G * seq_len
    n_tiles = rows // R
    assert rows % R == 0
    clamp = lambda i: jnp.maximum(i - 1, 0)
    n_prev = 0 if s_prev is None else s_prev.shape[0]
    state_blk = (G, GLA_HEADS, GLA_DK, GLA_DV)
    in_specs = ([pl.BlockSpec((N_META, D_MODEL), lambda i: (0, 0)),
                 pl.BlockSpec((R, D_MODEL), lambda i: (clamp(i), 0)),
                 pl.BlockSpec((None, G * POOL_BUF, D_POOL), lambda i: (layer, clamp(i), 0)),
                 pl.BlockSpec((None,) + state_blk, lambda i: (layer, clamp(i), 0, 0, 0))]
                + _mixer_weight_specs(layer))
    args = [x_meta, x_samp, hist, s0, *wts]
    if n_prev:
        in_specs.append(pl.BlockSpec((n_prev,) + state_blk, lambda i: (0, clamp(i), 0, 0, 0)))
        args.append(s_prev)
    kern = functools.partial(_mixer_short_kernel, seq_len=seq_len, n_prev=n_prev)

    return pl.pallas_call(
        kern,
        grid=(n_tiles + 1,),
        in_specs=in_specs,
        out_specs=[pl.BlockSpec((N_META, D_MODEL), lambda i: (0, 0)),
                   pl.BlockSpec((N_META, D_POOL), lambda i: (0, 0)),
                   pl.BlockSpec((GLA_HEADS, GLA_DK, GLA_DV), lambda i: (0, 0, 0)),
                   pl.BlockSpec((R, D_MODEL), lambda i: (clamp(i), 0)),
                   pl.BlockSpec((G * POOL_BUF, D_POOL), lambda i: (clamp(i), 0)),
                   pl.BlockSpec((n_prev + 1,) + state_blk, lambda i: (0, clamp(i), 0, 0, 0))],
        out_shape=[jax.ShapeDtypeStruct((N_META, D_MODEL), F32),
                   jax.ShapeDtypeStruct((N_META, D_POOL), F32),
                   jax.ShapeDtypeStruct((GLA_HEADS, GLA_DK, GLA_DV), F32),
                   jax.ShapeDtypeStruct((rows, D_MODEL), F32),
                   jax.ShapeDtypeStruct((n_tiles * G * POOL_BUF, D_POOL), F32),
                   jax.ShapeDtypeStruct((n_prev + 1,) + s0.shape[1:], F32)],
        compiler_params=pltpu.CompilerParams(dimension_semantics=("arbitrary",),
                                             vmem_limit_bytes=VMEM_LIMIT),
        name="mixer_short",
    )(*args)


def _ffn_tile(x_ref, wts, y_ref, conv_inputs, store_gate):
    w_up_ref, w_gate_ref, cw_ref, cb_ref, w_down_ref, g_ref, b_ref = wts
    x = x_ref[...]
    xb = x.astype(BF16)

    acts = []
    for j in range(D_FF // FF_CHUNK):
        cs = slice(j * FF_CHUNK, (j + 1) * FF_CHUNK)
        a = _dot(xb, w_up_ref[j])
        gt = _dot(xb, w_gate_ref[j])
        g1, g2 = conv_inputs(gt, cs)
        store_gate(gt, cs)
        gc = cb_ref[:, cs] + cw_ref[0:1, cs] * g2 + cw_ref[1:2, cs] * g1 + cw_ref[2:3, cs] * gt
        acts.append((a * _silu(gc)).astype(BF16))
    yield

    act = jnp.concatenate(acts, axis=1)
    rows = x.shape[0]
    rb = min(rows, DOWN_ROWS)
    for r0 in range(0, rows, rb):
        f = _dot(act[r0:r0 + rb], w_down_ref[...])
        y_ref[r0:r0 + rb, :] = _layer_norm(ALPHA * x[r0:r0 + rb] + f, g_ref[...], b_ref[...])


def _ffn_long_kernel(*refs, n_par):
    x_ref, cprev_ref = refs[:2]
    wts = refs[2:2 + N_FFN_W]
    y_ref, hist_out_ref, gbuf = refs[2 + N_FFN_W:]
    t = pl.program_id(1)
    T = x_ref.shape[1]

    @pl.when(t == 0)
    def _init():
        for j in range(n_par):
            gbuf[j, 0:CONV_PAD, :] = cprev_ref[...]

    def make_tile(j):
        def conv_inputs(gt, cs):
            gbuf[j, CONV_PAD:CONV_PAD + T, cs] = gt
            return gbuf[j, CONV_PAD - 1:CONV_PAD - 1 + T, cs], gbuf[j, CONV_PAD - 2:CONV_PAD - 2 + T, cs]

        def store_gate(gt, cs):
            gbuf[j, 0:CONV_PAD, cs] = gbuf[j, T:T + CONV_PAD, cs]

        return _ffn_tile(x_ref.at[j], wts, y_ref.at[j], conv_inputs, store_gate)

    for _ in itertools.zip_longest(*[make_tile(j) for j in range(n_par)]):
        pass

    @pl.when(t == pl.num_programs(1) - 1)
    def _final():
        for j in range(n_par):
            hist_out_ref[j] = gbuf[j, CONV_PAD - CONV_BUF:CONV_PAD, :]


def _ffn_short_kernel(xm_ref, xs_ref, hist_ref, w_up_ref, w_gate_ref, cw_ref, cb_ref, w_down_ref, g_ref, b_ref,
                      ym_ref, cm_ref, ys_ref, cs_ref, w_up_bf_ref, w_gate_bf_ref, w_down_bf_ref,
                      fm_ref, fs_ref, *, seq_len):
    j = pl.program_id(0)
    G, Ls = SEQ_GROUP, seq_len
    R = G * Ls
    rows = xs_ref.shape[0]
    n_tiles = rows // R

    w_up = w_up_ref[...].astype(BF16)
    w_gate = w_gate_ref[...].astype(BF16)
    w_down = w_down_ref[...].astype(BF16)
    w_up_bf_ref[...] = w_up
    w_gate_bf_ref[...] = w_gate
    w_down_bf_ref[...] = w_down

    @pl.when(j == 0)
    def _first():
        fm_ref[...] = jnp.zeros_like(fm_ref)
        fs_ref[...] = jnp.zeros_like(fs_ref)

    n_parts = SHORT_FFN_PARTS if n_tiles % SHORT_FFN_PARTS == 0 else 1
    part_tiles = n_tiles // n_parts
    part_rows = part_tiles * R
    hist_rows = CONV_BUF * G
    tok = (lax.broadcasted_iota(jnp.int32, (part_rows, FF_CHUNK), 0) & (R - 1)) >> (G.bit_length() - 1)
    zeros_tail = jnp.zeros((R - hist_rows, FF_CHUNK), F32)

    def sample_conv(gt, part):
        first = part * part_tiles
        hx = jnp.concatenate([piece for n in range(first, first + part_tiles)
                              for piece in (hist_ref[n * hist_rows:(n + 1) * hist_rows, :], zeros_tail)], axis=0)
        g1 = jnp.where(tok >= 1, _roll_rows(gt, G), _roll_rows(hx, -G))
        g2 = jnp.where(tok >= 2, _roll_rows(gt, 2 * G), hx)
        return g1, g2

    def sample_store(gt, part):
        for n in range(part_tiles):
            dst = (part * part_tiles + n) * hist_rows
            cs_ref[dst:dst + hist_rows, :] = gt[n * R + (Ls - CONV_BUF) * G:(n + 1) * R]

    rowm = lax.broadcasted_iota(jnp.int32, (N_META, FF_CHUNK), 0)

    def meta_conv(gt):
        return (jnp.where(rowm >= 1, _roll_rows(gt, 1), 0.0), jnp.where(rowm >= 2, _roll_rows(gt, 2), 0.0))

    def meta_store(gt):
        cm_ref[...] = gt[N_META - CONV_PAD:N_META]

    parts = [(xs_ref.at[pl.ds(p * part_rows, part_rows)], fs_ref.at[pl.ds(p * part_rows, part_rows)],
              functools.partial(sample_conv, part=p), functools.partial(sample_store, part=p))
             for p in range(n_parts)]
    parts.append((xm_ref, fm_ref, meta_conv, meta_store))

    proj = []
    for x_ref, _, _, _ in parts:
        xb = x_ref[...].astype(BF16)
        proj.append((_dot(xb, w_up), _dot(xb, w_gate)))
    acts = []
    for (a, gt), (_, _, conv_inputs, store_gate) in zip(proj, parts):
        g1, g2 = conv_inputs(gt)
        store_gate(gt)
        gc = cb_ref[...] + cw_ref[0:1, :] * g2 + cw_ref[1:2, :] * g1 + cw_ref[2:3, :] * gt
        acts.append((a * _silu(gc)).astype(BF16))
    for act, (_, f_ref, _, _) in zip(acts, parts):
        f_ref[...] += _dot(act, w_down)

    @pl.when(j == pl.num_programs(0) - 1)
    def _last():
        ys_ref[...] = _layer_norm(ALPHA * xs_ref[...] + fs_ref[...], g_ref[...], b_ref[...])
        ym_ref[...] = _layer_norm(ALPHA * xm_ref[...] + fm_ref[...], g_ref[...], b_ref[...])


def _ffn_weight_specs(layer):
    whole = lambda shape: pl.BlockSpec(shape, lambda *_: (0,) * len(shape), pipeline_mode=pl.Buffered(1))
    chunked = (D_FF // FF_CHUNK, D_MODEL, FF_CHUNK)
    specs = [whole(chunked), whole(chunked), _layer_spec((3, D_FF), layer), _layer_spec((1, D_FF), layer),
             whole((D_FF, D_MODEL)), _layer_spec((1, D_MODEL), layer), _layer_spec((1, D_MODEL), layer)]
    assert len(specs) == N_FFN_W
    return specs


def _ffn_long(x, cprev, wts, layer):
    B, L, _ = x.shape
    T = min(FFN_TILE, L)
    P = LONG_PAR if B % LONG_PAR == 0 else 1
    assert L % T == 0
    kern = functools.partial(_ffn_long_kernel, n_par=P)
    return pl.pallas_call(
        kern,
        grid=(B // P, L // T),
        in_specs=[pl.BlockSpec((P, T, D_MODEL), lambda b, t: (b, t, 0)),
                  pl.BlockSpec((CONV_PAD, D_FF), lambda b, t: (0, 0))] + _ffn_weight_specs(layer),
        out_specs=[pl.BlockSpec((P, T, D_MODEL), lambda b, t: (b, t, 0)),
                   pl.BlockSpec((P, CONV_BUF, D_FF), lambda b, t: (b, 0, 0))],
        out_shape=[jax.ShapeDtypeStruct((B, L, D_MODEL), F32),
                   jax.ShapeDtypeStruct((B, CONV_BUF, D_FF), F32)],
        scratch_shapes=[pltpu.VMEM((P, T + CONV_PAD, D_FF), F32)],
        compiler_params=pltpu.CompilerParams(dimension_semantics=("arbitrary", "arbitrary"),
                                             vmem_limit_bytes=VMEM_LIMIT),
        name="ffn_long",
    )(x, cprev, *wts)


def _ffn_short(x_meta, x_samp, hist, wts, layer, *, seq_len):
    rows = x_samp.shape[0]
    n_hist = hist.shape[1]
    C = FF_CHUNK
    const = lambda shape: pl.BlockSpec(shape, lambda j: (0,) * len(shape))
    kern = functools.partial(_ffn_short_kernel, seq_len=seq_len)
    return pl.pallas_call(
        kern,
        grid=(D_FF // C,),
        in_specs=[const((N_META, D_MODEL)),
                  const((rows, D_MODEL)),
                  pl.BlockSpec((None, n_hist, C), lambda j: (layer, 0, j)),
                  pl.BlockSpec((None, D_MODEL, C), lambda j: (layer, 0, j)),
                  pl.BlockSpec((None, D_MODEL, C), lambda j: (layer, 0, j)),
                  pl.BlockSpec((None, 3, C), lambda j: (layer, 0, j)),
                  pl.BlockSpec((None, 1, C), lambda j: (layer, 0, j)),
                  pl.BlockSpec((None, C, D_MODEL), lambda j: (layer, j, 0)),
                  pl.BlockSpec((None, 1, D_MODEL), lambda j: (layer, 0, 0)),
                  pl.BlockSpec((None, 1, D_MODEL), lambda j: (layer, 0, 0))],
        out_specs=[const((N_META, D_MODEL)),
                   pl.BlockSpec((CONV_PAD, C), lambda j: (0, j)),
                   const((rows, D_MODEL)),
                   pl.BlockSpec((n_hist, C), lambda j: (0, j)),
                   pl.BlockSpec((None, D_MODEL, C), lambda j: (j, 0, 0)),
                   pl.BlockSpec((None, D_MODEL, C), lambda j: (j, 0, 0)),
                   pl.BlockSpec((C, D_MODEL), lambda j: (j, 0))],
        out_shape=[jax.ShapeDtypeStruct((N_META, D_MODEL), F32),
                   jax.ShapeDtypeStruct((CONV_PAD, D_FF), F32),
                   jax.ShapeDtypeStruct((rows, D_MODEL), F32),
                   jax.ShapeDtypeStruct((n_hist, D_FF), F32),
                   jax.ShapeDtypeStruct((D_FF // C, D_MODEL, C), BF16),
                   jax.ShapeDtypeStruct((D_FF // C, D_MODEL, C), BF16),
                   jax.ShapeDtypeStruct((D_FF, D_MODEL), BF16)],
        scratch_shapes=[pltpu.VMEM((N_META, D_MODEL), F32),
                        pltpu.VMEM((rows, D_MODEL), F32)],
        compiler_params=pltpu.CompilerParams(dimension_semantics=("arbitrary",),
                                             vmem_limit_bytes=VMEM_LIMIT),
        name="ffn_short",
    )(x_meta, x_samp, hist, *wts)


def _to_group_major(a, axis):
    n, j = a.shape[axis], a.shape[axis + 1]
    lead, tail = a.shape[:axis], a.shape[axis + 2:]
    a = a.reshape(*lead, n // SEQ_GROUP, SEQ_GROUP, j, *tail)
    a = jnp.swapaxes(a, axis + 1, axis + 2)
    return a.reshape(*lead, n * j, *tail)


def _from_group_major(a, axis, j):
    rows = a.shape[axis]
    n = rows // j
    lead, tail = a.shape[:axis], a.shape[axis + 1:]
    a = a.reshape(*lead, n // SEQ_GROUP, j, SEQ_GROUP, *tail)
    a = jnp.swapaxes(a, axis + 1, axis + 2)
    return a.reshape(*lead, n, j, *tail)


def kernel(x_prompt, x_sample, state_pool, state_gla, state_conv, meta_tokens,
           w_in, w_a2, b_a, w_pool, pool_scale, gla_norm, w_out, ln1_g, ln1_b,
           w_up, w_gate, conv_w, conv_b, w_down, ln2_g, ln2_b):
    NB, LS = x_sample.shape[0], x_sample.shape[1]
    assert NB % SEQ_GROUP == 0 and LS & (LS - 1) == 0 and CONV_BUF <= LS <= POOL_BUF

    row = lambda a: a.reshape(DEPTH, 1, a.shape[-1])
    mix_w = (w_in.astype(BF16), w_a2.astype(BF16), row(b_a), w_pool.astype(BF16), row(pool_scale), row(gla_norm),
             w_out.astype(BF16), row(ln1_g), row(ln1_b))
    ffn_w = (w_up, w_gate, conv_w, row(conv_b), w_down, row(ln2_g), row(ln2_b))

    hm = meta_tokens.astype(F32)
    hp = x_prompt
    hs = _to_group_major(x_sample, 0)
    pool_hist = _to_group_major(state_pool, 1)
    conv_hist = _to_group_major(state_conv, 1)

    pp, gp, cp, ps_l, cs_l = [], [], [], [], []
    gs = None
    for l in range(DEPTH):
        hm1, um, sm, hs1, ps_new, gs = _mixer_short(hm, hs, pool_hist, state_gla, mix_w, l, seq_len=LS, s_prev=gs)
        hm, cm, hs, cs_new, w_up_bf, w_gate_bf, w_down_bf = _ffn_short(hm1, hs1, conv_hist, ffn_w, l, seq_len=LS)
        ps_l.append(ps_new)
        cs_l.append(cs_new)

        hp1, pbuf, snew = _mixer_long(hp, um, sm, mix_w, l)
        hp, cbuf = _ffn_long(hp1, cm, (w_up_bf, w_gate_bf) + ffn_w[2:4] + (w_down_bf,) + ffn_w[5:], l)
        pp.append(pbuf)
        gp.append(snew)
        cp.append(cbuf)

    ps = _from_group_major(jnp.stack(ps_l), 1, POOL_BUF)
    cs = _from_group_major(jnp.stack(cs_l), 1, CONV_BUF)
    return (hp, _from_group_major(hs, 0, LS), jnp.stack(pp), jnp.stack(gp), jnp.stack(cp), ps, gs, cs)
```

```python
import functools
import itertools

import jax
import jax.numpy as jnp
from jax import lax
from jax.experimental import pallas as pl
from jax.experimental.pallas import tpu as pltpu

F32 = jnp.float32
BF16 = jnp.bfloat16

D_MODEL = 1024
N_META = 16
D_POOL = 512
POOL_WINDOWS = (2, 4, 8, 16)
POOL_GROUP = 128
POOL_BUF = 15
POOL_PAD = 16
D_GLA = 512
GLA_HEADS = 4
GLA_DV = 128
GLA_DK = 64
D_GLA_K = 256
GATE_RANK = 16
GATE_TAU = 16.0
D_FF = 2816
CONV_BUF = 2
CONV_PAD = 8
DEPTH = 2
ALPHA = (2 * DEPTH) ** 0.25
LN_EPS = 1e-5
RMS_EPS = 1e-6
PAST_LEN = 16384

C_POOL, C_Q, C_K, C_V, C_R, C_Z, C_END = 0, 512, 768, 1024, 1536, 2048, 2064

LONG_TILE = 512
LONG_CHUNK = 64
LONG_PAR = 2
MIXER_PAR = 2
MIXER_SKEW = 1
FFN_TILE = 512
SEQ_GROUP = 16
GROUPS_PER_STEP = 2
FF_CHUNK = 256
DOWN_ROWS = 256
CAST_ROWS = 128
SHORT_FFN_PARTS = 4
VMEM_LIMIT = 56 * 1024 * 1024

N_MIX_W = 9
N_FFN_W = 7


def _dot(a, b):
    return jnp.dot(a, b, preferred_element_type=F32)


def _dot_nt(a, b):
    return lax.dot_general(a, b, (((1,), (1,)), ((), ())), preferred_element_type=F32)


def _dot_tn(a, b):
    return lax.dot_general(a, b, (((0,), (0,)), ((), ())), preferred_element_type=F32)


def _layer_norm(y, g, b):
    mu = jnp.mean(y, axis=-1, keepdims=True)
    yc = y - mu
    var = jnp.mean(yc * yc, axis=-1, keepdims=True)
    return yc * lax.rsqrt(var + LN_EPS) * g + b


def _silu(x):
    h = 0.5 * x
    return h + h * jnp.tanh(h)


def _log_sigmoid(z):
    return jnp.minimum(z, 0.0) - jnp.log(1.0 + jnp.exp(-jnp.abs(z)))


def _roll_rows(x, shift):
    n = x.shape[0]
    return pltpu.roll(x, shift % n, 0)


def _split_bf16(x):
    hi = x.astype(BF16)
    lo = (x - hi.astype(F32)).astype(BF16)
    return hi, lo


def _project_in(xb, w_in_ref):
    u = _dot(xb, w_in_ref[:, C_POOL:C_Q])
    zr = _dot(xb, w_in_ref[:, C_Z:C_END])
    q = _dot(xb, w_in_ref[:, C_Q:C_K]) * (GLA_DK ** -0.5)
    k = _dot(xb, w_in_ref[:, C_K:C_V])
    v = _dot(xb, w_in_ref[:, C_V:C_R])
    r = _dot(xb, w_in_ref[:, C_R:C_Z])
    return u, zr, q, k, v, r


def _gate_log_decay(zr, w_a2_ref, b_a_ref):
    z = _dot(zr.astype(BF16), w_a2_ref[...]) + b_a_ref[...]
    return _log_sigmoid(z) * (1.0 / GATE_TAU)


def _gla_output_gate(o, r, gnorm):
    parts = []
    for h in range(GLA_HEADS):
        oh = o[:, h * GLA_DV:(h + 1) * GLA_DV]
        ms = jnp.mean(oh * oh, axis=-1, keepdims=True)
        parts.append(oh * lax.rsqrt(ms + RMS_EPS) * gnorm)
    return jnp.concatenate(parts, axis=1) * _silu(r)


def _pool_project(d_groups, w_pool_ref, pscale_ref):
    ys = [_dot(d.astype(BF16), w_pool_ref[g]) for g, d in enumerate(d_groups)]
    return jnp.concatenate(ys, axis=1) * pscale_ref[...]


def _mix_out(x, y_pool, y_gla, w_out_ref, g_ref, b_ref):
    mix = jnp.concatenate([y_pool, y_gla], axis=1).astype(BF16)
    return _layer_norm(ALPHA * x + _dot(mix, w_out_ref[...]), g_ref[...], b_ref[...])


def _emit_staggered(stage_gens, first_round):
    live = list(zip(first_round, stage_gens))
    rnd = 0
    while live:
        for start, g in list(live):
            if rnd >= start and next(g, StopIteration) is StopIteration:
                live.remove((start, g))
        rnd += 1


def _head_pair_keys(k_p, lane128):
    zk = jnp.zeros_like(k_p)
    return jnp.concatenate([jnp.where(lane128 < GLA_DK, k_p, zk), jnp.where(lane128 >= GLA_DK, k_p, zk)], axis=0)


def _head_pair_values(v_p, lane256):
    zv = jnp.zeros_like(v_p)
    return jnp.concatenate([jnp.where(lane256 < GLA_DV, v_p, zv), jnp.where(lane256 >= GLA_DV, v_p, zv)], axis=0)


def _mixer_long_kernel(*refs, n_par, tile, chunk):
    x_ref, pprev_ref, s0_ref = refs[:3]
    wts = refs[3:3 + N_MIX_W]
    x1_ref, pbuf_ref, snew_ref, ubuf, sbd = refs[3 + N_MIX_W:]
    t = pl.program_id(1)
    zero_blk = jnp.zeros((GLA_DK, GLA_DV), F32)

    @pl.when(t == 0)
    def _init():
        for j in range(n_par):
            ubuf[j, 0:POOL_PAD, :] = pprev_ref[...]
            for p in range(2):
                top = jnp.concatenate([s0_ref[2 * p], zero_blk], axis=1)
                bot = jnp.concatenate([zero_blk, s0_ref[2 * p + 1]], axis=1)
                sbd[j, p] = jnp.concatenate([top, bot], axis=0)

    tiles = [_mixer_long_tile(x_ref.at[j], wts, x1_ref.at[j], ubuf.at[j], sbd.at[j], tile=tile, chunk=chunk)
             for j in range(n_par)]
    _emit_staggered(tiles, [MIXER_SKEW * j for j in range(n_par)])

    @pl.when(t == pl.num_programs(1) - 1)
    def _final():
        for j in range(n_par):
            pbuf_ref[j] = ubuf[j, POOL_PAD - POOL_BUF:POOL_PAD, :]
            for p in range(2):
                s_p = sbd[j, p]
                snew_ref[j, 2 * p] = s_p[0:GLA_DK, 0:GLA_DV]
                snew_ref[j, 2 * p + 1] = s_p[GLA_DK:2 * GLA_DK, GLA_DV:2 * GLA_DV]


def _mixer_long_tile(x_ref, wts, x1_ref, ubuf, sbd, *, tile, chunk):
    w_in_ref, w_a2_ref, b_a_ref, w_pool_ref, pscale_ref, gnorm_ref, w_out_ref, g_ref, b_ref = wts
    T, C = tile, chunk
    x = x_ref[...]
    xb = x.astype(BF16)

    u, zr, q, k, v, r = _project_in(xb, w_in_ref)
    yield

    ubuf[POOL_PAD:POOL_PAD + T, :] = u
    d_groups = []
    for g, w in enumerate(POOL_WINDOWS):
        s = ubuf[:, g * POOL_GROUP:(g + 1) * POOL_GROUP]
        sh = 1
        while sh < w:
            s = s + _roll_rows(s, sh)
            sh *= 2
        d_groups.append(s[POOL_PAD:, :] * (1.0 / w) - u[:, g * POOL_GROUP:(g + 1) * POOL_GROUP])
    y_pool = _pool_project(d_groups, w_pool_ref, pscale_ref)
    ubuf[0:POOL_PAD, :] = ubuf[T:T + POOL_PAD, :]
    loga = _gate_log_decay(zr, w_a2_ref, b_a_ref)

    tr = lax.broadcasted_iota(jnp.int32, (C, C), 0)
    tc = lax.broadcasted_iota(jnp.int32, (C, C), 1)
    tri = jnp.where(tc <= tr, 1.0, 0.0).astype(BF16)
    ar = lax.broadcasted_iota(jnp.int32, (C, 2 * C), 0)
    ac = lax.broadcasted_iota(jnp.int32, (C, 2 * C), 1) & (C - 1)
    causal = ac <= ar
    lane128 = lax.broadcasted_iota(jnp.int32, (C, 128), 1)
    lane256 = lax.broadcasted_iota(jnp.int32, (C, 256), 1)
    sr = lax.broadcasted_iota(jnp.int32, (128, 256), 0)
    sc = lax.broadcasted_iota(jnp.int32, (128, 256), 1)
    blockdiag = (sr >= GLA_DK) == (sc >= GLA_DV)
    mid = C // 2 - 1

    n_chunks = T // C
    pairs = [(c, p) for c in range(n_chunks) for p in range(2)]
    ks = [slice(128 * p, 128 * (p + 1)) for p in range(2)]
    vs = [slice(256 * p, 256 * (p + 1)) for p in range(2)]

    bcs = []
    for c in range(n_chunks):
        la_hi, la_lo = _split_bf16(loga[c * C:(c + 1) * C])
        bb = _dot(tri, jnp.concatenate([la_hi, la_lo], axis=1))
        bcs.append(bb[:, :D_GLA_K] + bb[:, D_GLA_K:])
    yield

    q_in, k_in, q_st, k_st, dec_t, vc = [], [], [], [], [], []
    for c in range(n_chunks):
        bc = bcs[c]
        bmid = bc[mid:mid + 1]
        bend = bc[C - 1:C]
        qc = q[c * C:(c + 1) * C]
        kc = k[c * C:(c + 1) * C]
        q_in.append((qc * jnp.exp(bc - bmid)).astype(BF16))
        k_in.append((kc * jnp.exp(bmid - bc)).astype(BF16))
        q_st.append((qc * jnp.exp(bc)).astype(BF16))
        k_st.append((kc * jnp.exp(bend - bc)).astype(BF16))
        dec_t.append(jnp.transpose(jnp.broadcast_to(jnp.exp(bend), (128, D_GLA_K))))
        vc.append(v[c * C:(c + 1) * C].astype(BF16))
    yield

    attn, upd = {}, {}
    for c, p in pairs:
        a = _dot_nt(q_in[c][:, ks[p]], _head_pair_keys(k_in[c][:, ks[p]], lane128))
        attn[c, p] = jnp.where(causal, a, 0.0).astype(BF16)
    for c, p in pairs:
        u_cp = _dot_tn(k_st[c][:, ks[p]], vc[c][:, vs[p]])
        upd[c, p] = jnp.where(blockdiag, u_cp, 0.0)
    yield

    s_vals = [sbd[p] for p in range(2)]
    s_start = {}
    for c, p in pairs:
        s_start[c, p] = s_vals[p].astype(BF16)
        dec_p = dec_t[c][ks[p], :]
        s_vals[p] = jnp.concatenate([dec_p, dec_p], axis=1) * s_vals[p] + upd[c, p]
    for p in range(2):
        sbd[p] = s_vals[p]

    o_rows = [[], []]
    for c, p in pairs:
        vblk = _head_pair_values(vc[c][:, vs[p]], lane256)
        lhs = jnp.concatenate([attn[c, p], q_st[c][:, ks[p]]], axis=1)
        rhs = jnp.concatenate([vblk, s_start[c, p]], axis=0)
        o_rows[p].append(_dot(lhs, rhs))
    o = jnp.concatenate([jnp.concatenate(o_rows[p], axis=0) for p in range(2)], axis=1)
    yield

    y_gla = _gla_output_gate(o, r, gnorm_ref[...])
    x1_ref[...] = _mix_out(x, y_pool, y_gla, w_out_ref, g_ref, b_ref)


def _layer_spec(shape, layer):
    nd = len(shape)
    return pl.BlockSpec((None,) + shape, lambda *_: (layer,) + (0,) * nd, pipeline_mode=pl.Buffered(1))


def _mixer_weight_specs(layer, projections_stacked=True):
    shapes = [(D_MODEL, C_END), (GATE_RANK, D_GLA_K), (1, D_GLA_K),
              (4, POOL_GROUP, POOL_GROUP), (1, D_POOL), (1, GLA_DV), (D_MODEL, D_MODEL), (1, D_MODEL), (1, D_MODEL)]
    assert len(shapes) == N_MIX_W
    specs = [_layer_spec(s, layer) for s in shapes]
    if not projections_stacked:
        for i in (0, 6):
            specs[i] = pl.BlockSpec(shapes[i], lambda *_: (0, 0), pipeline_mode=pl.Buffered(1))
    return specs


def _mixer_long(x, pprev, s0, wts, layer):
    B, L, _ = x.shape
    T = min(LONG_TILE, L)
    P = MIXER_PAR if B % MIXER_PAR == 0 else 1
    assert L % T == 0 and T % LONG_CHUNK == 0
    kern = functools.partial(_mixer_long_kernel, n_par=P, tile=T, chunk=LONG_CHUNK)
    return pl.pallas_call(
        kern,
        grid=(B // P, L // T),
        in_specs=[pl.BlockSpec((P, T, D_MODEL), lambda b, t: (b, t, 0)),
                  pl.BlockSpec((POOL_PAD, D_POOL), lambda b, t: (0, 0)),
                  pl.BlockSpec((GLA_HEADS, GLA_DK, GLA_DV), lambda b, t: (0, 0, 0))]
        + _mixer_weight_specs(layer, projections_stacked=False),
        out_specs=[pl.BlockSpec((P, T, D_MODEL), lambda b, t: (b, t, 0)),
                   pl.BlockSpec((P, POOL_BUF, D_POOL), lambda b, t: (b, 0, 0)),
                   pl.BlockSpec((P, GLA_HEADS, GLA_DK, GLA_DV), lambda b, t: (b, 0, 0, 0))],
        out_shape=[jax.ShapeDtypeStruct((B, L, D_MODEL), F32),
                   jax.ShapeDtypeStruct((B, POOL_BUF, D_POOL), F32),
                   jax.ShapeDtypeStruct((B, GLA_HEADS, GLA_DK, GLA_DV), F32)],
        scratch_shapes=[pltpu.VMEM((P, T + POOL_PAD, D_POOL), F32),
                        pltpu.VMEM((P, 2, 2 * GLA_DK, 2 * GLA_DV), F32)],
        compiler_params=pltpu.CompilerParams(dimension_semantics=("arbitrary", "arbitrary"),
                                             vmem_limit_bytes=VMEM_LIMIT),
        name="mixer_long",
    )(x, pprev, s0, *wts)


def _mixer_meta(x_ref, wts, x1_ref, u_ref, s_ref):
    w_in_ref, w_a2_ref, b_a_ref, w_pool_ref, pscale_ref, gnorm_ref, w_out_ref, g_ref, b_ref = wts
    L = N_META
    x = x_ref[...]
    xb = x.astype(BF16)
    u, zr, q, k, v, r = _project_in(xb, w_in_ref)
    u_ref[...] = u

    row128 = lax.broadcasted_iota(jnp.int32, (L, POOL_GROUP), 0)
    pos1 = lax.broadcasted_iota(jnp.int32, (L, 1), 0)
    d_groups = []
    for g, w in enumerate(POOL_WINDOWS):
        ug = u[:, g * POOL_GROUP:(g + 1) * POOL_GROUP]
        s = ug
        sh = 1
        while sh < w:
            s = s + jnp.where(row128 >= sh, _roll_rows(s, sh), 0.0)
            sh *= 2
        d_groups.append(s / jnp.minimum(w, pos1 + 1).astype(F32) - ug)
    y_pool = _pool_project(d_groups, w_pool_ref, pscale_ref)

    loga = _gate_log_decay(zr, w_a2_ref, b_a_ref)
    row256 = lax.broadcasted_iota(jnp.int32, (L, D_GLA_K), 0)
    b = loga
    sh = 1
    while sh < L:
        b = b + jnp.where(row256 >= sh, _roll_rows(b, sh), 0.0)
        sh *= 2
    bend = b[L - 1:L]
    q_in = (q * jnp.exp(b)).astype(BF16)
    k_in = (k * jnp.exp(-b)).astype(BF16)
    k_st = (k * jnp.exp(bend - b)).astype(BF16)
    vb = v.astype(BF16)
    ar = lax.broadcasted_iota(jnp.int32, (L, 2 * L), 0)
    ac = lax.broadcasted_iota(jnp.int32, (L, 2 * L), 1) & (L - 1)
    causal = ac <= ar
    lane128 = lax.broadcasted_iota(jnp.int32, (L, 128), 1)
    lane256 = lax.broadcasted_iota(jnp.int32, (L, 256), 1)
    attn = []
    for p in range(2):
        ks = slice(128 * p, 128 * (p + 1))
        a = _dot_nt(q_in[:, ks], _head_pair_keys(k_in[:, ks], lane128))
        attn.append(jnp.where(causal, a, 0.0).astype(BF16))
    o_parts = []
    for p in range(2):
        ks = slice(128 * p, 128 * (p + 1))
        vs = slice(256 * p, 256 * (p + 1))
        s_p = _dot_tn(k_st[:, ks], vb[:, vs])
        s_ref[2 * p] = s_p[0:GLA_DK, 0:GLA_DV]
        s_ref[2 * p + 1] = s_p[GLA_DK:2 * GLA_DK, GLA_DV:2 * GLA_DV]
        o_parts.append(_dot(attn[p], _head_pair_values(vb[:, vs], lane256)))
    o = jnp.concatenate(o_parts, axis=1)

    y_gla = _gla_output_gate(o, r, gnorm_ref[...])
    x1_ref[...] = _mix_out(x, y_pool, y_gla, w_out_ref, g_ref, b_ref)


def _mixer_sample(x_ref, hist_ref, s0_ref, wts, x1_ref, hist_out_ref, snew_ref, *, seq_len):
    w_in_ref, w_a2_ref, b_a_ref, w_pool_ref, pscale_ref, gnorm_ref, w_out_ref, g_ref, b_ref = wts
    G = SEQ_GROUP
    R = G * seq_len
    n_groups = x_ref.shape[0] // R
    x = x_ref[...]
    xb = x.astype(BF16)
    u, zr, q, k, v, r = _project_in(xb, w_in_ref)
    loga = _gate_log_decay(zr, w_a2_ref, b_a_ref)
    results = []
    gens = []
    for gi in range(n_groups):
        rs = slice(gi * R, (gi + 1) * R)
        hist_rows = pl.ds(gi * G * POOL_BUF, G * POOL_BUF)
        seqs = pl.ds(gi * G, G)
        gens.append(_sample_group(u[rs], q[rs], k[rs], v[rs], loga[rs], hist_ref.at[hist_rows], s0_ref.at[seqs],
                                  hist_out_ref.at[hist_rows], snew_ref.at[seqs], results, seq_len=seq_len))
    for _ in itertools.zip_longest(*gens):
        pass
    d_groups = [jnp.concatenate([res[0][g] for res in results], axis=0) for g in range(len(POOL_WINDOWS))]
    o = jnp.concatenate([res[1] for res in results], axis=0)
    y_pool = _pool_project(d_groups, w_pool_ref, pscale_ref)
    y_gla = _gla_output_gate(o, r, gnorm_ref[...])
    x1_ref[...] = _mix_out(x, y_pool, y_gla, w_out_ref, g_ref, b_ref)


def _sample_group(u, q, k, v, loga, hist_ref, s0_ref, hist_out_ref, snew_ref, results, *, seq_len):
    G, Ls = SEQ_GROUP, seq_len
    R = G * Ls
    NS = G * GLA_DK
    g_shift = G.bit_length() - 1
    hist_out_ref[0:(POOL_BUF - Ls) * G, :] = hist_ref[R:POOL_BUF * G, :]
    hist_out_ref[(POOL_BUF - Ls) * G:POOL_BUF * G, :] = u

    def blk(a, t):
        return a[t * G:(t + 1) * G]

    d_groups = []
    for g, w in enumerate(POOL_WINDOWS):
        cols = slice(g * POOL_GROUP, (g + 1) * POOL_GROUP)
        ug = u[:, cols]
        suffix = [None]
        acc = None
        for m in range(1, min(w - 1, POOL_BUF) + 1):
            h = hist_ref[(POOL_BUF - m) * G:(POOL_BUF - m + 1) * G, cols]
            acc = h if acc is None else acc + h
            suffix.append(acc)
        parts = []
        for t in range(Ls):
            wsum = blk(ug, t)
            for j in range(max(0, t - w + 1), t):
                wsum = wsum + blk(ug, j)
            m = w - 1 - t
            if m > 0:
                wsum = wsum + suffix[m]
            parts.append(wsum * (1.0 / w) - blk(ug, t))
        d_groups.append(jnp.concatenate(parts, axis=0))

    b_t = [blk(loga, 0)]
    for t in range(1, Ls):
        b_t.append(b_t[-1] + blk(loga, t))
    b = jnp.concatenate(b_t, axis=0)
    bend = jnp.concatenate([b_t[-1]] * Ls, axis=0)
    q_in = (q * jnp.exp(b)).astype(BF16)
    k_in = (k * jnp.exp(-b)).astype(BF16)
    k_st = k * jnp.exp(bend - b)
    dec_hi, dec_lo = _split_bf16(jnp.exp(bend))
    tok = lax.broadcasted_iota(jnp.int32, (R, D_GLA_K), 0) >> g_shift
    dec_rows = jnp.where(tok == Ls - 1, dec_hi, jnp.where(tok == Ls - 2, dec_lo, jnp.zeros_like(dec_lo)))
    vb = v.astype(BF16)

    ar = lax.broadcasted_iota(jnp.int32, (R, 2 * R), 0)
    ac = lax.broadcasted_iota(jnp.int32, (R, 2 * R), 1) & (R - 1)
    same_seq_causal = ((ac & (G - 1)) == (ar & (G - 1))) & ((ac >> g_shift) <= (ar >> g_shift))
    lane128 = lax.broadcasted_iota(jnp.int32, (R, 128), 1)
    lane256 = lax.broadcasted_iota(jnp.int32, (R, 256), 1)
    br = lax.broadcasted_iota(jnp.int32, (R, NS), 0)
    bcol = lax.broadcasted_iota(jnp.int32, (R, NS), 1)
    own_state = (bcol >> 6) == (br & (G - 1))
    ones_blk = jnp.ones((R, GLA_DV), BF16)
    zeros_blk = jnp.zeros((R, GLA_DV), BF16)

    def expand(xp, first):
        sw = pltpu.roll(xp, GLA_DK, 1)
        two = jnp.where(lane128 < GLA_DK, xp, sw) if first else jnp.where(lane128 < GLA_DK, sw, xp)
        rep = jnp.concatenate([two] * (NS // 128), axis=1)
        return jnp.where(own_state, rep, 0.0).astype(BF16)

    ks = [slice(128 * p, 128 * (p + 1)) for p in range(2)]
    vs = [slice(256 * p, 256 * (p + 1)) for p in range(2)]
    attn = []
    for p in range(2):
        a = _dot_nt(q_in[:, ks[p]], _head_pair_keys(k_in[:, ks[p]], lane128))
        attn.append(jnp.where(same_seq_causal, a, 0.0).astype(BF16))
    yield
    inter = []
    for h in range(GLA_HEADS):
        p, first = h // 2, h % 2 == 0
        s_flat = s0_ref[:, h].reshape(NS, GLA_DV)
        inter.append(_dot(expand(q_in[:, ks[p]].astype(F32), first), s_flat.astype(BF16)))
    for h in range(GLA_HEADS):
        p, first = h // 2, h % 2 == 0
        s_flat = s0_ref[:, h].reshape(NS, GLA_DV)
        lhs = jnp.concatenate([expand(k_st[:, ks[p]], first),
                               expand(dec_rows[:, ks[p]].astype(F32), first)], axis=0)
        v_h = vb[:, h * GLA_DV:(h + 1) * GLA_DV]
        rhs = jnp.concatenate([jnp.concatenate([v_h, zeros_blk], axis=1),
                               jnp.concatenate([zeros_blk, ones_blk], axis=1)], axis=0)
        ud = _dot_tn(lhs, rhs)
        s_new = ud[:, GLA_DV:] * s_flat + ud[:, :GLA_DV]
        snew_ref[:, h] = s_new.reshape(G, GLA_DK, GLA_DV)
    yield
    o_parts = []
    for p in range(2):
        o_intra = _dot(attn[p], _head_pair_values(vb[:, vs[p]], lane256))
        o_parts.append(o_intra + jnp.concatenate(inter[2 * p:2 * p + 2], axis=1))
    results.append((d_groups, jnp.concatenate(o_parts, axis=1)))


def _mixer_short_kernel(*refs, seq_len, fill_slabs):
    xm_ref, xs_ref, hist_ref, s0_ref = refs[:4]
    w_in_f32_ref, w_a2_ref, b_a_ref, w_pool_ref, pscale_ref, gnorm_ref, w_out_f32_ref, g_ref, b_ref = refs[4:4 + N_MIX_W]
    x1m_ref, um_ref, sm_ref, x1s_ref, us_ref, ss_ref, w_in_ref, w_out_ref = refs[-8:]
    wts = (w_in_ref, w_a2_ref, b_a_ref, w_pool_ref, pscale_ref, gnorm_ref, w_out_ref, g_ref, b_ref)
    i = pl.program_id(0)

    @pl.when(i == 0)
    def _meta():
        for r0 in range(0, D_MODEL, CAST_ROWS):
            w_in_ref[r0:r0 + CAST_ROWS, :] = w_in_f32_ref[r0:r0 + CAST_ROWS, :].astype(BF16)
            w_out_ref[r0:r0 + CAST_ROWS, :] = w_out_f32_ref[r0:r0 + CAST_ROWS, :].astype(BF16)
        _mixer_meta(xm_ref, wts, x1m_ref, um_ref, sm_ref)

    @pl.when(i > 0)
    def _sample():
        snew_ref = ss_ref.at[0] if fill_slabs else ss_ref
        _mixer_sample(xs_ref, hist_ref, s0_ref, wts, x1s_ref, us_ref, snew_ref, seq_len=seq_len)
        for a in range(1, fill_slabs):
            ss_ref[a] = ss_ref[0]


def _mixer_short(x_meta, x_samp, hist, s0, wts, layer, *, seq_len, s_stack=None):
    rows = x_samp.shape[0]
    n_groups = rows // (SEQ_GROUP * seq_len)
    per_step = GROUPS_PER_STEP if n_groups % GROUPS_PER_STEP == 0 else 1
    G = SEQ_GROUP * per_step
    R = G * seq_len
    n_tiles = rows // R
    assert rows % R == 0
    clamp = lambda i: jnp.maximum(i - 1, 0)
    n_layers = s0.shape[0]
    state_blk = (G, GLA_HEADS, GLA_DK, GLA_DV)
    wspecs = _mixer_weight_specs(layer)
    in_specs = ([pl.BlockSpec((N_META, D_MODEL), lambda i: (0, 0)),
                 pl.BlockSpec((R, D_MODEL), lambda i: (clamp(i), 0)),
                 pl.BlockSpec((None, G * POOL_BUF, D_POOL), lambda i: (layer, clamp(i), 0)),
                 pl.BlockSpec((None,) + state_blk, lambda i: (layer, clamp(i), 0, 0, 0))] + wspecs)
    args = [x_meta, x_samp, hist, s0, *wts]
    aliases = {}
    if s_stack is None:
        assert layer == 0
        state_spec = pl.BlockSpec((n_layers,) + state_blk, lambda i: (0, clamp(i), 0, 0, 0))
    else:
        state_spec = pl.BlockSpec((None,) + state_blk, lambda i: (layer, clamp(i), 0, 0, 0))
        in_specs.append(pl.BlockSpec(memory_space=pl.ANY))
        aliases = {len(args): 5}
        args.append(s_stack)
    n_args = len(args)
    body = functools.partial(_mixer_short_kernel, seq_len=seq_len, fill_slabs=n_layers if s_stack is None else 0)

    def kern(*refs):
        body(*refs[:4 + N_MIX_W], *refs[n_args:])

    whole = lambda shape: pl.BlockSpec(shape, lambda i: (0,) * len(shape))
    return pl.pallas_call(
        kern,
        grid=(n_tiles + 1,),
        in_specs=in_specs,
        out_specs=[whole((N_META, D_MODEL)),
                   whole((N_META, D_POOL)),
                   whole((GLA_HEADS, GLA_DK, GLA_DV)),
                   pl.BlockSpec((R, D_MODEL), lambda i: (clamp(i), 0)),
                   pl.BlockSpec((G * POOL_BUF, D_POOL), lambda i: (clamp(i), 0)),
                   state_spec,
                   whole((D_MODEL, C_END)),
                   whole((D_MODEL, D_MODEL))],
        out_shape=[jax.ShapeDtypeStruct((N_META, D_MODEL), F32),
                   jax.ShapeDtypeStruct((N_META, D_POOL), F32),
                   jax.ShapeDtypeStruct((GLA_HEADS, GLA_DK, GLA_DV), F32),
                   jax.ShapeDtypeStruct((rows, D_MODEL), F32),
                   jax.ShapeDtypeStruct((n_tiles * G * POOL_BUF, D_POOL), F32),
                   jax.ShapeDtypeStruct(s0.shape, F32),
                   jax.ShapeDtypeStruct((D_MODEL, C_END), BF16),
                   jax.ShapeDtypeStruct((D_MODEL, D_MODEL), BF16)],
        input_output_aliases=aliases,
        compiler_params=pltpu.CompilerParams(dimension_semantics=("arbitrary",),
                                             vmem_limit_bytes=VMEM_LIMIT),
        name="mixer_short",
    )(*args)


def _ffn_tile(x_ref, wts, y_ref, conv_inputs, store_gate):
    w_up_ref, w_gate_ref, cw_ref, cb_ref, w_down_ref, g_ref, b_ref = wts
    x = x_ref[...]
    xb = x.astype(BF16)

    acts = []
    for j in range(D_FF // FF_CHUNK):
        cs = slice(j * FF_CHUNK, (j + 1) * FF_CHUNK)
        a = _dot(xb, w_up_ref[j])
        gt = _dot(xb, w_gate_ref[j])
        g1, g2 = conv_inputs(gt, cs)
        store_gate(gt, cs)
        gc = cb_ref[:, cs] + cw_ref[0:1, cs] * g2 + cw_ref[1:2, cs] * g1 + cw_ref[2:3, cs] * gt
        acts.append((a * _silu(gc)).astype(BF16))
    yield

    act = jnp.concatenate(acts, axis=1)
    rows = x.shape[0]
    rb = min(rows, DOWN_ROWS)
    for r0 in range(0, rows, rb):
        f = _dot(act[r0:r0 + rb], w_down_ref[...])
        y_ref[r0:r0 + rb, :] = _layer_norm(ALPHA * x[r0:r0 + rb] + f, g_ref[...], b_ref[...])


def _ffn_long_kernel(*refs, n_par):
    x_ref, cprev_ref = refs[:2]
    wts = refs[2:2 + N_FFN_W]
    y_ref, hist_out_ref, gbuf = refs[2 + N_FFN_W:]
    t = pl.program_id(1)
    T = x_ref.shape[1]

    @pl.when(t == 0)
    def _init():
        for j in range(n_par):
            gbuf[j, 0:CONV_PAD, :] = cprev_ref[...]

    def make_tile(j):
        def conv_inputs(gt, cs):
            gbuf[j, CONV_PAD:CONV_PAD + T, cs] = gt
            return gbuf[j, CONV_PAD - 1:CONV_PAD - 1 + T, cs], gbuf[j, CONV_PAD - 2:CONV_PAD - 2 + T, cs]

        def store_gate(gt, cs):
            gbuf[j, 0:CONV_PAD, cs] = gbuf[j, T:T + CONV_PAD, cs]

        return _ffn_tile(x_ref.at[j], wts, y_ref.at[j], conv_inputs, store_gate)

    for _ in itertools.zip_longest(*[make_tile(j) for j in range(n_par)]):
        pass

    @pl.when(t == pl.num_programs(1) - 1)
    def _final():
        for j in range(n_par):
            hist_out_ref[j] = gbuf[j, CONV_PAD - CONV_BUF:CONV_PAD, :]


def _ffn_short_kernel(xm_ref, xs_ref, hist_ref, w_up_ref, w_gate_ref, cw_ref, cb_ref, w_down_ref, g_ref, b_ref,
                      ym_ref, cm_ref, ys_ref, cs_ref, w_up_bf_ref, w_gate_bf_ref, w_down_bf_ref,
                      fm_ref, fs_ref, *, seq_len):
    j = pl.program_id(0)
    G, Ls = SEQ_GROUP, seq_len
    R = G * Ls
    rows = xs_ref.shape[0]
    n_tiles = rows // R

    w_up = w_up_ref[...].astype(BF16)
    w_gate = w_gate_ref[...].astype(BF16)
    w_down = w_down_ref[...].astype(BF16)
    w_up_bf_ref[...] = w_up
    w_gate_bf_ref[...] = w_gate
    w_down_bf_ref[...] = w_down

    @pl.when(j == 0)
    def _first():
        fm_ref[...] = jnp.zeros_like(fm_ref)
        fs_ref[...] = jnp.zeros_like(fs_ref)

    n_parts = SHORT_FFN_PARTS if n_tiles % SHORT_FFN_PARTS == 0 else 1
    part_tiles = n_tiles // n_parts
    part_rows = part_tiles * R
    hist_rows = CONV_BUF * G
    tok = (lax.broadcasted_iota(jnp.int32, (part_rows, FF_CHUNK), 0) & (R - 1)) >> (G.bit_length() - 1)
    zeros_tail = jnp.zeros((R - hist_rows, FF_CHUNK), F32)

    def sample_conv(gt, part):
        first = part * part_tiles
        hx = jnp.concatenate([piece for n in range(first, first + part_tiles)
                              for piece in (hist_ref[n * hist_rows:(n + 1) * hist_rows, :], zeros_tail)], axis=0)
        g1 = jnp.where(tok >= 1, _roll_rows(gt, G), _roll_rows(hx, -G))
        g2 = jnp.where(tok >= 2, _roll_rows(gt, 2 * G), hx)
        return g1, g2

    def sample_store(gt, part):
        for n in range(part_tiles):
            dst = (part * part_tiles + n) * hist_rows
            cs_ref[dst:dst + hist_rows, :] = gt[n * R + (Ls - CONV_BUF) * G:(n + 1) * R]

    rowm = lax.broadcasted_iota(jnp.int32, (N_META, FF_CHUNK), 0)

    def meta_conv(gt):
        return (jnp.where(rowm >= 1, _roll_rows(gt, 1), 0.0), jnp.where(rowm >= 2, _roll_rows(gt, 2), 0.0))

    def meta_store(gt):
        cm_ref[...] = gt[N_META - CONV_PAD:N_META]

    parts = [(xs_ref.at[pl.ds(p * part_rows, part_rows)], fs_ref.at[pl.ds(p * part_rows, part_rows)],
              functools.partial(sample_conv, part=p), functools.partial(sample_store, part=p))
             for p in range(n_parts)]
    parts.append((xm_ref, fm_ref, meta_conv, meta_store))

    proj = []
    for x_ref, _, _, _ in parts:
        xb = x_ref[...].astype(BF16)
        proj.append((_dot(xb, w_up), _dot(xb, w_gate)))
    acts = []
    for (a, gt), (_, _, conv_inputs, store_gate) in zip(proj, parts):
        g1, g2 = conv_inputs(gt)
        store_gate(gt)
        gc = cb_ref[...] + cw_ref[0:1, :] * g2 + cw_ref[1:2, :] * g1 + cw_ref[2:3, :] * gt
        acts.append((a * _silu(gc)).astype(BF16))
    for act, (_, f_ref, _, _) in zip(acts, parts):
        f_ref[...] += _dot(act, w_down)

    @pl.when(j == pl.num_programs(0) - 1)
    def _last():
        ys_ref[...] = _layer_norm(ALPHA * xs_ref[...] + fs_ref[...], g_ref[...], b_ref[...])
        ym_ref[...] = _layer_norm(ALPHA * xm_ref[...] + fm_ref[...], g_ref[...], b_ref[...])


def _ffn_weight_specs(layer):
    whole = lambda shape: pl.BlockSpec(shape, lambda *_: (0,) * len(shape), pipeline_mode=pl.Buffered(1))
    chunked = (D_FF // FF_CHUNK, D_MODEL, FF_CHUNK)
    specs = [whole(chunked), whole(chunked), _layer_spec((3, D_FF), layer), _layer_spec((1, D_FF), layer),
             whole((D_FF, D_MODEL)), _layer_spec((1, D_MODEL), layer), _layer_spec((1, D_MODEL), layer)]
    assert len(specs) == N_FFN_W
    return specs


def _ffn_long(x, cprev, wts, layer):
    B, L, _ = x.shape
    T = min(FFN_TILE, L)
    P = LONG_PAR if B % LONG_PAR == 0 else 1
    assert L % T == 0
    kern = functools.partial(_ffn_long_kernel, n_par=P)
    return pl.pallas_call(
        kern,
        grid=(B // P, L // T),
        in_specs=[pl.BlockSpec((P, T, D_MODEL), lambda b, t: (b, t, 0)),
                  pl.BlockSpec((CONV_PAD, D_FF), lambda b, t: (0, 0))] + _ffn_weight_specs(layer),
        out_specs=[pl.BlockSpec((P, T, D_MODEL), lambda b, t: (b, t, 0)),
                   pl.BlockSpec((P, CONV_BUF, D_FF), lambda b, t: (b, 0, 0))],
        out_shape=[jax.ShapeDtypeStruct((B, L, D_MODEL), F32),
                   jax.ShapeDtypeStruct((B, CONV_BUF, D_FF), F32)],
        scratch_shapes=[pltpu.VMEM((P, T + CONV_PAD, D_FF), F32)],
        compiler_params=pltpu.CompilerParams(dimension_semantics=("arbitrary", "arbitrary"),
                                             vmem_limit_bytes=VMEM_LIMIT),
        name="ffn_long",
    )(x, cprev, *wts)


def _ffn_short(x_meta, x_samp, hist, wts, layer, *, seq_len):
    rows = x_samp.shape[0]
    n_hist = hist.shape[1]
    C = FF_CHUNK
    const = lambda shape: pl.BlockSpec(shape, lambda j: (0,) * len(shape))
    kern = functools.partial(_ffn_short_kernel, seq_len=seq_len)
    return pl.pallas_call(
        kern,
        grid=(D_FF // C,),
        in_specs=[const((N_META, D_MODEL)),
                  const((rows, D_MODEL)),
                  pl.BlockSpec((None, n_hist, C), lambda j: (layer, 0, j)),
                  pl.BlockSpec((None, D_MODEL, C), lambda j: (layer, 0, j)),
                  pl.BlockSpec((None, D_MODEL, C), lambda j: (layer, 0, j)),
                  pl.BlockSpec((None, 3, C), lambda j: (layer, 0, j)),
                  pl.BlockSpec((None, 1, C), lambda j: (layer, 0, j)),
                  pl.BlockSpec((None, C, D_MODEL), lambda j: (layer, j, 0)),
                  pl.BlockSpec((None, 1, D_MODEL), lambda j: (layer, 0, 0)),
                  pl.BlockSpec((None, 1, D_MODEL), lambda j: (layer, 0, 0))],
        out_specs=[const((N_META, D_MODEL)),
                   pl.BlockSpec((CONV_PAD, C), lambda j: (0, j)),
                   const((rows, D_MODEL)),
                   pl.BlockSpec((n_hist, C), lambda j: (0, j)),
                   pl.BlockSpec((None, D_MODEL, C), lambda j: (j, 0, 0)),
                   pl.BlockSpec((None, D_MODEL, C), lambda j: (j, 0, 0)),
                   pl.BlockSpec((C, D_MODEL), lambda j: (j, 0))],
        out_shape=[jax.ShapeDtypeStruct((N_META, D_MODEL), F32),
                   jax.ShapeDtypeStruct((CONV_PAD, D_FF), F32),
                   jax.ShapeDtypeStruct((rows, D_MODEL), F32),
                   jax.ShapeDtypeStruct((n_hist, D_FF), F32),
                   jax.ShapeDtypeStruct((D_FF // C, D_MODEL, C), BF16),
                   jax.ShapeDtypeStruct((D_FF // C, D_MODEL, C), BF16),
                   jax.ShapeDtypeStruct((D_FF, D_MODEL), BF16)],
        scratch_shapes=[pltpu.VMEM((N_META, D_MODEL), F32),
                        pltpu.VMEM((rows, D_MODEL), F32)],
        compiler_params=pltpu.CompilerParams(dimension_semantics=("arbitrary",),
                                             vmem_limit_bytes=VMEM_LIMIT),
        name="ffn_short",
    )(x_meta, x_samp, hist, *wts)


def _to_group_major(a, axis):
    n, j = a.shape[axis], a.shape[axis + 1]
    lead, tail = a.shape[:axis], a.shape[axis + 2:]
    a = a.reshape(*lead, n // SEQ_GROUP, SEQ_GROUP, j, *tail)
    a = jnp.swapaxes(a, axis + 1, axis + 2)
    return a.reshape(*lead, n * j, *tail)


def _from_group_major(a, axis, j):
    rows = a.shape[axis]
    n = rows // j
    lead, tail = a.shape[:axis], a.shape[axis + 1:]
    a = a.reshape(*lead, n // SEQ_GROUP, j, SEQ_GROUP, *tail)
    a = jnp.swapaxes(a, axis + 1, axis + 2)
    return a.reshape(*lead, n, j, *tail)


def kernel(x_prompt, x_sample, state_pool, state_gla, state_conv, meta_tokens,
           w_in, w_a2, b_a, w_pool, pool_scale, gla_norm, w_out, ln1_g, ln1_b,
           w_up, w_gate, conv_w, conv_b, w_down, ln2_g, ln2_b):
    NB, LS = x_sample.shape[0], x_sample.shape[1]
    assert NB % SEQ_GROUP == 0 and LS & (LS - 1) == 0 and CONV_BUF <= LS <= POOL_BUF

    row = lambda a: a.reshape(DEPTH, 1, a.shape[-1])
    mix_w = (w_in, w_a2.astype(BF16), row(b_a), w_pool.astype(BF16), row(pool_scale), row(gla_norm),
             w_out, row(ln1_g), row(ln1_b))
    ffn_w = (w_up, w_gate, conv_w, row(conv_b), w_down, row(ln2_g), row(ln2_b))

    hm = meta_tokens.astype(F32)
    hp = x_prompt
    hs = _to_group_major(x_sample, 0)
    pool_hist = _to_group_major(state_pool, 1)
    conv_hist = _to_group_major(state_conv, 1)

    pp, gp, cp, ps_l, cs_l = [], [], [], [], []
    gs = None
    for l in range(DEPTH):
        hm1, um, sm, hs1, ps_new, gs, w_in_bf, w_out_bf = _mixer_short(hm, hs, pool_hist, state_gla, mix_w, l,
                                                                       seq_len=LS, s_stack=gs)
        hm, cm, hs, cs_new, w_up_bf, w_gate_bf, w_down_bf = _ffn_short(hm1, hs1, conv_hist, ffn_w, l, seq_len=LS)
        ps_l.append(ps_new)
        cs_l.append(cs_new)

        hp1, pbuf, snew = _mixer_long(hp, um, sm, (w_in_bf,) + mix_w[1:6] + (w_out_bf,) + mix_w[7:], l)
        hp, cbuf = _ffn_long(hp1, cm, (w_up_bf, w_gate_bf) + ffn_w[2:4] + (w_down_bf,) + ffn_w[5:], l)
        pp.append(pbuf)
        gp.append(snew)
        cp.append(cbuf)

    ps = _from_group_major(jnp.stack(ps_l), 1, POOL_BUF)
    cs = _from_group_major(jnp.stack(cs_l), 1, CONV_BUF)
    return (hp, _from_group_major(hs, 0, LS), jnp.stack(pp), jnp.stack(gp), jnp.stack(cp), ps, gs, cs)
```

```python
import functools
import itertools

import jax
import jax.numpy as jnp
from jax import lax
from jax.experimental import pallas as pl
from jax.experimental.pallas import tpu as pltpu

F32 = jnp.float32
BF16 = jnp.bfloat16

D_MODEL = 1024
N_META = 16
D_POOL = 512
POOL_WINDOWS = (2, 4, 8, 16)
POOL_GROUP = 128
POOL_BUF = 15
POOL_PAD = 16
D_GLA = 512
GLA_HEADS = 4
GLA_DV = 128
GLA_DK = 64
D_GLA_K = 256
GATE_RANK = 16
GATE_TAU = 16.0
D_FF = 2816
CONV_BUF = 2
CONV_PAD = 8
DEPTH = 2
ALPHA = (2 * DEPTH) ** 0.25
LN_EPS = 1e-5
RMS_EPS = 1e-6
PAST_LEN = 16384

C_POOL, C_Q, C_K, C_V, C_R, C_Z, C_END = 0, 512, 768, 1024, 1536, 2048, 2064

LONG_TILE = 512
LONG_CHUNK = 64
LONG_PAR = 2
MIXER_PAR = 2
MIXER_SKEW = 1
FFN_TILE = 512
SEQ_GROUP = 16
GROUPS_PER_STEP = 2
FF_CHUNK = 256
DOWN_ROWS = 256
CAST_ROWS = 128
SHORT_FFN_PARTS = 4
VMEM_LIMIT = 56 * 1024 * 1024

N_MIX_W = 10
N_FFN_W = 7


def _dot(a, b):
    return jnp.dot(a, b, preferred_element_type=F32)


def _dot_nt(a, b):
    return lax.dot_general(a, b, (((1,), (1,)), ((), ())), preferred_element_type=F32)


def _dot_tn(a, b):
    return lax.dot_general(a, b, (((0,), (0,)), ((), ())), preferred_element_type=F32)


def _layer_norm(y, g, b):
    mu = jnp.mean(y, axis=-1, keepdims=True)
    yc = y - mu
    var = jnp.mean(yc * yc, axis=-1, keepdims=True)
    return yc * lax.rsqrt(var + LN_EPS) * g + b


def _silu(x):
    h = 0.5 * x
    return h + h * jnp.tanh(h)


def _log_sigmoid(z):
    return jnp.minimum(z, 0.0) - jnp.log(1.0 + jnp.exp(-jnp.abs(z)))


def _roll_rows(x, shift):
    n = x.shape[0]
    return pltpu.roll(x, shift % n, 0)


def _split_bf16(x):
    hi = x.astype(BF16)
    lo = (x - hi.astype(F32)).astype(BF16)
    return hi, lo


def _project_in(xb, w_in_ref, w_zr_ref):
    u = _dot(xb, w_in_ref[:, C_POOL:C_Q])
    zr = _dot(xb, w_zr_ref[...])
    q = _dot(xb, w_in_ref[:, C_Q:C_K]) * (GLA_DK ** -0.5)
    k = _dot(xb, w_in_ref[:, C_K:C_V])
    v = _dot(xb, w_in_ref[:, C_V:C_R])
    r = _dot(xb, w_in_ref[:, C_R:C_Z])
    return u, zr, q, k, v, r


def _gate_log_decay(zr, w_a2_ref, b_a_ref):
    z = _dot(zr.astype(BF16), w_a2_ref[...]) + b_a_ref[...]
    return _log_sigmoid(z) * (1.0 / GATE_TAU)


def _gla_output_gate(o, r, gnorm):
    parts = []
    for h in range(GLA_HEADS):
        oh = o[:, h * GLA_DV:(h + 1) * GLA_DV]
        ms = jnp.mean(oh * oh, axis=-1, keepdims=True)
        parts.append(oh * lax.rsqrt(ms + RMS_EPS) * gnorm)
    return jnp.concatenate(parts, axis=1) * _silu(r)


def _pool_project(d_groups, w_pool_ref, pscale_ref):
    ys = [_dot(d.astype(BF16), w_pool_ref[g]) for g, d in enumerate(d_groups)]
    return jnp.concatenate(ys, axis=1) * pscale_ref[...]


def _mix_out(x, y_pool, y_gla, w_out_ref, g_ref, b_ref):
    mix = jnp.concatenate([y_pool, y_gla], axis=1).astype(BF16)
    return _layer_norm(ALPHA * x + _dot(mix, w_out_ref[...]), g_ref[...], b_ref[...])


def _emit_staggered(stage_gens, first_round):
    live = list(zip(first_round, stage_gens))
    rnd = 0
    while live:
        for start, g in list(live):
            if rnd >= start and next(g, StopIteration) is StopIteration:
                live.remove((start, g))
        rnd += 1


def _head_pair_keys(k_p, lane128):
    zk = jnp.zeros_like(k_p)
    return jnp.concatenate([jnp.where(lane128 < GLA_DK, k_p, zk), jnp.where(lane128 >= GLA_DK, k_p, zk)], axis=0)


def _head_pair_values(v_p, lane256):
    zv = jnp.zeros_like(v_p)
    return jnp.concatenate([jnp.where(lane256 < GLA_DV, v_p, zv), jnp.where(lane256 >= GLA_DV, v_p, zv)], axis=0)


def _mixer_long_kernel(*refs, n_par, tile, chunk):
    x_ref, pprev_ref, s0_ref = refs[:3]
    wts = refs[3:3 + N_MIX_W]
    x1_ref, pbuf_ref, snew_ref, ubuf, sbd = refs[3 + N_MIX_W:]
    t = pl.program_id(1)
    zero_blk = jnp.zeros((GLA_DK, GLA_DV), F32)

    @pl.when(t == 0)
    def _init():
        for j in range(n_par):
            ubuf[j, 0:POOL_PAD, :] = pprev_ref[...]
            for p in range(2):
                top = jnp.concatenate([s0_ref[2 * p], zero_blk], axis=1)
                bot = jnp.concatenate([zero_blk, s0_ref[2 * p + 1]], axis=1)
                sbd[j, p] = jnp.concatenate([top, bot], axis=0)

    tiles = [_mixer_long_tile(x_ref.at[j], wts, x1_ref.at[j], ubuf.at[j], sbd.at[j], tile=tile, chunk=chunk)
             for j in range(n_par)]
    _emit_staggered(tiles, [MIXER_SKEW * j for j in range(n_par)])

    @pl.when(t == pl.num_programs(1) - 1)
    def _final():
        for j in range(n_par):
            pbuf_ref[j] = ubuf[j, POOL_PAD - POOL_BUF:POOL_PAD, :]
            for p in range(2):
                s_p = sbd[j, p]
                snew_ref[j, 2 * p] = s_p[0:GLA_DK, 0:GLA_DV]
                snew_ref[j, 2 * p + 1] = s_p[GLA_DK:2 * GLA_DK, GLA_DV:2 * GLA_DV]


def _mixer_long_tile(x_ref, wts, x1_ref, ubuf, sbd, *, tile, chunk):
    w_in_ref, w_zr_ref, w_a2_ref, b_a_ref, w_pool_ref, pscale_ref, gnorm_ref, w_out_ref, g_ref, b_ref = wts
    T, C = tile, chunk
    x = x_ref[...]
    xb = x.astype(BF16)

    u, zr, q, k, v, r = _project_in(xb, w_in_ref, w_zr_ref)
    yield

    ubuf[POOL_PAD:POOL_PAD + T, :] = u
    d_groups = []
    for g, w in enumerate(POOL_WINDOWS):
        s = ubuf[:, g * POOL_GROUP:(g + 1) * POOL_GROUP]
        sh = 1
        while sh < w:
            s = s + _roll_rows(s, sh)
            sh *= 2
        d_groups.append(s[POOL_PAD:, :] * (1.0 / w) - u[:, g * POOL_GROUP:(g + 1) * POOL_GROUP])
    y_pool = _pool_project(d_groups, w_pool_ref, pscale_ref)
    ubuf[0:POOL_PAD, :] = ubuf[T:T + POOL_PAD, :]
    loga = _gate_log_decay(zr, w_a2_ref, b_a_ref)

    tr = lax.broadcasted_iota(jnp.int32, (C, C), 0)
    tc = lax.broadcasted_iota(jnp.int32, (C, C), 1)
    tri = jnp.where(tc <= tr, 1.0, 0.0).astype(BF16)
    ar = lax.broadcasted_iota(jnp.int32, (C, 2 * C), 0)
    ac = lax.broadcasted_iota(jnp.int32, (C, 2 * C), 1) & (C - 1)
    causal = ac <= ar
    lane128 = lax.broadcasted_iota(jnp.int32, (C, 128), 1)
    lane256 = lax.broadcasted_iota(jnp.int32, (C, 256), 1)
    sr = lax.broadcasted_iota(jnp.int32, (128, 256), 0)
    sc = lax.broadcasted_iota(jnp.int32, (128, 256), 1)
    blockdiag = (sr >= GLA_DK) == (sc >= GLA_DV)
    mid = C // 2 - 1

    n_chunks = T // C
    pairs = [(c, p) for c in range(n_chunks) for p in range(2)]
    ks = [slice(128 * p, 128 * (p + 1)) for p in range(2)]
    vs = [slice(256 * p, 256 * (p + 1)) for p in range(2)]

    bcs = []
    for c in range(n_chunks):
        la_hi, la_lo = _split_bf16(loga[c * C:(c + 1) * C])
        bb = _dot(tri, jnp.concatenate([la_hi, la_lo], axis=1))
        bcs.append(bb[:, :D_GLA_K] + bb[:, D_GLA_K:])
    yield

    q_in, k_in, q_st, k_st, dec_t, vc = [], [], [], [], [], []
    for c in range(n_chunks):
        bc = bcs[c]
        bmid = bc[mid:mid + 1]
        bend = bc[C - 1:C]
        qc = q[c * C:(c + 1) * C]
        kc = k[c * C:(c + 1) * C]
        q_in.append((qc * jnp.exp(bc - bmid)).astype(BF16))
        k_in.append((kc * jnp.exp(bmid - bc)).astype(BF16))
        q_st.append((qc * jnp.exp(bc)).astype(BF16))
        k_st.append((kc * jnp.exp(bend - bc)).astype(BF16))
        dec_t.append(jnp.transpose(jnp.broadcast_to(jnp.exp(bend), (128, D_GLA_K))))
        vc.append(v[c * C:(c + 1) * C].astype(BF16))
    yield

    attn, upd = {}, {}
    for c, p in pairs:
        a = _dot_nt(q_in[c][:, ks[p]], _head_pair_keys(k_in[c][:, ks[p]], lane128))
        attn[c, p] = jnp.where(causal, a, 0.0).astype(BF16)
    for c, p in pairs:
        u_cp = _dot_tn(k_st[c][:, ks[p]], vc[c][:, vs[p]])
        upd[c, p] = jnp.where(blockdiag, u_cp, 0.0)
    yield

    s_vals = [sbd[p] for p in range(2)]
    s_start = {}
    for c, p in pairs:
        s_start[c, p] = s_vals[p].astype(BF16)
        dec_p = dec_t[c][ks[p], :]
        s_vals[p] = jnp.concatenate([dec_p, dec_p], axis=1) * s_vals[p] + upd[c, p]
    for p in range(2):
        sbd[p] = s_vals[p]

    o_rows = [[], []]
    for c, p in pairs:
        vblk = _head_pair_values(vc[c][:, vs[p]], lane256)
        lhs = jnp.concatenate([attn[c, p], q_st[c][:, ks[p]]], axis=1)
        rhs = jnp.concatenate([vblk, s_start[c, p]], axis=0)
        o_rows[p].append(_dot(lhs, rhs))
    o = jnp.concatenate([jnp.concatenate(o_rows[p], axis=0) for p in range(2)], axis=1)
    yield

    y_gla = _gla_output_gate(o, r, gnorm_ref[...])
    x1_ref[...] = _mix_out(x, y_pool, y_gla, w_out_ref, g_ref, b_ref)


def _layer_spec(shape, layer):
    nd = len(shape)
    return pl.BlockSpec((None,) + shape, lambda *_: (layer,) + (0,) * nd, pipeline_mode=pl.Buffered(1))


def _mixer_weight_specs(layer, projections_stacked=True):
    shapes = [(C_END, D_MODEL) if projections_stacked else (D_MODEL, C_Z), (D_MODEL, GATE_RANK),
              (GATE_RANK, D_GLA_K), (1, D_GLA_K), (4, POOL_GROUP, POOL_GROUP), (1, D_POOL), (1, GLA_DV),
              (D_MODEL, D_MODEL), (1, D_MODEL), (1, D_MODEL)]
    assert len(shapes) == N_MIX_W
    specs = [_layer_spec(s, layer) for s in shapes]
    if not projections_stacked:
        for i in (0, 7):
            specs[i] = pl.BlockSpec(shapes[i], lambda *_: (0, 0), pipeline_mode=pl.Buffered(1))
    return specs


def _mixer_long(x, pprev, s0, wts, layer):
    B, L, _ = x.shape
    T = min(LONG_TILE, L)
    P = MIXER_PAR if B % MIXER_PAR == 0 else 1
    assert L % T == 0 and T % LONG_CHUNK == 0
    kern = functools.partial(_mixer_long_kernel, n_par=P, tile=T, chunk=LONG_CHUNK)
    return pl.pallas_call(
        kern,
        grid=(B // P, L // T),
        in_specs=[pl.BlockSpec((P, T, D_MODEL), lambda b, t: (b, t, 0)),
                  pl.BlockSpec((POOL_PAD, D_POOL), lambda b, t: (0, 0)),
                  pl.BlockSpec((GLA_HEADS, GLA_DK, GLA_DV), lambda b, t: (0, 0, 0))]
        + _mixer_weight_specs(layer, projections_stacked=False),
        out_specs=[pl.BlockSpec((P, T, D_MODEL), lambda b, t: (b, t, 0)),
                   pl.BlockSpec((P, POOL_BUF, D_POOL), lambda b, t: (b, 0, 0)),
                   pl.BlockSpec((P, GLA_HEADS, GLA_DK, GLA_DV), lambda b, t: (b, 0, 0, 0))],
        out_shape=[jax.ShapeDtypeStruct((B, L, D_MODEL), F32),
                   jax.ShapeDtypeStruct((B, POOL_BUF, D_POOL), F32),
                   jax.ShapeDtypeStruct((B, GLA_HEADS, GLA_DK, GLA_DV), F32)],
        scratch_shapes=[pltpu.VMEM((P, T + POOL_PAD, D_POOL), F32),
                        pltpu.VMEM((P, 2, 2 * GLA_DK, 2 * GLA_DV), F32)],
        compiler_params=pltpu.CompilerParams(dimension_semantics=("arbitrary", "arbitrary"),
                                             vmem_limit_bytes=VMEM_LIMIT),
        name="mixer_long",
    )(x, pprev, s0, *wts)


def _mixer_meta(x_ref, wts, x1_ref, u_ref, s_ref):
    w_in_ref, w_zr_ref, w_a2_ref, b_a_ref, w_pool_ref, pscale_ref, gnorm_ref, w_out_ref, g_ref, b_ref = wts
    L = N_META
    x = x_ref[...]
    xb = x.astype(BF16)
    u, zr, q, k, v, r = _project_in(xb, w_in_ref, w_zr_ref)
    u_ref[...] = u

    row128 = lax.broadcasted_iota(jnp.int32, (L, POOL_GROUP), 0)
    pos1 = lax.broadcasted_iota(jnp.int32, (L, 1), 0)
    d_groups = []
    for g, w in enumerate(POOL_WINDOWS):
        ug = u[:, g * POOL_GROUP:(g + 1) * POOL_GROUP]
        s = ug
        sh = 1
        while sh < w:
            s = s + jnp.where(row128 >= sh, _roll_rows(s, sh), 0.0)
            sh *= 2
        d_groups.append(s / jnp.minimum(w, pos1 + 1).astype(F32) - ug)
    y_pool = _pool_project(d_groups, w_pool_ref, pscale_ref)

    loga = _gate_log_decay(zr, w_a2_ref, b_a_ref)
    row256 = lax.broadcasted_iota(jnp.int32, (L, D_GLA_K), 0)
    b = loga
    sh = 1
    while sh < L:
        b = b + jnp.where(row256 >= sh, _roll_rows(b, sh), 0.0)
        sh *= 2
    bend = b[L - 1:L]
    q_in = (q * jnp.exp(b)).astype(BF16)
    k_in = (k * jnp.exp(-b)).astype(BF16)
    k_st = (k * jnp.exp(bend - b)).astype(BF16)
    vb = v.astype(BF16)
    ar = lax.broadcasted_iota(jnp.int32, (L, 2 * L), 0)
    ac = lax.broadcasted_iota(jnp.int32, (L, 2 * L), 1) & (L - 1)
    causal = ac <= ar
    lane128 = lax.broadcasted_iota(jnp.int32, (L, 128), 1)
    lane256 = lax.broadcasted_iota(jnp.int32, (L, 256), 1)
    attn = []
    for p in range(2):
        ks = slice(128 * p, 128 * (p + 1))
        a = _dot_nt(q_in[:, ks], _head_pair_keys(k_in[:, ks], lane128))
        attn.append(jnp.where(causal, a, 0.0).astype(BF16))
    o_parts = []
    for p in range(2):
        ks = slice(128 * p, 128 * (p + 1))
        vs = slice(256 * p, 256 * (p + 1))
        s_p = _dot_tn(k_st[:, ks], vb[:, vs])
        s_ref[2 * p] = s_p[0:GLA_DK, 0:GLA_DV]
        s_ref[2 * p + 1] = s_p[GLA_DK:2 * GLA_DK, GLA_DV:2 * GLA_DV]
        o_parts.append(_dot(attn[p], _head_pair_values(vb[:, vs], lane256)))
    o = jnp.concatenate(o_parts, axis=1)

    y_gla = _gla_output_gate(o, r, gnorm_ref[...])
    x1_ref[...] = _mix_out(x, y_pool, y_gla, w_out_ref, g_ref, b_ref)


def _mixer_sample(x_ref, hist_ref, s0_ref, wts, x1_ref, hist_out_ref, snew_ref, *, seq_len):
    w_in_ref, w_zr_ref, w_a2_ref, b_a_ref, w_pool_ref, pscale_ref, gnorm_ref, w_out_ref, g_ref, b_ref = wts
    G = SEQ_GROUP
    R = G * seq_len
    n_groups = x_ref.shape[0] // R
    x = x_ref[...]
    xb = x.astype(BF16)
    u, zr, q, k, v, r = _project_in(xb, w_in_ref, w_zr_ref)
    loga = _gate_log_decay(zr, w_a2_ref, b_a_ref)
    results = []
    gens = []
    for gi in range(n_groups):
        rs = slice(gi * R, (gi + 1) * R)
        hist_rows = pl.ds(gi * G * POOL_BUF, G * POOL_BUF)
        seqs = pl.ds(gi * G, G)
        gens.append(_sample_group(u[rs], q[rs], k[rs], v[rs], loga[rs], hist_ref.at[hist_rows], s0_ref.at[seqs],
                                  hist_out_ref.at[hist_rows], snew_ref.at[seqs], results, seq_len=seq_len))
    for _ in itertools.zip_longest(*gens):
        pass
    d_groups = [jnp.concatenate([res[0][g] for res in results], axis=0) for g in range(len(POOL_WINDOWS))]
    o = jnp.concatenate([res[1] for res in results], axis=0)
    y_pool = _pool_project(d_groups, w_pool_ref, pscale_ref)
    y_gla = _gla_output_gate(o, r, gnorm_ref[...])
    x1_ref[...] = _mix_out(x, y_pool, y_gla, w_out_ref, g_ref, b_ref)


def _sample_group(u, q, k, v, loga, hist_ref, s0_ref, hist_out_ref, snew_ref, results, *, seq_len):
    G, Ls = SEQ_GROUP, seq_len
    R = G * Ls
    NS = G * GLA_DK
    g_shift = G.bit_length() - 1
    hist_out_ref[0:(POOL_BUF - Ls) * G, :] = hist_ref[R:POOL_BUF * G, :]
    hist_out_ref[(POOL_BUF - Ls) * G:POOL_BUF * G, :] = u

    def blk(a, t):
        return a[t * G:(t + 1) * G]

    d_groups = []
    for g, w in enumerate(POOL_WINDOWS):
        cols = slice(g * POOL_GROUP, (g + 1) * POOL_GROUP)
        ug = u[:, cols]
        suffix = [None]
        acc = None
        for m in range(1, min(w - 1, POOL_BUF) + 1):
            h = hist_ref[(POOL_BUF - m) * G:(POOL_BUF - m + 1) * G, cols]
            acc = h if acc is None else acc + h
            suffix.append(acc)
        parts = []
        for t in range(Ls):
            wsum = blk(ug, t)
            for j in range(max(0, t - w + 1), t):
                wsum = wsum + blk(ug, j)
            m = w - 1 - t
            if m > 0:
                wsum = wsum + suffix[m]
            parts.append(wsum * (1.0 / w) - blk(ug, t))
        d_groups.append(jnp.concatenate(parts, axis=0))

    b_t = [blk(loga, 0)]
    for t in range(1, Ls):
        b_t.append(b_t[-1] + blk(loga, t))
    b = jnp.concatenate(b_t, axis=0)
    bend = jnp.concatenate([b_t[-1]] * Ls, axis=0)
    q_in = (q * jnp.exp(b)).astype(BF16)
    k_in = (k * jnp.exp(-b)).astype(BF16)
    k_st = k * jnp.exp(bend - b)
    dec_hi, dec_lo = _split_bf16(jnp.exp(bend))
    tok = lax.broadcasted_iota(jnp.int32, (R, D_GLA_K), 0) >> g_shift
    dec_rows = jnp.where(tok == Ls - 1, dec_hi, jnp.where(tok == Ls - 2, dec_lo, jnp.zeros_like(dec_lo)))
    vb = v.astype(BF16)

    ar = lax.broadcasted_iota(jnp.int32, (R, 2 * R), 0)
    ac = lax.broadcasted_iota(jnp.int32, (R, 2 * R), 1) & (R - 1)
    same_seq_causal = ((ac & (G - 1)) == (ar & (G - 1))) & ((ac >> g_shift) <= (ar >> g_shift))
    lane128 = lax.broadcasted_iota(jnp.int32, (R, 128), 1)
    lane256 = lax.broadcasted_iota(jnp.int32, (R, 256), 1)
    br = lax.broadcasted_iota(jnp.int32, (R, NS), 0)
    bcol = lax.broadcasted_iota(jnp.int32, (R, NS), 1)
    own_state = (bcol >> 6) == (br & (G - 1))
    ones_blk = jnp.ones((R, GLA_DV), BF16)
    zeros_blk = jnp.zeros((R, GLA_DV), BF16)

    def expand(xp, first):
        sw = pltpu.roll(xp, GLA_DK, 1)
        two = jnp.where(lane128 < GLA_DK, xp, sw) if first else jnp.where(lane128 < GLA_DK, sw, xp)
        rep = jnp.concatenate([two] * (NS // 128), axis=1)
        return jnp.where(own_state, rep, 0.0).astype(BF16)

    ks = [slice(128 * p, 128 * (p + 1)) for p in range(2)]
    vs = [slice(256 * p, 256 * (p + 1)) for p in range(2)]
    attn = []
    for p in range(2):
        a = _dot_nt(q_in[:, ks[p]], _head_pair_keys(k_in[:, ks[p]], lane128))
        attn.append(jnp.where(same_seq_causal, a, 0.0).astype(BF16))
    yield
    inter = []
    for h in range(GLA_HEADS):
        p, first = h // 2, h % 2 == 0
        s_flat = s0_ref[:, h].reshape(NS, GLA_DV)
        inter.append(_dot(expand(q_in[:, ks[p]].astype(F32), first), s_flat.astype(BF16)))
    for h in range(GLA_HEADS):
        p, first = h // 2, h % 2 == 0
        s_flat = s0_ref[:, h].reshape(NS, GLA_DV)
        lhs = jnp.concatenate([expand(k_st[:, ks[p]], first),
                               expand(dec_rows[:, ks[p]].astype(F32), first)], axis=0)
        v_h = vb[:, h * GLA_DV:(h + 1) * GLA_DV]
        rhs = jnp.concatenate([jnp.concatenate([v_h, zeros_blk], axis=1),
                               jnp.concatenate([zeros_blk, ones_blk], axis=1)], axis=0)
        ud = _dot_tn(lhs, rhs)
        s_new = ud[:, GLA_DV:] * s_flat + ud[:, :GLA_DV]
        snew_ref[:, h] = s_new.reshape(G, GLA_DK, GLA_DV)
    yield
    o_parts = []
    for p in range(2):
        o_intra = _dot(attn[p], _head_pair_values(vb[:, vs[p]], lane256))
        o_parts.append(o_intra + jnp.concatenate(inter[2 * p:2 * p + 2], axis=1))
    results.append((d_groups, jnp.concatenate(o_parts, axis=1)))


def _mixer_short_kernel(*refs, seq_len, fill_slabs):
    xm_ref, xs_ref, hist_ref, s0_ref = refs[:4]
    (w_in_t_f32_ref, w_zr_ref, w_a2_ref, b_a_ref, w_pool_ref, pscale_ref, gnorm_ref, w_out_f32_ref,
     g_ref, b_ref) = refs[4:4 + N_MIX_W]
    x1m_ref, um_ref, sm_ref, x1s_ref, us_ref, ss_ref, w_in_ref, w_out_ref = refs[-8:]
    wts = (w_in_ref, w_zr_ref, w_a2_ref, b_a_ref, w_pool_ref, pscale_ref, gnorm_ref, w_out_ref, g_ref, b_ref)
    i = pl.program_id(0)

    @pl.when(i == 0)
    def _meta():
        for c0 in range(0, C_Z, CAST_ROWS):
            w_in_ref[:, c0:c0 + CAST_ROWS] = jnp.transpose(w_in_t_f32_ref[c0:c0 + CAST_ROWS, :]).astype(BF16)
        for r0 in range(0, D_MODEL, CAST_ROWS):
            w_out_ref[r0:r0 + CAST_ROWS, :] = w_out_f32_ref[r0:r0 + CAST_ROWS, :].astype(BF16)
        _mixer_meta(xm_ref, wts, x1m_ref, um_ref, sm_ref)

    @pl.when(i > 0)
    def _sample():
        snew_ref = ss_ref.at[0] if fill_slabs else ss_ref
        _mixer_sample(xs_ref, hist_ref, s0_ref, wts, x1s_ref, us_ref, snew_ref, seq_len=seq_len)
        for a in range(1, fill_slabs):
            ss_ref[a] = ss_ref[0]


def _mixer_short(x_meta, x_samp, hist, s0, wts, layer, *, seq_len, s_stack=None):
    rows = x_samp.shape[0]
    n_groups = rows // (SEQ_GROUP * seq_len)
    per_step = GROUPS_PER_STEP if n_groups % GROUPS_PER_STEP == 0 else 1
    G = SEQ_GROUP * per_step
    R = G * seq_len
    n_tiles = rows // R
    assert rows % R == 0
    clamp = lambda i: jnp.maximum(i - 1, 0)
    n_layers = s0.shape[0]
    state_blk = (G, GLA_HEADS, GLA_DK, GLA_DV)
    wspecs = _mixer_weight_specs(layer)
    in_specs = ([pl.BlockSpec((N_META, D_MODEL), lambda i: (0, 0)),
                 pl.BlockSpec((R, D_MODEL), lambda i: (clamp(i), 0)),
                 pl.BlockSpec((None, G * POOL_BUF, D_POOL), lambda i: (layer, clamp(i), 0)),
                 pl.BlockSpec((None,) + state_blk, lambda i: (layer, clamp(i), 0, 0, 0))] + wspecs)
    args = [x_meta, x_samp, hist, s0, *wts]
    aliases = {}
    if s_stack is None:
        assert layer == 0
        state_spec = pl.BlockSpec((n_layers,) + state_blk, lambda i: (0, clamp(i), 0, 0, 0))
    else:
        state_spec = pl.BlockSpec((None,) + state_blk, lambda i: (layer, clamp(i), 0, 0, 0))
        in_specs.append(pl.BlockSpec(memory_space=pl.ANY))
        aliases = {len(args): 5}
        args.append(s_stack)
    n_args = len(args)
    body = functools.partial(_mixer_short_kernel, seq_len=seq_len, fill_slabs=n_layers if s_stack is None else 0)

    def kern(*refs):
        body(*refs[:4 + N_MIX_W], *refs[n_args:])

    whole = lambda shape: pl.BlockSpec(shape, lambda i: (0,) * len(shape))
    return pl.pallas_call(
        kern,
        grid=(n_tiles + 1,),
        in_specs=in_specs,
        out_specs=[whole((N_META, D_MODEL)),
                   whole((N_META, D_POOL)),
                   whole((GLA_HEADS, GLA_DK, GLA_DV)),
                   pl.BlockSpec((R, D_MODEL), lambda i: (clamp(i), 0)),
                   pl.BlockSpec((G * POOL_BUF, D_POOL), lambda i: (clamp(i), 0)),
                   state_spec,
                   whole((D_MODEL, C_Z)),
                   whole((D_MODEL, D_MODEL))],
        out_shape=[jax.ShapeDtypeStruct((N_META, D_MODEL), F32),
                   jax.ShapeDtypeStruct((N_META, D_POOL), F32),
                   jax.ShapeDtypeStruct((GLA_HEADS, GLA_DK, GLA_DV), F32),
                   jax.ShapeDtypeStruct((rows, D_MODEL), F32),
                   jax.ShapeDtypeStruct((n_tiles * G * POOL_BUF, D_POOL), F32),
                   jax.ShapeDtypeStruct(s0.shape, F32),
                   jax.ShapeDtypeStruct((D_MODEL, C_Z), BF16),
                   jax.ShapeDtypeStruct((D_MODEL, D_MODEL), BF16)],
        input_output_aliases=aliases,
        compiler_params=pltpu.CompilerParams(dimension_semantics=("arbitrary",),
                                             vmem_limit_bytes=VMEM_LIMIT),
        name="mixer_short",
    )(*args)


def _ffn_tile(x_ref, wts, y_ref, conv_inputs, store_gate):
    w_up_ref, w_gate_ref, cw_ref, cb_ref, w_down_ref, g_ref, b_ref = wts
    x = x_ref[...]
    xb = x.astype(BF16)

    acts = []
    for j in range(D_FF // FF_CHUNK):
        cs = slice(j * FF_CHUNK, (j + 1) * FF_CHUNK)
        a = _dot(xb, w_up_ref[j])
        gt = _dot(xb, w_gate_ref[j])
        g1, g2 = conv_inputs(gt, cs)
        store_gate(gt, cs)
        gc = cb_ref[:, cs] + cw_ref[0:1, cs] * g2 + cw_ref[1:2, cs] * g1 + cw_ref[2:3, cs] * gt
        acts.append((a * _silu(gc)).astype(BF16))
    yield

    act = jnp.concatenate(acts, axis=1)
    rows = x.shape[0]
    rb = min(rows, DOWN_ROWS)
    for r0 in range(0, rows, rb):
        f = _dot(act[r0:r0 + rb], w_down_ref[...])
        y_ref[r0:r0 + rb, :] = _layer_norm(ALPHA * x[r0:r0 + rb] + f, g_ref[...], b_ref[...])


def _ffn_long_kernel(*refs, n_par):
    x_ref, cprev_ref = refs[:2]
    wts = refs[2:2 + N_FFN_W]
    y_ref, hist_out_ref, gbuf = refs[2 + N_FFN_W:]
    t = pl.program_id(1)
    T = x_ref.shape[1]

    @pl.when(t == 0)
    def _init():
        for j in range(n_par):
            gbuf[j, 0:CONV_PAD, :] = cprev_ref[...]

    def make_tile(j):
        def conv_inputs(gt, cs):
            gbuf[j, CONV_PAD:CONV_PAD + T, cs] = gt
            return gbuf[j, CONV_PAD - 1:CONV_PAD - 1 + T, cs], gbuf[j, CONV_PAD - 2:CONV_PAD - 2 + T, cs]

        def store_gate(gt, cs):
            gbuf[j, 0:CONV_PAD, cs] = gbuf[j, T:T + CONV_PAD, cs]

        return _ffn_tile(x_ref.at[j], wts, y_ref.at[j], conv_inputs, store_gate)

    for _ in itertools.zip_longest(*[make_tile(j) for j in range(n_par)]):
        pass

    @pl.when(t == pl.num_programs(1) - 1)
    def _final():
        for j in range(n_par):
            hist_out_ref[j] = gbuf[j, CONV_PAD - CONV_BUF:CONV_PAD, :]


def _ffn_short_kernel(xm_ref, xs_ref, hist_ref, w_up_ref, w_gate_ref, cw_ref, cb_ref, w_down_ref, g_ref, b_ref,
                      ym_ref, cm_ref, ys_ref, cs_ref, w_up_bf_ref, w_gate_bf_ref, w_down_bf_ref,
                      fm_ref, fs_ref, *, seq_len):
    j = pl.program_id(0)
    G, Ls = SEQ_GROUP, seq_len
    R = G * Ls
    rows = xs_ref.shape[0]
    n_tiles = rows // R

    w_up = w_up_ref[...].astype(BF16)
    w_gate = w_gate_ref[...].astype(BF16)
    w_down = w_down_ref[...].astype(BF16)
    w_up_bf_ref[...] = w_up
    w_gate_bf_ref[...] = w_gate
    w_down_bf_ref[...] = w_down

    @pl.when(j == 0)
    def _first():
        fm_ref[...] = jnp.zeros_like(fm_ref)
        fs_ref[...] = jnp.zeros_like(fs_ref)

    n_parts = SHORT_FFN_PARTS if n_tiles % SHORT_FFN_PARTS == 0 else 1
    part_tiles = n_tiles // n_parts
    part_rows = part_tiles * R
    hist_rows = CONV_BUF * G
    tok = (lax.broadcasted_iota(jnp.int32, (part_rows, FF_CHUNK), 0) & (R - 1)) >> (G.bit_length() - 1)
    zeros_tail = jnp.zeros((R - hist_rows, FF_CHUNK), F32)

    def sample_conv(gt, part):
        first = part * part_tiles
        hx = jnp.concatenate([piece for n in range(first, first + part_tiles)
                              for piece in (hist_ref[n * hist_rows:(n + 1) * hist_rows, :], zeros_tail)], axis=0)
        g1 = jnp.where(tok >= 1, _roll_rows(gt, G), _roll_rows(hx, -G))
        g2 = jnp.where(tok >= 2, _roll_rows(gt, 2 * G), hx)
        return g1, g2

    def sample_store(gt, part):
        for n in range(part_tiles):
            dst = (part * part_tiles + n) * hist_rows
            cs_ref[dst:dst + hist_rows, :] = gt[n * R + (Ls - CONV_BUF) * G:(n + 1) * R]

    rowm = lax.broadcasted_iota(jnp.int32, (N_META, FF_CHUNK), 0)

    def meta_conv(gt):
        return (jnp.where(rowm >= 1, _roll_rows(gt, 1), 0.0), jnp.where(rowm >= 2, _roll_rows(gt, 2), 0.0))

    def meta_store(gt):
        cm_ref[...] = gt[N_META - CONV_PAD:N_META]

    parts = [(xs_ref.at[pl.ds(p * part_rows, part_rows)], fs_ref.at[pl.ds(p * part_rows, part_rows)],
              functools.partial(sample_conv, part=p), functools.partial(sample_store, part=p))
             for p in range(n_parts)]
    parts.append((xm_ref, fm_ref, meta_conv, meta_store))

    proj = []
    for x_ref, _, _, _ in parts:
        xb = x_ref[...].astype(BF16)
        proj.append((_dot(xb, w_up), _dot(xb, w_gate)))
    acts = []
    for (a, gt), (_, _, conv_inputs, store_gate) in zip(proj, parts):
        g1, g2 = conv_inputs(gt)
        store_gate(gt)
        gc = cb_ref[...] + cw_ref[0:1, :] * g2 + cw_ref[1:2, :] * g1 + cw_ref[2:3, :] * gt
        acts.append((a * _silu(gc)).astype(BF16))
    for act, (_, f_ref, _, _) in zip(acts, parts):
        f_ref[...] += _dot(act, w_down)

    @pl.when(j == pl.num_programs(0) - 1)
    def _last():
        ys_ref[...] = _layer_norm(ALPHA * xs_ref[...] + fs_ref[...], g_ref[...], b_ref[...])
        ym_ref[...] = _layer_norm(ALPHA * xm_ref[...] + fm_ref[...], g_ref[...], b_ref[...])


def _ffn_weight_specs(layer):
    whole = lambda shape: pl.BlockSpec(shape, lambda *_: (0,) * len(shape), pipeline_mode=pl.Buffered(1))
    chunked = (D_FF // FF_CHUNK, D_MODEL, FF_CHUNK)
    specs = [whole(chunked), whole(chunked), _layer_spec((3, D_FF), layer), _layer_spec((1, D_FF), layer),
             whole((D_FF, D_MODEL)), _layer_spec((1, D_MODEL), layer), _layer_spec((1, D_MODEL), layer)]
    assert len(specs) == N_FFN_W
    return specs


def _ffn_long(x, cprev, wts, layer):
    B, L, _ = x.shape
    T = min(FFN_TILE, L)
    P = LONG_PAR if B % LONG_PAR == 0 else 1
    assert L % T == 0
    kern = functools.partial(_ffn_long_kernel, n_par=P)
    return pl.pallas_call(
        kern,
        grid=(B // P, L // T),
        in_specs=[pl.BlockSpec((P, T, D_MODEL), lambda b, t: (b, t, 0)),
                  pl.BlockSpec((CONV_PAD, D_FF), lambda b, t: (0, 0))] + _ffn_weight_specs(layer),
        out_specs=[pl.BlockSpec((P, T, D_MODEL), lambda b, t: (b, t, 0)),
                   pl.BlockSpec((P, CONV_BUF, D_FF), lambda b, t: (b, 0, 0))],
        out_shape=[jax.ShapeDtypeStruct((B, L, D_MODEL), F32),
                   jax.ShapeDtypeStruct((B, CONV_BUF, D_FF), F32)],
        scratch_shapes=[pltpu.VMEM((P, T + CONV_PAD, D_FF), F32)],
        compiler_params=pltpu.CompilerParams(dimension_semantics=("arbitrary", "arbitrary"),
                                             vmem_limit_bytes=VMEM_LIMIT),
        name="ffn_long",
    )(x, cprev, *wts)


def _ffn_short(x_meta, x_samp, hist, wts, layer, *, seq_len):
    rows = x_samp.shape[0]
    n_hist = hist.shape[1]
    C = FF_CHUNK
    const = lambda shape: pl.BlockSpec(shape, lambda j: (0,) * len(shape))
    kern = functools.partial(_ffn_short_kernel, seq_len=seq_len)
    return pl.pallas_call(
        kern,
        grid=(D_FF // C,),
        in_specs=[const((N_META, D_MODEL)),
                  const((rows, D_MODEL)),
                  pl.BlockSpec((None, n_hist, C), lambda j: (layer, 0, j)),
                  pl.BlockSpec((None, D_MODEL, C), lambda j: (layer, 0, j)),
                  pl.BlockSpec((None, D_MODEL, C), lambda j: (layer, 0, j)),
                  pl.BlockSpec((None, 3, C), lambda j: (layer, 0, j)),
                  pl.BlockSpec((None, 1, C), lambda j: (layer, 0, j)),
                  pl.BlockSpec((None, C, D_MODEL), lambda j: (layer, j, 0)),
                  pl.BlockSpec((None, 1, D_MODEL), lambda j: (layer, 0, 0)),
                  pl.BlockSpec((None, 1, D_MODEL), lambda j: (layer, 0, 0))],
        out_specs=[const((N_META, D_MODEL)),
                   pl.BlockSpec((CONV_PAD, C), lambda j: (0, j)),
                   const((rows, D_MODEL)),
                   pl.BlockSpec((n_hist, C), lambda j: (0, j)),
                   pl.BlockSpec((None, D_MODEL, C), lambda j: (j, 0, 0)),
                   pl.BlockSpec((None, D_MODEL, C), lambda j: (j, 0, 0)),
                   pl.BlockSpec((C, D_MODEL), lambda j: (j, 0))],
        out_shape=[jax.ShapeDtypeStruct((N_META, D_MODEL), F32),
                   jax.ShapeDtypeStruct((CONV_PAD, D_FF), F32),
                   jax.ShapeDtypeStruct((rows, D_MODEL), F32),
                   jax.ShapeDtypeStruct((n_hist, D_FF), F32),
                   jax.ShapeDtypeStruct((D_FF // C, D_MODEL, C), BF16),
                   jax.ShapeDtypeStruct((D_FF // C, D_MODEL, C), BF16),
                   jax.ShapeDtypeStruct((D_FF, D_MODEL), BF16)],
        scratch_shapes=[pltpu.VMEM((N_META, D_MODEL), F32),
                        pltpu.VMEM((rows, D_MODEL), F32)],
        compiler_params=pltpu.CompilerParams(dimension_semantics=("arbitrary",),
                                             vmem_limit_bytes=VMEM_LIMIT),
        name="ffn_short",
    )(x_meta, x_samp, hist, *wts)


def _to_group_major(a, axis):
    n, j = a.shape[axis], a.shape[axis + 1]
    lead, tail = a.shape[:axis], a.shape[axis + 2:]
    a = a.reshape(*lead, n // SEQ_GROUP, SEQ_GROUP, j, *tail)
    a = jnp.swapaxes(a, axis + 1, axis + 2)
    return a.reshape(*lead, n * j, *tail)


def _from_group_major(a, axis, j):
    rows = a.shape[axis]
    n = rows // j
    lead, tail = a.shape[:axis], a.shape[axis + 1:]
    a = a.reshape(*lead, n // SEQ_GROUP, j, SEQ_GROUP, *tail)
    a = jnp.swapaxes(a, axis + 1, axis + 2)
    return a.reshape(*lead, n, j, *tail)


def kernel(x_prompt, x_sample, state_pool, state_gla, state_conv, meta_tokens,
           w_in, w_a2, b_a, w_pool, pool_scale, gla_norm, w_out, ln1_g, ln1_b,
           w_up, w_gate, conv_w, conv_b, w_down, ln2_g, ln2_b):
    NB, LS = x_sample.shape[0], x_sample.shape[1]
    assert NB % SEQ_GROUP == 0 and LS & (LS - 1) == 0 and CONV_BUF <= LS <= POOL_BUF

    row = lambda a: a.reshape(DEPTH, 1, a.shape[-1])
    mix_w = (jnp.swapaxes(w_in, 1, 2), w_in[:, :, C_Z:].astype(BF16), w_a2.astype(BF16), row(b_a),
             w_pool.astype(BF16), row(pool_scale), row(gla_norm), w_out, row(ln1_g), row(ln1_b))
    ffn_w = (w_up, w_gate, conv_w, row(conv_b), w_down, row(ln2_g), row(ln2_b))

    hm = meta_tokens.astype(F32)
    hp = x_prompt
    hs = _to_group_major(x_sample, 0)
    pool_hist = _to_group_major(state_pool, 1)
    conv_hist = _to_group_major(state_conv, 1)

    pp, gp, cp, ps_l, cs_l = [], [], [], [], []
    gs = None
    for l in range(DEPTH):
        hm1, um, sm, hs1, ps_new, gs, w_in_bf, w_out_bf = _mixer_short(hm, hs, pool_hist, state_gla, mix_w, l,
                                                                       seq_len=LS, s_stack=gs)
        hm, cm, hs, cs_new, w_up_bf, w_gate_bf, w_down_bf = _ffn_short(hm1, hs1, conv_hist, ffn_w, l, seq_len=LS)
        ps_l.append(ps_new)
        cs_l.append(cs_new)

        hp1, pbuf, snew = _mixer_long(hp, um, sm, (w_in_bf,) + mix_w[1:7] + (w_out_bf,) + mix_w[8:], l)
        hp, cbuf = _ffn_long(hp1, cm, (w_up_bf, w_gate_bf) + ffn_w[2:4] + (w_down_bf,) + ffn_w[5:], l)
        pp.append(pbuf)
        gp.append(snew)
        cp.append(cbuf)

    ps = _from_group_major(jnp.stack(ps_l), 1, POOL_BUF)
    cs = _from_group_major(jnp.stack(cs_l), 1, CONV_BUF)
    return (hp, _from_group_major(hs, 0, LS), jnp.stack(pp), jnp.stack(gp), jnp.stack(cp), ps, gs, cs)
```

```python
import functools
import itertools

import jax
import jax.numpy as jnp
from jax import lax
from jax.experimental import pallas as pl
from jax.experimental.pallas import tpu as pltpu

F32 = jnp.float32
BF16 = jnp.bfloat16

D_MODEL = 1024
N_META = 16
D_POOL = 512
POOL_WINDOWS = (2, 4, 8, 16)
POOL_GROUP = 128
POOL_BUF = 15
POOL_PAD = 16
GLA_HEADS = 4
GLA_DV = 128
GLA_DK = 64
D_GLA_K = 256
GATE_RANK = 16
GATE_TAU = 16.0
D_FF = 2816
CONV_BUF = 2
CONV_PAD = 8
DEPTH = 2
ALPHA = (2 * DEPTH) ** 0.25
LN_EPS = 1e-5
RMS_EPS = 1e-6
PAST_LEN = 16384

C_POOL, C_Q, C_K, C_V, C_R, C_Z, C_END = 0, 512, 768, 1024, 1536, 2048, 2064

LONG_TILE = 512
LONG_CHUNK = 64
MIXER_PAR = 2
MIXER_SKEW = 1
FFN_TILE = 512
FFN_PAR = 2
SEQ_GROUP = 16
GROUPS_PER_STEP = 2
FF_CHUNK = 256
DOWN_ROWS = 256
CAST_ROWS = 128
SHORT_FFN_PARTS = 4
V7X_VMEM_BYTES = 64 * 1024 * 1024
VMEM_LIMIT = V7X_VMEM_BYTES - 8 * 1024 * 1024

N_MIX_W = 10
N_FFN_W = 7


def _dot(a, b):
    return jnp.dot(a, b, preferred_element_type=F32)


def _dot_nt(a, b):
    return lax.dot_general(a, b, (((1,), (1,)), ((), ())), preferred_element_type=F32)


def _dot_tn(a, b):
    return lax.dot_general(a, b, (((0,), (0,)), ((), ())), preferred_element_type=F32)


def _layer_norm(y, g, b):
    mu = jnp.mean(y, axis=-1, keepdims=True)
    yc = y - mu
    var = jnp.mean(yc * yc, axis=-1, keepdims=True)
    return yc * lax.rsqrt(var + LN_EPS) * g + b


def _silu(x):
    h = 0.5 * x
    return h + h * jnp.tanh(h)


def _log_sigmoid(z):
    return jnp.minimum(z, 0.0) - jnp.log(1.0 + jnp.exp(-jnp.abs(z)))


def _roll_rows(x, shift):
    n = x.shape[0]
    return pltpu.roll(x, shift % n, 0)


def _split_bf16(x):
    hi = x.astype(BF16)
    lo = (x - hi.astype(F32)).astype(BF16)
    return hi, lo


def _project_in(xb, w_in_ref, w_zr_ref):
    u = _dot(xb, w_in_ref[:, C_POOL:C_Q])
    zr = _dot(xb, w_zr_ref[...])
    q = _dot(xb, w_in_ref[:, C_Q:C_K]) * (GLA_DK ** -0.5)
    k = _dot(xb, w_in_ref[:, C_K:C_V])
    v = _dot(xb, w_in_ref[:, C_V:C_R])
    r = _dot(xb, w_in_ref[:, C_R:C_Z])
    return u, zr, q, k, v, r


def _gate_log_decay(zr, w_a2_ref, b_a_ref):
    z = _dot(zr.astype(BF16), w_a2_ref[...]) + b_a_ref[...]
    return _log_sigmoid(z) * (1.0 / GATE_TAU)


def _gla_output_gate(o, r, gnorm):
    parts = []
    for h in range(GLA_HEADS):
        oh = o[:, h * GLA_DV:(h + 1) * GLA_DV]
        ms = jnp.mean(oh * oh, axis=-1, keepdims=True)
        parts.append(oh * lax.rsqrt(ms + RMS_EPS) * gnorm)
    return jnp.concatenate(parts, axis=1) * _silu(r)


def _pool_project(d_groups, w_pool_ref, pscale_ref):
    ys = [_dot(d.astype(BF16), w_pool_ref[g]) for g, d in enumerate(d_groups)]
    return jnp.concatenate(ys, axis=1) * pscale_ref[...]


def _mix_out(x, y_pool, y_gla, w_out_ref, g_ref, b_ref):
    mix = jnp.concatenate([y_pool, y_gla], axis=1).astype(BF16)
    return _layer_norm(ALPHA * x + _dot(mix, w_out_ref[...]), g_ref[...], b_ref[...])


def _emit_staggered(stage_gens, first_round):
    live = list(zip(first_round, stage_gens))
    rnd = 0
    while live:
        for start, g in list(live):
            if rnd >= start and next(g, StopIteration) is StopIteration:
                live.remove((start, g))
        rnd += 1


def _head_pair_keys(k_p, lane128):
    zk = jnp.zeros_like(k_p)
    return jnp.concatenate([jnp.where(lane128 < GLA_DK, k_p, zk), jnp.where(lane128 >= GLA_DK, k_p, zk)], axis=0)


def _head_pair_values(v_p, lane256):
    zv = jnp.zeros_like(v_p)
    return jnp.concatenate([jnp.where(lane256 < GLA_DV, v_p, zv), jnp.where(lane256 >= GLA_DV, v_p, zv)], axis=0)


def _mixer_long_kernel(*refs, n_par, tile, chunk):
    x_ref, pprev_ref, s0_ref = refs[:3]
    wts = refs[3:3 + N_MIX_W]
    x1_ref, pbuf_ref, snew_ref, ubuf, sbd = refs[3 + N_MIX_W:]
    t = pl.program_id(1)
    zero_blk = jnp.zeros((GLA_DK, GLA_DV), F32)

    @pl.when(t == 0)
    def _init():
        for j in range(n_par):
            ubuf[j, 0:POOL_PAD, :] = pprev_ref[...]
            for p in range(2):
                top = jnp.concatenate([s0_ref[2 * p], zero_blk], axis=1)
                bot = jnp.concatenate([zero_blk, s0_ref[2 * p + 1]], axis=1)
                sbd[j, p] = jnp.concatenate([top, bot], axis=0)

    tiles = [_mixer_long_tile(x_ref.at[j], wts, x1_ref.at[j], ubuf.at[j], sbd.at[j], tile=tile, chunk=chunk)
             for j in range(n_par)]
    _emit_staggered(tiles, [MIXER_SKEW * j for j in range(n_par)])

    @pl.when(t == pl.num_programs(1) - 1)
    def _final():
        for j in range(n_par):
            pbuf_ref[j] = ubuf[j, POOL_PAD - POOL_BUF:POOL_PAD, :]
            for p in range(2):
                s_p = sbd[j, p]
                snew_ref[j, 2 * p] = s_p[0:GLA_DK, 0:GLA_DV]
                snew_ref[j, 2 * p + 1] = s_p[GLA_DK:2 * GLA_DK, GLA_DV:2 * GLA_DV]


def _mixer_long_tile(x_ref, wts, x1_ref, ubuf, sbd, *, tile, chunk):
    w_in_ref, w_zr_ref, w_a2_ref, b_a_ref, w_pool_ref, pscale_ref, gnorm_ref, w_out_ref, g_ref, b_ref = wts
    T, C = tile, chunk
    x = x_ref[...]
    xb = x.astype(BF16)

    u, zr, q, k, v, r = _project_in(xb, w_in_ref, w_zr_ref)
    yield

    ubuf[POOL_PAD:POOL_PAD + T, :] = u
    d_groups = []
    for g, w in enumerate(POOL_WINDOWS):
        s = ubuf[:, g * POOL_GROUP:(g + 1) * POOL_GROUP]
        sh = 1
        while sh < w:
            s = s + _roll_rows(s, sh)
            sh *= 2
        d_groups.append(s[POOL_PAD:, :] * (1.0 / w) - u[:, g * POOL_GROUP:(g + 1) * POOL_GROUP])
    y_pool = _pool_project(d_groups, w_pool_ref, pscale_ref)
    ubuf[0:POOL_PAD, :] = ubuf[T:T + POOL_PAD, :]
    loga = _gate_log_decay(zr, w_a2_ref, b_a_ref)

    tr = lax.broadcasted_iota(jnp.int32, (C, C), 0)
    tc = lax.broadcasted_iota(jnp.int32, (C, C), 1)
    tri = jnp.where(tc <= tr, 1.0, 0.0).astype(BF16)
    ar = lax.broadcasted_iota(jnp.int32, (C, 2 * C), 0)
    ac = lax.broadcasted_iota(jnp.int32, (C, 2 * C), 1) & (C - 1)
    causal = ac <= ar
    lane128 = lax.broadcasted_iota(jnp.int32, (C, 128), 1)
    lane256 = lax.broadcasted_iota(jnp.int32, (C, 256), 1)
    sr = lax.broadcasted_iota(jnp.int32, (128, 256), 0)
    sc = lax.broadcasted_iota(jnp.int32, (128, 256), 1)
    blockdiag = (sr >= GLA_DK) == (sc >= GLA_DV)
    mid = C // 2 - 1

    n_chunks = T // C
    pairs = [(c, p) for c in range(n_chunks) for p in range(2)]
    ks = [slice(128 * p, 128 * (p + 1)) for p in range(2)]
    vs = [slice(256 * p, 256 * (p + 1)) for p in range(2)]

    bcs = []
    for c in range(n_chunks):
        la_hi, la_lo = _split_bf16(loga[c * C:(c + 1) * C])
        bb = _dot(tri, jnp.concatenate([la_hi, la_lo], axis=1))
        bcs.append(bb[:, :D_GLA_K] + bb[:, D_GLA_K:])
    yield

    q_in, k_in, q_st, k_st, dec_t, vc = [], [], [], [], [], []
    for c in range(n_chunks):
        bc = bcs[c]
        bmid = bc[mid:mid + 1]
        bend = bc[C - 1:C]
        qc = q[c * C:(c + 1) * C]
        kc = k[c * C:(c + 1) * C]
        q_in.append((qc * jnp.exp(bc - bmid)).astype(BF16))
        k_in.append((kc * jnp.exp(bmid - bc)).astype(BF16))
        q_st.append((qc * jnp.exp(bc)).astype(BF16))
        k_st.append((kc * jnp.exp(bend - bc)).astype(BF16))
        dec_t.append(jnp.transpose(jnp.broadcast_to(jnp.exp(bend), (128, D_GLA_K))))
        vc.append(v[c * C:(c + 1) * C].astype(BF16))
    yield

    attn, upd = {}, {}
    for c, p in pairs:
        a = _dot_nt(q_in[c][:, ks[p]], _head_pair_keys(k_in[c][:, ks[p]], lane128))
        attn[c, p] = jnp.where(causal, a, 0.0).astype(BF16)
    for c, p in pairs:
        u_cp = _dot_tn(k_st[c][:, ks[p]], vc[c][:, vs[p]])
        upd[c, p] = jnp.where(blockdiag, u_cp, 0.0)
    yield

    s_vals = [sbd[p] for p in range(2)]
    s_start = {}
    for c, p in pairs:
        s_start[c, p] = s_vals[p].astype(BF16)
        dec_p = dec_t[c][ks[p], :]
        s_vals[p] = jnp.concatenate([dec_p, dec_p], axis=1) * s_vals[p] + upd[c, p]
    for p in range(2):
        sbd[p] = s_vals[p]

    o_rows = [[], []]
    for c, p in pairs:
        vblk = _head_pair_values(vc[c][:, vs[p]], lane256)
        lhs = jnp.concatenate([attn[c, p], q_st[c][:, ks[p]]], axis=1)
        rhs = jnp.concatenate([vblk, s_start[c, p]], axis=0)
        o_rows[p].append(_dot(lhs, rhs))
    o = jnp.concatenate([jnp.concatenate(o_rows[p], axis=0) for p in range(2)], axis=1)
    yield

    y_gla = _gla_output_gate(o, r, gnorm_ref[...])
    x1_ref[...] = _mix_out(x, y_pool, y_gla, w_out_ref, g_ref, b_ref)


def _layer_spec(shape, layer):
    nd = len(shape)
    return pl.BlockSpec((None,) + shape, lambda *_: (layer,) + (0,) * nd, pipeline_mode=pl.Buffered(1))


def _mixer_weight_specs(layer, projections_stacked=True):
    shapes = [(C_END, D_MODEL) if projections_stacked else (D_MODEL, C_Z), (D_MODEL, GATE_RANK),
              (GATE_RANK, D_GLA_K), (1, D_GLA_K), (4, POOL_GROUP, POOL_GROUP), (1, D_POOL), (1, GLA_DV),
              (D_MODEL, D_MODEL), (1, D_MODEL), (1, D_MODEL)]
    assert len(shapes) == N_MIX_W
    specs = [_layer_spec(s, layer) for s in shapes]
    if not projections_stacked:
        for i in (0, 7):
            specs[i] = pl.BlockSpec(shapes[i], lambda *_: (0, 0), pipeline_mode=pl.Buffered(1))
    return specs


def _mixer_long(x, pprev, s0, wts, layer):
    B, L, _ = x.shape
    T = min(LONG_TILE, L)
    P = MIXER_PAR if B % MIXER_PAR == 0 else 1
    assert L % T == 0 and T % LONG_CHUNK == 0
    kern = functools.partial(_mixer_long_kernel, n_par=P, tile=T, chunk=LONG_CHUNK)
    return pl.pallas_call(
        kern,
        grid=(B // P, L // T),
        in_specs=[pl.BlockSpec((P, T, D_MODEL), lambda b, t: (b, t, 0)),
                  pl.BlockSpec((POOL_PAD, D_POOL), lambda b, t: (0, 0)),
                  pl.BlockSpec((GLA_HEADS, GLA_DK, GLA_DV), lambda b, t: (0, 0, 0))]
        + _mixer_weight_specs(layer, projections_stacked=False),
        out_specs=[pl.BlockSpec((P, T, D_MODEL), lambda b, t: (b, t, 0)),
                   pl.BlockSpec((P, POOL_BUF, D_POOL), lambda b, t: (b, 0, 0)),
                   pl.BlockSpec((P, GLA_HEADS, GLA_DK, GLA_DV), lambda b, t: (b, 0, 0, 0))],
        out_shape=[jax.ShapeDtypeStruct((B, L, D_MODEL), F32),
                   jax.ShapeDtypeStruct((B, POOL_BUF, D_POOL), F32),
                   jax.ShapeDtypeStruct((B, GLA_HEADS, GLA_DK, GLA_DV), F32)],
        scratch_shapes=[pltpu.VMEM((P, T + POOL_PAD, D_POOL), F32),
                        pltpu.VMEM((P, 2, 2 * GLA_DK, 2 * GLA_DV), F32)],
        compiler_params=pltpu.CompilerParams(dimension_semantics=("arbitrary", "arbitrary"),
                                             vmem_limit_bytes=VMEM_LIMIT),
        name="mixer_long",
    )(x, pprev, s0, *wts)


def _mixer_meta(x_ref, wts, x1_ref, u_ref, s_ref):
    w_in_ref, w_zr_ref, w_a2_ref, b_a_ref, w_pool_ref, pscale_ref, gnorm_ref, w_out_ref, g_ref, b_ref = wts
    L = N_META
    x = x_ref[...]
    xb = x.astype(BF16)
    u, zr, q, k, v, r = _project_in(xb, w_in_ref, w_zr_ref)
    u_ref[...] = u

    row128 = lax.broadcasted_iota(jnp.int32, (L, POOL_GROUP), 0)
    pos1 = lax.broadcasted_iota(jnp.int32, (L, 1), 0)
    d_groups = []
    for g, w in enumerate(POOL_WINDOWS):
        ug = u[:, g * POOL_GROUP:(g + 1) * POOL_GROUP]
        s = ug
        sh = 1
        while sh < w:
            s = s + jnp.where(row128 >= sh, _roll_rows(s, sh), 0.0)
            sh *= 2
        d_groups.append(s / jnp.minimum(w, pos1 + 1).astype(F32) - ug)
    y_pool = _pool_project(d_groups, w_pool_ref, pscale_ref)

    loga = _gate_log_decay(zr, w_a2_ref, b_a_ref)
    row256 = lax.broadcasted_iota(jnp.int32, (L, D_GLA_K), 0)
    b = loga
    sh = 1
    while sh < L:
        b = b + jnp.where(row256 >= sh, _roll_rows(b, sh), 0.0)
        sh *= 2
    bend = b[L - 1:L]
    q_in = (q * jnp.exp(b)).astype(BF16)
    k_in = (k * jnp.exp(-b)).astype(BF16)
    k_st = (k * jnp.exp(bend - b)).astype(BF16)
    vb = v.astype(BF16)
    ar = lax.broadcasted_iota(jnp.int32, (L, 2 * L), 0)
    ac = lax.broadcasted_iota(jnp.int32, (L, 2 * L), 1) & (L - 1)
    causal = ac <= ar
    lane128 = lax.broadcasted_iota(jnp.int32, (L, 128), 1)
    lane256 = lax.broadcasted_iota(jnp.int32, (L, 256), 1)
    attn = []
    for p in range(2):
        ks = slice(128 * p, 128 * (p + 1))
        a = _dot_nt(q_in[:, ks], _head_pair_keys(k_in[:, ks], lane128))
        attn.append(jnp.where(causal, a, 0.0).astype(BF16))
    o_parts = []
    for p in range(2):
        ks = slice(128 * p, 128 * (p + 1))
        vs = slice(256 * p, 256 * (p + 1))
        s_p = _dot_tn(k_st[:, ks], vb[:, vs])
        s_ref[2 * p] = s_p[0:GLA_DK, 0:GLA_DV]
        s_ref[2 * p + 1] = s_p[GLA_DK:2 * GLA_DK, GLA_DV:2 * GLA_DV]
        o_parts.append(_dot(attn[p], _head_pair_values(vb[:, vs], lane256)))
    o = jnp.concatenate(o_parts, axis=1)

    y_gla = _gla_output_gate(o, r, gnorm_ref[...])
    x1_ref[...] = _mix_out(x, y_pool, y_gla, w_out_ref, g_ref, b_ref)


def _mixer_sample(x_ref, hist_ref, s0_ref, wts, x1_ref, hist_out_ref, snew_ref, *, seq_len):
    w_in_ref, w_zr_ref, w_a2_ref, b_a_ref, w_pool_ref, pscale_ref, gnorm_ref, w_out_ref, g_ref, b_ref = wts
    G = SEQ_GROUP
    R = G * seq_len
    n_groups = x_ref.shape[0] // R
    x = x_ref[...]
    xb = x.astype(BF16)
    u, zr, q, k, v, r = _project_in(xb, w_in_ref, w_zr_ref)
    loga = _gate_log_decay(zr, w_a2_ref, b_a_ref)
    results = []
    gens = []
    for gi in range(n_groups):
        rs = slice(gi * R, (gi + 1) * R)
        hist_rows = pl.ds(gi * G * POOL_BUF, G * POOL_BUF)
        seqs = pl.ds(gi * G, G)
        gens.append(_sample_group(u[rs], q[rs], k[rs], v[rs], loga[rs], hist_ref.at[hist_rows], s0_ref.at[seqs],
                                  hist_out_ref.at[hist_rows], snew_ref.at[seqs], results, seq_len=seq_len))
    for _ in itertools.zip_longest(*gens):
        pass
    d_groups = [jnp.concatenate([res[0][g] for res in results], axis=0) for g in range(len(POOL_WINDOWS))]
    o = jnp.concatenate([res[1] for res in results], axis=0)
    y_pool = _pool_project(d_groups, w_pool_ref, pscale_ref)
    y_gla = _gla_output_gate(o, r, gnorm_ref[...])
    x1_ref[...] = _mix_out(x, y_pool, y_gla, w_out_ref, g_ref, b_ref)


def _sample_group(u, q, k, v, loga, hist_ref, s0_ref, hist_out_ref, snew_ref, results, *, seq_len):
    G, Ls = SEQ_GROUP, seq_len
    R = G * Ls
    NS = G * GLA_DK
    g_shift = G.bit_length() - 1
    hist_out_ref[0:(POOL_BUF - Ls) * G, :] = hist_ref[R:POOL_BUF * G, :]
    hist_out_ref[(POOL_BUF - Ls) * G:POOL_BUF * G, :] = u

    def blk(a, t):
        return a[t * G:(t + 1) * G]

    d_groups = []
    for g, w in enumerate(POOL_WINDOWS):
        cols = slice(g * POOL_GROUP, (g + 1) * POOL_GROUP)
        ug = u[:, cols]
        suffix = [None]
        acc = None
        for m in range(1, min(w - 1, POOL_BUF) + 1):
            h = hist_ref[(POOL_BUF - m) * G:(POOL_BUF - m + 1) * G, cols]
            acc = h if acc is None else acc + h
            suffix.append(acc)
        parts = []
        for t in range(Ls):
            wsum = blk(ug, t)
            for j in range(max(0, t - w + 1), t):
                wsum = wsum + blk(ug, j)
            m = w - 1 - t
            if m > 0:
                wsum = wsum + suffix[m]
            parts.append(wsum * (1.0 / w) - blk(ug, t))
        d_groups.append(jnp.concatenate(parts, axis=0))

    b_t = [blk(loga, 0)]
    for t in range(1, Ls):
        b_t.append(b_t[-1] + blk(loga, t))
    b = jnp.concatenate(b_t, axis=0)
    bend = jnp.concatenate([b_t[-1]] * Ls, axis=0)
    q_in = (q * jnp.exp(b)).astype(BF16)
    k_in = (k * jnp.exp(-b)).astype(BF16)
    k_st = k * jnp.exp(bend - b)
    dec_hi, dec_lo = _split_bf16(jnp.exp(bend))
    tok = lax.broadcasted_iota(jnp.int32, (R, D_GLA_K), 0) >> g_shift
    dec_rows = jnp.where(tok == Ls - 1, dec_hi, jnp.where(tok == Ls - 2, dec_lo, jnp.zeros_like(dec_lo)))
    vb = v.astype(BF16)

    ar = lax.broadcasted_iota(jnp.int32, (R, 2 * R), 0)
    ac = lax.broadcasted_iota(jnp.int32, (R, 2 * R), 1) & (R - 1)
    same_seq_causal = ((ac & (G - 1)) == (ar & (G - 1))) & ((ac >> g_shift) <= (ar >> g_shift))
    lane128 = lax.broadcasted_iota(jnp.int32, (R, 128), 1)
    lane256 = lax.broadcasted_iota(jnp.int32, (R, 256), 1)
    br = lax.broadcasted_iota(jnp.int32, (R, NS), 0)
    bcol = lax.broadcasted_iota(jnp.int32, (R, NS), 1)
    own_state = (bcol >> 6) == (br & (G - 1))
    ones_blk = jnp.ones((R, GLA_DV), BF16)
    zeros_blk = jnp.zeros((R, GLA_DV), BF16)

    def expand(xp, first):
        sw = pltpu.roll(xp, GLA_DK, 1)
        two = jnp.where(lane128 < GLA_DK, xp, sw) if first else jnp.where(lane128 < GLA_DK, sw, xp)
        rep = jnp.concatenate([two] * (NS // 128), axis=1)
        return jnp.where(own_state, rep, 0.0).astype(BF16)

    ks = [slice(128 * p, 128 * (p + 1)) for p in range(2)]
    vs = [slice(256 * p, 256 * (p + 1)) for p in range(2)]
    attn = []
    for p in range(2):
        a = _dot_nt(q_in[:, ks[p]], _head_pair_keys(k_in[:, ks[p]], lane128))
        attn.append(jnp.where(same_seq_causal, a, 0.0).astype(BF16))
    yield
    inter = []
    for h in range(GLA_HEADS):
        p, first = h // 2, h % 2 == 0
        s_flat = s0_ref[:, h].reshape(NS, GLA_DV)
        inter.append(_dot(expand(q_in[:, ks[p]].astype(F32), first), s_flat.astype(BF16)))
    for h in range(GLA_HEADS):
        p, first = h // 2, h % 2 == 0
        s_flat = s0_ref[:, h].reshape(NS, GLA_DV)
        lhs = jnp.concatenate([expand(k_st[:, ks[p]], first),
                               expand(dec_rows[:, ks[p]].astype(F32), first)], axis=0)
        v_h = vb[:, h * GLA_DV:(h + 1) * GLA_DV]
        rhs = jnp.concatenate([jnp.concatenate([v_h, zeros_blk], axis=1),
                               jnp.concatenate([zeros_blk, ones_blk], axis=1)], axis=0)
        ud = _dot_tn(lhs, rhs)
        s_new = ud[:, GLA_DV:] * s_flat + ud[:, :GLA_DV]
        snew_ref[:, h] = s_new.reshape(G, GLA_DK, GLA_DV)
    yield
    o_parts = []
    for p in range(2):
        o_intra = _dot(attn[p], _head_pair_values(vb[:, vs[p]], lane256))
        o_parts.append(o_intra + jnp.concatenate(inter[2 * p:2 * p + 2], axis=1))
    results.append((d_groups, jnp.concatenate(o_parts, axis=1)))


def _mixer_short_kernel(*refs, seq_len, fill_slabs):
    xm_ref, xs_ref, hist_ref, s0_ref = refs[:4]
    (w_in_t_f32_ref, w_zr_ref, w_a2_ref, b_a_ref, w_pool_ref, pscale_ref, gnorm_ref, w_out_f32_ref,
     g_ref, b_ref) = refs[4:4 + N_MIX_W]
    x1m_ref, um_ref, sm_ref, x1s_ref, us_ref, ss_ref, w_in_ref, w_out_ref = refs[-8:]
    wts = (w_in_ref, w_zr_ref, w_a2_ref, b_a_ref, w_pool_ref, pscale_ref, gnorm_ref, w_out_ref, g_ref, b_ref)
    i = pl.program_id(0)

    @pl.when(i == 0)
    def _meta():
        for c0 in range(0, C_Z, CAST_ROWS):
            w_in_ref[:, c0:c0 + CAST_ROWS] = jnp.transpose(w_in_t_f32_ref[c0:c0 + CAST_ROWS, :]).astype(BF16)
        for r0 in range(0, D_MODEL, CAST_ROWS):
            w_out_ref[r0:r0 + CAST_ROWS, :] = w_out_f32_ref[r0:r0 + CAST_ROWS, :].astype(BF16)
        _mixer_meta(xm_ref, wts, x1m_ref, um_ref, sm_ref)

    @pl.when(i > 0)
    def _sample():
        snew_ref = ss_ref.at[0] if fill_slabs else ss_ref
        _mixer_sample(xs_ref, hist_ref, s0_ref, wts, x1s_ref, us_ref, snew_ref, seq_len=seq_len)
        for a in range(1, fill_slabs):
            ss_ref[a] = ss_ref[0]


def _mixer_short(x_meta, x_samp, hist, s0, wts, layer, *, seq_len, s_stack=None):
    rows = x_samp.shape[0]
    n_groups = rows // (SEQ_GROUP * seq_len)
    per_step = GROUPS_PER_STEP if n_groups % GROUPS_PER_STEP == 0 else 1
    G = SEQ_GROUP * per_step
    R = G * seq_len
    n_tiles = rows // R
    assert rows % R == 0
    clamp = lambda i: jnp.maximum(i - 1, 0)
    n_layers = s0.shape[0]
    state_blk = (G, GLA_HEADS, GLA_DK, GLA_DV)
    wspecs = _mixer_weight_specs(layer)
    in_specs = ([pl.BlockSpec((N_META, D_MODEL), lambda i: (0, 0)),
                 pl.BlockSpec((R, D_MODEL), lambda i: (clamp(i), 0)),
                 pl.BlockSpec((None, G * POOL_BUF, D_POOL), lambda i: (layer, clamp(i), 0)),
                 pl.BlockSpec((None,) + state_blk, lambda i: (layer, clamp(i), 0, 0, 0))] + wspecs)
    args = [x_meta, x_samp, hist, s0, *wts]
    aliases = {}
    if s_stack is None:
        assert layer == 0
        state_spec = pl.BlockSpec((n_layers,) + state_blk, lambda i: (0, clamp(i), 0, 0, 0))
    else:
        state_spec = pl.BlockSpec((None,) + state_blk, lambda i: (layer, clamp(i), 0, 0, 0))
        in_specs.append(pl.BlockSpec(memory_space=pl.ANY))
        aliases = {len(args): 5}
        args.append(s_stack)
    n_args = len(args)
    body = functools.partial(_mixer_short_kernel, seq_len=seq_len, fill_slabs=n_layers if s_stack is None else 0)

    def kern(*refs):
        body(*refs[:4 + N_MIX_W], *refs[n_args:])

    whole = lambda shape: pl.BlockSpec(shape, lambda i: (0,) * len(shape))
    return pl.pallas_call(
        kern,
        grid=(n_tiles + 1,),
        in_specs=in_specs,
        out_specs=[whole((N_META, D_MODEL)),
                   whole((N_META, D_POOL)),
                   whole((GLA_HEADS, GLA_DK, GLA_DV)),
                   pl.BlockSpec((R, D_MODEL), lambda i: (clamp(i), 0)),
                   pl.BlockSpec((G * POOL_BUF, D_POOL), lambda i: (clamp(i), 0)),
                   state_spec,
                   whole((D_MODEL, C_Z)),
                   whole((D_MODEL, D_MODEL))],
        out_shape=[jax.ShapeDtypeStruct((N_META, D_MODEL), F32),
                   jax.ShapeDtypeStruct((N_META, D_POOL), F32),
                   jax.ShapeDtypeStruct((GLA_HEADS, GLA_DK, GLA_DV), F32),
                   jax.ShapeDtypeStruct((rows, D_MODEL), F32),
                   jax.ShapeDtypeStruct((n_tiles * G * POOL_BUF, D_POOL), F32),
                   jax.ShapeDtypeStruct(s0.shape, F32),
                   jax.ShapeDtypeStruct((D_MODEL, C_Z), BF16),
                   jax.ShapeDtypeStruct((D_MODEL, D_MODEL), BF16)],
        input_output_aliases=aliases,
        compiler_params=pltpu.CompilerParams(dimension_semantics=("arbitrary",),
                                             vmem_limit_bytes=VMEM_LIMIT),
        name="mixer_short",
    )(*args)


def _ffn_tile(x_ref, wts, y_ref, conv_inputs, store_gate):
    w_up_ref, w_gate_ref, cw_ref, cb_ref, w_down_ref, g_ref, b_ref = wts
    x = x_ref[...]
    xb = x.astype(BF16)

    acts = []
    for j in range(D_FF // FF_CHUNK):
        cs = slice(j * FF_CHUNK, (j + 1) * FF_CHUNK)
        a = _dot(xb, w_up_ref[j])
        gt = _dot(xb, w_gate_ref[j])
        g1, g2 = conv_inputs(gt, cs)
        store_gate(gt, cs)
        gc = cb_ref[:, cs] + cw_ref[0:1, cs] * g2 + cw_ref[1:2, cs] * g1 + cw_ref[2:3, cs] * gt
        acts.append((a * _silu(gc)).astype(BF16))
    yield

    act = jnp.concatenate(acts, axis=1)
    rows = x.shape[0]
    rb = min(rows, DOWN_ROWS)
    for r0 in range(0, rows, rb):
        f = _dot(act[r0:r0 + rb], w_down_ref[...])
        y_ref[r0:r0 + rb, :] = _layer_norm(ALPHA * x[r0:r0 + rb] + f, g_ref[...], b_ref[...])


def _ffn_long_kernel(*refs, n_par):
    x_ref, cprev_ref = refs[:2]
    wts = refs[2:2 + N_FFN_W]
    y_ref, hist_out_ref, gbuf = refs[2 + N_FFN_W:]
    t = pl.program_id(1)
    T = x_ref.shape[1]

    @pl.when(t == 0)
    def _init():
        for j in range(n_par):
            gbuf[j, 0:CONV_PAD, :] = cprev_ref[...]

    def make_tile(j):
        def conv_inputs(gt, cs):
            gbuf[j, CONV_PAD:CONV_PAD + T, cs] = gt
            return gbuf[j, CONV_PAD - 1:CONV_PAD - 1 + T, cs], gbuf[j, CONV_PAD - 2:CONV_PAD - 2 + T, cs]

        def store_gate(gt, cs):
            gbuf[j, 0:CONV_PAD, cs] = gbuf[j, T:T + CONV_PAD, cs]

        return _ffn_tile(x_ref.at[j], wts, y_ref.at[j], conv_inputs, store_gate)

    for _ in itertools.zip_longest(*[make_tile(j) for j in range(n_par)]):
        pass

    @pl.when(t == pl.num_programs(1) - 1)
    def _final():
        for j in range(n_par):
            hist_out_ref[j] = gbuf[j, CONV_PAD - CONV_BUF:CONV_PAD, :]


def _ffn_short_kernel(xm_ref, xs_ref, hist_ref, w_up_ref, w_gate_ref, cw_ref, cb_ref, w_down_ref, g_ref, b_ref,
                      ym_ref, cm_ref, ys_ref, cs_ref, w_up_bf_ref, w_gate_bf_ref, w_down_bf_ref,
                      fm_ref, fs_ref, *, seq_len):
    j = pl.program_id(0)
    G, Ls = SEQ_GROUP, seq_len
    R = G * Ls
    rows = xs_ref.shape[0]
    n_tiles = rows // R

    w_up = w_up_ref[...].astype(BF16)
    w_gate = w_gate_ref[...].astype(BF16)
    w_down = w_down_ref[...].astype(BF16)
    w_up_bf_ref[...] = w_up
    w_gate_bf_ref[...] = w_gate
    w_down_bf_ref[...] = w_down

    @pl.when(j == 0)
    def _first():
        fm_ref[...] = jnp.zeros_like(fm_ref)
        fs_ref[...] = jnp.zeros_like(fs_ref)

    n_parts = SHORT_FFN_PARTS if n_tiles % SHORT_FFN_PARTS == 0 else 1
    part_tiles = n_tiles // n_parts
    part_rows = part_tiles * R
    hist_rows = CONV_BUF * G
    tok = (lax.broadcasted_iota(jnp.int32, (part_rows, FF_CHUNK), 0) & (R - 1)) >> (G.bit_length() - 1)
    zeros_tail = jnp.zeros((R - hist_rows, FF_CHUNK), F32)

    def sample_conv(gt, part):
        first = part * part_tiles
        hx = jnp.concatenate([piece for n in range(first, first + part_tiles)
                              for piece in (hist_ref[n * hist_rows:(n + 1) * hist_rows, :], zeros_tail)], axis=0)
        g1 = jnp.where(tok >= 1, _roll_rows(gt, G), _roll_rows(hx, -G))
        g2 = jnp.where(tok >= 2, _roll_rows(gt, 2 * G), hx)
        return g1, g2

    def sample_store(gt, part):
        for n in range(part_tiles):
            dst = (part * part_tiles + n) * hist_rows
            cs_ref[dst:dst + hist_rows, :] = gt[n * R + (Ls - CONV_BUF) * G:(n + 1) * R]

    rowm = lax.broadcasted_iota(jnp.int32, (N_META, FF_CHUNK), 0)

    def meta_conv(gt):
        return (jnp.where(rowm >= 1, _roll_rows(gt, 1), 0.0), jnp.where(rowm >= 2, _roll_rows(gt, 2), 0.0))

    def meta_store(gt):
        cm_ref[...] = gt[N_META - CONV_PAD:N_META]

    parts = [(xs_ref.at[pl.ds(p * part_rows, part_rows)], fs_ref.at[pl.ds(p * part_rows, part_rows)],
              functools.partial(sample_conv, part=p), functools.partial(sample_store, part=p))
             for p in range(n_parts)]
    parts.append((xm_ref, fm_ref, meta_conv, meta_store))

    proj = []
    for x_ref, _, _, _ in parts:
        xb = x_ref[...].astype(BF16)
        proj.append((_dot(xb, w_up), _dot(xb, w_gate)))
    acts = []
    for (a, gt), (_, _, conv_inputs, store_gate) in zip(proj, parts):
        g1, g2 = conv_inputs(gt)
        store_gate(gt)
        gc = cb_ref[...] + cw_ref[0:1, :] * g2 + cw_ref[1:2, :] * g1 + cw_ref[2:3, :] * gt
        acts.append((a * _silu(gc)).astype(BF16))
    for act, (_, f_ref, _, _) in zip(acts, parts):
        f_ref[...] += _dot(act, w_down)

    @pl.when(j == pl.num_programs(0) - 1)
    def _last():
        ys_ref[...] = _layer_norm(ALPHA * xs_ref[...] + fs_ref[...], g_ref[...], b_ref[...])
        ym_ref[...] = _layer_norm(ALPHA * xm_ref[...] + fm_ref[...], g_ref[...], b_ref[...])


def _ffn_weight_specs(layer):
    whole = lambda shape: pl.BlockSpec(shape, lambda *_: (0,) * len(shape), pipeline_mode=pl.Buffered(1))
    chunked = (D_FF // FF_CHUNK, D_MODEL, FF_CHUNK)
    specs = [whole(chunked), whole(chunked), _layer_spec((3, D_FF), layer), _layer_spec((1, D_FF), layer),
             whole((D_FF, D_MODEL)), _layer_spec((1, D_MODEL), layer), _layer_spec((1, D_MODEL), layer)]
    assert len(specs) == N_FFN_W
    return specs


def _ffn_long(x, cprev, wts, layer):
    B, L, _ = x.shape
    T = min(FFN_TILE, L)
    P = FFN_PAR if B % FFN_PAR == 0 else 1
    assert L % T == 0
    kern = functools.partial(_ffn_long_kernel, n_par=P)
    return pl.pallas_call(
        kern,
        grid=(B // P, L // T),
        in_specs=[pl.BlockSpec((P, T, D_MODEL), lambda b, t: (b, t, 0)),
                  pl.BlockSpec((CONV_PAD, D_FF), lambda b, t: (0, 0))] + _ffn_weight_specs(layer),
        out_specs=[pl.BlockSpec((P, T, D_MODEL), lambda b, t: (b, t, 0)),
                   pl.BlockSpec((P, CONV_BUF, D_FF), lambda b, t: (b, 0, 0))],
        out_shape=[jax.ShapeDtypeStruct((B, L, D_MODEL), F32),
                   jax.ShapeDtypeStruct((B, CONV_BUF, D_FF), F32)],
        scratch_shapes=[pltpu.VMEM((P, T + CONV_PAD, D_FF), F32)],
        compiler_params=pltpu.CompilerParams(dimension_semantics=("arbitrary", "arbitrary"),
                                             vmem_limit_bytes=VMEM_LIMIT),
        name="ffn_long",
    )(x, cprev, *wts)


def _ffn_short(x_meta, x_samp, hist, wts, layer, *, seq_len):
    rows = x_samp.shape[0]
    n_hist = hist.shape[1]
    C = FF_CHUNK
    const = lambda shape: pl.BlockSpec(shape, lambda j: (0,) * len(shape))
    kern = functools.partial(_ffn_short_kernel, seq_len=seq_len)
    return pl.pallas_call(
        kern,
        grid=(D_FF // C,),
        in_specs=[const((N_META, D_MODEL)),
                  const((rows, D_MODEL)),
                  pl.BlockSpec((None, n_hist, C), lambda j: (layer, 0, j)),
                  pl.BlockSpec((None, D_MODEL, C), lambda j: (layer, 0, j)),
                  pl.BlockSpec((None, D_MODEL, C), lambda j: (layer, 0, j)),
                  pl.BlockSpec((None, 3, C), lambda j: (layer, 0, j)),
                  pl.BlockSpec((None, 1, C), lambda j: (layer, 0, j)),
                  pl.BlockSpec((None, C, D_MODEL), lambda j: (layer, j, 0)),
                  pl.BlockSpec((None, 1, D_MODEL), lambda j: (layer, 0, 0)),
                  pl.BlockSpec((None, 1, D_MODEL), lambda j: (layer, 0, 0))],
        out_specs=[const((N_META, D_MODEL)),
                   pl.BlockSpec((CONV_PAD, C), lambda j: (0, j)),
                   const((rows, D_MODEL)),
                   pl.BlockSpec((n_hist, C), lambda j: (0, j)),
                   pl.BlockSpec((None, D_MODEL, C), lambda j: (j, 0, 0)),
                   pl.BlockSpec((None, D_MODEL, C), lambda j: (j, 0, 0)),
                   pl.BlockSpec((C, D_MODEL), lambda j: (j, 0))],
        out_shape=[jax.ShapeDtypeStruct((N_META, D_MODEL), F32),
                   jax.ShapeDtypeStruct((CONV_PAD, D_FF), F32),
                   jax.ShapeDtypeStruct((rows, D_MODEL), F32),
                   jax.ShapeDtypeStruct((n_hist, D_FF), F32),
                   jax.ShapeDtypeStruct((D_FF // C, D_MODEL, C), BF16),
                   jax.ShapeDtypeStruct((D_FF // C, D_MODEL, C), BF16),
                   jax.ShapeDtypeStruct((D_FF, D_MODEL), BF16)],
        scratch_shapes=[pltpu.VMEM((N_META, D_MODEL), F32),
                        pltpu.VMEM((rows, D_MODEL), F32)],
        compiler_params=pltpu.CompilerParams(dimension_semantics=("arbitrary",),
                                             vmem_limit_bytes=VMEM_LIMIT),
        name="ffn_short",
    )(x_meta, x_samp, hist, *wts)


def _to_group_major(a, axis):
    n, j = a.shape[axis], a.shape[axis + 1]
    lead, tail = a.shape[:axis], a.shape[axis + 2:]
    a = a.reshape(*lead, n // SEQ_GROUP, SEQ_GROUP, j, *tail)
    a = jnp.swapaxes(a, axis + 1, axis + 2)
    return a.reshape(*lead, n * j, *tail)


def _from_group_major(a, axis, j):
    rows = a.shape[axis]
    n = rows // j
    lead, tail = a.shape[:axis], a.shape[axis + 1:]
    a = a.reshape(*lead, n // SEQ_GROUP, j, SEQ_GROUP, *tail)
    a = jnp.swapaxes(a, axis + 1, axis + 2)
    return a.reshape(*lead, n, j, *tail)


def kernel(x_prompt, x_sample, state_pool, state_gla, state_conv, meta_tokens,
           w_in, w_a2, b_a, w_pool, pool_scale, gla_norm, w_out, ln1_g, ln1_b,
           w_up, w_gate, conv_w, conv_b, w_down, ln2_g, ln2_b):
    NB, LS = x_sample.shape[0], x_sample.shape[1]
    assert NB % SEQ_GROUP == 0 and LS & (LS - 1) == 0 and CONV_BUF <= LS <= POOL_BUF
    assert PAST_LEN >= POOL_BUF and N_META > POOL_BUF

    row = lambda a: a.reshape(DEPTH, 1, a.shape[-1])
    mix_w = (jnp.swapaxes(w_in, 1, 2), w_in[:, :, C_Z:].astype(BF16), w_a2.astype(BF16), row(b_a),
             w_pool.astype(BF16), row(pool_scale), row(gla_norm), w_out, row(ln1_g), row(ln1_b))
    ffn_w = (w_up, w_gate, conv_w, row(conv_b), w_down, row(ln2_g), row(ln2_b))

    hm = meta_tokens.astype(F32)
    hp = x_prompt
    hs = _to_group_major(x_sample, 0)
    pool_hist = _to_group_major(state_pool, 1)
    conv_hist = _to_group_major(state_conv, 1)

    pp, gp, cp, ps_l, cs_l = [], [], [], [], []
    gs = None
    for l in range(DEPTH):
        hm1, um, sm, hs1, ps_new, gs, w_in_bf, w_out_bf = _mixer_short(hm, hs, pool_hist, state_gla, mix_w, l,
                                                                       seq_len=LS, s_stack=gs)
        hm, cm, hs, cs_new, w_up_bf, w_gate_bf, w_down_bf = _ffn_short(hm1, hs1, conv_hist, ffn_w, l, seq_len=LS)
        ps_l.append(ps_new)
        cs_l.append(cs_new)

        hp1, pbuf, snew = _mixer_long(hp, um, sm, (w_in_bf,) + mix_w[1:7] + (w_out_bf,) + mix_w[8:], l)
        hp, cbuf = _ffn_long(hp1, cm, (w_up_bf, w_gate_bf) + ffn_w[2:4] + (w_down_bf,) + ffn_w[5:], l)
        pp.append(pbuf)
        gp.append(snew)
        cp.append(cbuf)

    ps = _from_group_major(jnp.stack(ps_l), 1, POOL_BUF)
    cs = _from_group_major(jnp.stack(cs_l), 1, CONV_BUF)
    return (hp, _from_group_major(hs, 0, LS), jnp.stack(pp), jnp.stack(gp), jnp.stack(cp), ps, gs, cs)
```

```python
import functools
import itertools

import jax
import jax.numpy as jnp
from jax import lax
from jax.experimental import pallas as pl
from jax.experimental.pallas import tpu as pltpu

F32 = jnp.float32
BF16 = jnp.bfloat16

D_MODEL = 1024
N_META = 16
D_POOL = 512
POOL_WINDOWS = (2, 4, 8, 16)
POOL_GROUP = 128
POOL_BUF = 15
POOL_PAD = 16
GLA_HEADS = 4
GLA_DV = 128
GLA_DK = 64
D_GLA_K = 256
GATE_RANK = 16
GATE_TAU = 16.0
D_FF = 2816
CONV_BUF = 2
CONV_PAD = 8
DEPTH = 2
ALPHA = (2 * DEPTH) ** 0.25
LN_EPS = 1e-5
RMS_EPS = 1e-6
PAST_LEN = 16384

C_POOL, C_Q, C_K, C_V, C_R, C_Z, C_END = 0, 512, 768, 1024, 1536, 2048, 2064

LONG_TILE = 512
LONG_CHUNK = 64
MIXER_PAR = 2
MIXER_SKEW = 1
FFN_TILE = 512
FFN_PAR = 2
SEQ_GROUP = 16
GROUPS_PER_STEP = 2
FF_CHUNK = 256
DOWN_ROWS = 256
CAST_ROWS = 128
SHORT_FFN_PARTS = 4
V7X_VMEM_BYTES = 64 * 1024 * 1024
VMEM_LIMIT = V7X_VMEM_BYTES - 8 * 1024 * 1024

N_MIX_W = 10
N_FFN_W = 7


def _dot(a, b):
    return jnp.dot(a, b, preferred_element_type=F32)


def _dot_nt(a, b):
    return lax.dot_general(a, b, (((1,), (1,)), ((), ())), preferred_element_type=F32)


def _dot_tn(a, b):
    return lax.dot_general(a, b, (((0,), (0,)), ((), ())), preferred_element_type=F32)


def _layer_norm(y, g, b):
    mu = jnp.mean(y, axis=-1, keepdims=True)
    yc = y - mu
    var = jnp.mean(yc * yc, axis=-1, keepdims=True)
    return yc * lax.rsqrt(var + LN_EPS) * g + b


def _silu(x):
    h = 0.5 * x
    return h + h * jnp.tanh(h)


def _log_sigmoid(z):
    return jnp.minimum(z, 0.0) - jnp.log(1.0 + jnp.exp(-jnp.abs(z)))


def _roll_rows(x, shift):
    n = x.shape[0]
    return pltpu.roll(x, shift % n, 0)


def _split_bf16(x):
    hi = x.astype(BF16)
    lo = (x - hi.astype(F32)).astype(BF16)
    return hi, lo


def _project_in(xb, w_in_ref, w_zr_ref):
    u = _dot(xb, w_in_ref[:, C_POOL:C_Q])
    zr = _dot(xb, w_zr_ref[...])
    q = _dot(xb, w_in_ref[:, C_Q:C_K]) * (GLA_DK ** -0.5)
    k = _dot(xb, w_in_ref[:, C_K:C_V])
    v = _dot(xb, w_in_ref[:, C_V:C_R])
    r = _dot(xb, w_in_ref[:, C_R:C_Z])
    return u, zr, q, k, v, r


def _gate_log_decay(zr, w_a2_ref, b_a_ref):
    z = _dot(zr.astype(BF16), w_a2_ref[...]) + b_a_ref[...]
    return _log_sigmoid(z) * (1.0 / GATE_TAU)


def _gla_output_gate(o, r, gnorm):
    parts = []
    for h in range(GLA_HEADS):
        oh = o[:, h * GLA_DV:(h + 1) * GLA_DV]
        ms = jnp.mean(oh * oh, axis=-1, keepdims=True)
        parts.append(oh * lax.rsqrt(ms + RMS_EPS) * gnorm)
    return jnp.concatenate(parts, axis=1) * _silu(r)


def _pool_project(d_groups, w_pool_ref, pscale_ref):
    ys = [_dot(d.astype(BF16), w_pool_ref[g]) for g, d in enumerate(d_groups)]
    return jnp.concatenate(ys, axis=1) * pscale_ref[...]


def _mix_out(x, y_pool, y_gla, w_out_ref, g_ref, b_ref):
    mix = jnp.concatenate([y_pool, y_gla], axis=1).astype(BF16)
    return _layer_norm(ALPHA * x + _dot(mix, w_out_ref[...]), g_ref[...], b_ref[...])


def _emit_staggered(stage_gens, first_round):
    live = list(zip(first_round, stage_gens))
    rnd = 0
    while live:
        for start, g in list(live):
            if rnd >= start and next(g, StopIteration) is StopIteration:
                live.remove((start, g))
        rnd += 1


def _head_pair_keys(k_p, lane128):
    zk = jnp.zeros_like(k_p)
    return jnp.concatenate([jnp.where(lane128 < GLA_DK, k_p, zk), jnp.where(lane128 >= GLA_DK, k_p, zk)], axis=0)


def _head_pair_values(v_p, lane256):
    zv = jnp.zeros_like(v_p)
    return jnp.concatenate([jnp.where(lane256 < GLA_DV, v_p, zv), jnp.where(lane256 >= GLA_DV, v_p, zv)], axis=0)


def _mixer_long_kernel(*refs, n_par, tile, chunk):
    x_ref, pprev_ref, s0_ref = refs[:3]
    wts = refs[3:3 + N_MIX_W]
    ffn_f32 = refs[3 + N_MIX_W:6 + N_MIX_W]
    x1_ref, pbuf_ref, snew_ref = refs[6 + N_MIX_W:9 + N_MIX_W]
    ffn_bf16 = refs[9 + N_MIX_W:12 + N_MIX_W]
    ubuf, sbd = refs[12 + N_MIX_W:]
    t = pl.program_id(1)
    zero_blk = jnp.zeros((GLA_DK, GLA_DV), F32)

    w_up_ref, w_gate_ref, w_down_ref = ffn_f32
    w_up_bf_ref, w_gate_bf_ref, w_down_bf_ref = ffn_bf16
    for src, dst in ((w_up_ref, w_up_bf_ref), (w_gate_ref, w_gate_bf_ref)):
        for c in range(D_FF // FF_CHUNK):
            dst[c] = src[:, c * FF_CHUNK:(c + 1) * FF_CHUNK].astype(BF16)
    w_down_bf_ref[...] = w_down_ref[...].astype(BF16)

    @pl.when(t == 0)
    def _init():
        for j in range(n_par):
            ubuf[j, 0:POOL_PAD, :] = pprev_ref[...]
            for p in range(2):
                top = jnp.concatenate([s0_ref[2 * p], zero_blk], axis=1)
                bot = jnp.concatenate([zero_blk, s0_ref[2 * p + 1]], axis=1)
                sbd[j, p] = jnp.concatenate([top, bot], axis=0)

    tiles = [_mixer_long_tile(x_ref.at[j], wts, x1_ref.at[j], ubuf.at[j], sbd.at[j], tile=tile, chunk=chunk)
             for j in range(n_par)]
    _emit_staggered(tiles, [MIXER_SKEW * j for j in range(n_par)])

    @pl.when(t == pl.num_programs(1) - 1)
    def _final():
        for j in range(n_par):
            pbuf_ref[j] = ubuf[j, POOL_PAD - POOL_BUF:POOL_PAD, :]
            for p in range(2):
                s_p = sbd[j, p]
                snew_ref[j, 2 * p] = s_p[0:GLA_DK, 0:GLA_DV]
                snew_ref[j, 2 * p + 1] = s_p[GLA_DK:2 * GLA_DK, GLA_DV:2 * GLA_DV]


def _mixer_long_tile(x_ref, wts, x1_ref, ubuf, sbd, *, tile, chunk):
    w_in_ref, w_zr_ref, w_a2_ref, b_a_ref, w_pool_ref, pscale_ref, gnorm_ref, w_out_ref, g_ref, b_ref = wts
    T, C = tile, chunk
    x = x_ref[...]
    xb = x.astype(BF16)

    u, zr, q, k, v, r = _project_in(xb, w_in_ref, w_zr_ref)
    yield

    ubuf[POOL_PAD:POOL_PAD + T, :] = u
    d_groups = []
    for g, w in enumerate(POOL_WINDOWS):
        s = ubuf[:, g * POOL_GROUP:(g + 1) * POOL_GROUP]
        sh = 1
        while sh < w:
            s = s + _roll_rows(s, sh)
            sh *= 2
        d_groups.append(s[POOL_PAD:, :] * (1.0 / w) - u[:, g * POOL_GROUP:(g + 1) * POOL_GROUP])
    y_pool = _pool_project(d_groups, w_pool_ref, pscale_ref)
    ubuf[0:POOL_PAD, :] = ubuf[T:T + POOL_PAD, :]
    loga = _gate_log_decay(zr, w_a2_ref, b_a_ref)

    tr = lax.broadcasted_iota(jnp.int32, (C, C), 0)
    tc = lax.broadcasted_iota(jnp.int32, (C, C), 1)
    tri = jnp.where(tc <= tr, 1.0, 0.0).astype(BF16)
    ar = lax.broadcasted_iota(jnp.int32, (C, 2 * C), 0)
    ac = lax.broadcasted_iota(jnp.int32, (C, 2 * C), 1) & (C - 1)
    causal = ac <= ar
    lane128 = lax.broadcasted_iota(jnp.int32, (C, 128), 1)
    lane256 = lax.broadcasted_iota(jnp.int32, (C, 256), 1)
    sr = lax.broadcasted_iota(jnp.int32, (128, 256), 0)
    sc = lax.broadcasted_iota(jnp.int32, (128, 256), 1)
    blockdiag = (sr >= GLA_DK) == (sc >= GLA_DV)
    mid = C // 2 - 1

    n_chunks = T // C
    pairs = [(c, p) for c in range(n_chunks) for p in range(2)]
    ks = [slice(128 * p, 128 * (p + 1)) for p in range(2)]
    vs = [slice(256 * p, 256 * (p + 1)) for p in range(2)]

    bcs = []
    for c in range(n_chunks):
        la_hi, la_lo = _split_bf16(loga[c * C:(c + 1) * C])
        bb = _dot(tri, jnp.concatenate([la_hi, la_lo], axis=1))
        bcs.append(bb[:, :D_GLA_K] + bb[:, D_GLA_K:])
    yield

    q_in, k_in, q_st, k_st, dec_t, vc = [], [], [], [], [], []
    for c in range(n_chunks):
        bc = bcs[c]
        bmid = bc[mid:mid + 1]
        bend = bc[C - 1:C]
        qc = q[c * C:(c + 1) * C]
        kc = k[c * C:(c + 1) * C]
        q_in.append((qc * jnp.exp(bc - bmid)).astype(BF16))
        k_in.append((kc * jnp.exp(bmid - bc)).astype(BF16))
        q_st.append((qc * jnp.exp(bc)).astype(BF16))
        k_st.append((kc * jnp.exp(bend - bc)).astype(BF16))
        dec_t.append(jnp.transpose(jnp.broadcast_to(jnp.exp(bend), (128, D_GLA_K))))
        vc.append(v[c * C:(c + 1) * C].astype(BF16))
    yield

    attn, upd = {}, {}
    for c, p in pairs:
        a = _dot_nt(q_in[c][:, ks[p]], _head_pair_keys(k_in[c][:, ks[p]], lane128))
        attn[c, p] = jnp.where(causal, a, 0.0).astype(BF16)
    for c, p in pairs:
        u_cp = _dot_tn(k_st[c][:, ks[p]], vc[c][:, vs[p]])
        upd[c, p] = jnp.where(blockdiag, u_cp, 0.0)
    yield

    s_vals = [sbd[p] for p in range(2)]
    s_start = {}
    for c, p in pairs:
        s_start[c, p] = s_vals[p].astype(BF16)
        dec_p = dec_t[c][ks[p], :]
        s_vals[p] = jnp.concatenate([dec_p, dec_p], axis=1) * s_vals[p] + upd[c, p]
    for p in range(2):
        sbd[p] = s_vals[p]

    o_rows = [[], []]
    for c, p in pairs:
        vblk = _head_pair_values(vc[c][:, vs[p]], lane256)
        lhs = jnp.concatenate([attn[c, p], q_st[c][:, ks[p]]], axis=1)
        rhs = jnp.concatenate([vblk, s_start[c, p]], axis=0)
        o_rows[p].append(_dot(lhs, rhs))
    o = jnp.concatenate([jnp.concatenate(o_rows[p], axis=0) for p in range(2)], axis=1)
    yield

    y_gla = _gla_output_gate(o, r, gnorm_ref[...])
    x1_ref[...] = _mix_out(x, y_pool, y_gla, w_out_ref, g_ref, b_ref)


def _layer_spec(shape, layer):
    nd = len(shape)
    return pl.BlockSpec((None,) + shape, lambda *_: (layer,) + (0,) * nd, pipeline_mode=pl.Buffered(1))


def _mixer_weight_specs(layer, projections_stacked=True):
    shapes = [(C_END, D_MODEL) if projections_stacked else (D_MODEL, C_Z), (D_MODEL, GATE_RANK),
              (GATE_RANK, D_GLA_K), (1, D_GLA_K), (4, POOL_GROUP, POOL_GROUP), (1, D_POOL), (1, GLA_DV),
              (D_MODEL, D_MODEL), (1, D_MODEL), (1, D_MODEL)]
    assert len(shapes) == N_MIX_W
    specs = [_layer_spec(s, layer) for s in shapes]
    if not projections_stacked:
        for i in (0, 7):
            specs[i] = pl.BlockSpec(shapes[i], lambda *_: (0, 0), pipeline_mode=pl.Buffered(1))
    return specs


def _mixer_long(x, pprev, s0, wts, ffn_f32, layer):
    B, L, _ = x.shape
    T = min(LONG_TILE, L)
    P = MIXER_PAR if B % MIXER_PAR == 0 else 1
    assert L % T == 0 and T % LONG_CHUNK == 0
    n_time = L // T
    n_steps = (B // P) * n_time
    up_rows, down_rows = D_MODEL // n_steps, D_FF // n_steps
    assert up_rows * n_steps == D_MODEL and down_rows * n_steps == D_FF and up_rows % 16 == 0 and down_rows % 16 == 0
    n_chunks = D_FF // FF_CHUNK
    step = lambda b, t: b * n_time + t
    kern = functools.partial(_mixer_long_kernel, n_par=P, tile=T, chunk=LONG_CHUNK)
    return pl.pallas_call(
        kern,
        grid=(B // P, n_time),
        in_specs=[pl.BlockSpec((P, T, D_MODEL), lambda b, t: (b, t, 0)),
                  pl.BlockSpec((POOL_PAD, D_POOL), lambda b, t: (0, 0)),
                  pl.BlockSpec((GLA_HEADS, GLA_DK, GLA_DV), lambda b, t: (0, 0, 0))]
        + _mixer_weight_specs(layer, projections_stacked=False)
        + [pl.BlockSpec((None, up_rows, D_FF), lambda b, t: (layer, step(b, t), 0)),
           pl.BlockSpec((None, up_rows, D_FF), lambda b, t: (layer, step(b, t), 0)),
           pl.BlockSpec((None, down_rows, D_MODEL), lambda b, t: (layer, step(b, t), 0))],
        out_specs=[pl.BlockSpec((P, T, D_MODEL), lambda b, t: (b, t, 0)),
                   pl.BlockSpec((P, POOL_BUF, D_POOL), lambda b, t: (b, 0, 0)),
                   pl.BlockSpec((P, GLA_HEADS, GLA_DK, GLA_DV), lambda b, t: (b, 0, 0, 0)),
                   pl.BlockSpec((n_chunks, up_rows, FF_CHUNK), lambda b, t: (0, step(b, t), 0)),
                   pl.BlockSpec((n_chunks, up_rows, FF_CHUNK), lambda b, t: (0, step(b, t), 0)),
                   pl.BlockSpec((down_rows, D_MODEL), lambda b, t: (step(b, t), 0))],
        out_shape=[jax.ShapeDtypeStruct((B, L, D_MODEL), F32),
                   jax.ShapeDtypeStruct((B, POOL_BUF, D_POOL), F32),
                   jax.ShapeDtypeStruct((B, GLA_HEADS, GLA_DK, GLA_DV), F32),
                   jax.ShapeDtypeStruct((n_chunks, D_MODEL, FF_CHUNK), BF16),
                   jax.ShapeDtypeStruct((n_chunks, D_MODEL, FF_CHUNK), BF16),
                   jax.ShapeDtypeStruct((D_FF, D_MODEL), BF16)],
        scratch_shapes=[pltpu.VMEM((P, T + POOL_PAD, D_POOL), F32),
                        pltpu.VMEM((P, 2, 2 * GLA_DK, 2 * GLA_DV), F32)],
        compiler_params=pltpu.CompilerParams(dimension_semantics=("arbitrary", "arbitrary"),
                                             vmem_limit_bytes=VMEM_LIMIT),
        name="mixer_long",
    )(x, pprev, s0, *wts, *ffn_f32)


def _mixer_meta(x_ref, wts, x1_ref, u_ref, s_ref):
    w_in_ref, w_zr_ref, w_a2_ref, b_a_ref, w_pool_ref, pscale_ref, gnorm_ref, w_out_ref, g_ref, b_ref = wts
    L = N_META
    x = x_ref[...]
    xb = x.astype(BF16)
    u, zr, q, k, v, r = _project_in(xb, w_in_ref, w_zr_ref)
    u_ref[...] = u

    row128 = lax.broadcasted_iota(jnp.int32, (L, POOL_GROUP), 0)
    pos1 = lax.broadcasted_iota(jnp.int32, (L, 1), 0)
    d_groups = []
    for g, w in enumerate(POOL_WINDOWS):
        ug = u[:, g * POOL_GROUP:(g + 1) * POOL_GROUP]
        s = ug
        sh = 1
        while sh < w:
            s = s + jnp.where(row128 >= sh, _roll_rows(s, sh), 0.0)
            sh *= 2
        d_groups.append(s / jnp.minimum(w, pos1 + 1).astype(F32) - ug)
    y_pool = _pool_project(d_groups, w_pool_ref, pscale_ref)

    loga = _gate_log_decay(zr, w_a2_ref, b_a_ref)
    row256 = lax.broadcasted_iota(jnp.int32, (L, D_GLA_K), 0)
    b = loga
    sh = 1
    while sh < L:
        b = b + jnp.where(row256 >= sh, _roll_rows(b, sh), 0.0)
        sh *= 2
    bend = b[L - 1:L]
    q_in = (q * jnp.exp(b)).astype(BF16)
    k_in = (k * jnp.exp(-b)).astype(BF16)
    k_st = (k * jnp.exp(bend - b)).astype(BF16)
    vb = v.astype(BF16)
    ar = lax.broadcasted_iota(jnp.int32, (L, 2 * L), 0)
    ac = lax.broadcasted_iota(jnp.int32, (L, 2 * L), 1) & (L - 1)
    causal = ac <= ar
    lane128 = lax.broadcasted_iota(jnp.int32, (L, 128), 1)
    lane256 = lax.broadcasted_iota(jnp.int32, (L, 256), 1)
    attn = []
    for p in range(2):
        ks = slice(128 * p, 128 * (p + 1))
        a = _dot_nt(q_in[:, ks], _head_pair_keys(k_in[:, ks], lane128))
        attn.append(jnp.where(causal, a, 0.0).astype(BF16))
    o_parts = []
    for p in range(2):
        ks = slice(128 * p, 128 * (p + 1))
        vs = slice(256 * p, 256 * (p + 1))
        s_p = _dot_tn(k_st[:, ks], vb[:, vs])
        s_ref[2 * p] = s_p[0:GLA_DK, 0:GLA_DV]
        s_ref[2 * p + 1] = s_p[GLA_DK:2 * GLA_DK, GLA_DV:2 * GLA_DV]
        o_parts.append(_dot(attn[p], _head_pair_values(vb[:, vs], lane256)))
    o = jnp.concatenate(o_parts, axis=1)

    y_gla = _gla_output_gate(o, r, gnorm_ref[...])
    x1_ref[...] = _mix_out(x, y_pool, y_gla, w_out_ref, g_ref, b_ref)


def _mixer_sample(x_ref, hist_ref, s0_ref, wts, x1_ref, hist_out_ref, snew_ref, *, seq_len):
    w_in_ref, w_zr_ref, w_a2_ref, b_a_ref, w_pool_ref, pscale_ref, gnorm_ref, w_out_ref, g_ref, b_ref = wts
    G = SEQ_GROUP
    R = G * seq_len
    n_groups = x_ref.shape[0] // R
    x = x_ref[...]
    xb = x.astype(BF16)
    u, zr, q, k, v, r = _project_in(xb, w_in_ref, w_zr_ref)
    loga = _gate_log_decay(zr, w_a2_ref, b_a_ref)
    results = []
    gens = []
    for gi in range(n_groups):
        rs = slice(gi * R, (gi + 1) * R)
        hist_rows = pl.ds(gi * G * POOL_BUF, G * POOL_BUF)
        seqs = pl.ds(gi * G, G)
        gens.append(_sample_group(u[rs], q[rs], k[rs], v[rs], loga[rs], hist_ref.at[hist_rows], s0_ref.at[seqs],
                                  hist_out_ref.at[hist_rows], snew_ref.at[seqs], results, seq_len=seq_len))
    for _ in itertools.zip_longest(*gens):
        pass
    d_groups = [jnp.concatenate([res[0][g] for res in results], axis=0) for g in range(len(POOL_WINDOWS))]
    o = jnp.concatenate([res[1] for res in results], axis=0)
    y_pool = _pool_project(d_groups, w_pool_ref, pscale_ref)
    y_gla = _gla_output_gate(o, r, gnorm_ref[...])
    x1_ref[...] = _mix_out(x, y_pool, y_gla, w_out_ref, g_ref, b_ref)


def _sample_group(u, q, k, v, loga, hist_ref, s0_ref, hist_out_ref, snew_ref, results, *, seq_len):
    G, Ls = SEQ_GROUP, seq_len
    R = G * Ls
    NS = G * GLA_DK
    g_shift = G.bit_length() - 1
    hist_out_ref[0:(POOL_BUF - Ls) * G, :] = hist_ref[R:POOL_BUF * G, :]
    hist_out_ref[(POOL_BUF - Ls) * G:POOL_BUF * G, :] = u

    def blk(a, t):
        return a[t * G:(t + 1) * G]

    d_groups = []
    for g, w in enumerate(POOL_WINDOWS):
        cols = slice(g * POOL_GROUP, (g + 1) * POOL_GROUP)
        ug = u[:, cols]
        suffix = [None]
        acc = None
        for m in range(1, min(w - 1, POOL_BUF) + 1):
            h = hist_ref[(POOL_BUF - m) * G:(POOL_BUF - m + 1) * G, cols]
            acc = h if acc is None else acc + h
            suffix.append(acc)
        parts = []
        for t in range(Ls):
            wsum = blk(ug, t)
            for j in range(max(0, t - w + 1), t):
                wsum = wsum + blk(ug, j)
            m = w - 1 - t
            if m > 0:
                wsum = wsum + suffix[m]
            parts.append(wsum * (1.0 / w) - blk(ug, t))
        d_groups.append(jnp.concatenate(parts, axis=0))

    b_t = [blk(loga, 0)]
    for t in range(1, Ls):
        b_t.append(b_t[-1] + blk(loga, t))
    b = jnp.concatenate(b_t, axis=0)
    bend = jnp.concatenate([b_t[-1]] * Ls, axis=0)
    q_in = (q * jnp.exp(b)).astype(BF16)
    k_in = (k * jnp.exp(-b)).astype(BF16)
    k_st = k * jnp.exp(bend - b)
    dec_hi, dec_lo = _split_bf16(jnp.exp(bend))
    tok = lax.broadcasted_iota(jnp.int32, (R, D_GLA_K), 0) >> g_shift
    dec_rows = jnp.where(tok == Ls - 1, dec_hi, jnp.where(tok == Ls - 2, dec_lo, jnp.zeros_like(dec_lo)))
    vb = v.astype(BF16)

    ar = lax.broadcasted_iota(jnp.int32, (R, 2 * R), 0)
    ac = lax.broadcasted_iota(jnp.int32, (R, 2 * R), 1) & (R - 1)
    same_seq_causal = ((ac & (G - 1)) == (ar & (G - 1))) & ((ac >> g_shift) <= (ar >> g_shift))
    lane128 = lax.broadcasted_iota(jnp.int32, (R, 128), 1)
    lane256 = lax.broadcasted_iota(jnp.int32, (R, 256), 1)
    br = lax.broadcasted_iota(jnp.int32, (R, NS), 0)
    bcol = lax.broadcasted_iota(jnp.int32, (R, NS), 1)
    own_state = (bcol >> 6) == (br & (G - 1))
    ones_blk = jnp.ones((R, GLA_DV), BF16)
    zeros_blk = jnp.zeros((R, GLA_DV), BF16)

    def expand(xp, first):
        sw = pltpu.roll(xp, GLA_DK, 1)
        two = jnp.where(lane128 < GLA_DK, xp, sw) if first else jnp.where(lane128 < GLA_DK, sw, xp)
        rep = jnp.concatenate([two] * (NS // 128), axis=1)
        return jnp.where(own_state, rep, 0.0).astype(BF16)

    ks = [slice(128 * p, 128 * (p + 1)) for p in range(2)]
    vs = [slice(256 * p, 256 * (p + 1)) for p in range(2)]
    attn = []
    for p in range(2):
        a = _dot_nt(q_in[:, ks[p]], _head_pair_keys(k_in[:, ks[p]], lane128))
        attn.append(jnp.where(same_seq_causal, a, 0.0).astype(BF16))
    yield
    inter = []
    for h in range(GLA_HEADS):
        p, first = h // 2, h % 2 == 0
        s_flat = s0_ref[:, h].reshape(NS, GLA_DV)
        inter.append(_dot(expand(q_in[:, ks[p]].astype(F32), first), s_flat.astype(BF16)))
    for h in range(GLA_HEADS):
        p, first = h // 2, h % 2 == 0
        s_flat = s0_ref[:, h].reshape(NS, GLA_DV)
        lhs = jnp.concatenate([expand(k_st[:, ks[p]], first),
                               expand(dec_rows[:, ks[p]].astype(F32), first)], axis=0)
        v_h = vb[:, h * GLA_DV:(h + 1) * GLA_DV]
        rhs = jnp.concatenate([jnp.concatenate([v_h, zeros_blk], axis=1),
                               jnp.concatenate([zeros_blk, ones_blk], axis=1)], axis=0)
        ud = _dot_tn(lhs, rhs)
        s_new = ud[:, GLA_DV:] * s_flat + ud[:, :GLA_DV]
        snew_ref[:, h] = s_new.reshape(G, GLA_DK, GLA_DV)
    yield
    o_parts = []
    for p in range(2):
        o_intra = _dot(attn[p], _head_pair_values(vb[:, vs[p]], lane256))
        o_parts.append(o_intra + jnp.concatenate(inter[2 * p:2 * p + 2], axis=1))
    results.append((d_groups, jnp.concatenate(o_parts, axis=1)))


def _mixer_short_kernel(*refs, seq_len, fill_slabs):
    xm_ref, xs_ref, hist_ref, s0_ref = refs[:4]
    (w_in_t_f32_ref, w_zr_ref, w_a2_ref, b_a_ref, w_pool_ref, pscale_ref, gnorm_ref, w_out_f32_ref,
     g_ref, b_ref) = refs[4:4 + N_MIX_W]
    x1m_ref, um_ref, sm_ref, x1s_ref, us_ref, ss_ref, w_in_ref, w_out_ref = refs[-8:]
    wts = (w_in_ref, w_zr_ref, w_a2_ref, b_a_ref, w_pool_ref, pscale_ref, gnorm_ref, w_out_ref, g_ref, b_ref)
    i = pl.program_id(0)

    @pl.when(i == 0)
    def _meta():
        for c0 in range(0, C_Z, CAST_ROWS):
            w_in_ref[:, c0:c0 + CAST_ROWS] = jnp.transpose(w_in_t_f32_ref[c0:c0 + CAST_ROWS, :]).astype(BF16)
        for r0 in range(0, D_MODEL, CAST_ROWS):
            w_out_ref[r0:r0 + CAST_ROWS, :] = w_out_f32_ref[r0:r0 + CAST_ROWS, :].astype(BF16)
        _mixer_meta(xm_ref, wts, x1m_ref, um_ref, sm_ref)

    @pl.when(i > 0)
    def _sample():
        snew_ref = ss_ref.at[0] if fill_slabs else ss_ref
        _mixer_sample(xs_ref, hist_ref, s0_ref, wts, x1s_ref, us_ref, snew_ref, seq_len=seq_len)
        for a in range(1, fill_slabs):
            ss_ref[a] = ss_ref[0]


def _mixer_short(x_meta, x_samp, hist, s0, wts, layer, *, seq_len, s_stack=None):
    rows = x_samp.shape[0]
    n_groups = rows // (SEQ_GROUP * seq_len)
    per_step = GROUPS_PER_STEP if n_groups % GROUPS_PER_STEP == 0 else 1
    G = SEQ_GROUP * per_step
    R = G * seq_len
    n_tiles = rows // R
    assert rows % R == 0
    clamp = lambda i: jnp.maximum(i - 1, 0)
    n_layers = s0.shape[0]
    state_blk = (G, GLA_HEADS, GLA_DK, GLA_DV)
    wspecs = _mixer_weight_specs(layer)
    in_specs = ([pl.BlockSpec((N_META, D_MODEL), lambda i: (0, 0)),
                 pl.BlockSpec((R, D_MODEL), lambda i: (clamp(i), 0)),
                 pl.BlockSpec((None, G * POOL_BUF, D_POOL), lambda i: (layer, clamp(i), 0)),
                 pl.BlockSpec((None,) + state_blk, lambda i: (layer, clamp(i), 0, 0, 0))] + wspecs)
    args = [x_meta, x_samp, hist, s0, *wts]
    aliases = {}
    if s_stack is None:
        assert layer == 0
        state_spec = pl.BlockSpec((n_layers,) + state_blk, lambda i: (0, clamp(i), 0, 0, 0))
    else:
        state_spec = pl.BlockSpec((None,) + state_blk, lambda i: (layer, clamp(i), 0, 0, 0))
        in_specs.append(pl.BlockSpec(memory_space=pl.ANY))
        aliases = {len(args): 5}
        args.append(s_stack)
    n_args = len(args)
    body = functools.partial(_mixer_short_kernel, seq_len=seq_len, fill_slabs=n_layers if s_stack is None else 0)

    def kern(*refs):
        body(*refs[:4 + N_MIX_W], *refs[n_args:])

    whole = lambda shape: pl.BlockSpec(shape, lambda i: (0,) * len(shape))
    return pl.pallas_call(
        kern,
        grid=(n_tiles + 1,),
        in_specs=in_specs,
        out_specs=[whole((N_META, D_MODEL)),
                   whole((N_META, D_POOL)),
                   whole((GLA_HEADS, GLA_DK, GLA_DV)),
                   pl.BlockSpec((R, D_MODEL), lambda i: (clamp(i), 0)),
                   pl.BlockSpec((G * POOL_BUF, D_POOL), lambda i: (clamp(i), 0)),
                   state_spec,
                   whole((D_MODEL, C_Z)),
                   whole((D_MODEL, D_MODEL))],
        out_shape=[jax.ShapeDtypeStruct((N_META, D_MODEL), F32),
                   jax.ShapeDtypeStruct((N_META, D_POOL), F32),
                   jax.ShapeDtypeStruct((GLA_HEADS, GLA_DK, GLA_DV), F32),
                   jax.ShapeDtypeStruct((rows, D_MODEL), F32),
                   jax.ShapeDtypeStruct((n_tiles * G * POOL_BUF, D_POOL), F32),
                   jax.ShapeDtypeStruct(s0.shape, F32),
                   jax.ShapeDtypeStruct((D_MODEL, C_Z), BF16),
                   jax.ShapeDtypeStruct((D_MODEL, D_MODEL), BF16)],
        input_output_aliases=aliases,
        compiler_params=pltpu.CompilerParams(dimension_semantics=("arbitrary",),
                                             vmem_limit_bytes=VMEM_LIMIT),
        name="mixer_short",
    )(*args)


def _ffn_tile(x_ref, wts, y_ref, conv_inputs, store_gate):
    w_up_ref, w_gate_ref, cw_ref, cb_ref, w_down_ref, g_ref, b_ref = wts
    x = x_ref[...]
    xb = x.astype(BF16)

    acts = []
    for j in range(D_FF // FF_CHUNK):
        cs = slice(j * FF_CHUNK, (j + 1) * FF_CHUNK)
        a = _dot(xb, w_up_ref[j])
        gt = _dot(xb, w_gate_ref[j])
        g1, g2 = conv_inputs(gt, cs)
        store_gate(gt, cs)
        gc = cb_ref[:, cs] + cw_ref[0:1, cs] * g2 + cw_ref[1:2, cs] * g1 + cw_ref[2:3, cs] * gt
        acts.append((a * _silu(gc)).astype(BF16))
    yield

    act = jnp.concatenate(acts, axis=1)
    rows = x.shape[0]
    rb = min(rows, DOWN_ROWS)
    for r0 in range(0, rows, rb):
        f = _dot(act[r0:r0 + rb], w_down_ref[...])
        y_ref[r0:r0 + rb, :] = _layer_norm(ALPHA * x[r0:r0 + rb] + f, g_ref[...], b_ref[...])


def _ffn_long_kernel(*refs, n_par):
    x_ref, cprev_ref = refs[:2]
    wts = refs[2:2 + N_FFN_W]
    y_ref, hist_out_ref, gbuf = refs[2 + N_FFN_W:]
    t = pl.program_id(1)
    T = x_ref.shape[1]

    @pl.when(t == 0)
    def _init():
        for j in range(n_par):
            gbuf[j, 0:CONV_PAD, :] = cprev_ref[...]

    def make_tile(j):
        def conv_inputs(gt, cs):
            gbuf[j, CONV_PAD:CONV_PAD + T, cs] = gt
            return gbuf[j, CONV_PAD - 1:CONV_PAD - 1 + T, cs], gbuf[j, CONV_PAD - 2:CONV_PAD - 2 + T, cs]

        def store_gate(gt, cs):
            gbuf[j, 0:CONV_PAD, cs] = gbuf[j, T:T + CONV_PAD, cs]

        return _ffn_tile(x_ref.at[j], wts, y_ref.at[j], conv_inputs, store_gate)

    for _ in itertools.zip_longest(*[make_tile(j) for j in range(n_par)]):
        pass

    @pl.when(t == pl.num_programs(1) - 1)
    def _final():
        for j in range(n_par):
            hist_out_ref[j] = gbuf[j, CONV_PAD - CONV_BUF:CONV_PAD, :]


def _ffn_short_kernel(xm_ref, xs_ref, hist_ref, w_up_ref, w_gate_ref, cw_ref, cb_ref, w_down_ref, g_ref, b_ref,
                      ym_ref, cm_ref, ys_ref, cs_ref, fm_ref, fs_ref, *, seq_len):
    j = pl.program_id(0)
    G, Ls = SEQ_GROUP, seq_len
    R = G * Ls
    rows = xs_ref.shape[0]
    n_tiles = rows // R

    w_up = w_up_ref[...]
    w_gate = w_gate_ref[...]
    w_down = w_down_ref[...]

    @pl.when(j == 0)
    def _first():
        fm_ref[...] = jnp.zeros_like(fm_ref)
        fs_ref[...] = jnp.zeros_like(fs_ref)

    n_parts = SHORT_FFN_PARTS if n_tiles % SHORT_FFN_PARTS == 0 else 1
    part_tiles = n_tiles // n_parts
    part_rows = part_tiles * R
    hist_rows = CONV_BUF * G
    tok = (lax.broadcasted_iota(jnp.int32, (part_rows, FF_CHUNK), 0) & (R - 1)) >> (G.bit_length() - 1)
    zeros_tail = jnp.zeros((R - hist_rows, FF_CHUNK), F32)

    def sample_conv(gt, part):
        first = part * part_tiles
        hx = jnp.concatenate([piece for n in range(first, first + part_tiles)
                              for piece in (hist_ref[n * hist_rows:(n + 1) * hist_rows, :], zeros_tail)], axis=0)
        g1 = jnp.where(tok >= 1, _roll_rows(gt, G), _roll_rows(hx, -G))
        g2 = jnp.where(tok >= 2, _roll_rows(gt, 2 * G), hx)
        return g1, g2

    def sample_store(gt, part):
        for n in range(part_tiles):
            dst = (part * part_tiles + n) * hist_rows
            cs_ref[dst:dst + hist_rows, :] = gt[n * R + (Ls - CONV_BUF) * G:(n + 1) * R]

    rowm = lax.broadcasted_iota(jnp.int32, (N_META, FF_CHUNK), 0)

    def meta_conv(gt):
        return (jnp.where(rowm >= 1, _roll_rows(gt, 1), 0.0), jnp.where(rowm >= 2, _roll_rows(gt, 2), 0.0))

    def meta_store(gt):
        cm_ref[...] = gt[N_META - CONV_PAD:N_META]

    parts = [(xs_ref.at[pl.ds(p * part_rows, part_rows)], fs_ref.at[pl.ds(p * part_rows, part_rows)],
              functools.partial(sample_conv, part=p), functools.partial(sample_store, part=p))
             for p in range(n_parts)]
    parts.append((xm_ref, fm_ref, meta_conv, meta_store))

    proj = []
    for x_ref, _, _, _ in parts:
        xb = x_ref[...].astype(BF16)
        proj.append((_dot(xb, w_up), _dot(xb, w_gate)))
    acts = []
    for (a, gt), (_, _, conv_inputs, store_gate) in zip(proj, parts):
        g1, g2 = conv_inputs(gt)
        store_gate(gt)
        gc = cb_ref[...] + cw_ref[0:1, :] * g2 + cw_ref[1:2, :] * g1 + cw_ref[2:3, :] * gt
        acts.append((a * _silu(gc)).astype(BF16))
    for act, (_, f_ref, _, _) in zip(acts, parts):
        f_ref[...] += _dot(act, w_down)

    @pl.when(j == pl.num_programs(0) - 1)
    def _last():
        ys_ref[...] = _layer_norm(ALPHA * xs_ref[...] + fs_ref[...], g_ref[...], b_ref[...])
        ym_ref[...] = _layer_norm(ALPHA * xm_ref[...] + fm_ref[...], g_ref[...], b_ref[...])


def _ffn_weight_specs(layer):
    whole = lambda shape: pl.BlockSpec(shape, lambda *_: (0,) * len(shape), pipeline_mode=pl.Buffered(1))
    chunked = (D_FF // FF_CHUNK, D_MODEL, FF_CHUNK)
    specs = [whole(chunked), whole(chunked), _layer_spec((3, D_FF), layer), _layer_spec((1, D_FF), layer),
             whole((D_FF, D_MODEL)), _layer_spec((1, D_MODEL), layer), _layer_spec((1, D_MODEL), layer)]
    assert len(specs) == N_FFN_W
    return specs


def _ffn_long(x, cprev, wts, layer):
    B, L, _ = x.shape
    T = min(FFN_TILE, L)
    P = FFN_PAR if B % FFN_PAR == 0 else 1
    assert L % T == 0
    kern = functools.partial(_ffn_long_kernel, n_par=P)
    return pl.pallas_call(
        kern,
        grid=(B // P, L // T),
        in_specs=[pl.BlockSpec((P, T, D_MODEL), lambda b, t: (b, t, 0)),
                  pl.BlockSpec((CONV_PAD, D_FF), lambda b, t: (0, 0))] + _ffn_weight_specs(layer),
        out_specs=[pl.BlockSpec((P, T, D_MODEL), lambda b, t: (b, t, 0)),
                   pl.BlockSpec((P, CONV_BUF, D_FF), lambda b, t: (b, 0, 0))],
        out_shape=[jax.ShapeDtypeStruct((B, L, D_MODEL), F32),
                   jax.ShapeDtypeStruct((B, CONV_BUF, D_FF), F32)],
        scratch_shapes=[pltpu.VMEM((P, T + CONV_PAD, D_FF), F32)],
        compiler_params=pltpu.CompilerParams(dimension_semantics=("arbitrary", "arbitrary"),
                                             vmem_limit_bytes=VMEM_LIMIT),
        name="ffn_long",
    )(x, cprev, *wts)


def _ffn_short(x_meta, x_samp, hist, wts, layer, *, seq_len):
    rows = x_samp.shape[0]
    n_hist = hist.shape[1]
    C = FF_CHUNK
    const = lambda shape: pl.BlockSpec(shape, lambda j: (0,) * len(shape))
    kern = functools.partial(_ffn_short_kernel, seq_len=seq_len)
    return pl.pallas_call(
        kern,
        grid=(D_FF // C,),
        in_specs=[const((N_META, D_MODEL)),
                  const((rows, D_MODEL)),
                  pl.BlockSpec((None, n_hist, C), lambda j: (layer, 0, j)),
                  pl.BlockSpec((None, D_MODEL, C), lambda j: (j, 0, 0)),
                  pl.BlockSpec((None, D_MODEL, C), lambda j: (j, 0, 0)),
                  pl.BlockSpec((None, 3, C), lambda j: (layer, 0, j)),
                  pl.BlockSpec((None, 1, C), lambda j: (layer, 0, j)),
                  pl.BlockSpec((C, D_MODEL), lambda j: (j, 0)),
                  pl.BlockSpec((None, 1, D_MODEL), lambda j: (layer, 0, 0)),
                  pl.BlockSpec((None, 1, D_MODEL), lambda j: (layer, 0, 0))],
        out_specs=[const((N_META, D_MODEL)),
                   pl.BlockSpec((CONV_PAD, C), lambda j: (0, j)),
                   const((rows, D_MODEL)),
                   pl.BlockSpec((n_hist, C), lambda j: (0, j))],
        out_shape=[jax.ShapeDtypeStruct((N_META, D_MODEL), F32),
                   jax.ShapeDtypeStruct((CONV_PAD, D_FF), F32),
                   jax.ShapeDtypeStruct((rows, D_MODEL), F32),
                   jax.ShapeDtypeStruct((n_hist, D_FF), F32)],
        scratch_shapes=[pltpu.VMEM((N_META, D_MODEL), F32),
                        pltpu.VMEM((rows, D_MODEL), F32)],
        compiler_params=pltpu.CompilerParams(dimension_semantics=("arbitrary",),
                                             vmem_limit_bytes=VMEM_LIMIT),
        name="ffn_short",
    )(x_meta, x_samp, hist, *wts)


def _to_group_major(a, axis):
    n, j = a.shape[axis], a.shape[axis + 1]
    lead, tail = a.shape[:axis], a.shape[axis + 2:]
    a = a.reshape(*lead, n // SEQ_GROUP, SEQ_GROUP, j, *tail)
    a = jnp.swapaxes(a, axis + 1, axis + 2)
    return a.reshape(*lead, n * j, *tail)


def _from_group_major(a, axis, j):
    rows = a.shape[axis]
    n = rows // j
    lead, tail = a.shape[:axis], a.shape[axis + 1:]
    a = a.reshape(*lead, n // SEQ_GROUP, j, SEQ_GROUP, *tail)
    a = jnp.swapaxes(a, axis + 1, axis + 2)
    return a.reshape(*lead, n, j, *tail)


def kernel(x_prompt, x_sample, state_pool, state_gla, state_conv, meta_tokens,
           w_in, w_a2, b_a, w_pool, pool_scale, gla_norm, w_out, ln1_g, ln1_b,
           w_up, w_gate, conv_w, conv_b, w_down, ln2_g, ln2_b):
    NB, LS = x_sample.shape[0], x_sample.shape[1]
    assert NB % SEQ_GROUP == 0 and LS & (LS - 1) == 0 and CONV_BUF <= LS <= POOL_BUF
    assert PAST_LEN >= POOL_BUF and N_META > POOL_BUF

    row = lambda a: a.reshape(DEPTH, 1, a.shape[-1])
    mix_w = (jnp.swapaxes(w_in, 1, 2), w_in[:, :, C_Z:].astype(BF16), w_a2.astype(BF16), row(b_a),
             w_pool.astype(BF16), row(pool_scale), row(gla_norm), w_out, row(ln1_g), row(ln1_b))
    ffn_w = (w_up, w_gate, conv_w, row(conv_b), w_down, row(ln2_g), row(ln2_b))

    hm = meta_tokens.astype(F32)
    hp = x_prompt
    hs = _to_group_major(x_sample, 0)
    pool_hist = _to_group_major(state_pool, 1)
    conv_hist = _to_group_major(state_conv, 1)

    pp, gp, cp, ps_l, cs_l = [], [], [], [], []
    gs = None
    for l in range(DEPTH):
        hm1, um, sm, hs1, ps_new, gs, w_in_bf, w_out_bf = _mixer_short(hm, hs, pool_hist, state_gla, mix_w, l,
                                                                       seq_len=LS, s_stack=gs)
        hp1, pbuf, snew, w_up_bf, w_gate_bf, w_down_bf = _mixer_long(
            hp, um, sm, (w_in_bf,) + mix_w[1:7] + (w_out_bf,) + mix_w[8:], (ffn_w[0], ffn_w[1], ffn_w[4]), l)
        ffn_wl = (w_up_bf, w_gate_bf) + ffn_w[2:4] + (w_down_bf,) + ffn_w[5:]
        hm, cm, hs, cs_new = _ffn_short(hm1, hs1, conv_hist, ffn_wl, l, seq_len=LS)
        ps_l.append(ps_new)
        cs_l.append(cs_new)
        hp, cbuf = _ffn_long(hp1, cm, ffn_wl, l)
        pp.append(pbuf)
        gp.append(snew)
        cp.append(cbuf)

    ps = _from_group_major(jnp.stack(ps_l), 1, POOL_BUF)
    cs = _from_group_major(jnp.stack(cs_l), 1, CONV_BUF)
    return (hp, _from_group_major(hs, 0, LS), jnp.stack(pp), jnp.stack(gp), jnp.stack(cp), ps, gs, cs)
```

```python
import functools
import itertools

import jax
import jax.numpy as jnp
from jax import lax
from jax.experimental import pallas as pl
from jax.experimental.pallas import tpu as pltpu

F32 = jnp.float32
BF16 = jnp.bfloat16

D_MODEL = 1024
N_META = 16
D_POOL = 512
POOL_WINDOWS = (2, 4, 8, 16)
POOL_GROUP = 128
POOL_BUF = 15
POOL_PAD = 16
GLA_HEADS = 4
GLA_DV = 128
GLA_DK = 64
D_GLA_K = 256
GATE_RANK = 16
GATE_TAU = 16.0
D_FF = 2816
CONV_BUF = 2
CONV_PAD = 8
DEPTH = 2
ALPHA = (2 * DEPTH) ** 0.25
LN_EPS = 1e-5
RMS_EPS = 1e-6
PAST_LEN = 16384

C_POOL, C_Q, C_K, C_V, C_R, C_Z, C_END = 0, 512, 768, 1024, 1536, 2048, 2064

LONG_TILE = 512
LONG_CHUNK = 64
MIXER_PAR = 2
MIXER_SKEW = 1
FFN_TILE = 512
FFN_PAR = 2
SEQ_GROUP = 16
GROUPS_PER_STEP = 2
FF_CHUNK = 256
DOWN_ROWS = 256
CAST_ROWS = 128
SHORT_FFN_PARTS = 4
V7X_VMEM_BYTES = 64 * 1024 * 1024
VMEM_LIMIT = V7X_VMEM_BYTES - 8 * 1024 * 1024

N_MIX_W = 10
N_FFN_W = 7


def _dot(a, b):
    return jnp.dot(a, b, preferred_element_type=F32)


def _dot_nt(a, b):
    return lax.dot_general(a, b, (((1,), (1,)), ((), ())), preferred_element_type=F32)


def _dot_tn(a, b):
    return lax.dot_general(a, b, (((0,), (0,)), ((), ())), preferred_element_type=F32)


def _layer_norm(y, g, b):
    mu = jnp.mean(y, axis=-1, keepdims=True)
    yc = y - mu
    var = jnp.mean(yc * yc, axis=-1, keepdims=True)
    return yc * lax.rsqrt(var + LN_EPS) * g + b


def _silu(x):
    h = 0.5 * x
    return h + h * jnp.tanh(h)


def _log_sigmoid(z):
    return jnp.minimum(z, 0.0) - jnp.log(1.0 + jnp.exp(-jnp.abs(z)))


def _roll_rows(x, shift):
    n = x.shape[0]
    return pltpu.roll(x, shift % n, 0)


def _split_bf16(x):
    hi = x.astype(BF16)
    lo = (x - hi.astype(F32)).astype(BF16)
    return hi, lo


def _project_in(xb, w_in_ref, w_zr_ref):
    u = _dot(xb, w_in_ref[:, C_POOL:C_Q])
    zr = _dot(xb, w_zr_ref[...])
    q = _dot(xb, w_in_ref[:, C_Q:C_K]) * (GLA_DK ** -0.5)
    k = _dot(xb, w_in_ref[:, C_K:C_V])
    v = _dot(xb, w_in_ref[:, C_V:C_R])
    r = _dot(xb, w_in_ref[:, C_R:C_Z])
    return u, zr, q, k, v, r


def _gate_log_decay(zr, w_a2_ref, b_a_ref):
    z = _dot(zr.astype(BF16), w_a2_ref[...]) + b_a_ref[...]
    return _log_sigmoid(z) * (1.0 / GATE_TAU)


def _gla_output_gate(o, r, gnorm):
    parts = []
    for h in range(GLA_HEADS):
        oh = o[:, h * GLA_DV:(h + 1) * GLA_DV]
        ms = jnp.mean(oh * oh, axis=-1, keepdims=True)
        parts.append(oh * lax.rsqrt(ms + RMS_EPS) * gnorm)
    return jnp.concatenate(parts, axis=1) * _silu(r)


def _pool_project(d_groups, w_pool_ref, pscale_ref):
    ys = [_dot(d.astype(BF16), w_pool_ref[g]) for g, d in enumerate(d_groups)]
    return jnp.concatenate(ys, axis=1) * pscale_ref[...]


def _mix_out(x, y_pool, y_gla, w_out_ref, g_ref, b_ref):
    mix = jnp.concatenate([y_pool, y_gla], axis=1).astype(BF16)
    return _layer_norm(ALPHA * x + _dot(mix, w_out_ref[...]), g_ref[...], b_ref[...])


def _emit_staggered(stage_gens, first_round):
    live = list(zip(first_round, stage_gens))
    rnd = 0
    while live:
        for start, g in list(live):
            if rnd >= start and next(g, StopIteration) is StopIteration:
                live.remove((start, g))
        rnd += 1


def _head_pair_keys(k_p, lane128):
    zk = jnp.zeros_like(k_p)
    return jnp.concatenate([jnp.where(lane128 < GLA_DK, k_p, zk), jnp.where(lane128 >= GLA_DK, k_p, zk)], axis=0)


def _head_pair_values(v_p, lane256):
    zv = jnp.zeros_like(v_p)
    return jnp.concatenate([jnp.where(lane256 < GLA_DV, v_p, zv), jnp.where(lane256 >= GLA_DV, v_p, zv)], axis=0)


def _mixer_long_kernel(*refs, n_par, tile, chunk):
    x_ref, pprev_ref, s0_ref = refs[:3]
    wts = refs[3:3 + N_MIX_W]
    ffn_f32 = refs[3 + N_MIX_W:6 + N_MIX_W]
    x1_ref, pbuf_ref, snew_ref = refs[6 + N_MIX_W:9 + N_MIX_W]
    ffn_bf16 = refs[9 + N_MIX_W:12 + N_MIX_W]
    ubuf, sbd = refs[12 + N_MIX_W:]
    t = pl.program_id(1)
    zero_blk = jnp.zeros((GLA_DK, GLA_DV), F32)

    @pl.when(t == 0)
    def _init():
        for j in range(n_par):
            ubuf[j, 0:POOL_PAD, :] = pprev_ref[...]
            for p in range(2):
                top = jnp.concatenate([s0_ref[2 * p], zero_blk], axis=1)
                bot = jnp.concatenate([zero_blk, s0_ref[2 * p + 1]], axis=1)
                sbd[j, p] = jnp.concatenate([top, bot], axis=0)

    tiles = [_mixer_long_tile(x_ref.at[j], wts, x1_ref.at[j], ubuf.at[j], sbd.at[j], tile=tile, chunk=chunk)
             for j in range(n_par)]
    _emit_staggered(tiles + [_round_ffn_weights(ffn_f32, ffn_bf16)], [MIXER_SKEW * j for j in range(n_par)] + [0])

    @pl.when(t == pl.num_programs(1) - 1)
    def _final():
        for j in range(n_par):
            pbuf_ref[j] = ubuf[j, POOL_PAD - POOL_BUF:POOL_PAD, :]
            for p in range(2):
                s_p = sbd[j, p]
                snew_ref[j, 2 * p] = s_p[0:GLA_DK, 0:GLA_DV]
                snew_ref[j, 2 * p + 1] = s_p[GLA_DK:2 * GLA_DK, GLA_DV:2 * GLA_DV]


def _round_ffn_weights(ffn_f32, ffn_bf16):
    w_up_ref, w_gate_ref, w_down_ref = ffn_f32
    w_up_bf_ref, w_gate_bf_ref, w_down_bf_ref = ffn_bf16
    for src, dst in ((w_up_ref, w_up_bf_ref), (w_gate_ref, w_gate_bf_ref)):
        for c in range(D_FF // FF_CHUNK):
            dst[c] = src[:, c * FF_CHUNK:(c + 1) * FF_CHUNK].astype(BF16)
        yield
    w_down_bf_ref[...] = w_down_ref[...].astype(BF16)


def _mixer_long_tile(x_ref, wts, x1_ref, ubuf, sbd, *, tile, chunk):
    w_in_ref, w_zr_ref, w_a2_ref, b_a_ref, w_pool_ref, pscale_ref, gnorm_ref, w_out_ref, g_ref, b_ref = wts
    T, C = tile, chunk
    x = x_ref[...]
    xb = x.astype(BF16)

    u, zr, q, k, v, r = _project_in(xb, w_in_ref, w_zr_ref)
    yield

    ubuf[POOL_PAD:POOL_PAD + T, :] = u
    d_groups = []
    for g, w in enumerate(POOL_WINDOWS):
        s = ubuf[:, g * POOL_GROUP:(g + 1) * POOL_GROUP]
        sh = 1
        while sh < w:
            s = s + _roll_rows(s, sh)
            sh *= 2
        d_groups.append(s[POOL_PAD:, :] * (1.0 / w) - u[:, g * POOL_GROUP:(g + 1) * POOL_GROUP])
    y_pool = _pool_project(d_groups, w_pool_ref, pscale_ref)
    ubuf[0:POOL_PAD, :] = ubuf[T:T + POOL_PAD, :]
    loga = _gate_log_decay(zr, w_a2_ref, b_a_ref)

    tr = lax.broadcasted_iota(jnp.int32, (C, C), 0)
    tc = lax.broadcasted_iota(jnp.int32, (C, C), 1)
    tri = jnp.where(tc <= tr, 1.0, 0.0).astype(BF16)
    ar = lax.broadcasted_iota(jnp.int32, (C, 2 * C), 0)
    ac = lax.broadcasted_iota(jnp.int32, (C, 2 * C), 1) & (C - 1)
    causal = ac <= ar
    lane128 = lax.broadcasted_iota(jnp.int32, (C, 128), 1)
    lane256 = lax.broadcasted_iota(jnp.int32, (C, 256), 1)
    sr = lax.broadcasted_iota(jnp.int32, (128, 256), 0)
    sc = lax.broadcasted_iota(jnp.int32, (128, 256), 1)
    blockdiag = (sr >= GLA_DK) == (sc >= GLA_DV)
    mid = C // 2 - 1

    n_chunks = T // C
    pairs = [(c, p) for c in range(n_chunks) for p in range(2)]
    ks = [slice(128 * p, 128 * (p + 1)) for p in range(2)]
    vs = [slice(256 * p, 256 * (p + 1)) for p in range(2)]

    bcs = []
    for c in range(n_chunks):
        la_hi, la_lo = _split_bf16(loga[c * C:(c + 1) * C])
        bb = _dot(tri, jnp.concatenate([la_hi, la_lo], axis=1))
        bcs.append(bb[:, :D_GLA_K] + bb[:, D_GLA_K:])
    yield

    q_in, k_in, q_st, k_st, dec_t, vc = [], [], [], [], [], []
    for c in range(n_chunks):
        bc = bcs[c]
        bmid = bc[mid:mid + 1]
        bend = bc[C - 1:C]
        qc = q[c * C:(c + 1) * C]
        kc = k[c * C:(c + 1) * C]
        q_in.append((qc * jnp.exp(bc - bmid)).astype(BF16))
        k_in.append((kc * jnp.exp(bmid - bc)).astype(BF16))
        q_st.append((qc * jnp.exp(bc)).astype(BF16))
        k_st.append((kc * jnp.exp(bend - bc)).astype(BF16))
        dec_t.append(jnp.transpose(jnp.broadcast_to(jnp.exp(bend), (128, D_GLA_K))))
        vc.append(v[c * C:(c + 1) * C].astype(BF16))
    yield

    attn, upd = {}, {}
    for c, p in pairs:
        a = _dot_nt(q_in[c][:, ks[p]], _head_pair_keys(k_in[c][:, ks[p]], lane128))
        attn[c, p] = jnp.where(causal, a, 0.0).astype(BF16)
    for c, p in pairs:
        u_cp = _dot_tn(k_st[c][:, ks[p]], vc[c][:, vs[p]])
        upd[c, p] = jnp.where(blockdiag, u_cp, 0.0)
    yield

    s_vals = [sbd[p] for p in range(2)]
    s_start = {}
    for c, p in pairs:
        s_start[c, p] = s_vals[p].astype(BF16)
        dec_p = dec_t[c][ks[p], :]
        s_vals[p] = jnp.concatenate([dec_p, dec_p], axis=1) * s_vals[p] + upd[c, p]
    for p in range(2):
        sbd[p] = s_vals[p]

    o_rows = [[], []]
    for c, p in pairs:
        vblk = _head_pair_values(vc[c][:, vs[p]], lane256)
        lhs = jnp.concatenate([attn[c, p], q_st[c][:, ks[p]]], axis=1)
        rhs = jnp.concatenate([vblk, s_start[c, p]], axis=0)
        o_rows[p].append(_dot(lhs, rhs))
    o = jnp.concatenate([jnp.concatenate(o_rows[p], axis=0) for p in range(2)], axis=1)
    yield

    y_gla = _gla_output_gate(o, r, gnorm_ref[...])
    x1_ref[...] = _mix_out(x, y_pool, y_gla, w_out_ref, g_ref, b_ref)


def _layer_spec(shape, layer):
    nd = len(shape)
    return pl.BlockSpec((None,) + shape, lambda *_: (layer,) + (0,) * nd, pipeline_mode=pl.Buffered(1))


def _mixer_weight_specs(layer, projections_stacked=True):
    shapes = [(C_END, D_MODEL) if projections_stacked else (D_MODEL, C_Z), (D_MODEL, GATE_RANK),
              (GATE_RANK, D_GLA_K), (1, D_GLA_K), (4, POOL_GROUP, POOL_GROUP), (1, D_POOL), (1, GLA_DV),
              (D_MODEL, D_MODEL), (1, D_MODEL), (1, D_MODEL)]
    assert len(shapes) == N_MIX_W
    specs = [_layer_spec(s, layer) for s in shapes]
    if not projections_stacked:
        for i in (0, 7):
            specs[i] = pl.BlockSpec(shapes[i], lambda *_: (0, 0), pipeline_mode=pl.Buffered(1))
    return specs


def _mixer_long(x, pprev, s0, wts, ffn_f32, layer):
    B, L, _ = x.shape
    T = min(LONG_TILE, L)
    P = MIXER_PAR if B % MIXER_PAR == 0 else 1
    assert L % T == 0 and T % LONG_CHUNK == 0
    n_time = L // T
    n_steps = (B // P) * n_time
    up_rows, down_rows = D_MODEL // n_steps, D_FF // n_steps
    assert up_rows * n_steps == D_MODEL and down_rows * n_steps == D_FF and up_rows % 16 == 0 and down_rows % 16 == 0
    n_chunks = D_FF // FF_CHUNK
    step = lambda b, t: b * n_time + t
    kern = functools.partial(_mixer_long_kernel, n_par=P, tile=T, chunk=LONG_CHUNK)
    return pl.pallas_call(
        kern,
        grid=(B // P, n_time),
        in_specs=[pl.BlockSpec((P, T, D_MODEL), lambda b, t: (b, t, 0)),
                  pl.BlockSpec((POOL_PAD, D_POOL), lambda b, t: (0, 0)),
                  pl.BlockSpec((GLA_HEADS, GLA_DK, GLA_DV), lambda b, t: (0, 0, 0))]
        + _mixer_weight_specs(layer, projections_stacked=False)
        + [pl.BlockSpec((None, up_rows, D_FF), lambda b, t: (layer, step(b, t), 0)),
           pl.BlockSpec((None, up_rows, D_FF), lambda b, t: (layer, step(b, t), 0)),
           pl.BlockSpec((None, down_rows, D_MODEL), lambda b, t: (layer, step(b, t), 0))],
        out_specs=[pl.BlockSpec((P, T, D_MODEL), lambda b, t: (b, t, 0)),
                   pl.BlockSpec((P, POOL_BUF, D_POOL), lambda b, t: (b, 0, 0)),
                   pl.BlockSpec((P, GLA_HEADS, GLA_DK, GLA_DV), lambda b, t: (b, 0, 0, 0)),
                   pl.BlockSpec((n_chunks, up_rows, FF_CHUNK), lambda b, t: (0, step(b, t), 0)),
                   pl.BlockSpec((n_chunks, up_rows, FF_CHUNK), lambda b, t: (0, step(b, t), 0)),
                   pl.BlockSpec((down_rows, D_MODEL), lambda b, t: (step(b, t), 0))],
        out_shape=[jax.ShapeDtypeStruct((B, L, D_MODEL), F32),
                   jax.ShapeDtypeStruct((B, POOL_BUF, D_POOL), F32),
                   jax.ShapeDtypeStruct((B, GLA_HEADS, GLA_DK, GLA_DV), F32),
                   jax.ShapeDtypeStruct((n_chunks, D_MODEL, FF_CHUNK), BF16),
                   jax.ShapeDtypeStruct((n_chunks, D_MODEL, FF_CHUNK), BF16),
                   jax.ShapeDtypeStruct((D_FF, D_MODEL), BF16)],
        scratch_shapes=[pltpu.VMEM((P, T + POOL_PAD, D_POOL), F32),
                        pltpu.VMEM((P, 2, 2 * GLA_DK, 2 * GLA_DV), F32)],
        compiler_params=pltpu.CompilerParams(dimension_semantics=("arbitrary", "arbitrary"),
                                             vmem_limit_bytes=VMEM_LIMIT),
        name="mixer_long",
    )(x, pprev, s0, *wts, *ffn_f32)


def _mixer_meta(x_ref, wts, x1_ref, u_ref, s_ref):
    w_in_ref, w_zr_ref, w_a2_ref, b_a_ref, w_pool_ref, pscale_ref, gnorm_ref, w_out_ref, g_ref, b_ref = wts
    L = N_META
    x = x_ref[...]
    xb = x.astype(BF16)
    u, zr, q, k, v, r = _project_in(xb, w_in_ref, w_zr_ref)
    u_ref[...] = u

    row128 = lax.broadcasted_iota(jnp.int32, (L, POOL_GROUP), 0)
    pos1 = lax.broadcasted_iota(jnp.int32, (L, 1), 0)
    d_groups = []
    for g, w in enumerate(POOL_WINDOWS):
        ug = u[:, g * POOL_GROUP:(g + 1) * POOL_GROUP]
        s = ug
        sh = 1
        while sh < w:
            s = s + jnp.where(row128 >= sh, _roll_rows(s, sh), 0.0)
            sh *= 2
        d_groups.append(s / jnp.minimum(w, pos1 + 1).astype(F32) - ug)
    y_pool = _pool_project(d_groups, w_pool_ref, pscale_ref)

    loga = _gate_log_decay(zr, w_a2_ref, b_a_ref)
    row256 = lax.broadcasted_iota(jnp.int32, (L, D_GLA_K), 0)
    b = loga
    sh = 1
    while sh < L:
        b = b + jnp.where(row256 >= sh, _roll_rows(b, sh), 0.0)
        sh *= 2
    bend = b[L - 1:L]
    q_in = (q * jnp.exp(b)).astype(BF16)
    k_in = (k * jnp.exp(-b)).astype(BF16)
    k_st = (k * jnp.exp(bend - b)).astype(BF16)
    vb = v.astype(BF16)
    ar = lax.broadcasted_iota(jnp.int32, (L, 2 * L), 0)
    ac = lax.broadcasted_iota(jnp.int32, (L, 2 * L), 1) & (L - 1)
    causal = ac <= ar
    lane128 = lax.broadcasted_iota(jnp.int32, (L, 128), 1)
    lane256 = lax.broadcasted_iota(jnp.int32, (L, 256), 1)
    attn = []
    for p in range(2):
        ks = slice(128 * p, 128 * (p + 1))
        a = _dot_nt(q_in[:, ks], _head_pair_keys(k_in[:, ks], lane128))
        attn.append(jnp.where(causal, a, 0.0).astype(BF16))
    o_parts = []
    for p in range(2):
        ks = slice(128 * p, 128 * (p + 1))
        vs = slice(256 * p, 256 * (p + 1))
        s_p = _dot_tn(k_st[:, ks], vb[:, vs])
        s_ref[2 * p] = s_p[0:GLA_DK, 0:GLA_DV]
        s_ref[2 * p + 1] = s_p[GLA_DK:2 * GLA_DK, GLA_DV:2 * GLA_DV]
        o_parts.append(_dot(attn[p], _head_pair_values(vb[:, vs], lane256)))
    o = jnp.concatenate(o_parts, axis=1)

    y_gla = _gla_output_gate(o, r, gnorm_ref[...])
    x1_ref[...] = _mix_out(x, y_pool, y_gla, w_out_ref, g_ref, b_ref)


def _mixer_sample(x_ref, hist_ref, s0_ref, wts, x1_ref, hist_out_ref, snew_ref, *, seq_len):
    w_in_ref, w_zr_ref, w_a2_ref, b_a_ref, w_pool_ref, pscale_ref, gnorm_ref, w_out_ref, g_ref, b_ref = wts
    G = SEQ_GROUP
    R = G * seq_len
    n_groups = x_ref.shape[0] // R
    x = x_ref[...]
    xb = x.astype(BF16)
    u, zr, q, k, v, r = _project_in(xb, w_in_ref, w_zr_ref)
    loga = _gate_log_decay(zr, w_a2_ref, b_a_ref)
    results = []
    gens = []
    for gi in range(n_groups):
        rs = slice(gi * R, (gi + 1) * R)
        hist_rows = pl.ds(gi * G * POOL_BUF, G * POOL_BUF)
        seqs = pl.ds(gi * G, G)
        gens.append(_sample_group(u[rs], q[rs], k[rs], v[rs], loga[rs], hist_ref.at[hist_rows], s0_ref.at[seqs],
                                  hist_out_ref.at[hist_rows], snew_ref.at[seqs], results, seq_len=seq_len))
    for _ in itertools.zip_longest(*gens):
        pass
    d_groups = [jnp.concatenate([res[0][g] for res in results], axis=0) for g in range(len(POOL_WINDOWS))]
    o = jnp.concatenate([res[1] for res in results], axis=0)
    y_pool = _pool_project(d_groups, w_pool_ref, pscale_ref)
    y_gla = _gla_output_gate(o, r, gnorm_ref[...])
    x1_ref[...] = _mix_out(x, y_pool, y_gla, w_out_ref, g_ref, b_ref)


def _sample_group(u, q, k, v, loga, hist_ref, s0_ref, hist_out_ref, snew_ref, results, *, seq_len):
    G, Ls = SEQ_GROUP, seq_len
    R = G * Ls
    NS = G * GLA_DK
    g_shift = G.bit_length() - 1
    hist_out_ref[0:(POOL_BUF - Ls) * G, :] = hist_ref[R:POOL_BUF * G, :]
    hist_out_ref[(POOL_BUF - Ls) * G:POOL_BUF * G, :] = u

    def blk(a, t):
        return a[t * G:(t + 1) * G]

    d_groups = []
    for g, w in enumerate(POOL_WINDOWS):
        cols = slice(g * POOL_GROUP, (g + 1) * POOL_GROUP)
        ug = u[:, cols]
        suffix = [None]
        acc = None
        for m in range(1, min(w - 1, POOL_BUF) + 1):
            h = hist_ref[(POOL_BUF - m) * G:(POOL_BUF - m + 1) * G, cols]
            acc = h if acc is None else acc + h
            suffix.append(acc)
        parts = []
        for t in range(Ls):
            wsum = blk(ug, t)
            for j in range(max(0, t - w + 1), t):
                wsum = wsum + blk(ug, j)
            m = w - 1 - t
            if m > 0:
                wsum = wsum + suffix[m]
            parts.append(wsum * (1.0 / w) - blk(ug, t))
        d_groups.append(jnp.concatenate(parts, axis=0))

    b_t = [blk(loga, 0)]
    for t in range(1, Ls):
        b_t.append(b_t[-1] + blk(loga, t))
    b = jnp.concatenate(b_t, axis=0)
    bend = jnp.concatenate([b_t[-1]] * Ls, axis=0)
    q_in = (q * jnp.exp(b)).astype(BF16)
    k_in = (k * jnp.exp(-b)).astype(BF16)
    k_st = k * jnp.exp(bend - b)
    dec_hi, dec_lo = _split_bf16(jnp.exp(bend))
    tok = lax.broadcasted_iota(jnp.int32, (R, D_GLA_K), 0) >> g_shift
    dec_rows = jnp.where(tok == Ls - 1, dec_hi, jnp.where(tok == Ls - 2, dec_lo, jnp.zeros_like(dec_lo)))
    vb = v.astype(BF16)

    ar = lax.broadcasted_iota(jnp.int32, (R, 2 * R), 0)
    ac = lax.broadcasted_iota(jnp.int32, (R, 2 * R), 1) & (R - 1)
    same_seq_causal = ((ac & (G - 1)) == (ar & (G - 1))) & ((ac >> g_shift) <= (ar >> g_shift))
    lane128 = lax.broadcasted_iota(jnp.int32, (R, 128), 1)
    lane256 = lax.broadcasted_iota(jnp.int32, (R, 256), 1)
    br = lax.broadcasted_iota(jnp.int32, (R, NS), 0)
    bcol = lax.broadcasted_iota(jnp.int32, (R, NS), 1)
    own_state = (bcol >> 6) == (br & (G - 1))
    ones_blk = jnp.ones((R, GLA_DV), BF16)
    zeros_blk = jnp.zeros((R, GLA_DV), BF16)

    def expand(xp, first):
        sw = pltpu.roll(xp, GLA_DK, 1)
        two = jnp.where(lane128 < GLA_DK, xp, sw) if first else jnp.where(lane128 < GLA_DK, sw, xp)
        rep = jnp.concatenate([two] * (NS // 128), axis=1)
        return jnp.where(own_state, rep, 0.0).astype(BF16)

    ks = [slice(128 * p, 128 * (p + 1)) for p in range(2)]
    vs = [slice(256 * p, 256 * (p + 1)) for p in range(2)]
    attn = []
    for p in range(2):
        a = _dot_nt(q_in[:, ks[p]], _head_pair_keys(k_in[:, ks[p]], lane128))
        attn.append(jnp.where(same_seq_causal, a, 0.0).astype(BF16))
    yield
    inter = []
    for h in range(GLA_HEADS):
        p, first = h // 2, h % 2 == 0
        s_flat = s0_ref[:, h].reshape(NS, GLA_DV)
        inter.append(_dot(expand(q_in[:, ks[p]].astype(F32), first), s_flat.astype(BF16)))
    for h in range(GLA_HEADS):
        p, first = h // 2, h % 2 == 0
        s_flat = s0_ref[:, h].reshape(NS, GLA_DV)
        lhs = jnp.concatenate([expand(k_st[:, ks[p]], first),
                               expand(dec_rows[:, ks[p]].astype(F32), first)], axis=0)
        v_h = vb[:, h * GLA_DV:(h + 1) * GLA_DV]
        rhs = jnp.concatenate([jnp.concatenate([v_h, zeros_blk], axis=1),
                               jnp.concatenate([zeros_blk, ones_blk], axis=1)], axis=0)
        ud = _dot_tn(lhs, rhs)
        s_new = ud[:, GLA_DV:] * s_flat + ud[:, :GLA_DV]
        snew_ref[:, h] = s_new.reshape(G, GLA_DK, GLA_DV)
    yield
    o_parts = []
    for p in range(2):
        o_intra = _dot(attn[p], _head_pair_values(vb[:, vs[p]], lane256))
        o_parts.append(o_intra + jnp.concatenate(inter[2 * p:2 * p + 2], axis=1))
    results.append((d_groups, jnp.concatenate(o_parts, axis=1)))


def _mixer_short_kernel(*refs, seq_len, fill_slabs):
    xm_ref, xs_ref, hist_ref, s0_ref = refs[:4]
    (w_in_t_f32_ref, w_zr_ref, w_a2_ref, b_a_ref, w_pool_ref, pscale_ref, gnorm_ref, w_out_f32_ref,
     g_ref, b_ref) = refs[4:4 + N_MIX_W]
    x1m_ref, um_ref, sm_ref, x1s_ref, us_ref, ss_ref, w_in_ref, w_out_ref = refs[-8:]
    wts = (w_in_ref, w_zr_ref, w_a2_ref, b_a_ref, w_pool_ref, pscale_ref, gnorm_ref, w_out_ref, g_ref, b_ref)
    i = pl.program_id(0)

    @pl.when(i == 0)
    def _meta():
        for c0 in range(0, C_Z, CAST_ROWS):
            w_in_ref[:, c0:c0 + CAST_ROWS] = jnp.transpose(w_in_t_f32_ref[c0:c0 + CAST_ROWS, :]).astype(BF16)
        for r0 in range(0, D_MODEL, CAST_ROWS):
            w_out_ref[r0:r0 + CAST_ROWS, :] = w_out_f32_ref[r0:r0 + CAST_ROWS, :].astype(BF16)
        _mixer_meta(xm_ref, wts, x1m_ref, um_ref, sm_ref)

    @pl.when(i > 0)
    def _sample():
        snew_ref = ss_ref.at[0] if fill_slabs else ss_ref
        _mixer_sample(xs_ref, hist_ref, s0_ref, wts, x1s_ref, us_ref, snew_ref, seq_len=seq_len)
        for a in range(1, fill_slabs):
            ss_ref[a] = ss_ref[0]


def _mixer_short(x_meta, x_samp, hist, s0, wts, layer, *, seq_len, s_stack=None):
    rows = x_samp.shape[0]
    n_groups = rows // (SEQ_GROUP * seq_len)
    per_step = GROUPS_PER_STEP if n_groups % GROUPS_PER_STEP == 0 else 1
    G = SEQ_GROUP * per_step
    R = G * seq_len
    n_tiles = rows // R
    assert rows % R == 0
    clamp = lambda i: jnp.maximum(i - 1, 0)
    n_layers = s0.shape[0]
    state_blk = (G, GLA_HEADS, GLA_DK, GLA_DV)
    wspecs = _mixer_weight_specs(layer)
    in_specs = ([pl.BlockSpec((N_META, D_MODEL), lambda i: (0, 0)),
                 pl.BlockSpec((R, D_MODEL), lambda i: (clamp(i), 0)),
                 pl.BlockSpec((None, G * POOL_BUF, D_POOL), lambda i: (layer, clamp(i), 0)),
                 pl.BlockSpec((None,) + state_blk, lambda i: (layer, clamp(i), 0, 0, 0))] + wspecs)
    args = [x_meta, x_samp, hist, s0, *wts]
    aliases = {}
    if s_stack is None:
        assert layer == 0
        state_spec = pl.BlockSpec((n_layers,) + state_blk, lambda i: (0, clamp(i), 0, 0, 0))
    else:
        state_spec = pl.BlockSpec((None,) + state_blk, lambda i: (layer, clamp(i), 0, 0, 0))
        in_specs.append(pl.BlockSpec(memory_space=pl.ANY))
        aliases = {len(args): 5}
        args.append(s_stack)
    n_args = len(args)
    body = functools.partial(_mixer_short_kernel, seq_len=seq_len, fill_slabs=n_layers if s_stack is None else 0)

    def kern(*refs):
        body(*refs[:4 + N_MIX_W], *refs[n_args:])

    whole = lambda shape: pl.BlockSpec(shape, lambda i: (0,) * len(shape))
    return pl.pallas_call(
        kern,
        grid=(n_tiles + 1,),
        in_specs=in_specs,
        out_specs=[whole((N_META, D_MODEL)),
                   whole((N_META, D_POOL)),
                   whole((GLA_HEADS, GLA_DK, GLA_DV)),
                   pl.BlockSpec((R, D_MODEL), lambda i: (clamp(i), 0)),
                   pl.BlockSpec((G * POOL_BUF, D_POOL), lambda i: (clamp(i), 0)),
                   state_spec,
                   whole((D_MODEL, C_Z)),
                   whole((D_MODEL, D_MODEL))],
        out_shape=[jax.ShapeDtypeStruct((N_META, D_MODEL), F32),
                   jax.ShapeDtypeStruct((N_META, D_POOL), F32),
                   jax.ShapeDtypeStruct((GLA_HEADS, GLA_DK, GLA_DV), F32),
                   jax.ShapeDtypeStruct((rows, D_MODEL), F32),
                   jax.ShapeDtypeStruct((n_tiles * G * POOL_BUF, D_POOL), F32),
                   jax.ShapeDtypeStruct(s0.shape, F32),
                   jax.ShapeDtypeStruct((D_MODEL, C_Z), BF16),
                   jax.ShapeDtypeStruct((D_MODEL, D_MODEL), BF16)],
        input_output_aliases=aliases,
        compiler_params=pltpu.CompilerParams(dimension_semantics=("arbitrary",),
                                             vmem_limit_bytes=VMEM_LIMIT),
        name="mixer_short",
    )(*args)


def _ffn_tile(x_ref, wts, y_ref, conv_inputs, store_gate):
    w_up_ref, w_gate_ref, cw_ref, cb_ref, w_down_ref, g_ref, b_ref = wts
    x = x_ref[...]
    xb = x.astype(BF16)

    acts = []
    for j in range(D_FF // FF_CHUNK):
        cs = slice(j * FF_CHUNK, (j + 1) * FF_CHUNK)
        a = _dot(xb, w_up_ref[j])
        gt = _dot(xb, w_gate_ref[j])
        g1, g2 = conv_inputs(gt, cs)
        store_gate(gt, cs)
        gc = cb_ref[:, cs] + cw_ref[0:1, cs] * g2 + cw_ref[1:2, cs] * g1 + cw_ref[2:3, cs] * gt
        acts.append((a * _silu(gc)).astype(BF16))
    yield

    act = jnp.concatenate(acts, axis=1)
    rows = x.shape[0]
    rb = min(rows, DOWN_ROWS)
    for r0 in range(0, rows, rb):
        f = _dot(act[r0:r0 + rb], w_down_ref[...])
        y_ref[r0:r0 + rb, :] = _layer_norm(ALPHA * x[r0:r0 + rb] + f, g_ref[...], b_ref[...])


def _ffn_long_kernel(*refs, n_par):
    x_ref, cprev_ref = refs[:2]
    wts = refs[2:2 + N_FFN_W]
    y_ref, hist_out_ref, gbuf = refs[2 + N_FFN_W:]
    t = pl.program_id(1)
    T = x_ref.shape[1]

    @pl.when(t == 0)
    def _init():
        for j in range(n_par):
            gbuf[j, 0:CONV_PAD, :] = cprev_ref[...]

    def make_tile(j):
        def conv_inputs(gt, cs):
            gbuf[j, CONV_PAD:CONV_PAD + T, cs] = gt
            return gbuf[j, CONV_PAD - 1:CONV_PAD - 1 + T, cs], gbuf[j, CONV_PAD - 2:CONV_PAD - 2 + T, cs]

        def store_gate(gt, cs):
            gbuf[j, 0:CONV_PAD, cs] = gbuf[j, T:T + CONV_PAD, cs]

        return _ffn_tile(x_ref.at[j], wts, y_ref.at[j], conv_inputs, store_gate)

    for _ in itertools.zip_longest(*[make_tile(j) for j in range(n_par)]):
        pass

    @pl.when(t == pl.num_programs(1) - 1)
    def _final():
        for j in range(n_par):
            hist_out_ref[j] = gbuf[j, CONV_PAD - CONV_BUF:CONV_PAD, :]


def _ffn_short_kernel(xm_ref, xs_ref, hist_ref, w_up_ref, w_gate_ref, cw_ref, cb_ref, w_down_ref, g_ref, b_ref,
                      ym_ref, cm_ref, ys_ref, cs_ref, fm_ref, fs_ref, *, seq_len):
    j = pl.program_id(0)
    G, Ls = SEQ_GROUP, seq_len
    R = G * Ls
    rows = xs_ref.shape[0]
    n_tiles = rows // R

    w_up = w_up_ref[...]
    w_gate = w_gate_ref[...]
    w_down = w_down_ref[...]

    @pl.when(j == 0)
    def _first():
        fm_ref[...] = jnp.zeros_like(fm_ref)
        fs_ref[...] = jnp.zeros_like(fs_ref)

    n_parts = SHORT_FFN_PARTS if n_tiles % SHORT_FFN_PARTS == 0 else 1
    part_tiles = n_tiles // n_parts
    part_rows = part_tiles * R
    hist_rows = CONV_BUF * G
    tok = (lax.broadcasted_iota(jnp.int32, (part_rows, FF_CHUNK), 0) & (R - 1)) >> (G.bit_length() - 1)
    zeros_tail = jnp.zeros((R - hist_rows, FF_CHUNK), F32)

    def sample_conv(gt, part):
        first = part * part_tiles
        hx = jnp.concatenate([piece for n in range(first, first + part_tiles)
                              for piece in (hist_ref[n * hist_rows:(n + 1) * hist_rows, :], zeros_tail)], axis=0)
        g1 = jnp.where(tok >= 1, _roll_rows(gt, G), _roll_rows(hx, -G))
        g2 = jnp.where(tok >= 2, _roll_rows(gt, 2 * G), hx)
        return g1, g2

    def sample_store(gt, part):
        for n in range(part_tiles):
            dst = (part * part_tiles + n) * hist_rows
            cs_ref[dst:dst + hist_rows, :] = gt[n * R + (Ls - CONV_BUF) * G:(n + 1) * R]

    rowm = lax.broadcasted_iota(jnp.int32, (N_META, FF_CHUNK), 0)

    def meta_conv(gt):
        return (jnp.where(rowm >= 1, _roll_rows(gt, 1), 0.0), jnp.where(rowm >= 2, _roll_rows(gt, 2), 0.0))

    def meta_store(gt):
        cm_ref[...] = gt[N_META - CONV_PAD:N_META]

    parts = [(xs_ref.at[pl.ds(p * part_rows, part_rows)], fs_ref.at[pl.ds(p * part_rows, part_rows)],
              functools.partial(sample_conv, part=p), functools.partial(sample_store, part=p))
             for p in range(n_parts)]
    parts.append((xm_ref, fm_ref, meta_conv, meta_store))

    proj = []
    for x_ref, _, _, _ in parts:
        xb = x_ref[...].astype(BF16)
        proj.append((_dot(xb, w_up), _dot(xb, w_gate)))
    acts = []
    for (a, gt), (_, _, conv_inputs, store_gate) in zip(proj, parts):
        g1, g2 = conv_inputs(gt)
        store_gate(gt)
        gc = cb_ref[...] + cw_ref[0:1, :] * g2 + cw_ref[1:2, :] * g1 + cw_ref[2:3, :] * gt
        acts.append((a * _silu(gc)).astype(BF16))
    for act, (_, f_ref, _, _) in zip(acts, parts):
        f_ref[...] += _dot(act, w_down)

    @pl.when(j == pl.num_programs(0) - 1)
    def _last():
        ys_ref[...] = _layer_norm(ALPHA * xs_ref[...] + fs_ref[...], g_ref[...], b_ref[...])
        ym_ref[...] = _layer_norm(ALPHA * xm_ref[...] + fm_ref[...], g_ref[...], b_ref[...])


def _ffn_weight_specs(layer):
    whole = lambda shape: pl.BlockSpec(shape, lambda *_: (0,) * len(shape), pipeline_mode=pl.Buffered(1))
    chunked = (D_FF // FF_CHUNK, D_MODEL, FF_CHUNK)
    specs = [whole(chunked), whole(chunked), _layer_spec((3, D_FF), layer), _layer_spec((1, D_FF), layer),
             whole((D_FF, D_MODEL)), _layer_spec((1, D_MODEL), layer), _layer_spec((1, D_MODEL), layer)]
    assert len(specs) == N_FFN_W
    return specs


def _ffn_long(x, cprev, wts, layer):
    B, L, _ = x.shape
    T = min(FFN_TILE, L)
    P = FFN_PAR if B % FFN_PAR == 0 else 1
    assert L % T == 0
    kern = functools.partial(_ffn_long_kernel, n_par=P)
    return pl.pallas_call(
        kern,
        grid=(B // P, L // T),
        in_specs=[pl.BlockSpec((P, T, D_MODEL), lambda b, t: (b, t, 0)),
                  pl.BlockSpec((CONV_PAD, D_FF), lambda b, t: (0, 0))] + _ffn_weight_specs(layer),
        out_specs=[pl.BlockSpec((P, T, D_MODEL), lambda b, t: (b, t, 0)),
                   pl.BlockSpec((P, CONV_BUF, D_FF), lambda b, t: (b, 0, 0))],
        out_shape=[jax.ShapeDtypeStruct((B, L, D_MODEL), F32),
                   jax.ShapeDtypeStruct((B, CONV_BUF, D_FF), F32)],
        scratch_shapes=[pltpu.VMEM((P, T + CONV_PAD, D_FF), F32)],
        compiler_params=pltpu.CompilerParams(dimension_semantics=("arbitrary", "arbitrary"),
                                             vmem_limit_bytes=VMEM_LIMIT),
        name="ffn_long",
    )(x, cprev, *wts)


def _ffn_short(x_meta, x_samp, hist, wts, layer, *, seq_len):
    rows = x_samp.shape[0]
    n_hist = hist.shape[1]
    C = FF_CHUNK
    const = lambda shape: pl.BlockSpec(shape, lambda j: (0,) * len(shape))
    kern = functools.partial(_ffn_short_kernel, seq_len=seq_len)
    return pl.pallas_call(
        kern,
        grid=(D_FF // C,),
        in_specs=[const((N_META, D_MODEL)),
                  const((rows, D_MODEL)),
                  pl.BlockSpec((None, n_hist, C), lambda j: (layer, 0, j)),
                  pl.BlockSpec((None, D_MODEL, C), lambda j: (j, 0, 0)),
                  pl.BlockSpec((None, D_MODEL, C), lambda j: (j, 0, 0)),
                  pl.BlockSpec((None, 3, C), lambda j: (layer, 0, j)),
                  pl.BlockSpec((None, 1, C), lambda j: (layer, 0, j)),
                  pl.BlockSpec((C, D_MODEL), lambda j: (j, 0)),
                  pl.BlockSpec((None, 1, D_MODEL), lambda j: (layer, 0, 0)),
                  pl.BlockSpec((None, 1, D_MODEL), lambda j: (layer, 0, 0))],
        out_specs=[const((N_META, D_MODEL)),
                   pl.BlockSpec((CONV_PAD, C), lambda j: (0, j)),
                   const((rows, D_MODEL)),
                   pl.BlockSpec((n_hist, C), lambda j: (0, j))],
        out_shape=[jax.ShapeDtypeStruct((N_META, D_MODEL), F32),
                   jax.ShapeDtypeStruct((CONV_PAD, D_FF), F32),
                   jax.ShapeDtypeStruct((rows, D_MODEL), F32),
                   jax.ShapeDtypeStruct((n_hist, D_FF), F32)],
        scratch_shapes=[pltpu.VMEM((N_META, D_MODEL), F32),
                        pltpu.VMEM((rows, D_MODEL), F32)],
        compiler_params=pltpu.CompilerParams(dimension_semantics=("arbitrary",),
                                             vmem_limit_bytes=VMEM_LIMIT),
        name="ffn_short",
    )(x_meta, x_samp, hist, *wts)


def _to_group_major(a, axis):
    n, j = a.shape[axis], a.shape[axis + 1]
    lead, tail = a.shape[:axis], a.shape[axis + 2:]
    a = a.reshape(*lead, n // SEQ_GROUP, SEQ_GROUP, j, *tail)
    a = jnp.swapaxes(a, axis + 1, axis + 2)
    return a.reshape(*lead, n * j, *tail)


def _from_group_major(a, axis, j):
    rows = a.shape[axis]
    n = rows // j
    lead, tail = a.shape[:axis], a.shape[axis + 1:]
    a = a.reshape(*lead, n // SEQ_GROUP, j, SEQ_GROUP, *tail)
    a = jnp.swapaxes(a, axis + 1, axis + 2)
    return a.reshape(*lead, n, j, *tail)


def kernel(x_prompt, x_sample, state_pool, state_gla, state_conv, meta_tokens,
           w_in, w_a2, b_a, w_pool, pool_scale, gla_norm, w_out, ln1_g, ln1_b,
           w_up, w_gate, conv_w, conv_b, w_down, ln2_g, ln2_b):
    NB, LS = x_sample.shape[0], x_sample.shape[1]
    assert NB % SEQ_GROUP == 0 and LS & (LS - 1) == 0 and CONV_BUF <= LS <= POOL_BUF
    assert PAST_LEN >= POOL_BUF and N_META > POOL_BUF

    row = lambda a: a.reshape(DEPTH, 1, a.shape[-1])
    mix_w = (jnp.swapaxes(w_in, 1, 2), w_in[:, :, C_Z:].astype(BF16), w_a2.astype(BF16), row(b_a),
             w_pool.astype(BF16), row(pool_scale), row(gla_norm), w_out, row(ln1_g), row(ln1_b))
    ffn_w = (w_up, w_gate, conv_w, row(conv_b), w_down, row(ln2_g), row(ln2_b))

    hm = meta_tokens.astype(F32)
    hp = x_prompt
    hs = _to_group_major(x_sample, 0)
    pool_hist = _to_group_major(state_pool, 1)
    conv_hist = _to_group_major(state_conv, 1)

    pp, gp, cp, ps_l, cs_l = [], [], [], [], []
    gs = None
    for l in range(DEPTH):
        hm1, um, sm, hs1, ps_new, gs, w_in_bf, w_out_bf = _mixer_short(hm, hs, pool_hist, state_gla, mix_w, l,
                                                                       seq_len=LS, s_stack=gs)
        hp1, pbuf, snew, w_up_bf, w_gate_bf, w_down_bf = _mixer_long(
            hp, um, sm, (w_in_bf,) + mix_w[1:7] + (w_out_bf,) + mix_w[8:], (ffn_w[0], ffn_w[1], ffn_w[4]), l)
        ffn_wl = (w_up_bf, w_gate_bf) + ffn_w[2:4] + (w_down_bf,) + ffn_w[5:]
        hm, cm, hs, cs_new = _ffn_short(hm1, hs1, conv_hist, ffn_wl, l, seq_len=LS)
        ps_l.append(ps_new)
        cs_l.append(cs_new)
        hp, cbuf = _ffn_long(hp1, cm, ffn_wl, l)
        pp.append(pbuf)
        gp.append(snew)
        cp.append(cbuf)

    ps = _from_group_major(jnp.stack(ps_l), 1, POOL_BUF)
    cs = _from_group_major(jnp.stack(cs_l), 1, CONV_BUF)
    return (hp, _from_group_major(hs, 0, LS), jnp.stack(pp), jnp.stack(gp), jnp.stack(cp), ps, gs, cs)
```

```python
import functools
import itertools

import jax
import jax.numpy as jnp
from jax import lax
from jax.experimental import pallas as pl
from jax.experimental.pallas import tpu as pltpu

F32 = jnp.float32
BF16 = jnp.bfloat16

D_MODEL = 1024
N_META = 16
D_POOL = 512
POOL_WINDOWS = (2, 4, 8, 16)
POOL_GROUP = 128
POOL_BUF = 15
POOL_PAD = 16
GLA_HEADS = 4
GLA_DV = 128
GLA_DK = 64
D_GLA_K = 256
GATE_RANK = 16
GATE_TAU = 16.0
D_FF = 2816
CONV_BUF = 2
CONV_PAD = 8
DEPTH = 2
ALPHA = (2 * DEPTH) ** 0.25
LN_EPS = 1e-5
RMS_EPS = 1e-6
PAST_LEN = 16384

C_POOL, C_Q, C_K, C_V, C_R, C_Z, C_END = 0, 512, 768, 1024, 1536, 2048, 2064

LONG_TILE = 512
LONG_CHUNK = 64
MIXER_PAR = 2
MIXER_SKEW = 1
FFN_TILE = 512
FFN_PAR = 2
SEQ_GROUP = 16
GROUPS_PER_STEP = 2
FF_CHUNK = 256
DOWN_ROWS = 256
CAST_ROWS = 128
SHORT_FFN_PARTS = 4
V7X_VMEM_BYTES = 64 * 1024 * 1024
VMEM_LIMIT = V7X_VMEM_BYTES - 8 * 1024 * 1024

N_MIX_W = 10
N_FFN_W = 7


def _dot(a, b):
    return jnp.dot(a, b, preferred_element_type=F32)


def _dot_nt(a, b):
    return lax.dot_general(a, b, (((1,), (1,)), ((), ())), preferred_element_type=F32)


def _dot_tn(a, b):
    return lax.dot_general(a, b, (((0,), (0,)), ((), ())), preferred_element_type=F32)


def _layer_norm(y, g, b):
    mu = jnp.mean(y, axis=-1, keepdims=True)
    yc = y - mu
    var = jnp.mean(yc * yc, axis=-1, keepdims=True)
    return yc * lax.rsqrt(var + LN_EPS) * g + b


def _silu(x):
    h = 0.5 * x
    return h + h * jnp.tanh(h)


def _log_sigmoid(z):
    return jnp.minimum(z, 0.0) - jnp.log(1.0 + jnp.exp(-jnp.abs(z)))


def _roll_rows(x, shift):
    n = x.shape[0]
    return pltpu.roll(x, shift % n, 0)


def _split_bf16(x):
    hi = x.astype(BF16)
    lo = (x - hi.astype(F32)).astype(BF16)
    return hi, lo


def _project_in(xb, w_in_ref, w_zr_ref):
    u = _dot(xb, w_in_ref[:, C_POOL:C_Q])
    zr = _dot(xb, w_zr_ref[...])
    q = _dot(xb, w_in_ref[:, C_Q:C_K]) * (GLA_DK ** -0.5)
    k = _dot(xb, w_in_ref[:, C_K:C_V])
    v = _dot(xb, w_in_ref[:, C_V:C_R])
    r = _dot(xb, w_in_ref[:, C_R:C_Z])
    return u, zr, q, k, v, r


def _gate_log_decay(zr, w_a2_ref, b_a_ref):
    z = _dot(zr.astype(BF16), w_a2_ref[...]) + b_a_ref[...]
    return _log_sigmoid(z) * (1.0 / GATE_TAU)


def _gla_output_gate(o, r, gnorm):
    parts = []
    for h in range(GLA_HEADS):
        oh = o[:, h * GLA_DV:(h + 1) * GLA_DV]
        ms = jnp.mean(oh * oh, axis=-1, keepdims=True)
        parts.append(oh * lax.rsqrt(ms + RMS_EPS) * gnorm)
    return jnp.concatenate(parts, axis=1) * _silu(r)


def _pool_project(d_groups, w_pool_ref, pscale_ref):
    ys = [_dot(d.astype(BF16), w_pool_ref[g]) for g, d in enumerate(d_groups)]
    return jnp.concatenate(ys, axis=1) * pscale_ref[...]


def _mix_out(x, y_pool, y_gla, w_out_ref, g_ref, b_ref):
    mix = jnp.concatenate([y_pool, y_gla], axis=1).astype(BF16)
    return _layer_norm(ALPHA * x + _dot(mix, w_out_ref[...]), g_ref[...], b_ref[...])


def _emit_staggered(stage_gens, first_round):
    live = list(zip(first_round, stage_gens))
    rnd = 0
    while live:
        for start, g in list(live):
            if rnd >= start and next(g, StopIteration) is StopIteration:
                live.remove((start, g))
        rnd += 1


def _head_pair_keys(k_p, lane128):
    zk = jnp.zeros_like(k_p)
    return jnp.concatenate([jnp.where(lane128 < GLA_DK, k_p, zk), jnp.where(lane128 >= GLA_DK, k_p, zk)], axis=0)


def _head_pair_values(v_p, lane256):
    zv = jnp.zeros_like(v_p)
    return jnp.concatenate([jnp.where(lane256 < GLA_DV, v_p, zv), jnp.where(lane256 >= GLA_DV, v_p, zv)], axis=0)


def _mixer_long_kernel(*refs, n_par, tile, chunk):
    x_ref, pprev_ref, s0_ref = refs[:3]
    wts = refs[3:3 + N_MIX_W]
    ffn_f32 = refs[3 + N_MIX_W:6 + N_MIX_W]
    x1_ref, pbuf_ref, snew_ref = refs[6 + N_MIX_W:9 + N_MIX_W]
    ffn_bf16 = refs[9 + N_MIX_W:12 + N_MIX_W]
    ubuf, sbd = refs[12 + N_MIX_W:]
    t = pl.program_id(1)
    zero_blk = jnp.zeros((GLA_DK, GLA_DV), F32)

    w_up_ref, w_gate_ref, w_down_ref = ffn_f32
    w_up_bf_ref, w_gate_bf_ref, w_down_bf_ref = ffn_bf16
    for src, dst in ((w_up_ref, w_up_bf_ref), (w_gate_ref, w_gate_bf_ref)):
        for c in range(D_FF // FF_CHUNK):
            dst[c] = src[:, c * FF_CHUNK:(c + 1) * FF_CHUNK].astype(BF16)
    w_down_bf_ref[...] = w_down_ref[...].astype(BF16)

    @pl.when(t == 0)
    def _init():
        for j in range(n_par):
            ubuf[j, 0:POOL_PAD, :] = pprev_ref[...]
            for p in range(2):
                top = jnp.concatenate([s0_ref[2 * p], zero_blk], axis=1)
                bot = jnp.concatenate([zero_blk, s0_ref[2 * p + 1]], axis=1)
                sbd[j, p] = jnp.concatenate([top, bot], axis=0)

    tiles = [_mixer_long_tile(x_ref.at[j], wts, x1_ref.at[j], ubuf.at[j], sbd.at[j], tile=tile, chunk=chunk)
             for j in range(n_par)]
    _emit_staggered(tiles, [MIXER_SKEW * j for j in range(n_par)])

    @pl.when(t == pl.num_programs(1) - 1)
    def _final():
        for j in range(n_par):
            pbuf_ref[j] = ubuf[j, POOL_PAD - POOL_BUF:POOL_PAD, :]
            for p in range(2):
                s_p = sbd[j, p]
                snew_ref[j, 2 * p] = s_p[0:GLA_DK, 0:GLA_DV]
                snew_ref[j, 2 * p + 1] = s_p[GLA_DK:2 * GLA_DK, GLA_DV:2 * GLA_DV]


def _mixer_long_tile(x_ref, wts, x1_ref, ubuf, sbd, *, tile, chunk):
    w_in_ref, w_zr_ref, w_a2_ref, b_a_ref, w_pool_ref, pscale_ref, gnorm_ref, w_out_ref, g_ref, b_ref = wts
    T, C = tile, chunk
    x = x_ref[...]
    xb = x.astype(BF16)

    u, zr, q, k, v, r = _project_in(xb, w_in_ref, w_zr_ref)
    yield

    ubuf[POOL_PAD:POOL_PAD + T, :] = u
    d_groups = []
    for g, w in enumerate(POOL_WINDOWS):
        s = ubuf[:, g * POOL_GROUP:(g + 1) * POOL_GROUP]
        sh = 1
        while sh < w:
            s = s + _roll_rows(s, sh)
            sh *= 2
        d_groups.append(s[POOL_PAD:, :] * (1.0 / w) - u[:, g * POOL_GROUP:(g + 1) * POOL_GROUP])
    y_pool = _pool_project(d_groups, w_pool_ref, pscale_ref)
    ubuf[0:POOL_PAD, :] = ubuf[T:T + POOL_PAD, :]
    loga = _gate_log_decay(zr, w_a2_ref, b_a_ref)

    tr = lax.broadcasted_iota(jnp.int32, (C, C), 0)
    tc = lax.broadcasted_iota(jnp.int32, (C, C), 1)
    tri = jnp.where(tc <= tr, 1.0, 0.0).astype(BF16)
    ar = lax.broadcasted_iota(jnp.int32, (C, 2 * C), 0)
    ac = lax.broadcasted_iota(jnp.int32, (C, 2 * C), 1) & (C - 1)
    causal = ac <= ar
    lane128 = lax.broadcasted_iota(jnp.int32, (C, 128), 1)
    lane256 = lax.broadcasted_iota(jnp.int32, (C, 256), 1)
    sr = lax.broadcasted_iota(jnp.int32, (128, 256), 0)
    sc = lax.broadcasted_iota(jnp.int32, (128, 256), 1)
    blockdiag = (sr >= GLA_DK) == (sc >= GLA_DV)
    mid = C // 2 - 1

    n_chunks = T // C
    pairs = [(c, p) for c in range(n_chunks) for p in range(2)]
    ks = [slice(128 * p, 128 * (p + 1)) for p in range(2)]
    vs = [slice(256 * p, 256 * (p + 1)) for p in range(2)]

    bcs = []
    for c in range(n_chunks):
        la_hi, la_lo = _split_bf16(loga[c * C:(c + 1) * C])
        bb = _dot(tri, jnp.concatenate([la_hi, la_lo], axis=1))
        bcs.append(bb[:, :D_GLA_K] + bb[:, D_GLA_K:])
    yield

    q_in, k_in, q_st, k_st, dec_t, vc = [], [], [], [], [], []
    for c in range(n_chunks):
        bc = bcs[c]
        bmid = bc[mid:mid + 1]
        bend = bc[C - 1:C]
        qc = q[c * C:(c + 1) * C]
        kc = k[c * C:(c + 1) * C]
        q_in.append((qc * jnp.exp(bc - bmid)).astype(BF16))
        k_in.append((kc * jnp.exp(bmid - bc)).astype(BF16))
        q_st.append((qc * jnp.exp(bc)).astype(BF16))
        k_st.append((kc * jnp.exp(bend - bc)).astype(BF16))
        dec_t.append(jnp.transpose(jnp.broadcast_to(jnp.exp(bend), (128, D_GLA_K))))
        vc.append(v[c * C:(c + 1) * C].astype(BF16))
    yield

    attn, upd = {}, {}
    for c, p in pairs:
        a = _dot_nt(q_in[c][:, ks[p]], _head_pair_keys(k_in[c][:, ks[p]], lane128))
        attn[c, p] = jnp.where(causal, a, 0.0).astype(BF16)
    for c, p in pairs:
        u_cp = _dot_tn(k_st[c][:, ks[p]], vc[c][:, vs[p]])
        upd[c, p] = jnp.where(blockdiag, u_cp, 0.0)
    yield

    s_vals = [sbd[p] for p in range(2)]
    s_start = {}
    for c, p in pairs:
        s_start[c, p] = s_vals[p].astype(BF16)
        dec_p = dec_t[c][ks[p], :]
        s_vals[p] = jnp.concatenate([dec_p, dec_p], axis=1) * s_vals[p] + upd[c, p]
    for p in range(2):
        sbd[p] = s_vals[p]

    o_rows = [[], []]
    for c, p in pairs:
        vblk = _head_pair_values(vc[c][:, vs[p]], lane256)
        lhs = jnp.concatenate([attn[c, p], q_st[c][:, ks[p]]], axis=1)
        rhs = jnp.concatenate([vblk, s_start[c, p]], axis=0)
        o_rows[p].append(_dot(lhs, rhs))
    o = jnp.concatenate([jnp.concatenate(o_rows[p], axis=0) for p in range(2)], axis=1)
    yield

    y_gla = _gla_output_gate(o, r, gnorm_ref[...])
    x1_ref[...] = _mix_out(x, y_pool, y_gla, w_out_ref, g_ref, b_ref)


def _layer_spec(shape, layer):
    nd = len(shape)
    return pl.BlockSpec((None,) + shape, lambda *_: (layer,) + (0,) * nd, pipeline_mode=pl.Buffered(1))


def _mixer_weight_specs(layer, projections_stacked=True):
    shapes = [(C_END, D_MODEL) if projections_stacked else (D_MODEL, C_Z), (D_MODEL, GATE_RANK),
              (GATE_RANK, D_GLA_K), (1, D_GLA_K), (4, POOL_GROUP, POOL_GROUP), (1, D_POOL), (1, GLA_DV),
              (D_MODEL, D_MODEL), (1, D_MODEL), (1, D_MODEL)]
    assert len(shapes) == N_MIX_W
    specs = [_layer_spec(s, layer) for s in shapes]
    if not projections_stacked:
        for i in (0, 7):
            specs[i] = pl.BlockSpec(shapes[i], lambda *_: (0, 0), pipeline_mode=pl.Buffered(1))
    return specs


def _mixer_long(x, pprev, s0, wts, ffn_f32, layer):
    B, L, _ = x.shape
    T = min(LONG_TILE, L)
    P = MIXER_PAR if B % MIXER_PAR == 0 else 1
    assert L % T == 0 and T % LONG_CHUNK == 0
    n_time = L // T
    n_steps = (B // P) * n_time
    up_rows, down_rows = D_MODEL // n_steps, D_FF // n_steps
    assert up_rows * n_steps == D_MODEL and down_rows * n_steps == D_FF and up_rows % 16 == 0 and down_rows % 16 == 0
    n_chunks = D_FF // FF_CHUNK
    step = lambda b, t: b * n_time + t
    kern = functools.partial(_mixer_long_kernel, n_par=P, tile=T, chunk=LONG_CHUNK)
    return pl.pallas_call(
        kern,
        grid=(B // P, n_time),
        in_specs=[pl.BlockSpec((P, T, D_MODEL), lambda b, t: (b, t, 0)),
                  pl.BlockSpec((POOL_PAD, D_POOL), lambda b, t: (0, 0)),
                  pl.BlockSpec((GLA_HEADS, GLA_DK, GLA_DV), lambda b, t: (0, 0, 0))]
        + _mixer_weight_specs(layer, projections_stacked=False)
        + [pl.BlockSpec((None, up_rows, D_FF), lambda b, t: (layer, step(b, t), 0)),
           pl.BlockSpec((None, up_rows, D_FF), lambda b, t: (layer, step(b, t), 0)),
           pl.BlockSpec((None, down_rows, D_MODEL), lambda b, t: (layer, step(b, t), 0))],
        out_specs=[pl.BlockSpec((P, T, D_MODEL), lambda b, t: (b, t, 0)),
                   pl.BlockSpec((P, POOL_BUF, D_POOL), lambda b, t: (b, 0, 0)),
                   pl.BlockSpec((P, GLA_HEADS, GLA_DK, GLA_DV), lambda b, t: (b, 0, 0, 0)),
                   pl.BlockSpec((n_chunks, up_rows, FF_CHUNK), lambda b, t: (0, step(b, t), 0)),
                   pl.BlockSpec((n_chunks, up_rows, FF_CHUNK), lambda b, t: (0, step(b, t), 0)),
                   pl.BlockSpec((down_rows, D_MODEL), lambda b, t: (step(b, t), 0))],
        out_shape=[jax.ShapeDtypeStruct((B, L, D_MODEL), F32),
                   jax.ShapeDtypeStruct((B, POOL_BUF, D_POOL), F32),
                   jax.ShapeDtypeStruct((B, GLA_HEADS, GLA_DK, GLA_DV), F32),
                   jax.ShapeDtypeStruct((n_chunks, D_MODEL, FF_CHUNK), BF16),
                   jax.ShapeDtypeStruct((n_chunks, D_MODEL, FF_CHUNK), BF16),
                   jax.ShapeDtypeStruct((D_FF, D_MODEL), BF16)],
        scratch_shapes=[pltpu.VMEM((P, T + POOL_PAD, D_POOL), F32),
                        pltpu.VMEM((P, 2, 2 * GLA_DK, 2 * GLA_DV), F32)],
        compiler_params=pltpu.CompilerParams(dimension_semantics=("arbitrary", "arbitrary"),
                                             vmem_limit_bytes=VMEM_LIMIT),
        name="mixer_long",
    )(x, pprev, s0, *wts, *ffn_f32)


def _mixer_meta(x_ref, wts, x1_ref, u_ref, s_ref):
    w_in_ref, w_zr_ref, w_a2_ref, b_a_ref, w_pool_ref, pscale_ref, gnorm_ref, w_out_ref, g_ref, b_ref = wts
    L = N_META
    x = x_ref[...]
    xb = x.astype(BF16)
    u, zr, q, k, v, r = _project_in(xb, w_in_ref, w_zr_ref)
    u_ref[...] = u

    row128 = lax.broadcasted_iota(jnp.int32, (L, POOL_GROUP), 0)
    pos1 = lax.broadcasted_iota(jnp.int32, (L, 1), 0)
    d_groups = []
    for g, w in enumerate(POOL_WINDOWS):
        ug = u[:, g * POOL_GROUP:(g + 1) * POOL_GROUP]
        s = ug
        sh = 1
        while sh < w:
            s = s + jnp.where(row128 >= sh, _roll_rows(s, sh), 0.0)
            sh *= 2
        d_groups.append(s / jnp.minimum(w, pos1 + 1).astype(F32) - ug)
    y_pool = _pool_project(d_groups, w_pool_ref, pscale_ref)

    loga = _gate_log_decay(zr, w_a2_ref, b_a_ref)
    row256 = lax.broadcasted_iota(jnp.int32, (L, D_GLA_K), 0)
    b = loga
    sh = 1
    while sh < L:
        b = b + jnp.where(row256 >= sh, _roll_rows(b, sh), 0.0)
        sh *= 2
    bend = b[L - 1:L]
    q_in = (q * jnp.exp(b)).astype(BF16)
    k_in = (k * jnp.exp(-b)).astype(BF16)
    k_st = (k * jnp.exp(bend - b)).astype(BF16)
    vb = v.astype(BF16)
    ar = lax.broadcasted_iota(jnp.int32, (L, 2 * L), 0)
    ac = lax.broadcasted_iota(jnp.int32, (L, 2 * L), 1) & (L - 1)
    causal = ac <= ar
    lane128 = lax.broadcasted_iota(jnp.int32, (L, 128), 1)
    lane256 = lax.broadcasted_iota(jnp.int32, (L, 256), 1)
    attn = []
    for p in range(2):
        ks = slice(128 * p, 128 * (p + 1))
        a = _dot_nt(q_in[:, ks], _head_pair_keys(k_in[:, ks], lane128))
        attn.append(jnp.where(causal, a, 0.0).astype(BF16))
    o_parts = []
    for p in range(2):
        ks = slice(128 * p, 128 * (p + 1))
        vs = slice(256 * p, 256 * (p + 1))
        s_p = _dot_tn(k_st[:, ks], vb[:, vs])
        s_ref[2 * p] = s_p[0:GLA_DK, 0:GLA_DV]
        s_ref[2 * p + 1] = s_p[GLA_DK:2 * GLA_DK, GLA_DV:2 * GLA_DV]
        o_parts.append(_dot(attn[p], _head_pair_values(vb[:, vs], lane256)))
    o = jnp.concatenate(o_parts, axis=1)

    y_gla = _gla_output_gate(o, r, gnorm_ref[...])
    x1_ref[...] = _mix_out(x, y_pool, y_gla, w_out_ref, g_ref, b_ref)


def _mixer_sample(x_ref, hist_ref, s0_ref, wts, x1_ref, hist_out_ref, snew_ref, *, seq_len):
    w_in_ref, w_zr_ref, w_a2_ref, b_a_ref, w_pool_ref, pscale_ref, gnorm_ref, w_out_ref, g_ref, b_ref = wts
    G = SEQ_GROUP
    R = G * seq_len
    n_groups = x_ref.shape[0] // R
    x = x_ref[...]
    xb = x.astype(BF16)
    u, zr, q, k, v, r = _project_in(xb, w_in_ref, w_zr_ref)
    loga = _gate_log_decay(zr, w_a2_ref, b_a_ref)
    results = []
    gens = []
    for gi in range(n_groups):
        rs = slice(gi * R, (gi + 1) * R)
        hist_rows = pl.ds(gi * G * POOL_BUF, G * POOL_BUF)
        seqs = pl.ds(gi * G, G)
        gens.append(_sample_group(u[rs], q[rs], k[rs], v[rs], loga[rs], hist_ref.at[hist_rows], s0_ref.at[seqs],
                                  hist_out_ref.at[hist_rows], snew_ref.at[seqs], results, seq_len=seq_len))
    for _ in itertools.zip_longest(*gens):
        pass
    d_groups = [jnp.concatenate([res[0][g] for res in results], axis=0) for g in range(len(POOL_WINDOWS))]
    o = jnp.concatenate([res[1] for res in results], axis=0)
    y_pool = _pool_project(d_groups, w_pool_ref, pscale_ref)
    y_gla = _gla_output_gate(o, r, gnorm_ref[...])
    x1_ref[...] = _mix_out(x, y_pool, y_gla, w_out_ref, g_ref, b_ref)


def _sample_group(u, q, k, v, loga, hist_ref, s0_ref, hist_out_ref, snew_ref, results, *, seq_len):
    G, Ls = SEQ_GROUP, seq_len
    R = G * Ls
    NS = G * GLA_DK
    g_shift = G.bit_length() - 1
    hist_out_ref[0:(POOL_BUF - Ls) * G, :] = hist_ref[R:POOL_BUF * G, :]
    hist_out_ref[(POOL_BUF - Ls) * G:POOL_BUF * G, :] = u

    def blk(a, t):
        return a[t * G:(t + 1) * G]

    d_groups = []
    for g, w in enumerate(POOL_WINDOWS):
        cols = slice(g * POOL_GROUP, (g + 1) * POOL_GROUP)
        ug = u[:, cols]
        suffix = [None]
        acc = None
        for m in range(1, min(w - 1, POOL_BUF) + 1):
            h = hist_ref[(POOL_BUF - m) * G:(POOL_BUF - m + 1) * G, cols]
            acc = h if acc is None else acc + h
            suffix.append(acc)
        parts = []
        for t in range(Ls):
            wsum = blk(ug, t)
            for j in range(max(0, t - w + 1), t):
                wsum = wsum + blk(ug, j)
            m = w - 1 - t
            if m > 0:
                wsum = wsum + suffix[m]
            parts.append(wsum * (1.0 / w) - blk(ug, t))
        d_groups.append(jnp.concatenate(parts, axis=0))

    b_t = [blk(loga, 0)]
    for t in range(1, Ls):
        b_t.append(b_t[-1] + blk(loga, t))
    b = jnp.concatenate(b_t, axis=0)
    bend = jnp.concatenate([b_t[-1]] * Ls, axis=0)
    q_in = (q * jnp.exp(b)).astype(BF16)
    k_in = (k * jnp.exp(-b)).astype(BF16)
    k_st = k * jnp.exp(bend - b)
    dec_hi, dec_lo = _split_bf16(jnp.exp(bend))
    tok = lax.broadcasted_iota(jnp.int32, (R, D_GLA_K), 0) >> g_shift
    dec_rows = jnp.where(tok == Ls - 1, dec_hi, jnp.where(tok == Ls - 2, dec_lo, jnp.zeros_like(dec_lo)))
    vb = v.astype(BF16)

    ar = lax.broadcasted_iota(jnp.int32, (R, 2 * R), 0)
    ac = lax.broadcasted_iota(jnp.int32, (R, 2 * R), 1) & (R - 1)
    same_seq_causal = ((ac & (G - 1)) == (ar & (G - 1))) & ((ac >> g_shift) <= (ar >> g_shift))
    lane128 = lax.broadcasted_iota(jnp.int32, (R, 128), 1)
    lane256 = lax.broadcasted_iota(jnp.int32, (R, 256), 1)
    br = lax.broadcasted_iota(jnp.int32, (R, NS), 0)
    bcol = lax.broadcasted_iota(jnp.int32, (R, NS), 1)
    own_state = (bcol >> 6) == (br & (G - 1))
    ones_blk = jnp.ones((R, GLA_DV), BF16)
    zeros_blk = jnp.zeros((R, GLA_DV), BF16)

    def expand(xp, first):
        sw = pltpu.roll(xp, GLA_DK, 1)
        two = jnp.where(lane128 < GLA_DK, xp, sw) if first else jnp.where(lane128 < GLA_DK, sw, xp)
        rep = jnp.concatenate([two] * (NS // 128), axis=1)
        return jnp.where(own_state, rep, 0.0).astype(BF16)

    ks = [slice(128 * p, 128 * (p + 1)) for p in range(2)]
    vs = [slice(256 * p, 256 * (p + 1)) for p in range(2)]
    attn = []
    for p in range(2):
        a = _dot_nt(q_in[:, ks[p]], _head_pair_keys(k_in[:, ks[p]], lane128))
        attn.append(jnp.where(same_seq_causal, a, 0.0).astype(BF16))
    yield
    inter = []
    for h in range(GLA_HEADS):
        p, first = h // 2, h % 2 == 0
        s_flat = s0_ref[:, h].reshape(NS, GLA_DV)
        inter.append(_dot(expand(q_in[:, ks[p]].astype(F32), first), s_flat.astype(BF16)))
    for h in range(GLA_HEADS):
        p, first = h // 2, h % 2 == 0
        s_flat = s0_ref[:, h].reshape(NS, GLA_DV)
        lhs = jnp.concatenate([expand(k_st[:, ks[p]], first),
                               expand(dec_rows[:, ks[p]].astype(F32), first)], axis=0)
        v_h = vb[:, h * GLA_DV:(h + 1) * GLA_DV]
        rhs = jnp.concatenate([jnp.concatenate([v_h, zeros_blk], axis=1),
                               jnp.concatenate([zeros_blk, ones_blk], axis=1)], axis=0)
        ud = _dot_tn(lhs, rhs)
        s_new = ud[:, GLA_DV:] * s_flat + ud[:, :GLA_DV]
        snew_ref[:, h] = s_new.reshape(G, GLA_DK, GLA_DV)
    yield
    o_parts = []
    for p in range(2):
        o_intra = _dot(attn[p], _head_pair_values(vb[:, vs[p]], lane256))
        o_parts.append(o_intra + jnp.concatenate(inter[2 * p:2 * p + 2], axis=1))
    results.append((d_groups, jnp.concatenate(o_parts, axis=1)))


def _mixer_short_kernel(*refs, seq_len, fill_slabs):
    xm_ref, xs_ref, hist_ref, s0_ref = refs[:4]
    (w_in_t_f32_ref, w_zr_ref, w_a2_ref, b_a_ref, w_pool_ref, pscale_ref, gnorm_ref, w_out_f32_ref,
     g_ref, b_ref) = refs[4:4 + N_MIX_W]
    x1m_ref, um_ref, sm_ref, x1s_ref, us_ref, ss_ref, w_in_ref, w_out_ref = refs[-8:]
    wts = (w_in_ref, w_zr_ref, w_a2_ref, b_a_ref, w_pool_ref, pscale_ref, gnorm_ref, w_out_ref, g_ref, b_ref)
    i = pl.program_id(0)

    @pl.when(i == 0)
    def _meta():
        for c0 in range(0, C_Z, CAST_ROWS):
            w_in_ref[:, c0:c0 + CAST_ROWS] = jnp.transpose(w_in_t_f32_ref[c0:c0 + CAST_ROWS, :]).astype(BF16)
        for r0 in range(0, D_MODEL, CAST_ROWS):
            w_out_ref[r0:r0 + CAST_ROWS, :] = w_out_f32_ref[r0:r0 + CAST_ROWS, :].astype(BF16)
        _mixer_meta(xm_ref, wts, x1m_ref, um_ref, sm_ref)

    @pl.when(i > 0)
    def _sample():
        snew_ref = ss_ref.at[0] if fill_slabs else ss_ref
        _mixer_sample(xs_ref, hist_ref, s0_ref, wts, x1s_ref, us_ref, snew_ref, seq_len=seq_len)
        for a in range(1, fill_slabs):
            ss_ref[a] = ss_ref[0]


def _mixer_short(x_meta, x_samp, hist, s0, wts, layer, *, seq_len, s_stack=None):
    rows = x_samp.shape[0]
    n_groups = rows // (SEQ_GROUP * seq_len)
    per_step = GROUPS_PER_STEP if n_groups % GROUPS_PER_STEP == 0 else 1
    G = SEQ_GROUP * per_step
    R = G * seq_len
    n_tiles = rows // R
    assert rows % R == 0
    clamp = lambda i: jnp.maximum(i - 1, 0)
    n_layers = s0.shape[0]
    state_blk = (G, GLA_HEADS, GLA_DK, GLA_DV)
    wspecs = _mixer_weight_specs(layer)
    in_specs = ([pl.BlockSpec((N_META, D_MODEL), lambda i: (0, 0)),
                 pl.BlockSpec((R, D_MODEL), lambda i: (clamp(i), 0)),
                 pl.BlockSpec((None, G * POOL_BUF, D_POOL), lambda i: (layer, clamp(i), 0)),
                 pl.BlockSpec((None,) + state_blk, lambda i: (layer, clamp(i), 0, 0, 0))] + wspecs)
    args = [x_meta, x_samp, hist, s0, *wts]
    aliases = {}
    if s_stack is None:
        assert layer == 0
        state_spec = pl.BlockSpec((n_layers,) + state_blk, lambda i: (0, clamp(i), 0, 0, 0))
    else:
        state_spec = pl.BlockSpec((None,) + state_blk, lambda i: (layer, clamp(i), 0, 0, 0))
        in_specs.append(pl.BlockSpec(memory_space=pl.ANY))
        aliases = {len(args): 5}
        args.append(s_stack)
    n_args = len(args)
    body = functools.partial(_mixer_short_kernel, seq_len=seq_len, fill_slabs=n_layers if s_stack is None else 0)

    def kern(*refs):
        body(*refs[:4 + N_MIX_W], *refs[n_args:])

    whole = lambda shape: pl.BlockSpec(shape, lambda i: (0,) * len(shape))
    return pl.pallas_call(
        kern,
        grid=(n_tiles + 1,),
        in_specs=in_specs,
        out_specs=[whole((N_META, D_MODEL)),
                   whole((N_META, D_POOL)),
                   whole((GLA_HEADS, GLA_DK, GLA_DV)),
                   pl.BlockSpec((R, D_MODEL), lambda i: (clamp(i), 0)),
                   pl.BlockSpec((G * POOL_BUF, D_POOL), lambda i: (clamp(i), 0)),
                   state_spec,
                   whole((D_MODEL, C_Z)),
                   whole((D_MODEL, D_MODEL))],
        out_shape=[jax.ShapeDtypeStruct((N_META, D_MODEL), F32),
                   jax.ShapeDtypeStruct((N_META, D_POOL), F32),
                   jax.ShapeDtypeStruct((GLA_HEADS, GLA_DK, GLA_DV), F32),
                   jax.ShapeDtypeStruct((rows, D_MODEL), F32),
                   jax.ShapeDtypeStruct((n_tiles * G * POOL_BUF, D_POOL), F32),
                   jax.ShapeDtypeStruct(s0.shape, F32),
                   jax.ShapeDtypeStruct((D_MODEL, C_Z), BF16),
                   jax.ShapeDtypeStruct((D_MODEL, D_MODEL), BF16)],
        input_output_aliases=aliases,
        compiler_params=pltpu.CompilerParams(dimension_semantics=("arbitrary",),
                                             vmem_limit_bytes=VMEM_LIMIT),
        name="mixer_short",
    )(*args)


def _ffn_tile(x_ref, wts, y_ref, conv_inputs, store_gate):
    w_up_ref, w_gate_ref, cw_ref, cb_ref, w_down_ref, g_ref, b_ref = wts
    x = x_ref[...]
    xb = x.astype(BF16)

    acts = []
    for j in range(D_FF // FF_CHUNK):
        cs = slice(j * FF_CHUNK, (j + 1) * FF_CHUNK)
        a = _dot(xb, w_up_ref[j])
        gt = _dot(xb, w_gate_ref[j])
        g1, g2 = conv_inputs(gt, cs)
        store_gate(gt, cs)
        gc = cb_ref[:, cs] + cw_ref[0:1, cs] * g2 + cw_ref[1:2, cs] * g1 + cw_ref[2:3, cs] * gt
        acts.append((a * _silu(gc)).astype(BF16))
    yield

    act = jnp.concatenate(acts, axis=1)
    rows = x.shape[0]
    rb = min(rows, DOWN_ROWS)
    for r0 in range(0, rows, rb):
        f = _dot(act[r0:r0 + rb], w_down_ref[...])
        y_ref[r0:r0 + rb, :] = _layer_norm(ALPHA * x[r0:r0 + rb] + f, g_ref[...], b_ref[...])


def _ffn_long_kernel(*refs, n_par):
    x_ref, cprev_ref = refs[:2]
    wts = refs[2:2 + N_FFN_W]
    y_ref, hist_out_ref, gbuf = refs[2 + N_FFN_W:]
    t = pl.program_id(1)
    T = x_ref.shape[1]

    @pl.when(t == 0)
    def _init():
        for j in range(n_par):
            gbuf[j, 0:CONV_PAD, :] = cprev_ref[...]

    def make_tile(j):
        def conv_inputs(gt, cs):
            gbuf[j, CONV_PAD:CONV_PAD + T, cs] = gt
            return gbuf[j, CONV_PAD - 1:CONV_PAD - 1 + T, cs], gbuf[j, CONV_PAD - 2:CONV_PAD - 2 + T, cs]

        def store_gate(gt, cs):
            gbuf[j, 0:CONV_PAD, cs] = gbuf[j, T:T + CONV_PAD, cs]

        return _ffn_tile(x_ref.at[j], wts, y_ref.at[j], conv_inputs, store_gate)

    for _ in itertools.zip_longest(*[make_tile(j) for j in range(n_par)]):
        pass

    @pl.when(t == pl.num_programs(1) - 1)
    def _final():
        for j in range(n_par):
            hist_out_ref[j] = gbuf[j, CONV_PAD - CONV_BUF:CONV_PAD, :]


def _ffn_short_kernel(xm_ref, xs_ref, hist_ref, w_up_ref, w_gate_ref, cw_ref, cb_ref, w_down_ref, g_ref, b_ref,
                      ym_ref, cm_ref, ys_ref, cs_ref, ys3_ref, fm_ref, fs_ref, *, seq_len):
    j = pl.program_id(0)
    G, Ls = SEQ_GROUP, seq_len
    R = G * Ls
    rows = xs_ref.shape[0]
    n_tiles = rows // R

    w_up = w_up_ref[...]
    w_gate = w_gate_ref[...]
    w_down = w_down_ref[...]

    @pl.when(j == 0)
    def _first():
        fm_ref[...] = jnp.zeros_like(fm_ref)
        fs_ref[...] = jnp.zeros_like(fs_ref)

    n_parts = SHORT_FFN_PARTS if n_tiles % SHORT_FFN_PARTS == 0 else 1
    part_tiles = n_tiles // n_parts
    part_rows = part_tiles * R
    hist_rows = CONV_BUF * G
    tok = (lax.broadcasted_iota(jnp.int32, (part_rows, FF_CHUNK), 0) & (R - 1)) >> (G.bit_length() - 1)
    zeros_tail = jnp.zeros((R - hist_rows, FF_CHUNK), F32)

    def sample_conv(gt, part):
        first = part * part_tiles
        hx = jnp.concatenate([piece for n in range(first, first + part_tiles)
                              for piece in (hist_ref[n * hist_rows:(n + 1) * hist_rows, :], zeros_tail)], axis=0)
        g1 = jnp.where(tok >= 1, _roll_rows(gt, G), _roll_rows(hx, -G))
        g2 = jnp.where(tok >= 2, _roll_rows(gt, 2 * G), hx)
        return g1, g2

    def sample_store(gt, part):
        for n in range(part_tiles):
            dst = (part * part_tiles + n) * hist_rows
            cs_ref[dst:dst + hist_rows, :] = gt[n * R + (Ls - CONV_BUF) * G:(n + 1) * R]

    rowm = lax.broadcasted_iota(jnp.int32, (N_META, FF_CHUNK), 0)

    def meta_conv(gt):
        return (jnp.where(rowm >= 1, _roll_rows(gt, 1), 0.0), jnp.where(rowm >= 2, _roll_rows(gt, 2), 0.0))

    def meta_store(gt):
        cm_ref[...] = gt[N_META - CONV_PAD:N_META]

    parts = [(xs_ref.at[pl.ds(p * part_rows, part_rows)], fs_ref.at[pl.ds(p * part_rows, part_rows)],
              functools.partial(sample_conv, part=p), functools.partial(sample_store, part=p))
             for p in range(n_parts)]
    parts.append((xm_ref, fm_ref, meta_conv, meta_store))

    proj = []
    for x_ref, _, _, _ in parts:
        xb = x_ref[...].astype(BF16)
        proj.append((_dot(xb, w_up), _dot(xb, w_gate)))
    acts = []
    for (a, gt), (_, _, conv_inputs, store_gate) in zip(proj, parts):
        g1, g2 = conv_inputs(gt)
        store_gate(gt)
        gc = cb_ref[...] + cw_ref[0:1, :] * g2 + cw_ref[1:2, :] * g1 + cw_ref[2:3, :] * gt
        acts.append((a * _silu(gc)).astype(BF16))
    for act, (_, f_ref, _, _) in zip(acts, parts):
        f_ref[...] += _dot(act, w_down)

    @pl.when(j == pl.num_programs(0) - 1)
    def _last():
        ys = _layer_norm(ALPHA * xs_ref[...] + fs_ref[...], g_ref[...], b_ref[...])
        ys_ref[...] = ys
        for n in range(n_tiles):
            for tk in range(Ls):
                ys3_ref[n * G:(n + 1) * G, tk, :] = ys[n * R + tk * G:n * R + (tk + 1) * G]
        ym_ref[...] = _layer_norm(ALPHA * xm_ref[...] + fm_ref[...], g_ref[...], b_ref[...])


def _ffn_weight_specs(layer):
    whole = lambda shape: pl.BlockSpec(shape, lambda *_: (0,) * len(shape), pipeline_mode=pl.Buffered(1))
    chunked = (D_FF // FF_CHUNK, D_MODEL, FF_CHUNK)
    specs = [whole(chunked), whole(chunked), _layer_spec((3, D_FF), layer), _layer_spec((1, D_FF), layer),
             whole((D_FF, D_MODEL)), _layer_spec((1, D_MODEL), layer), _layer_spec((1, D_MODEL), layer)]
    assert len(specs) == N_FFN_W
    return specs


def _ffn_long(x, cprev, wts, layer):
    B, L, _ = x.shape
    T = min(FFN_TILE, L)
    P = FFN_PAR if B % FFN_PAR == 0 else 1
    assert L % T == 0
    kern = functools.partial(_ffn_long_kernel, n_par=P)
    return pl.pallas_call(
        kern,
        grid=(B // P, L // T),
        in_specs=[pl.BlockSpec((P, T, D_MODEL), lambda b, t: (b, t, 0)),
                  pl.BlockSpec((CONV_PAD, D_FF), lambda b, t: (0, 0))] + _ffn_weight_specs(layer),
        out_specs=[pl.BlockSpec((P, T, D_MODEL), lambda b, t: (b, t, 0)),
                   pl.BlockSpec((P, CONV_BUF, D_FF), lambda b, t: (b, 0, 0))],
        out_shape=[jax.ShapeDtypeStruct((B, L, D_MODEL), F32),
                   jax.ShapeDtypeStruct((B, CONV_BUF, D_FF), F32)],
        scratch_shapes=[pltpu.VMEM((P, T + CONV_PAD, D_FF), F32)],
        compiler_params=pltpu.CompilerParams(dimension_semantics=("arbitrary", "arbitrary"),
                                             vmem_limit_bytes=VMEM_LIMIT),
        name="ffn_long",
    )(x, cprev, *wts)


def _ffn_short(x_meta, x_samp, hist, wts, layer, *, seq_len):
    rows = x_samp.shape[0]
    n_hist = hist.shape[1]
    C = FF_CHUNK
    const = lambda shape: pl.BlockSpec(shape, lambda j: (0,) * len(shape))
    kern = functools.partial(_ffn_short_kernel, seq_len=seq_len)
    return pl.pallas_call(
        kern,
        grid=(D_FF // C,),
        in_specs=[const((N_META, D_MODEL)),
                  const((rows, D_MODEL)),
                  pl.BlockSpec((None, n_hist, C), lambda j: (layer, 0, j)),
                  pl.BlockSpec((None, D_MODEL, C), lambda j: (j, 0, 0)),
                  pl.BlockSpec((None, D_MODEL, C), lambda j: (j, 0, 0)),
                  pl.BlockSpec((None, 3, C), lambda j: (layer, 0, j)),
                  pl.BlockSpec((None, 1, C), lambda j: (layer, 0, j)),
                  pl.BlockSpec((C, D_MODEL), lambda j: (j, 0)),
                  pl.BlockSpec((None, 1, D_MODEL), lambda j: (layer, 0, 0)),
                  pl.BlockSpec((None, 1, D_MODEL), lambda j: (layer, 0, 0))],
        out_specs=[const((N_META, D_MODEL)),
                   pl.BlockSpec((CONV_PAD, C), lambda j: (0, j)),
                   const((rows, D_MODEL)),
                   pl.BlockSpec((n_hist, C), lambda j: (0, j)),
                   const((rows // seq_len, seq_len, D_MODEL))],
        out_shape=[jax.ShapeDtypeStruct((N_META, D_MODEL), F32),
                   jax.ShapeDtypeStruct((CONV_PAD, D_FF), F32),
                   jax.ShapeDtypeStruct((rows, D_MODEL), F32),
                   jax.ShapeDtypeStruct((n_hist, D_FF), F32),
                   jax.ShapeDtypeStruct((rows // seq_len, seq_len, D_MODEL), F32)],
        scratch_shapes=[pltpu.VMEM((N_META, D_MODEL), F32),
                        pltpu.VMEM((rows, D_MODEL), F32)],
        compiler_params=pltpu.CompilerParams(dimension_semantics=("arbitrary",),
                                             vmem_limit_bytes=VMEM_LIMIT),
        name="ffn_short",
    )(x_meta, x_samp, hist, *wts)


def _to_group_major(a, axis):
    n, j = a.shape[axis], a.shape[axis + 1]
    lead, tail = a.shape[:axis], a.shape[axis + 2:]
    a = a.reshape(*lead, n // SEQ_GROUP, SEQ_GROUP, j, *tail)
    a = jnp.swapaxes(a, axis + 1, axis + 2)
    return a.reshape(*lead, n * j, *tail)


def _from_group_major(a, axis, j):
    rows = a.shape[axis]
    n = rows // j
    lead, tail = a.shape[:axis], a.shape[axis + 1:]
    a = a.reshape(*lead, n // SEQ_GROUP, j, SEQ_GROUP, *tail)
    a = jnp.swapaxes(a, axis + 1, axis + 2)
    return a.reshape(*lead, n, j, *tail)


def kernel(x_prompt, x_sample, state_pool, state_gla, state_conv, meta_tokens,
           w_in, w_a2, b_a, w_pool, pool_scale, gla_norm, w_out, ln1_g, ln1_b,
           w_up, w_gate, conv_w, conv_b, w_down, ln2_g, ln2_b):
    NB, LS = x_sample.shape[0], x_sample.shape[1]
    assert NB % SEQ_GROUP == 0 and LS & (LS - 1) == 0 and CONV_BUF <= LS <= POOL_BUF
    assert PAST_LEN >= POOL_BUF and N_META > POOL_BUF

    row = lambda a: a.reshape(DEPTH, 1, a.shape[-1])
    mix_w = (jnp.swapaxes(w_in, 1, 2), w_in[:, :, C_Z:].astype(BF16), w_a2.astype(BF16), row(b_a),
             w_pool.astype(BF16), row(pool_scale), row(gla_norm), w_out, row(ln1_g), row(ln1_b))
    ffn_w = (w_up, w_gate, conv_w, row(conv_b), w_down, row(ln2_g), row(ln2_b))

    hm = meta_tokens.astype(F32)
    hp = x_prompt
    hs = _to_group_major(x_sample, 0)
    pool_hist = _to_group_major(state_pool, 1)
    conv_hist = _to_group_major(state_conv, 1)

    pp, gp, cp, ps_l, cs_l = [], [], [], [], []
    gs = None
    for l in range(DEPTH):
        hm1, um, sm, hs1, ps_new, gs, w_in_bf, w_out_bf = _mixer_short(hm, hs, pool_hist, state_gla, mix_w, l,
                                                                       seq_len=LS, s_stack=gs)
        hp1, pbuf, snew, w_up_bf, w_gate_bf, w_down_bf = _mixer_long(
            hp, um, sm, (w_in_bf,) + mix_w[1:7] + (w_out_bf,) + mix_w[8:], (ffn_w[0], ffn_w[1], ffn_w[4]), l)
        ffn_wl = (w_up_bf, w_gate_bf) + ffn_w[2:4] + (w_down_bf,) + ffn_w[5:]
        hm, cm, hs, cs_new, y_sample = _ffn_short(hm1, hs1, conv_hist, ffn_wl, l, seq_len=LS)
        ps_l.append(ps_new)
        cs_l.append(cs_new)
        hp, cbuf = _ffn_long(hp1, cm, ffn_wl, l)
        pp.append(pbuf)
        gp.append(snew)
        cp.append(cbuf)

    ps = _from_group_major(jnp.stack(ps_l), 1, POOL_BUF)
    cs = _from_group_major(jnp.stack(cs_l), 1, CONV_BUF)
    return (hp, y_sample, jnp.stack(pp), jnp.stack(gp), jnp.stack(cp), ps, gs, cs)
```
